```python
import math
import jax, jax.numpy as jnp
from jax import lax
import numpy as np

D_MODEL = 2048
BATCH = 2
SEQ = 4096
DEPTH = 1

N_META = 16
BLOCK = 128
N_PAD = BLOCK - N_META
WINDOW = 128
ATT_HEADS = 16
ATT_KV_HEADS = 4
ATT_GROUP = ATT_HEADS // ATT_KV_HEADS
ATT_HEAD_DIM = D_MODEL // 32
ATT_WIDTH = ATT_HEADS * ATT_HEAD_DIM
ATT_KV_WIDTH = ATT_KV_HEADS * ATT_HEAD_DIM
ML_HEADS = 4
ML_V_DIM = D_MODEL // 8
ML_QK_DIM = ML_V_DIM // 2
ML_WIDTH = ML_HEADS * ML_V_DIM
ML_QK_WIDTH = ML_HEADS * ML_QK_DIM
CONV_WIDTH = 4
GATE_SOFTCAP = 15.0
IN_SIZES = (ATT_WIDTH, ATT_KV_WIDTH, ATT_KV_WIDTH,
            ML_QK_WIDTH, ML_QK_WIDTH, ML_WIDTH,
            ML_WIDTH, ML_HEADS, ML_HEADS)
IN_COLS = sum(IN_SIZES)
MIX_WIDTH = ATT_WIDTH + ML_WIDTH
D_FF = 4 * D_MODEL
N_BUCKETS = 32
MAX_DISTANCE = 128
EPS = 1e-6
NEG = -1e30

kernel_name = "hymba_swa_sink_mlstm_sqrelu"


def rmsnorm(x, g):
    x32 = x.astype(jnp.float32)
    y = x32 * lax.rsqrt(jnp.mean(x32 * x32, axis=-1, keepdims=True) + EPS)
    return (y * g.astype(jnp.float32)).astype(x.dtype)


def t5_bucket(dist):
    max_exact = N_BUCKETS // 2
    d = jnp.maximum(dist, 0)
    ratio = jnp.maximum(d, max_exact).astype(jnp.float32) / max_exact
    large = max_exact + (jnp.log(ratio) / math.log(MAX_DISTANCE / max_exact)
                         * (N_BUCKETS - max_exact)).astype(jnp.int32)
    large = jnp.minimum(large, N_BUCKETS - 1)
    return jnp.where(d < max_exact, d, large)


def positional_tables(rel_bias, n_blocks):
    seq_len = n_blocks * BLOCK
    r = jnp.arange(BLOCK)[:, None]
    c = jnp.arange(2 * BLOCK)[None, :]
    dist_band = r + BLOCK - c
    bias_band = rel_bias.astype(jnp.float32)[t5_bucket(dist_band)]
    bias_band = bias_band.reshape(BLOCK, 2 * BLOCK, ATT_KV_HEADS, ATT_GROUP).transpose(2, 3, 0, 1)
    blk = jnp.arange(n_blocks)[:, None, None]
    k_pos = (blk - 1) * BLOCK + c[None]
    mask_band = (dist_band >= 0) & (dist_band < WINDOW) & (k_pos >= BLOCK)
    q_pos = jnp.arange(seq_len)[:, None]
    m_pos = N_PAD + jnp.arange(N_META)[None, :]
    dist_meta = q_pos - m_pos
    bias_meta = rel_bias.astype(jnp.float32)[t5_bucket(dist_meta)]
    bias_meta = bias_meta.reshape(n_blocks, BLOCK, N_META, ATT_KV_HEADS, ATT_GROUP).transpose(0, 3, 4, 1, 2)
    mask_meta = (dist_meta >= 0).reshape(n_blocks, BLOCK, N_META)
    return bias_band, mask_band, bias_meta, mask_meta


def swa_sink_attention(q, k, v, sinks, bias_band, mask_band, bias_meta, mask_meta):
    B, LP, _ = q.shape
    NB = LP // BLOCK
    q = q.reshape(B, NB, BLOCK, ATT_KV_HEADS, ATT_GROUP, ATT_HEAD_DIM)
    k = k.reshape(B, NB, BLOCK, ATT_KV_HEADS, ATT_HEAD_DIM)
    v = v.reshape(B, NB, BLOCK, ATT_KV_HEADS, ATT_HEAD_DIM)
    k_prev = jnp.concatenate([jnp.zeros_like(k[:, :1]), k[:, :-1]], axis=1)
    v_prev = jnp.concatenate([jnp.zeros_like(v[:, :1]), v[:, :-1]], axis=1)
    k_band = jnp.concatenate([k_prev, k], axis=2)
    v_band = jnp.concatenate([v_prev, v], axis=2)
    k_meta = k[:, 0, N_PAD:]
    v_meta = v[:, 0, N_PAD:]
    scale = ATT_HEAD_DIM ** -0.5
    s_band = jnp.einsum('bnqhgd,bnkhd->bnhgqk', q, k_band).astype(jnp.float32) * scale + bias_band
    s_meta = jnp.einsum('bnqhgd,bmhd->bnhgqm', q, k_meta).astype(jnp.float32) * scale + bias_meta
    logits = jnp.concatenate([s_band, s_meta], axis=-1)
    mask = jnp.concatenate([mask_band, mask_meta], axis=-1)[:, None, None]
    logits = jnp.where(mask, logits, NEG)
    sink = sinks.astype(jnp.float32).reshape(ATT_KV_HEADS, ATT_GROUP)[:, :, None, None]
    m = jnp.maximum(jnp.max(logits, axis=-1, keepdims=True), sink)
    p = jnp.exp(logits - m)
    denom = jnp.sum(p, axis=-1, keepdims=True) + jnp.exp(sink - m)
    p = (p / denom).astype(v.dtype)
    o = (jnp.einsum('bnhgqk,bnkhd->bnqhgd', p[..., :2 * BLOCK], v_band)
         + jnp.einsum('bnhgqm,bmhd->bnqhgd', p[..., 2 * BLOCK:], v_meta))
    return o.reshape(B, LP, ATT_WIDTH)


def causal_conv_silu(x, w, b):
    L = x.shape[1]
    xp = jnp.pad(x, ((0, 0), (CONV_WIDTH - 1, 0), (0, 0)))
    y = b
    for j in range(CONV_WIDTH):
        y = y + xp[:, j:j + L] * w[j]
    return jax.nn.silu(y)


def mlstm_chunkwise(q, k, v, i_pre, f_pre, valid):
    B, LP, _ = q.shape
    NC = LP // BLOCK
    f32 = jnp.float32
    q = q.astype(f32).reshape(B, NC, BLOCK, ML_HEADS, ML_QK_DIM) * (ML_QK_DIM ** -0.5)
    k = k.astype(f32).reshape(B, NC, BLOCK, ML_HEADS, ML_QK_DIM)
    v = v.astype(f32).reshape(B, NC, BLOCK, ML_HEADS, ML_V_DIM)
    vm = valid[None, :, None]
    log_i = jnp.where(vm, i_pre.astype(f32), NEG)
    log_f = jnp.where(vm, jax.nn.log_sigmoid(f_pre.astype(f32)), 0.0)
    log_i = log_i.reshape(B, NC, BLOCK, ML_HEADS).transpose(0, 1, 3, 2)
    log_f = log_f.reshape(B, NC, BLOCK, ML_HEADS).transpose(0, 1, 3, 2)
    b = jnp.cumsum(log_f, axis=-1)
    b_last = b[..., -1]
    g = b_last[..., None] - b + log_i
    m_loc = jnp.max(g, axis=-1)
    w = jnp.exp(g - m_loc[..., None])
    C_loc = jnp.einsum('bnhl,bnlhk,bnlhv->bnhkv', w, k, v)
    n_loc = jnp.einsum('bnhl,bnlhk->bnhk', w, k)

    def step(carry, xs):
        C, n, m = carry
        bl, ml, Cl, nl = xs
        m_new = jnp.maximum(bl + m, ml)
        a = jnp.exp(bl + m - m_new)
        c = jnp.exp(ml - m_new)
        C_new = a[..., None, None] * C + c[..., None, None] * Cl
        n_new = a[..., None] * n + c[..., None] * nl
        return (C_new, n_new, m_new), (C, n, m)

    init = (jnp.zeros((B, ML_HEADS, ML_QK_DIM, ML_V_DIM), f32),
            jnp.zeros((B, ML_HEADS, ML_QK_DIM), f32),
            jnp.zeros((B, ML_HEADS), f32))
    xs = (jnp.moveaxis(b_last, 1, 0), jnp.moveaxis(m_loc, 1, 0),
          jnp.moveaxis(C_loc, 1, 0), jnp.moveaxis(n_loc, 1, 0))
    _, (C_prev, n_prev, m_prev) = lax.scan(step, init, xs)
    C_prev = jnp.moveaxis(C_prev, 0, 1)
    n_prev = jnp.moveaxis(n_prev, 0, 1)
    m_prev = jnp.moveaxis(m_prev, 0, 1)
    D = b[..., :, None] - b[..., None, :] + log_i[..., None, :]
    causal = jnp.tril(jnp.ones((BLOCK, BLOCK), dtype=bool))
    D = jnp.where(causal, D, -jnp.inf)
    inter_log = b + m_prev[..., None]
    m_t = jnp.maximum(inter_log, jnp.max(D, axis=-1))
    S = jnp.einsum('bnthk,bnshk->bnhts', q, k) * jnp.exp(D - m_t[..., None])
    a = jnp.exp(inter_log - m_t)
    a_t = a.transpose(0, 1, 3, 2)[..., None]
    num = (jnp.einsum('bnhts,bnshv->bnthv', S, v)
           + a_t * jnp.einsum('bnthk,bnhkv->bnthv', q, C_prev))
    den = jnp.sum(S, axis=-1) + a * jnp.einsum('bnthk,bnhk->bnht', q, n_prev)
    den = jnp.maximum(jnp.abs(den), jnp.exp(-m_t))
    h = num / den.transpose(0, 1, 3, 2)[..., None]
    return h.reshape(B, LP, ML_HEADS, ML_V_DIM)


def setup_inputs(seed: int = 0) -> dict:
    key = jax.random.key(seed)
    ks = jax.random.split(key, 16)
    f32 = jnp.float32
    nrm = lambda k, shape, s: jax.random.normal(k, shape, f32) * s
    return {
        "x": nrm(ks[0], (BATCH, SEQ, D_MODEL), 1.0),
        "meta_tokens": nrm(ks[1], (N_META, D_MODEL), 1.0),
        "w_in": nrm(ks[2], (DEPTH, D_MODEL, IN_COLS), D_MODEL ** -0.5),
        "conv_w": nrm(ks[3], (DEPTH, CONV_WIDTH, 2 * ML_QK_WIDTH), CONV_WIDTH ** -0.5),
        "conv_b": nrm(ks[4], (DEPTH, 2 * ML_QK_WIDTH), 0.01),
        "b_igate": nrm(ks[5], (DEPTH, ML_HEADS), 0.1),
        "b_fgate": jnp.linspace(3.0, 6.0, ML_HEADS, dtype=f32)[None] + nrm(ks[6], (DEPTH, ML_HEADS), 0.1),
        "attn_sinks": nrm(ks[7], (DEPTH, ATT_HEADS), 0.5),
        "rel_bias": nrm(ks[8], (N_BUCKETS, ATT_HEADS), 0.5),
        "mh_norm": 1.0 + nrm(ks[9], (DEPTH, ML_WIDTH), 0.02),
        "w_out": nrm(ks[10], (DEPTH, MIX_WIDTH, D_MODEL), MIX_WIDTH ** -0.5),
        "norm_mix": 1.0 + nrm(ks[11], (DEPTH, D_MODEL), 0.02),
        "norm_mlp": 1.0 + nrm(ks[12], (DEPTH, D_MODEL), 0.02),
        "w_up": nrm(ks[13], (DEPTH, D_MODEL, D_FF), D_MODEL ** -0.5),
        "w_down": nrm(ks[14], (DEPTH, D_FF, D_MODEL), D_FF ** -0.5),
        "norm_final": 1.0 + nrm(ks[15], (D_MODEL,), 0.02),
    }


def reference(x, meta_tokens, w_in, conv_w, conv_b, b_igate, b_fgate, attn_sinks,
              rel_bias, mh_norm, w_out, norm_mix, norm_mlp, w_up, w_down, norm_final):
    B, S, D = x.shape
    LP = BLOCK + S
    NB = LP // BLOCK
    meta = jnp.broadcast_to(meta_tokens[None].astype(x.dtype), (B, N_META, D))
    h = jnp.concatenate([jnp.zeros((B, N_PAD, D), x.dtype), meta, x], axis=1)
    valid = jnp.arange(LP) >= N_PAD
    bias_band, mask_band, bias_meta, mask_meta = positional_tables(rel_bias, NB)
    offs = np.cumsum(IN_SIZES)[:-1].tolist()

    for l in range(DEPTH):
        u = rmsnorm(h, norm_mix[l])
        proj = u @ w_in[l]
        a_q, a_k, a_v, m_q, m_k, m_v, m_o, m_i, m_f = jnp.split(proj, offs, axis=-1)
        att = swa_sink_attention(a_q, a_k, a_v, attn_sinks[l],
                                 bias_band, mask_band, bias_meta, mask_meta)
        qk = causal_conv_silu(jnp.concatenate([m_q, m_k], axis=-1), conv_w[l], conv_b[l])
        m_q, m_k = qk[..., :ML_QK_WIDTH], qk[..., ML_QK_WIDTH:]
        i_pre = GATE_SOFTCAP * jnp.tanh((m_i + b_igate[l]) / GATE_SOFTCAP)
        f_pre = GATE_SOFTCAP * jnp.tanh((m_f + b_fgate[l]) / GATE_SOFTCAP)
        hm = mlstm_chunkwise(m_q, m_k, m_v, i_pre, f_pre, valid)
        hm = hm * lax.rsqrt(jnp.mean(hm * hm, axis=-1, keepdims=True) + EPS)
        hm = hm.reshape(B, LP, ML_WIDTH) * mh_norm[l].astype(jnp.float32)
        hm = (hm * jax.nn.sigmoid(m_o.astype(jnp.float32))).astype(x.dtype)
        mix = jnp.concatenate([att.astype(x.dtype), hm], axis=-1)
        h = h + mix @ w_out[l]
        u = rmsnorm(h, norm_mlp[l])
        h = h + jnp.square(jax.nn.relu(u @ w_up[l])) @ w_down[l]

    h = rmsnorm(h, norm_final)
    return h[:, BLOCK:]
```

```python
import functools
import math

import jax
import jax.numpy as jnp
from jax import lax
from jax.experimental import pallas as pl
from jax.experimental.pallas import tpu as pltpu

F32 = jnp.float32
BF16 = jnp.bfloat16

D_MODEL = 2048
N_META = 16
BLOCK = 128
N_PAD = BLOCK - N_META
WINDOW = 128
ATT_HEADS = 16
ATT_KV_HEADS = 4
ATT_GROUP = ATT_HEADS // ATT_KV_HEADS
ATT_HEAD_DIM = 64
ATT_WIDTH = ATT_HEADS * ATT_HEAD_DIM
ATT_KV_WIDTH = ATT_KV_HEADS * ATT_HEAD_DIM
ML_HEADS = 4
ML_V_DIM = 256
ML_QK_DIM = 128
ML_WIDTH = ML_HEADS * ML_V_DIM
ML_QK_WIDTH = ML_HEADS * ML_QK_DIM
CONV_WIDTH = 4
GATE_SOFTCAP = 15.0
D_FF = 4 * D_MODEL
N_BUCKETS = 32
MAX_DISTANCE = 128
EPS = 1e-6
NEG = -1e30

NAT_MQK = 0
NAT_MV = 2 * ML_QK_WIDTH
NAT_MO = NAT_MV + ML_WIDTH
NAT_K = NAT_MO + ML_WIDTH
NAT_COLS = NAT_K + ATT_KV_WIDTH
T_ROWS = ATT_WIDTH + ATT_KV_WIDTH
GATE_LANES = 128

V7X_VMEM_LIMIT = 60 * 1024 * 1024


def _params(n_axes, vmem=V7X_VMEM_LIMIT):
    return pltpu.CompilerParams(dimension_semantics=("arbitrary",) * n_axes,
                                vmem_limit_bytes=vmem)


IN_TM = 512
IN_TN = NAT_COLS // 2


def _rms(x, g):
    return x * lax.rsqrt(jnp.mean(x * x, axis=-1, keepdims=True) + EPS) * g


def _inproj_kernel(n_real_tiles, x_ref, lead_ref, g_ref, wn_ref, wt_ref, wg_ref,
                   nat_ref, t_ref, gate_ref, u_ref):
    i = pl.program_id(0)
    j = pl.program_id(1)

    @pl.when(j == 0)
    def _():
        @pl.when(i < n_real_tiles)
        def _():
            u_ref[...] = _rms(x_ref[...], g_ref[...]).astype(BF16)

        @pl.when(i == n_real_tiles)
        def _():
            u_ref[...] = jnp.zeros_like(u_ref)
            u_ref[0:BLOCK, :] = _rms(lead_ref[...], g_ref[...]).astype(BF16)

        gate_ref[...] = jnp.dot(u_ref[...], wg_ref[...], preferred_element_type=F32)

    @pl.when(j < 2)
    def _():
        nat_ref[...] = jnp.dot(u_ref[...], wn_ref[...],
                               preferred_element_type=F32).astype(BF16)

    @pl.when(j == 2)
    def _():
        t_ref[...] = lax.dot_general(wt_ref[...], u_ref[...], (((1,), (1,)), ((), ())),
                                     preferred_element_type=F32).astype(BF16)


def _inproj(x2, lead, g, w_nat, w_t, w_g):
    n_real = x2.shape[0]
    n_real_tiles = n_real // IN_TM
    rows = (n_real_tiles + 1) * IN_TM
    return pl.pallas_call(
        functools.partial(_inproj_kernel, n_real_tiles),
        grid=(n_real_tiles + 1, 3),
        in_specs=[
            pl.BlockSpec((IN_TM, D_MODEL), lambda i, j: (jnp.minimum(i, n_real_tiles - 1), 0)),
            pl.BlockSpec((BLOCK, D_MODEL), lambda i, j: (0, 0)),
            pl.BlockSpec((1, D_MODEL), lambda i, j: (0, 0)),
            pl.BlockSpec((D_MODEL, IN_TN), lambda i, j: (0, jnp.minimum(j, 1))),
            pl.BlockSpec((T_ROWS, D_MODEL), lambda i, j: (0, 0)),
            pl.BlockSpec((D_MODEL, GATE_LANES), lambda i, j: (0, 0)),
        ],
        out_specs=[
            pl.BlockSpec((IN_TM, IN_TN), lambda i, j: (i, jnp.minimum(j, 1))),
            pl.BlockSpec((T_ROWS, IN_TM), lambda i, j: (0, i)),
            pl.BlockSpec((IN_TM, GATE_LANES), lambda i, j: (i, 0)),
        ],
        out_shape=[
            jax.ShapeDtypeStruct((rows, NAT_COLS), BF16),
            jax.ShapeDtypeStruct((T_ROWS, rows), BF16),
            jax.ShapeDtypeStruct((rows, GATE_LANES), F32),
        ],
        scratch_shapes=[pltpu.VMEM((IN_TM, D_MODEL), BF16)],
        compiler_params=_params(2),
        name="inproj",
    )(x2, lead, g, w_nat, w_t, w_g)


N_KEYS = 2 * BLOCK + N_META
GQ = ATT_GROUP * BLOCK


def _attn_kernel(sink_ref, qt_ref, kc_ref, kp_ref, km_ref, vc_ref, vp_ref, vl_ref,
                 bb_ref, bm_ref, o_ref):
    n = pl.program_id(1)
    c = lax.broadcasted_iota(jnp.int32, (2 * BLOCK, BLOCK), 0)
    r = lax.broadcasted_iota(jnp.int32, (2 * BLOCK, BLOCK), 1)
    dist = r + BLOCK - c
    band_ok = (dist >= 0) & (dist < WINDOW) & ((c >= BLOCK) | (n >= 1))
    mm = lax.broadcasted_iota(jnp.int32, (N_META, BLOCK), 0)
    rr = lax.broadcasted_iota(jnp.int32, (N_META, BLOCK), 1)
    meta_ok = (n + 1) * BLOCK + rr >= N_PAD + mm

    kall = jnp.concatenate([kp_ref[...], kc_ref[...], km_ref[...]], axis=0)
    zpad = jnp.zeros((N_PAD, BLOCK), BF16)
    for h in range(ATT_KV_HEADS):
        lo = (h // 2) * 128
        kpair = kall[:, lo:lo + 128]
        qh = jnp.concatenate(
            [qt_ref[(ATT_GROUP * h + g) * ATT_HEAD_DIM:(ATT_GROUP * h + g + 1) * ATT_HEAD_DIM, :]
             for g in range(ATT_GROUP)], axis=1)
        zq = jnp.zeros_like(qh)
        qz = jnp.concatenate([qh, zq] if h % 2 == 0 else [zq, qh], axis=0)
        st = jnp.dot(kpair, qz, preferred_element_type=F32)
        p_band, p_lead, inv = [], [], []
        for g in range(ATT_GROUP):
            sl = st[:, g * BLOCK:(g + 1) * BLOCK]
            sink = sink_ref[ATT_GROUP * h + g]
            lb = jnp.where(band_ok, sl[:2 * BLOCK] + bb_ref[h, :, g * BLOCK:(g + 1) * BLOCK], NEG)
            lm = jnp.where(meta_ok, sl[2 * BLOCK:] + bm_ref[0, h, :, g * BLOCK:(g + 1) * BLOCK], NEG)
            m = jnp.maximum(jnp.max(lb, axis=0, keepdims=True), jnp.max(lm, axis=0, keepdims=True))
            m = jnp.maximum(m, sink)
            pb = jnp.exp(lb - m)
            pm = jnp.exp(lm - m)
            den = (jnp.sum(pb, axis=0, keepdims=True) + jnp.sum(pm, axis=0, keepdims=True)
                   + jnp.exp(sink - m))
            inv.append(1.0 / den)
            p_band.append(pb.astype(BF16))
            p_lead.append(jnp.concatenate([zpad, pm.astype(BF16)], axis=0))
        pbt = jnp.concatenate(p_band, axis=1)
        plt = jnp.concatenate(p_lead, axis=1)
        hs = slice(h * ATT_HEAD_DIM, (h + 1) * ATT_HEAD_DIM)
        vband = jnp.concatenate([vp_ref[hs, :], vc_ref[hs, :]], axis=1)
        ot = (jnp.dot(vband, pbt, preferred_element_type=F32)
              + jnp.dot(vl_ref[hs, :], plt, preferred_element_type=F32))
        for g in range(ATT_GROUP):
            row = (ATT_GROUP * h + g) * ATT_HEAD_DIM
            o_ref[row:row + ATT_HEAD_DIM, :] = (
                ot[:, g * BLOCK:(g + 1) * BLOCK] * inv[g]).astype(BF16)


def _attention(sinks, nat, proj_t, bias_band_t, bias_meta_t, batch, nblk):
    lead_blk = batch * nblk
    cur = lambda b, n: b * nblk + n
    prev = lambda b, n: b * nblk + jnp.maximum(n - 1, 0)
    k_col = NAT_K // ATT_KV_WIDTH
    v_row = ATT_WIDTH // ATT_KV_WIDTH
    return pl.pallas_call(
        _attn_kernel,
        grid=(batch, nblk),
        in_specs=[
            pl.BlockSpec(memory_space=pltpu.SMEM),
            pl.BlockSpec((ATT_WIDTH, BLOCK), lambda b, n: (0, cur(b, n))),
            pl.BlockSpec((BLOCK, ATT_KV_WIDTH), lambda b, n: (cur(b, n), k_col)),
            pl.BlockSpec((BLOCK, ATT_KV_WIDTH), lambda b, n: (prev(b, n), k_col)),
            pl.BlockSpec((N_META, ATT_KV_WIDTH),
                         lambda b, n: (lead_blk * (BLOCK // N_META) + N_PAD // N_META, k_col)),
            pl.BlockSpec((ATT_KV_WIDTH, BLOCK), lambda b, n: (v_row, cur(b, n))),
            pl.BlockSpec((ATT_KV_WIDTH, BLOCK), lambda b, n: (v_row, prev(b, n))),
            pl.BlockSpec((ATT_KV_WIDTH, BLOCK), lambda b, n: (v_row, lead_blk)),
            pl.BlockSpec((ATT_KV_HEADS, 2 * BLOCK, GQ), lambda b, n: (0, 0, 0)),
            pl.BlockSpec((1, ATT_KV_HEADS, N_META, GQ), lambda b, n: (n, 0, 0, 0)),
        ],
        out_specs=pl.BlockSpec((ATT_WIDTH, BLOCK), lambda b, n: (0, cur(b, n))),
        out_shape=jax.ShapeDtypeStruct((ATT_WIDTH, batch * nblk * BLOCK), BF16),
        compiler_params=_params(2),
        name="attention",
    )(sinks, proj_t, nat, nat, nat, proj_t, proj_t, proj_t, bias_band_t, bias_meta_t)


TAIL = 8


def _mlstm_kernel(p_ref, gate_ref, gb_ref, cw_ref, cb_ref, mhn_ref, o_ref,
                  c_ref, n_ref, m_ref, tail_ref):
    n = pl.program_id(1)

    @pl.when(n == 0)
    def _():
        c_ref[...] = jnp.zeros_like(c_ref)
        n_ref[...] = jnp.zeros_like(n_ref)
        m_ref[...] = jnp.zeros_like(m_ref)
        tail_ref[...] = jnp.zeros_like(tail_ref)

    t_idx = lax.broadcasted_iota(jnp.int32, (BLOCK, GATE_LANES), 0)
    valid = (n > 0) | (t_idx >= N_PAD)
    pre = GATE_SOFTCAP * jnp.tanh((gate_ref[...] + gb_ref[...]) / GATE_SOFTCAP)
    log_i = jnp.where(valid, pre, NEG)
    log_sig = jnp.minimum(pre, 0.0) - jnp.log1p(jnp.exp(-jnp.abs(pre)))
    log_f = jnp.where(valid, log_sig, 0.0)
    row = lax.broadcasted_iota(jnp.int32, (BLOCK, BLOCK), 0)
    col = lax.broadcasted_iota(jnp.int32, (BLOCK, BLOCK), 1)
    causal = col <= row
    tril = jnp.where(causal, 1.0, 0.0).astype(F32)
    b_all = jnp.dot(tril, log_f, preferred_element_type=F32,
                    precision=lax.Precision.HIGHEST)
    log_i_t = log_i.T
    b_all_t = b_all.T

    xqk = p_ref[:, NAT_MQK:NAT_MV].astype(F32)
    tail = tail_ref[...]
    row8 = lax.broadcasted_iota(jnp.int32, (TAIL, 2 * ML_QK_WIDTH), 0)
    y = cb_ref[...] + cw_ref[CONV_WIDTH - 1:CONV_WIDTH, :] * xqk
    for k in range(1, CONV_WIDTH):
        rolled = pltpu.roll(xqk, k, 0)
        head = jnp.where(row8 < k, pltpu.roll(tail, k, 0), rolled[0:TAIL])
        shifted = jnp.concatenate([head, rolled[TAIL:]], axis=0)
        y = y + cw_ref[CONV_WIDTH - 1 - k:CONV_WIDTH - k, :] * shifted
    tail_ref[...] = xqk[BLOCK - TAIL:, :]
    act = y * jax.nn.sigmoid(y)
    q_all = act[:, :ML_QK_WIDTH] * (ML_QK_DIM ** -0.5)
    k_all = act[:, ML_QK_WIDTH:]

    for h in range(ML_HEADS):
        q_f = q_all[:, h * ML_QK_DIM:(h + 1) * ML_QK_DIM]
        k_f = k_all[:, h * ML_QK_DIM:(h + 1) * ML_QK_DIM]
        q_h = q_f.astype(BF16)
        k_h = k_f.astype(BF16)
        v_h = p_ref[:, NAT_MV + h * ML_V_DIM:NAT_MV + (h + 1) * ML_V_DIM]
        b_col = b_all[:, ML_HEADS + h:ML_HEADS + h + 1]
        li_col = log_i[:, h:h + 1]
        b_row = b_all_t[ML_HEADS + h:ML_HEADS + h + 1, :]
        li_row = log_i_t[h:h + 1, :]
        b_last = b_all[BLOCK - 1:BLOCK, ML_HEADS + h:ML_HEADS + h + 1]
        c_prev = c_ref[h]
        n_prev = n_ref[h][0:1, :]
        m_prev = m_ref[h][0:1, 0:1]

        dmat = jnp.where(causal, b_col - b_row + li_row, -jnp.inf)
        inter = b_col + m_prev
        m_t = jnp.maximum(inter, jnp.max(dmat, axis=1, keepdims=True))
        s = lax.dot_general(q_h, k_h, (((1,), (1,)), ((), ())),
                            preferred_element_type=F32) * jnp.exp(dmat - m_t)
        a_t = jnp.exp(inter - m_t)
        num = (jnp.dot(s.astype(BF16), v_h, preferred_element_type=F32)
               + a_t * jnp.dot(q_h, c_prev.astype(BF16), preferred_element_type=F32))
        den = (jnp.sum(s, axis=1, keepdims=True)
               + a_t * jnp.sum(q_f * n_prev, axis=1, keepdims=True))
        den = jnp.maximum(jnp.abs(den), jnp.exp(-m_t))
        hh = num / den
        hh = hh * lax.rsqrt(jnp.mean(hh * hh, axis=-1, keepdims=True) + EPS)
        vs = slice(h * ML_V_DIM, (h + 1) * ML_V_DIM)
        hh = hh * mhn_ref[:, vs]
        m_o = p_ref[:, NAT_MO + h * ML_V_DIM:NAT_MO + (h + 1) * ML_V_DIM].astype(F32)
        o_ref[:, vs] = (hh * jax.nn.sigmoid(m_o)).astype(BF16)

        g_col = b_last - b_col + li_col
        m_loc = jnp.max(g_col, axis=0, keepdims=True)
        wk = jnp.exp(g_col - m_loc) * k_f
        c_loc = lax.dot_general(wk.astype(BF16), v_h, (((0,), (0,)), ((), ())),
                                preferred_element_type=F32)
        n_loc = jnp.sum(wk, axis=0, keepdims=True)
        m_new = jnp.maximum(b_last + m_prev, m_loc)
        a = jnp.exp(b_last + m_prev - m_new)
        cc = jnp.exp(m_loc - m_new)
        c_ref[h] = a * c_prev + cc * c_loc
        n_ref[h] = jnp.broadcast_to(a * n_prev + cc * n_loc, (8, ML_QK_DIM))
        m_ref[h] = jnp.broadcast_to(m_new, (8, 128))


def _mlstm(nat, gates, gate_bias, conv_w, conv_b, mh_norm, batch, nblk):
    lead_blk = batch * nblk
    src = lambda b, n: jnp.where(n == 0, lead_blk, b * nblk + n - 1)
    return pl.pallas_call(
        _mlstm_kernel,
        grid=(batch, nblk + 1),
        in_specs=[
            pl.BlockSpec((BLOCK, NAT_K), lambda b, n: (src(b, n), 0)),
            pl.BlockSpec((BLOCK, GATE_LANES), lambda b, n: (src(b, n), 0)),
            pl.BlockSpec((1, GATE_LANES), lambda b, n: (0, 0)),
            pl.BlockSpec((CONV_WIDTH, 2 * ML_QK_WIDTH), lambda b, n: (0, 0)),
            pl.BlockSpec((1, 2 * ML_QK_WIDTH), lambda b, n: (0, 0)),
            pl.BlockSpec((1, ML_WIDTH), lambda b, n: (0, 0)),
        ],
        out_specs=pl.BlockSpec((BLOCK, ML_WIDTH), lambda b, n: (b * nblk + jnp.maximum(n - 1, 0), 0)),
        out_shape=jax.ShapeDtypeStruct((batch * nblk * BLOCK, ML_WIDTH), BF16),
        scratch_shapes=[
            pltpu.VMEM((ML_HEADS, ML_QK_DIM, ML_V_DIM), F32),
            pltpu.VMEM((ML_HEADS, 8, ML_QK_DIM), F32),
            pltpu.VMEM((ML_HEADS, 8, 128), F32),
            pltpu.VMEM((TAIL, 2 * ML_QK_WIDTH), F32),
        ],
        compiler_params=_params(2),
        name="mlstm",
    )(nat, gates, gate_bias, conv_w, conv_b, mh_norm)


OUT_TM = 512


def _outproj_kernel(att_ref, hm_ref, x_ref, wa_ref, wm_ref, g_ref, h_ref, u_ref):
    h = (x_ref[...]
         + lax.dot_general(att_ref[...], wa_ref[...], (((0,), (0,)), ((), ())),
                           preferred_element_type=F32)
         + jnp.dot(hm_ref[...], wm_ref[...], preferred_element_type=F32))
    h_ref[...] = h
    u_ref[...] = _rms(h, g_ref[...]).astype(BF16)


def _outproj(att_t, hm, x2, w_att, w_ml, g):
    rows = x2.shape[0]
    return pl.pallas_call(
        _outproj_kernel,
        grid=(rows // OUT_TM,),
        in_specs=[
            pl.BlockSpec((ATT_WIDTH, OUT_TM), lambda i: (0, i)),
            pl.BlockSpec((OUT_TM, ML_WIDTH), lambda i: (i, 0)),
            pl.BlockSpec((OUT_TM, D_MODEL), lambda i: (i, 0)),
            pl.BlockSpec((ATT_WIDTH, D_MODEL), lambda i: (0, 0)),
            pl.BlockSpec((ML_WIDTH, D_MODEL), lambda i: (0, 0)),
            pl.BlockSpec((1, D_MODEL), lambda i: (0, 0)),
        ],
        out_specs=[
            pl.BlockSpec((OUT_TM, D_MODEL), lambda i: (i, 0)),
            pl.BlockSpec((OUT_TM, D_MODEL), lambda i: (i, 0)),
        ],
        out_shape=[
            jax.ShapeDtypeStruct((rows, D_MODEL), F32),
            jax.ShapeDtypeStruct((rows, D_MODEL), BF16),
        ],
        compiler_params=_params(1),
        name="outproj",
    )(att_t, hm, x2, w_att, w_ml, g)


MLP_TM = 1024
MLP_TF = 512


def _mlp_kernel(u_ref, h_ref, wu_ref, wd_ref, g_ref, o_ref):
    j = pl.program_id(1)

    @pl.when(j == 0)
    def _():
        o_ref[...] = h_ref[...]

    a = jnp.dot(u_ref[...], wu_ref[...], preferred_element_type=F32)
    a = jnp.square(jnp.maximum(a, 0.0)).astype(BF16)
    o_ref[...] += jnp.dot(a, wd_ref[...], preferred_element_type=F32)

    @pl.when(j == pl.num_programs(1) - 1)
    def _():
        o_ref[...] = _rms(o_ref[...], g_ref[...])


def _mlp(u, h, w_up, w_down, g):
    rows = u.shape[0]
    return pl.pallas_call(
        _mlp_kernel,
        grid=(rows // MLP_TM, D_FF // MLP_TF),
        in_specs=[
            pl.BlockSpec((MLP_TM, D_MODEL), lambda i, j: (i, 0)),
            pl.BlockSpec((MLP_TM, D_MODEL), lambda i, j: (i, 0)),
            pl.BlockSpec((D_MODEL, MLP_TF), lambda i, j: (0, j)),
            pl.BlockSpec((MLP_TF, D_MODEL), lambda i, j: (j, 0)),
            pl.BlockSpec((1, D_MODEL), lambda i, j: (0, 0)),
        ],
        out_specs=pl.BlockSpec((MLP_TM, D_MODEL), lambda i, j: (i, 0)),
        out_shape=jax.ShapeDtypeStruct((rows, D_MODEL), F32),
        compiler_params=_params(2),
        name="mlp",
    )(u, h, w_up, w_down, g)


def _t5_bucket(dist):
    max_exact = N_BUCKETS // 2
    d = jnp.maximum(dist, 0)
    ratio = jnp.maximum(d, max_exact).astype(F32) / max_exact
    large = max_exact + (jnp.log(ratio) / math.log(MAX_DISTANCE / max_exact)
                         * (N_BUCKETS - max_exact)).astype(jnp.int32)
    large = jnp.minimum(large, N_BUCKETS - 1)
    return jnp.where(d < max_exact, d, large)


def _bias_tables(rel_bias, nblk):
    rb = rel_bias.astype(F32)
    c = jnp.arange(2 * BLOCK)[:, None]
    r = jnp.arange(BLOCK)[None, :]
    band = rb[_t5_bucket(r + BLOCK - c)]
    band = band.reshape(2 * BLOCK, BLOCK, ATT_KV_HEADS, ATT_GROUP).transpose(2, 0, 3, 1)
    band = band.reshape(ATT_KV_HEADS, 2 * BLOCK, GQ)
    q_pos = BLOCK + jnp.arange(nblk * BLOCK)[None, :]
    m_pos = N_PAD + jnp.arange(N_META)[:, None]
    meta = rb[_t5_bucket(q_pos - m_pos)]
    meta = meta.reshape(N_META, nblk, BLOCK, ATT_KV_HEADS, ATT_GROUP).transpose(1, 3, 0, 4, 2)
    meta = meta.reshape(nblk, ATT_KV_HEADS, N_META, GQ)
    return band, meta


def kernel(x, meta_tokens, w_in, conv_w, conv_b, b_igate, b_fgate, attn_sinks, rel_bias,
           mh_norm, w_out, norm_mix, norm_mlp, w_up, w_down, norm_final):
    batch, seq, d = x.shape
    nblk = seq // BLOCK
    x2 = x.reshape(batch * seq, d)
    lead = jnp.concatenate([jnp.zeros((N_PAD, d), x.dtype), meta_tokens.astype(x.dtype)], axis=0)

    w = w_in[0]
    o_q, o_k, o_v, o_mq, o_mv, o_mo, o_i = 0, 1024, 1280, 1536, 2560, 3584, 4608
    w_nat = jnp.concatenate([w[:, o_mq:o_i], w[:, o_k:o_v]], axis=1).astype(BF16)
    w_t = jnp.concatenate([w[:, o_q:o_k] * (ATT_HEAD_DIM ** -0.5), w[:, o_v:o_mq]], axis=1).T.astype(BF16)
    w_g = jnp.pad(w[:, o_i:], ((0, 0), (0, GATE_LANES - 2 * ML_HEADS))).astype(BF16)
    gate_bias = jnp.pad(jnp.concatenate([b_igate[0], b_fgate[0]]).astype(F32),
                        (0, GATE_LANES - 2 * ML_HEADS))[None]
    bias_band_t, bias_meta_t = _bias_tables(rel_bias, nblk)

    nat, proj_t, gates = _inproj(x2, lead, norm_mix[0][None].astype(F32), w_nat, w_t, w_g)
    att_t = _attention(attn_sinks[0].astype(F32), nat, proj_t, bias_band_t, bias_meta_t, batch, nblk)
    hm = _mlstm(nat, gates, gate_bias, conv_w[0].astype(F32), conv_b[0][None].astype(F32),
                mh_norm[0][None].astype(F32), batch, nblk)
    wo = w_out[0].astype(BF16)
    h2, u2 = _outproj(att_t, hm, x2, wo[:ATT_WIDTH], wo[ATT_WIDTH:], norm_mlp[0][None].astype(F32))
    out = _mlp(u2, h2, w_up[0].astype(BF16), w_down[0].astype(BF16), norm_final[None].astype(F32))
    return out.reshape(batch, seq, d)
```

```python
import functools
import math

import jax
import jax.numpy as jnp
from jax import lax
from jax.experimental import pallas as pl
from jax.experimental.pallas import tpu as pltpu

F32 = jnp.float32
BF16 = jnp.bfloat16

D_MODEL = 2048
N_META = 16
BLOCK = 128
N_PAD = BLOCK - N_META
WINDOW = 128
ATT_HEADS = 16
ATT_KV_HEADS = 4
ATT_GROUP = ATT_HEADS // ATT_KV_HEADS
ATT_HEAD_DIM = 64
ATT_WIDTH = ATT_HEADS * ATT_HEAD_DIM
ATT_KV_WIDTH = ATT_KV_HEADS * ATT_HEAD_DIM
ML_HEADS = 4
ML_V_DIM = 256
ML_QK_DIM = 128
ML_WIDTH = ML_HEADS * ML_V_DIM
ML_QK_WIDTH = ML_HEADS * ML_QK_DIM
CONV_WIDTH = 4
GATE_SOFTCAP = 15.0
D_FF = 4 * D_MODEL
N_BUCKETS = 32
MAX_DISTANCE = 128
EPS = 1e-6
NEG = -1e30

GATE_LANES = 128
NAT_MQK = 0
NAT_MV = 2 * ML_QK_WIDTH
NAT_MO = NAT_MV + ML_WIDTH
NAT_K = NAT_MO + ML_WIDTH
NAT_COLS = NAT_K + ATT_KV_WIDTH + GATE_LANES
T_ROWS = ATT_WIDTH + ATT_KV_WIDTH

V7X_VMEM_LIMIT = 60 * 1024 * 1024


def _params(n_axes, vmem=V7X_VMEM_LIMIT):
    return pltpu.CompilerParams(dimension_semantics=("arbitrary",) * n_axes,
                                vmem_limit_bytes=vmem)


IN_TM = 1024
IN_NAT_STEPS = 3
IN_TN = NAT_COLS // IN_NAT_STEPS
MXU_COLS = 256


def _rms(x, g):
    return x * lax.rsqrt(jnp.mean(x * x, axis=-1, keepdims=True) + EPS) * g


def _inproj_kernel(x_ref, g_ref, wn_ref, wq_ref, *rest):
    nat_ref, t_ref, gate_ref, u_ref = rest[-4:]
    j = pl.program_id(1)

    @pl.when(j == 0)
    def _():
        u_ref[...] = _rms(x_ref[...], g_ref[...]).astype(BF16)

    @pl.when(j < IN_NAT_STEPS)
    def _():
        r = jnp.dot(u_ref[...], wn_ref[...], preferred_element_type=F32)
        nat_ref[...] = r.astype(BF16)

        @pl.when(j == IN_NAT_STEPS - 1)
        def _():
            gate_ref[...] = r[:, IN_TN - GATE_LANES:]

    @pl.when(j == IN_NAT_STEPS)
    def _():
        for c in range(T_ROWS // MXU_COLS):
            cs = slice(c * MXU_COLS, (c + 1) * MXU_COLS)
            r = jnp.dot(u_ref[...], wq_ref[:, cs], preferred_element_type=F32)
            t_ref[cs, :] = r.T.astype(BF16)


def _inproj_call(x_rows, g, w_nat, w_qv, tm, row_off, total_rows, prev=None):
    n_tiles = x_rows.shape[0] // tm
    off = row_off // tm
    last_nat = IN_NAT_STEPS - 1
    in_specs = [
        pl.BlockSpec((tm, D_MODEL), lambda i, j: (i, 0)),
        pl.BlockSpec((1, D_MODEL), lambda i, j: (0, 0)),
        pl.BlockSpec((D_MODEL, IN_TN), lambda i, j: (0, jnp.minimum(j, last_nat))),
        pl.BlockSpec((D_MODEL, T_ROWS), lambda i, j: (0, 0)),
    ]
    args = [x_rows, g, w_nat, w_qv]
    aliases = {}
    if prev is not None:
        in_specs += [pl.BlockSpec(memory_space=pl.ANY)] * 3
        aliases = {len(args) + k: k for k in range(3)}
        args += list(prev)
    return pl.pallas_call(
        _inproj_kernel,
        grid=(n_tiles, IN_NAT_STEPS + 1),
        in_specs=in_specs,
        out_specs=[
            pl.BlockSpec((tm, IN_TN), lambda i, j: (off + i, jnp.minimum(j, last_nat))),
            pl.BlockSpec((T_ROWS, tm), lambda i, j: (0, off + i)),
            pl.BlockSpec((tm, GATE_LANES), lambda i, j: (off + i, 0)),
        ],
        out_shape=[
            jax.ShapeDtypeStruct((total_rows, NAT_COLS), BF16),
            jax.ShapeDtypeStruct((T_ROWS, total_rows), BF16),
            jax.ShapeDtypeStruct((total_rows, GATE_LANES), F32),
        ],
        scratch_shapes=[pltpu.VMEM((tm, D_MODEL), BF16)],
        input_output_aliases=aliases,
        compiler_params=_params(2),
        name="inproj_lead" if prev is not None else "inproj",
    )(*args)


def _inproj(x2, lead, g, w_nat, w_qv):
    n_real = x2.shape[0]
    total = n_real + BLOCK
    outs = _inproj_call(x2, g, w_nat, w_qv, IN_TM, 0, total)
    return _inproj_call(lead, g, w_nat, w_qv, BLOCK, n_real, total, prev=outs)


N_KEYS = 2 * BLOCK + N_META
GQ = ATT_GROUP * BLOCK


def _attn_kernel(sink_ref, qt_ref, kc_ref, kp_ref, km_ref, vc_ref, vp_ref, vl_ref,
                 bb_ref, bm_ref, o_ref):
    n = pl.program_id(1)
    c = lax.broadcasted_iota(jnp.int32, (2 * BLOCK, BLOCK), 0)
    r = lax.broadcasted_iota(jnp.int32, (2 * BLOCK, BLOCK), 1)
    dist = r + BLOCK - c
    band_ok = (dist >= 0) & (dist < WINDOW) & ((c >= BLOCK) | (n >= 1))
    mm = lax.broadcasted_iota(jnp.int32, (N_META, BLOCK), 0)
    rr = lax.broadcasted_iota(jnp.int32, (N_META, BLOCK), 1)
    meta_ok = (n + 1) * BLOCK + rr >= N_PAD + mm

    kall = jnp.concatenate([kp_ref[...], kc_ref[...], km_ref[...]], axis=0)
    zpad = jnp.zeros((N_PAD, BLOCK), BF16)
    for h in range(ATT_KV_HEADS):
        lo = (h // 2) * 128
        kpair = kall[:, lo:lo + 128]
        qh = jnp.concatenate(
            [qt_ref[(ATT_GROUP * h + g) * ATT_HEAD_DIM:(ATT_GROUP * h + g + 1) * ATT_HEAD_DIM, :]
             for g in range(ATT_GROUP)], axis=1)
        zq = jnp.zeros_like(qh)
        qz = jnp.concatenate([qh, zq] if h % 2 == 0 else [zq, qh], axis=0)
        st = jnp.dot(kpair, qz, preferred_element_type=F32)
        p_band, p_lead, inv = [], [], []
        for g in range(ATT_GROUP):
            sl = st[:, g * BLOCK:(g + 1) * BLOCK]
            sink = sink_ref[ATT_GROUP * h + g]
            lb = jnp.where(band_ok, sl[:2 * BLOCK] + bb_ref[h, :, g * BLOCK:(g + 1) * BLOCK], NEG)
            lm = jnp.where(meta_ok, sl[2 * BLOCK:] + bm_ref[0, h, :, g * BLOCK:(g + 1) * BLOCK], NEG)
            m = jnp.maximum(jnp.max(lb, axis=0, keepdims=True), jnp.max(lm, axis=0, keepdims=True))
            m = jnp.maximum(m, sink)
            pb = jnp.exp(lb - m)
            pm = jnp.exp(lm - m)
            den = (jnp.sum(pb, axis=0, keepdims=True) + jnp.sum(pm, axis=0, keepdims=True)
                   + jnp.exp(sink - m))
            inv.append(1.0 / den)
            p_band.append(pb.astype(BF16))
            p_lead.append(jnp.concatenate([zpad, pm.astype(BF16)], axis=0))
        pbt = jnp.concatenate(p_band, axis=1)
        plt = jnp.concatenate(p_lead, axis=1)
        hs = slice(h * ATT_HEAD_DIM, (h + 1) * ATT_HEAD_DIM)
        vband = jnp.concatenate([vp_ref[hs, :], vc_ref[hs, :]], axis=1)
        ot = (jnp.dot(vband, pbt, preferred_element_type=F32)
              + jnp.dot(vl_ref[hs, :], plt, preferred_element_type=F32))
        for g in range(ATT_GROUP):
            row = (ATT_GROUP * h + g) * ATT_HEAD_DIM
            o_ref[row:row + ATT_HEAD_DIM, :] = (
                ot[:, g * BLOCK:(g + 1) * BLOCK] * inv[g]).astype(BF16)


def _attention(sinks, nat, proj_t, bias_band_t, bias_meta_t, batch, nblk):
    lead_blk = batch * nblk
    cur = lambda b, n: b * nblk + n
    prev = lambda b, n: b * nblk + jnp.maximum(n - 1, 0)
    k_col = NAT_K // ATT_KV_WIDTH
    v_row = ATT_WIDTH // ATT_KV_WIDTH
    return pl.pallas_call(
        _attn_kernel,
        grid=(batch, nblk),
        in_specs=[
            pl.BlockSpec(memory_space=pltpu.SMEM),
            pl.BlockSpec((ATT_WIDTH, BLOCK), lambda b, n: (0, cur(b, n))),
            pl.BlockSpec((BLOCK, ATT_KV_WIDTH), lambda b, n: (cur(b, n), k_col)),
            pl.BlockSpec((BLOCK, ATT_KV_WIDTH), lambda b, n: (prev(b, n), k_col)),
            pl.BlockSpec((N_META, ATT_KV_WIDTH),
                         lambda b, n: (lead_blk * (BLOCK // N_META) + N_PAD // N_META, k_col)),
            pl.BlockSpec((ATT_KV_WIDTH, BLOCK), lambda b, n: (v_row, cur(b, n))),
            pl.BlockSpec((ATT_KV_WIDTH, BLOCK), lambda b, n: (v_row, prev(b, n))),
            pl.BlockSpec((ATT_KV_WIDTH, BLOCK), lambda b, n: (v_row, lead_blk)),
            pl.BlockSpec((ATT_KV_HEADS, 2 * BLOCK, GQ), lambda b, n: (0, 0, 0)),
            pl.BlockSpec((1, ATT_KV_HEADS, N_META, GQ), lambda b, n: (n, 0, 0, 0)),
        ],
        out_specs=pl.BlockSpec((ATT_WIDTH, BLOCK), lambda b, n: (0, cur(b, n))),
        out_shape=jax.ShapeDtypeStruct((ATT_WIDTH, batch * nblk * BLOCK), BF16),
        compiler_params=_params(2),
        name="attention",
    )(sinks, proj_t, nat, nat, nat, proj_t, proj_t, proj_t, bias_band_t, bias_meta_t)


TAIL = 8


def _mlstm_kernel(p_ref, gate_ref, gb_ref, cw_ref, cb_ref, mhn_ref, o_ref,
                  c_ref, n_ref, m_ref, tail_ref):
    n = pl.program_id(1)

    @pl.when(n == 0)
    def _():
        c_ref[...] = jnp.zeros_like(c_ref)
        n_ref[...] = jnp.zeros_like(n_ref)
        m_ref[...] = jnp.zeros_like(m_ref)
        tail_ref[...] = jnp.zeros_like(tail_ref)

    t_idx = lax.broadcasted_iota(jnp.int32, (BLOCK, GATE_LANES), 0)
    valid = (n > 0) | (t_idx >= N_PAD)
    pre = GATE_SOFTCAP * jnp.tanh((gate_ref[...] + gb_ref[...]) / GATE_SOFTCAP)
    log_i = jnp.where(valid, pre, NEG)
    log_sig = jnp.minimum(pre, 0.0) - jnp.log1p(jnp.exp(-jnp.abs(pre)))
    log_f = jnp.where(valid, log_sig, 0.0)
    row = lax.broadcasted_iota(jnp.int32, (BLOCK, BLOCK), 0)
    col = lax.broadcasted_iota(jnp.int32, (BLOCK, BLOCK), 1)
    causal = col <= row
    tril = jnp.where(causal, 1.0, 0.0).astype(F32)
    b_all = jnp.dot(tril, log_f, preferred_element_type=F32,
                    precision=lax.Precision.HIGHEST)
    log_i_t = log_i.T
    b_all_t = b_all.T

    xqk = p_ref[:, NAT_MQK:NAT_MV].astype(F32)
    tail = tail_ref[...]
    row8 = lax.broadcasted_iota(jnp.int32, (TAIL, 2 * ML_QK_WIDTH), 0)
    y = cb_ref[...] + cw_ref[CONV_WIDTH - 1:CONV_WIDTH, :] * xqk
    for k in range(1, CONV_WIDTH):
        rolled = pltpu.roll(xqk, k, 0)
        head = jnp.where(row8 < k, pltpu.roll(tail, k, 0), rolled[0:TAIL])
        shifted = jnp.concatenate([head, rolled[TAIL:]], axis=0)
        y = y + cw_ref[CONV_WIDTH - 1 - k:CONV_WIDTH - k, :] * shifted
    tail_ref[...] = xqk[BLOCK - TAIL:, :]
    act = y * jax.nn.sigmoid(y)
    q_all = act[:, :ML_QK_WIDTH] * (ML_QK_DIM ** -0.5)
    k_all = act[:, ML_QK_WIDTH:]

    for h in range(ML_HEADS):
        q_f = q_all[:, h * ML_QK_DIM:(h + 1) * ML_QK_DIM]
        k_f = k_all[:, h * ML_QK_DIM:(h + 1) * ML_QK_DIM]
        q_h = q_f.astype(BF16)
        k_h = k_f.astype(BF16)
        v_h = p_ref[:, NAT_MV + h * ML_V_DIM:NAT_MV + (h + 1) * ML_V_DIM]
        b_col = b_all[:, ML_HEADS + h:ML_HEADS + h + 1]
        li_col = log_i[:, h:h + 1]
        b_row = b_all_t[ML_HEADS + h:ML_HEADS + h + 1, :]
        li_row = log_i_t[h:h + 1, :]
        b_last = b_all[BLOCK - 1:BLOCK, ML_HEADS + h:ML_HEADS + h + 1]
        c_prev = c_ref[h]
        n_prev = n_ref[h][0:1, :]
        m_prev = m_ref[h][0:1, 0:1]

        dmat = jnp.where(causal, b_col - b_row + li_row, -jnp.inf)
        inter = b_col + m_prev
        m_t = jnp.maximum(inter, jnp.max(dmat, axis=1, keepdims=True))
        s = lax.dot_general(q_h, k_h, (((1,), (1,)), ((), ())),
                            preferred_element_type=F32) * jnp.exp(dmat - m_t)
        a_t = jnp.exp(inter - m_t)
        num = (jnp.dot(s.astype(BF16), v_h, preferred_element_type=F32)
               + a_t * jnp.dot(q_h, c_prev.astype(BF16), preferred_element_type=F32))
        den = (jnp.sum(s, axis=1, keepdims=True)
               + a_t * jnp.sum(q_f * n_prev, axis=1, keepdims=True))
        den = jnp.maximum(jnp.abs(den), jnp.exp(-m_t))
        hh = num / den
        hh = hh * lax.rsqrt(jnp.mean(hh * hh, axis=-1, keepdims=True) + EPS)
        vs = slice(h * ML_V_DIM, (h + 1) * ML_V_DIM)
        hh = hh * mhn_ref[:, vs]
        m_o = p_ref[:, NAT_MO + h * ML_V_DIM:NAT_MO + (h + 1) * ML_V_DIM].astype(F32)
        o_ref[:, vs] = (hh * jax.nn.sigmoid(m_o)).astype(BF16)

        g_col = b_last - b_col + li_col
        m_loc = jnp.max(g_col, axis=0, keepdims=True)
        wk = jnp.exp(g_col - m_loc) * k_f
        c_loc = lax.dot_general(wk.astype(BF16), v_h, (((0,), (0,)), ((), ())),
                                preferred_element_type=F32)
        n_loc = jnp.sum(wk, axis=0, keepdims=True)
        m_new = jnp.maximum(b_last + m_prev, m_loc)
        a = jnp.exp(b_last + m_prev - m_new)
        cc = jnp.exp(m_loc - m_new)
        c_ref[h] = a * c_prev + cc * c_loc
        n_ref[h] = jnp.broadcast_to(a * n_prev + cc * n_loc, (8, ML_QK_DIM))
        m_ref[h] = jnp.broadcast_to(m_new, (8, 128))


def _mlstm(nat, gates, gate_bias, conv_w, conv_b, mh_norm, batch, nblk):
    lead_blk = batch * nblk
    src = lambda b, n: jnp.where(n == 0, lead_blk, b * nblk + n - 1)
    return pl.pallas_call(
        _mlstm_kernel,
        grid=(batch, nblk + 1),
        in_specs=[
            pl.BlockSpec((BLOCK, NAT_K), lambda b, n: (src(b, n), 0)),
            pl.BlockSpec((BLOCK, GATE_LANES), lambda b, n: (src(b, n), 0)),
            pl.BlockSpec((1, GATE_LANES), lambda b, n: (0, 0)),
            pl.BlockSpec((CONV_WIDTH, 2 * ML_QK_WIDTH), lambda b, n: (0, 0)),
            pl.BlockSpec((1, 2 * ML_QK_WIDTH), lambda b, n: (0, 0)),
            pl.BlockSpec((1, ML_WIDTH), lambda b, n: (0, 0)),
        ],
        out_specs=pl.BlockSpec((BLOCK, ML_WIDTH), lambda b, n: (b * nblk + jnp.maximum(n - 1, 0), 0)),
        out_shape=jax.ShapeDtypeStruct((batch * nblk * BLOCK, ML_WIDTH), BF16),
        scratch_shapes=[
            pltpu.VMEM((ML_HEADS, ML_QK_DIM, ML_V_DIM), F32),
            pltpu.VMEM((ML_HEADS, 8, ML_QK_DIM), F32),
            pltpu.VMEM((ML_HEADS, 8, 128), F32),
            pltpu.VMEM((TAIL, 2 * ML_QK_WIDTH), F32),
        ],
        compiler_params=_params(2),
        name="mlstm",
    )(nat, gates, gate_bias, conv_w, conv_b, mh_norm)


OUT_TM = 512


def _outproj_kernel(att_ref, hm_ref, x_ref, wa_ref, wm_ref, g_ref, h_ref, u_ref):
    h = (x_ref[...]
         + lax.dot_general(att_ref[...], wa_ref[...], (((0,), (0,)), ((), ())),
                           preferred_element_type=F32)
         + jnp.dot(hm_ref[...], wm_ref[...], preferred_element_type=F32))
    h_ref[...] = h
    u_ref[...] = _rms(h, g_ref[...]).astype(BF16)


def _outproj(att_t, hm, x2, w_out, g):
    rows = x2.shape[0]
    w_att = w_ml = w_out
    return pl.pallas_call(
        _outproj_kernel,
        grid=(rows // OUT_TM,),
        in_specs=[
            pl.BlockSpec((ATT_WIDTH, OUT_TM), lambda i: (0, i)),
            pl.BlockSpec((OUT_TM, ML_WIDTH), lambda i: (i, 0)),
            pl.BlockSpec((OUT_TM, D_MODEL), lambda i: (i, 0)),
            pl.BlockSpec((ATT_WIDTH, D_MODEL), lambda i: (0, 0)),
            pl.BlockSpec((ML_WIDTH, D_MODEL), lambda i: (ATT_WIDTH // ML_WIDTH, 0)),
            pl.BlockSpec((1, D_MODEL), lambda i: (0, 0)),
        ],
        out_specs=[
            pl.BlockSpec((OUT_TM, D_MODEL), lambda i: (i, 0)),
            pl.BlockSpec((OUT_TM, D_MODEL), lambda i: (i, 0)),
        ],
        out_shape=[
            jax.ShapeDtypeStruct((rows, D_MODEL), F32),
            jax.ShapeDtypeStruct((rows, D_MODEL), BF16),
        ],
        compiler_params=_params(1),
        name="outproj",
    )(att_t, hm, x2, w_att, w_ml, g)


MLP_TM = 1024
MLP_TF = 512


def _mlp_kernel(u_ref, h_ref, wu_ref, wd_ref, g_ref, o_ref):
    j = pl.program_id(1)

    @pl.when(j == 0)
    def _():
        o_ref[...] = h_ref[...]

    a = jnp.dot(u_ref[...], wu_ref[...], preferred_element_type=F32)
    a = jnp.square(jnp.maximum(a, 0.0)).astype(BF16)
    o_ref[...] += jnp.dot(a, wd_ref[...], preferred_element_type=F32)

    @pl.when(j == pl.num_programs(1) - 1)
    def _():
        o_ref[...] = _rms(o_ref[...], g_ref[...])


def _mlp(u, h, w_up, w_down, g):
    rows = u.shape[0]
    return pl.pallas_call(
        _mlp_kernel,
        grid=(rows // MLP_TM, D_FF // MLP_TF),
        in_specs=[
            pl.BlockSpec((MLP_TM, D_MODEL), lambda i, j: (i, 0)),
            pl.BlockSpec((MLP_TM, D_MODEL), lambda i, j: (i, 0)),
            pl.BlockSpec((D_MODEL, MLP_TF), lambda i, j: (0, j)),
            pl.BlockSpec((MLP_TF, D_MODEL), lambda i, j: (j, 0)),
            pl.BlockSpec((1, D_MODEL), lambda i, j: (0, 0)),
        ],
        out_specs=pl.BlockSpec((MLP_TM, D_MODEL), lambda i, j: (i, 0)),
        out_shape=jax.ShapeDtypeStruct((rows, D_MODEL), F32),
        compiler_params=_params(2),
        name="mlp",
    )(u, h, w_up, w_down, g)


def _t5_bucket(dist):
    max_exact = N_BUCKETS // 2
    d = jnp.maximum(dist, 0)
    ratio = jnp.maximum(d, max_exact).astype(F32) / max_exact
    large = max_exact + (jnp.log(ratio) / math.log(MAX_DISTANCE / max_exact)
                         * (N_BUCKETS - max_exact)).astype(jnp.int32)
    large = jnp.minimum(large, N_BUCKETS - 1)
    return jnp.where(d < max_exact, d, large)


BIAS_ROWS = 16


def _bias_kernel(rb_ref, bkt_band_ref, bkt_meta_ref, band_ref, meta_ref):
    def lookup(bkt):
        def body(b, accs):
            hit = bkt == b
            return tuple(jnp.where(hit, rb_ref[b, hd], a) for hd, a in enumerate(accs))
        init = tuple(jnp.zeros(bkt.shape, F32) for _ in range(ATT_HEADS))
        return lax.fori_loop(0, N_BUCKETS, body, init)

    def band_rows(ci, carry):
        r0 = pl.multiple_of(ci * BIAS_ROWS, BIAS_ROWS)
        accs = lookup(bkt_band_ref[pl.ds(r0, BIAS_ROWS), :])
        for hd in range(ATT_HEADS):
            h, g = divmod(hd, ATT_GROUP)
            band_ref[h, pl.ds(r0, BIAS_ROWS), g * BLOCK:(g + 1) * BLOCK] = accs[hd]
        return carry

    lax.fori_loop(0, 2 * BLOCK // BIAS_ROWS, band_rows, 0)

    def meta_block(n, carry):
        accs = lookup(bkt_meta_ref[n])
        for hd in range(ATT_HEADS):
            h, g = divmod(hd, ATT_GROUP)
            meta_ref[n, h, :, g * BLOCK:(g + 1) * BLOCK] = accs[hd]
        return carry

    lax.fori_loop(0, meta_ref.shape[0], meta_block, 0)


def _bias_tables(rel_bias, nblk):
    c = jnp.arange(2 * BLOCK)[:, None]
    r = jnp.arange(BLOCK)[None, :]
    bkt_band = _t5_bucket(r + BLOCK - c).astype(jnp.int32)
    q_pos = BLOCK + jnp.arange(nblk * BLOCK)[None, :]
    m_pos = N_PAD + jnp.arange(N_META)[:, None]
    bkt_meta = _t5_bucket(q_pos - m_pos).astype(jnp.int32)
    bkt_meta = bkt_meta.reshape(N_META, nblk, BLOCK).transpose(1, 0, 2)
    vmem = pl.BlockSpec(memory_space=pltpu.VMEM)
    return pl.pallas_call(
        _bias_kernel,
        in_specs=[pl.BlockSpec(memory_space=pltpu.SMEM), vmem, vmem],
        out_specs=[vmem, vmem],
        out_shape=[
            jax.ShapeDtypeStruct((ATT_KV_HEADS, 2 * BLOCK, GQ), F32),
            jax.ShapeDtypeStruct((nblk, ATT_KV_HEADS, N_META, GQ), F32),
        ],
        name="bias_tables",
    )(rel_bias.astype(F32), bkt_band, bkt_meta)


def kernel(x, meta_tokens, w_in, conv_w, conv_b, b_igate, b_fgate, attn_sinks, rel_bias,
           mh_norm, w_out, norm_mix, norm_mlp, w_up, w_down, norm_final):
    batch, seq, d = x.shape
    nblk = seq // BLOCK
    x2 = x.reshape(batch * seq, d)
    lead = jnp.concatenate([jnp.zeros((N_PAD, d), x.dtype), meta_tokens.astype(x.dtype)], axis=0)

    w = w_in[0]
    o_q, o_k, o_v, o_mq, o_mv, o_mo, o_i = 0, 1024, 1280, 1536, 2560, 3584, 4608
    w_nat = jnp.concatenate(
        [w[:, o_mq:o_i], w[:, o_k:o_v], w[:, o_i:],
         jnp.zeros((d, GATE_LANES - 2 * ML_HEADS), w.dtype)], axis=1).astype(BF16)
    w_qv = jnp.concatenate([w[:, o_q:o_k] * (ATT_HEAD_DIM ** -0.5), w[:, o_v:o_mq]], axis=1).astype(BF16)
    gate_bias = jnp.pad(jnp.concatenate([b_igate[0], b_fgate[0]]).astype(F32),
                        (0, GATE_LANES - 2 * ML_HEADS))[None]
    bias_band_t, bias_meta_t = _bias_tables(rel_bias, nblk)

    nat, proj_t, gates = _inproj(x2, lead, norm_mix[0][None].astype(F32), w_nat, w_qv)
    att_t = _attention(attn_sinks[0].astype(F32), nat, proj_t, bias_band_t, bias_meta_t, batch, nblk)
    hm = _mlstm(nat, gates, gate_bias, conv_w[0].astype(F32), conv_b[0][None].astype(F32),
                mh_norm[0][None].astype(F32), batch, nblk)
    h2, u2 = _outproj(att_t, hm, x2, w_out[0].astype(BF16), norm_mlp[0][None].astype(F32))
    out = _mlp(u2, h2, w_up[0].astype(BF16), w_down[0].astype(BF16), norm_final[None].astype(F32))
    return out.reshape(batch, seq, d)
```

```python
import functools
import math

import jax
import jax.numpy as jnp
from jax import lax
from jax.experimental import pallas as pl
from jax.experimental.pallas import tpu as pltpu

F32 = jnp.float32
BF16 = jnp.bfloat16

D_MODEL = 2048
N_META = 16
BLOCK = 128
N_PAD = BLOCK - N_META
WINDOW = 128
ATT_HEADS = 16
ATT_KV_HEADS = 4
ATT_GROUP = ATT_HEADS // ATT_KV_HEADS
ATT_HEAD_DIM = 64
ATT_WIDTH = ATT_HEADS * ATT_HEAD_DIM
ATT_KV_WIDTH = ATT_KV_HEADS * ATT_HEAD_DIM
ML_HEADS = 4
ML_V_DIM = 256
ML_QK_DIM = 128
ML_WIDTH = ML_HEADS * ML_V_DIM
ML_QK_WIDTH = ML_HEADS * ML_QK_DIM
CONV_WIDTH = 4
GATE_SOFTCAP = 15.0
D_FF = 4 * D_MODEL
N_BUCKETS = 32
MAX_DISTANCE = 128
EPS = 1e-6
NEG = -1e30
LOG2E = math.log2(math.e)

GATE_LANES = 128
NAT_MQK = 0
NAT_MV = 2 * ML_QK_WIDTH
NAT_MO = NAT_MV + ML_WIDTH
NAT_K = NAT_MO + ML_WIDTH
NAT_COLS = NAT_K + ATT_KV_WIDTH + GATE_LANES
T_ROWS = ATT_WIDTH + ATT_KV_WIDTH

V7X_VMEM_LIMIT = 60 * 1024 * 1024


def _params(n_axes, vmem=V7X_VMEM_LIMIT):
    return pltpu.CompilerParams(dimension_semantics=("arbitrary",) * n_axes,
                                vmem_limit_bytes=vmem)


IN_TM = 1024
IN_NAT_STEPS = 3
IN_TN = NAT_COLS // IN_NAT_STEPS
MXU_COLS = 256


def _rms(x, g):
    return x * lax.rsqrt(jnp.mean(x * x, axis=-1, keepdims=True) + EPS) * g


def _inproj_kernel(x_ref, g_ref, wn_ref, wq_ref, *rest):
    nat_ref, t_ref, gate_ref, u_ref = rest[-4:]
    j = pl.program_id(1)

    @pl.when(j == 0)
    def _():
        u_ref[...] = _rms(x_ref[...], g_ref[...]).astype(BF16)

    @pl.when(j < IN_NAT_STEPS)
    def _():
        r = jnp.dot(u_ref[...], wn_ref[...], preferred_element_type=F32)
        nat_ref[...] = r.astype(BF16)

        @pl.when(j == IN_NAT_STEPS - 1)
        def _():
            gate_ref[...] = r[:, IN_TN - GATE_LANES:]

    @pl.when(j == IN_NAT_STEPS)
    def _():
        for c in range(T_ROWS // MXU_COLS):
            cs = slice(c * MXU_COLS, (c + 1) * MXU_COLS)
            r = jnp.dot(u_ref[...], wq_ref[:, cs], preferred_element_type=F32)
            t_ref[cs, :] = r.T.astype(BF16)


def _inproj_call(x_rows, g, w_nat, w_qv, tm, row_off, total_rows, prev=None):
    n_tiles = x_rows.shape[0] // tm
    off = row_off // tm
    last_nat = IN_NAT_STEPS - 1
    in_specs = [
        pl.BlockSpec((tm, D_MODEL), lambda i, j: (i, 0)),
        pl.BlockSpec((1, D_MODEL), lambda i, j: (0, 0)),
        pl.BlockSpec((D_MODEL, IN_TN), lambda i, j: (0, jnp.minimum(j, last_nat))),
        pl.BlockSpec((D_MODEL, T_ROWS), lambda i, j: (0, 0)),
    ]
    args = [x_rows, g, w_nat, w_qv]
    aliases = {}
    if prev is not None:
        in_specs += [pl.BlockSpec(memory_space=pl.ANY)] * 3
        aliases = {len(args) + k: k for k in range(3)}
        args += list(prev)
    return pl.pallas_call(
        _inproj_kernel,
        grid=(n_tiles, IN_NAT_STEPS + 1),
        in_specs=in_specs,
        out_specs=[
            pl.BlockSpec((tm, IN_TN), lambda i, j: (off + i, jnp.minimum(j, last_nat))),
            pl.BlockSpec((T_ROWS, tm), lambda i, j: (0, off + i)),
            pl.BlockSpec((tm, GATE_LANES), lambda i, j: (off + i, 0)),
        ],
        out_shape=[
            jax.ShapeDtypeStruct((total_rows, NAT_COLS), BF16),
            jax.ShapeDtypeStruct((T_ROWS, total_rows), BF16),
            jax.ShapeDtypeStruct((total_rows, GATE_LANES), F32),
        ],
        scratch_shapes=[pltpu.VMEM((tm, D_MODEL), BF16)],
        input_output_aliases=aliases,
        compiler_params=_params(2),
        name="inproj_lead" if prev is not None else "inproj",
    )(*args)


def _inproj(x2, lead, g, w_nat, w_qv):
    n_real = x2.shape[0]
    total = n_real + BLOCK
    outs = _inproj_call(x2, g, w_nat, w_qv, IN_TM, 0, total)
    return _inproj_call(lead, g, w_nat, w_qv, BLOCK, n_real, total, prev=outs)


N_KEYS = 2 * BLOCK + N_META
GQ = ATT_GROUP * BLOCK


def _attn_kernel(sink_ref, qt_ref, kc_ref, kp_ref, km_ref, vc_ref, vp_ref, vl_ref,
                 bb_ref, bm_ref, wo_ref, o_ref, wo_bf_ref):
    n = pl.program_id(1)
    wo_bf_ref[...] = wo_ref[...].astype(BF16)
    c = lax.broadcasted_iota(jnp.int32, (2 * BLOCK, BLOCK), 0)
    r = lax.broadcasted_iota(jnp.int32, (2 * BLOCK, BLOCK), 1)
    dist = r + BLOCK - c
    band_ok = (dist >= 0) & (dist < WINDOW) & ((c >= BLOCK) | (n >= 1))
    mm = lax.broadcasted_iota(jnp.int32, (N_META, BLOCK), 0)
    rr = lax.broadcasted_iota(jnp.int32, (N_META, BLOCK), 1)
    meta_ok = (n + 1) * BLOCK + rr >= N_PAD + mm

    kall = jnp.concatenate([kp_ref[...], kc_ref[...], km_ref[...]], axis=0)
    zpad = jnp.zeros((N_PAD, BLOCK), BF16)
    for h in range(ATT_KV_HEADS):
        lo = (h // 2) * 128
        kpair = kall[:, lo:lo + 128]
        qh = jnp.concatenate(
            [qt_ref[(ATT_GROUP * h + g) * ATT_HEAD_DIM:(ATT_GROUP * h + g + 1) * ATT_HEAD_DIM, :]
             for g in range(ATT_GROUP)], axis=1)
        zq = jnp.zeros_like(qh)
        qz = jnp.concatenate([qh, zq] if h % 2 == 0 else [zq, qh], axis=0)
        st = jnp.dot(kpair, qz, preferred_element_type=F32)
        p_band, p_lead, inv = [], [], []
        for g in range(ATT_GROUP):
            sl = st[:, g * BLOCK:(g + 1) * BLOCK]
            sink = jnp.full((1, BLOCK), sink_ref[ATT_GROUP * h + g], F32) * LOG2E
            lb = jnp.where(band_ok, sl[:2 * BLOCK] + bb_ref[h, :, g * BLOCK:(g + 1) * BLOCK], NEG * LOG2E)
            lm = jnp.where(meta_ok, sl[2 * BLOCK:] + bm_ref[0, h, :, g * BLOCK:(g + 1) * BLOCK], NEG * LOG2E)
            m = jnp.maximum(jnp.max(lb, axis=0, keepdims=True), jnp.max(lm, axis=0, keepdims=True))
            m = jnp.maximum(m, sink)
            pb = jnp.exp2(lb - m)
            pm = jnp.exp2(lm - m)
            den = (jnp.sum(pb, axis=0, keepdims=True) + jnp.sum(pm, axis=0, keepdims=True)
                   + jnp.exp2(sink - m))
            inv.append(1.0 / den)
            p_band.append(pb.astype(BF16))
            p_lead.append(jnp.concatenate([zpad, pm.astype(BF16)], axis=0))
        pbt = jnp.concatenate(p_band, axis=1)
        plt = jnp.concatenate(p_lead, axis=1)
        hs = slice(h * ATT_HEAD_DIM, (h + 1) * ATT_HEAD_DIM)
        vband = jnp.concatenate([vp_ref[hs, :], vc_ref[hs, :]], axis=1)
        ot = (jnp.dot(vband, pbt, preferred_element_type=F32)
              + jnp.dot(vl_ref[hs, :], plt, preferred_element_type=F32))
        for g in range(ATT_GROUP):
            row = (ATT_GROUP * h + g) * ATT_HEAD_DIM
            o_ref[row:row + ATT_HEAD_DIM, :] = (
                ot[:, g * BLOCK:(g + 1) * BLOCK] * inv[g]).astype(BF16)


def _attention(sinks, nat, proj_t, bias_band_t, bias_meta_t, w_out, batch, nblk):
    lead_blk = batch * nblk
    wo_rows = w_out.shape[0] // (batch * nblk)
    cur = lambda b, n: b * nblk + n
    prev = lambda b, n: b * nblk + jnp.maximum(n - 1, 0)
    k_col = NAT_K // ATT_KV_WIDTH
    v_row = ATT_WIDTH // ATT_KV_WIDTH
    return pl.pallas_call(
        _attn_kernel,
        grid=(batch, nblk),
        in_specs=[
            pl.BlockSpec(memory_space=pltpu.SMEM),
            pl.BlockSpec((ATT_WIDTH, BLOCK), lambda b, n: (0, cur(b, n))),
            pl.BlockSpec((BLOCK, ATT_KV_WIDTH), lambda b, n: (cur(b, n), k_col)),
            pl.BlockSpec((BLOCK, ATT_KV_WIDTH), lambda b, n: (prev(b, n), k_col)),
            pl.BlockSpec((N_META, ATT_KV_WIDTH),
                         lambda b, n: (lead_blk * (BLOCK // N_META) + N_PAD // N_META, k_col)),
            pl.BlockSpec((ATT_KV_WIDTH, BLOCK), lambda b, n: (v_row, cur(b, n))),
            pl.BlockSpec((ATT_KV_WIDTH, BLOCK), lambda b, n: (v_row, prev(b, n))),
            pl.BlockSpec((ATT_KV_WIDTH, BLOCK), lambda b, n: (v_row, lead_blk)),
            pl.BlockSpec((ATT_KV_HEADS, 2 * BLOCK, GQ), lambda b, n: (0, 0, 0)),
            pl.BlockSpec((1, ATT_KV_HEADS, N_META, GQ), lambda b, n: (n, 0, 0, 0)),
            pl.BlockSpec((wo_rows, D_MODEL), lambda b, n: (cur(b, n), 0)),
        ],
        out_specs=[
            pl.BlockSpec((ATT_WIDTH, BLOCK), lambda b, n: (0, cur(b, n))),
            pl.BlockSpec((wo_rows, D_MODEL), lambda b, n: (cur(b, n), 0)),
        ],
        out_shape=[
            jax.ShapeDtypeStruct((ATT_WIDTH, batch * nblk * BLOCK), BF16),
            jax.ShapeDtypeStruct(w_out.shape, BF16),
        ],
        compiler_params=_params(2),
        name="attention",
    )(sinks, proj_t, nat, nat, nat, proj_t, proj_t, proj_t, bias_band_t, bias_meta_t, w_out)


TAIL = 8


def _mlstm_kernel(p_ref, gate_ref, gb_ref, cw_ref, cb_ref, mhn_ref, wu_ref, wd_ref,
                  o_ref, wu_bf_ref, wd_bf_ref, c_ref, n_ref, m_ref, tail_ref):
    n = pl.program_id(1)
    wu_bf_ref[...] = wu_ref[...].astype(BF16)
    wd_bf_ref[...] = wd_ref[...].astype(BF16)

    @pl.when(n == 0)
    def _():
        c_ref[...] = jnp.zeros_like(c_ref)
        n_ref[...] = jnp.zeros_like(n_ref)
        m_ref[...] = jnp.zeros_like(m_ref)
        tail_ref[...] = jnp.zeros_like(tail_ref)

    t_idx = lax.broadcasted_iota(jnp.int32, (BLOCK, GATE_LANES), 0)
    valid = (n > 0) | (t_idx >= N_PAD)
    pre = GATE_SOFTCAP * jnp.tanh((gate_ref[...] + gb_ref[...]) / GATE_SOFTCAP)
    log_i = jnp.where(valid, pre, NEG)
    log_sig = jnp.minimum(pre, 0.0) - jnp.log1p(jnp.exp(-jnp.abs(pre)))
    log_f = jnp.where(valid, log_sig, 0.0)
    row = lax.broadcasted_iota(jnp.int32, (BLOCK, BLOCK), 0)
    col = lax.broadcasted_iota(jnp.int32, (BLOCK, BLOCK), 1)
    causal = col <= row
    tril = jnp.where(causal, 1.0, 0.0).astype(F32)
    b_all = jnp.dot(tril, log_f, preferred_element_type=F32,
                    precision=lax.Precision.HIGHEST)
    log_i_t = log_i.T
    b_all_t = b_all.T

    xqk = p_ref[:, NAT_MQK:NAT_MV].astype(F32)
    tail = tail_ref[...]
    row8 = lax.broadcasted_iota(jnp.int32, (TAIL, 2 * ML_QK_WIDTH), 0)
    y = cb_ref[...] + cw_ref[CONV_WIDTH - 1:CONV_WIDTH, :] * xqk
    for k in range(1, CONV_WIDTH):
        rolled = pltpu.roll(xqk, k, 0)
        head = jnp.where(row8 < k, pltpu.roll(tail, k, 0), rolled[0:TAIL])
        shifted = jnp.concatenate([head, rolled[TAIL:]], axis=0)
        y = y + cw_ref[CONV_WIDTH - 1 - k:CONV_WIDTH - k, :] * shifted
    tail_ref[...] = xqk[BLOCK - TAIL:, :]
    act = y * jax.nn.sigmoid(y)
    q_all = act[:, :ML_QK_WIDTH] * (ML_QK_DIM ** -0.5)
    k_all = act[:, ML_QK_WIDTH:]

    for h in range(ML_HEADS):
        q_f = q_all[:, h * ML_QK_DIM:(h + 1) * ML_QK_DIM]
        k_f = k_all[:, h * ML_QK_DIM:(h + 1) * ML_QK_DIM]
        q_h = q_f.astype(BF16)
        k_h = k_f.astype(BF16)
        v_h = p_ref[:, NAT_MV + h * ML_V_DIM:NAT_MV + (h + 1) * ML_V_DIM]
        b_col = b_all[:, ML_HEADS + h:ML_HEADS + h + 1]
        li_col = log_i[:, h:h + 1]
        b_row = b_all_t[ML_HEADS + h:ML_HEADS + h + 1, :]
        li_row = log_i_t[h:h + 1, :]
        b_last = b_all[BLOCK - 1:BLOCK, ML_HEADS + h:ML_HEADS + h + 1]
        c_prev = c_ref[h]
        n_prev = n_ref[h][0:1, :]
        m_prev = m_ref[h][0:1, 0:1]

        dmat = jnp.where(causal, b_col - b_row + li_row, -jnp.inf)
        inter = b_col + m_prev
        m_t = jnp.maximum(inter, jnp.max(dmat, axis=1, keepdims=True))
        s = lax.dot_general(q_h, k_h, (((1,), (1,)), ((), ())),
                            preferred_element_type=F32) * jnp.exp(dmat - m_t)
        a_t = jnp.exp(inter - m_t)
        num = (jnp.dot(s.astype(BF16), v_h, preferred_element_type=F32)
               + a_t * jnp.dot(q_h, c_prev.astype(BF16), preferred_element_type=F32))
        den = (jnp.sum(s, axis=1, keepdims=True)
               + a_t * jnp.sum(q_f * n_prev, axis=1, keepdims=True))
        den = jnp.maximum(jnp.abs(den), jnp.exp(-m_t))
        hh = num / den
        hh = hh * lax.rsqrt(jnp.mean(hh * hh, axis=-1, keepdims=True) + EPS)
        vs = slice(h * ML_V_DIM, (h + 1) * ML_V_DIM)
        hh = hh * mhn_ref[:, vs]
        m_o = p_ref[:, NAT_MO + h * ML_V_DIM:NAT_MO + (h + 1) * ML_V_DIM].astype(F32)
        o_ref[:, vs] = (hh * jax.nn.sigmoid(m_o)).astype(BF16)

        g_col = b_last - b_col + li_col
        m_loc = jnp.max(g_col, axis=0, keepdims=True)
        wk = jnp.exp(g_col - m_loc) * k_f
        c_loc = lax.dot_general(wk.astype(BF16), v_h, (((0,), (0,)), ((), ())),
                                preferred_element_type=F32)
        n_loc = jnp.sum(wk, axis=0, keepdims=True)
        m_new = jnp.maximum(b_last + m_prev, m_loc)
        a = jnp.exp(b_last + m_prev - m_new)
        cc = jnp.exp(m_loc - m_new)
        c_ref[h] = a * c_prev + cc * c_loc
        n_ref[h] = jnp.broadcast_to(a * n_prev + cc * n_loc, (8, ML_QK_DIM))
        m_ref[h] = jnp.broadcast_to(m_new, (8, 128))


def _mlstm(nat, gates, gate_bias, conv_w, conv_b, mh_norm, w_up, w_down, batch, nblk):
    lead_blk = batch * nblk
    src = lambda b, n: jnp.where(n == 0, lead_blk, b * nblk + n - 1)
    n_slices = batch * nblk
    ff_slice = D_FF // n_slices
    wsl = lambda b, n: jnp.minimum(b * (nblk + 1) + n, n_slices - 1)
    return pl.pallas_call(
        _mlstm_kernel,
        grid=(batch, nblk + 1),
        in_specs=[
            pl.BlockSpec((BLOCK, NAT_K), lambda b, n: (src(b, n), 0)),
            pl.BlockSpec((BLOCK, GATE_LANES), lambda b, n: (src(b, n), 0)),
            pl.BlockSpec((1, GATE_LANES), lambda b, n: (0, 0)),
            pl.BlockSpec((CONV_WIDTH, 2 * ML_QK_WIDTH), lambda b, n: (0, 0)),
            pl.BlockSpec((1, 2 * ML_QK_WIDTH), lambda b, n: (0, 0)),
            pl.BlockSpec((1, ML_WIDTH), lambda b, n: (0, 0)),
            pl.BlockSpec((D_MODEL, ff_slice), lambda b, n: (0, wsl(b, n))),
            pl.BlockSpec((ff_slice, D_MODEL), lambda b, n: (wsl(b, n), 0)),
        ],
        out_specs=[
            pl.BlockSpec((BLOCK, ML_WIDTH), lambda b, n: (b * nblk + jnp.maximum(n - 1, 0), 0)),
            pl.BlockSpec((D_MODEL, ff_slice), lambda b, n: (0, wsl(b, n))),
            pl.BlockSpec((ff_slice, D_MODEL), lambda b, n: (wsl(b, n), 0)),
        ],
        out_shape=[
            jax.ShapeDtypeStruct((batch * nblk * BLOCK, ML_WIDTH), BF16),
            jax.ShapeDtypeStruct(w_up.shape, BF16),
            jax.ShapeDtypeStruct(w_down.shape, BF16),
        ],
        scratch_shapes=[
            pltpu.VMEM((ML_HEADS, ML_QK_DIM, ML_V_DIM), F32),
            pltpu.VMEM((ML_HEADS, 8, ML_QK_DIM), F32),
            pltpu.VMEM((ML_HEADS, 8, 128), F32),
            pltpu.VMEM((TAIL, 2 * ML_QK_WIDTH), F32),
        ],
        compiler_params=_params(2),
        name="mlstm",
    )(nat, gates, gate_bias, conv_w, conv_b, mh_norm, w_up, w_down)


OUT_TM = 512


def _outproj_kernel(att_ref, hm_ref, x_ref, wa_ref, wm_ref, g_ref, h_ref, u_ref):
    h = (x_ref[...]
         + lax.dot_general(att_ref[...], wa_ref[...], (((0,), (0,)), ((), ())),
                           preferred_element_type=F32)
         + jnp.dot(hm_ref[...], wm_ref[...], preferred_element_type=F32))
    h_ref[...] = h
    u_ref[...] = _rms(h, g_ref[...]).astype(BF16)


def _outproj(att_t, hm, x2, w_out, g):
    rows = x2.shape[0]
    w_att = w_ml = w_out
    return pl.pallas_call(
        _outproj_kernel,
        grid=(rows // OUT_TM,),
        in_specs=[
            pl.BlockSpec((ATT_WIDTH, OUT_TM), lambda i: (0, i)),
            pl.BlockSpec((OUT_TM, ML_WIDTH), lambda i: (i, 0)),
            pl.BlockSpec((OUT_TM, D_MODEL), lambda i: (i, 0)),
            pl.BlockSpec((ATT_WIDTH, D_MODEL), lambda i: (0, 0)),
            pl.BlockSpec((ML_WIDTH, D_MODEL), lambda i: (ATT_WIDTH // ML_WIDTH, 0)),
            pl.BlockSpec((1, D_MODEL), lambda i: (0, 0)),
        ],
        out_specs=[
            pl.BlockSpec((OUT_TM, D_MODEL), lambda i: (i, 0)),
            pl.BlockSpec((OUT_TM, D_MODEL), lambda i: (i, 0)),
        ],
        out_shape=[
            jax.ShapeDtypeStruct((rows, D_MODEL), F32),
            jax.ShapeDtypeStruct((rows, D_MODEL), BF16),
        ],
        compiler_params=_params(1),
        name="outproj",
    )(att_t, hm, x2, w_att, w_ml, g)


MLP_TM = 1024
MLP_TF = 512


def _mlp_kernel(u_ref, h_ref, wu_ref, wd_ref, g_ref, o_ref):
    j = pl.program_id(1)

    @pl.when(j == 0)
    def _():
        o_ref[...] = h_ref[...]

    a = jnp.dot(u_ref[...], wu_ref[...], preferred_element_type=F32)
    a = jnp.square(jnp.maximum(a, 0.0)).astype(BF16)
    o_ref[...] += jnp.dot(a, wd_ref[...], preferred_element_type=F32)

    @pl.when(j == pl.num_programs(1) - 1)
    def _():
        o_ref[...] = _rms(o_ref[...], g_ref[...])


def _mlp(u, h, w_up, w_down, g):
    rows = u.shape[0]
    return pl.pallas_call(
        _mlp_kernel,
        grid=(rows // MLP_TM, D_FF // MLP_TF),
        in_specs=[
            pl.BlockSpec((MLP_TM, D_MODEL), lambda i, j: (i, 0)),
            pl.BlockSpec((MLP_TM, D_MODEL), lambda i, j: (i, 0)),
            pl.BlockSpec((D_MODEL, MLP_TF), lambda i, j: (0, j)),
            pl.BlockSpec((MLP_TF, D_MODEL), lambda i, j: (j, 0)),
            pl.BlockSpec((1, D_MODEL), lambda i, j: (0, 0)),
        ],
        out_specs=pl.BlockSpec((MLP_TM, D_MODEL), lambda i, j: (i, 0)),
        out_shape=jax.ShapeDtypeStruct((rows, D_MODEL), F32),
        compiler_params=_params(2),
        name="mlp",
    )(u, h, w_up, w_down, g)


def _t5_bucket(dist):
    max_exact = N_BUCKETS // 2
    d = jnp.maximum(dist, 0)
    ratio = jnp.maximum(d, max_exact).astype(F32) / max_exact
    large = max_exact + (jnp.log(ratio) / math.log(MAX_DISTANCE / max_exact)
                         * (N_BUCKETS - max_exact)).astype(jnp.int32)
    large = jnp.minimum(large, N_BUCKETS - 1)
    return jnp.where(d < max_exact, d, large)


BIAS_ROWS = 16


def _bias_kernel(rb_ref, bkt_band_ref, bkt_meta_ref, band_ref, meta_ref):
    def lookup(bkt):
        def body(b, accs):
            hit = bkt == b
            return tuple(jnp.where(hit, rb_ref[b, hd], a) for hd, a in enumerate(accs))
        init = tuple(jnp.zeros(bkt.shape, F32) for _ in range(ATT_HEADS))
        return lax.fori_loop(0, N_BUCKETS, body, init)

    def band_rows(ci, carry):
        r0 = pl.multiple_of(ci * BIAS_ROWS, BIAS_ROWS)
        accs = lookup(bkt_band_ref[pl.ds(r0, BIAS_ROWS), :])
        for hd in range(ATT_HEADS):
            h, g = divmod(hd, ATT_GROUP)
            band_ref[h, pl.ds(r0, BIAS_ROWS), g * BLOCK:(g + 1) * BLOCK] = accs[hd] * LOG2E
        return carry

    lax.fori_loop(0, 2 * BLOCK // BIAS_ROWS, band_rows, 0)

    def meta_block(n, carry):
        accs = lookup(bkt_meta_ref[n])
        for hd in range(ATT_HEADS):
            h, g = divmod(hd, ATT_GROUP)
            meta_ref[n, h, :, g * BLOCK:(g + 1) * BLOCK] = accs[hd] * LOG2E
        return carry

    lax.fori_loop(0, meta_ref.shape[0], meta_block, 0)


def _bias_tables(rel_bias, nblk):
    c = jnp.arange(2 * BLOCK)[:, None]
    r = jnp.arange(BLOCK)[None, :]
    bkt_band = _t5_bucket(r + BLOCK - c).astype(jnp.int32)
    q_pos = BLOCK + jnp.arange(nblk * BLOCK)[None, :]
    m_pos = N_PAD + jnp.arange(N_META)[:, None]
    bkt_meta = _t5_bucket(q_pos - m_pos).astype(jnp.int32)
    bkt_meta = bkt_meta.reshape(N_META, nblk, BLOCK).transpose(1, 0, 2)
    vmem = pl.BlockSpec(memory_space=pltpu.VMEM)
    return pl.pallas_call(
        _bias_kernel,
        in_specs=[pl.BlockSpec(memory_space=pltpu.SMEM), vmem, vmem],
        out_specs=[vmem, vmem],
        out_shape=[
            jax.ShapeDtypeStruct((ATT_KV_HEADS, 2 * BLOCK, GQ), F32),
            jax.ShapeDtypeStruct((nblk, ATT_KV_HEADS, N_META, GQ), F32),
        ],
        name="bias_tables",
    )(rel_bias.astype(F32), bkt_band, bkt_meta)


LANE = 128
W_IN_OFFS = dict(q=0, k=1024, v=1280, mq=1536, mk=2048, mv=2560, mo=3584, gates=4608)
NAT_SRC = ([W_IN_OFFS["mq"] // LANE + t for t in range((W_IN_OFFS["gates"] - W_IN_OFFS["mq"]) // LANE)]
           + [W_IN_OFFS["k"] // LANE + t for t in range(ATT_KV_WIDTH // LANE)]
           + [W_IN_OFFS["gates"] // LANE])
QV_SRC = ([W_IN_OFFS["q"] // LANE + t for t in range(ATT_WIDTH // LANE)]
          + [W_IN_OFFS["v"] // LANE + t for t in range(ATT_KV_WIDTH // LANE)])
N_NAT_BLK = len(NAT_SRC)
N_Q_BLK = ATT_WIDTH // LANE


def _win_prep_kernel(src_ref, w_ref, nat_ref, qv_ref):
    j = pl.program_id(0)
    wv = w_ref[0]

    @pl.when(j < N_NAT_BLK - 1)
    def _():
        nat_ref[...] = wv.astype(BF16)

    @pl.when(j == N_NAT_BLK - 1)
    def _():
        lane = lax.broadcasted_iota(jnp.int32, wv.shape, 1)
        nat_ref[...] = jnp.where(lane < 2 * ML_HEADS, wv, 0.0).astype(BF16)

    @pl.when((j >= N_NAT_BLK) & (j < N_NAT_BLK + N_Q_BLK))
    def _():
        qv_ref[...] = (wv * (ATT_HEAD_DIM ** -0.5 * LOG2E)).astype(BF16)

    @pl.when(j >= N_NAT_BLK + N_Q_BLK)
    def _():
        qv_ref[...] = wv.astype(BF16)


def _win_prep(w_in):
    src = jnp.asarray(NAT_SRC + QV_SRC, jnp.int32)
    d = w_in.shape[1]
    return pl.pallas_call(
        _win_prep_kernel,
        grid_spec=pltpu.PrefetchScalarGridSpec(
            num_scalar_prefetch=1,
            grid=(len(NAT_SRC) + len(QV_SRC),),
            in_specs=[pl.BlockSpec((1, d, LANE), lambda j, src: (0, 0, src[j]))],
            out_specs=[
                pl.BlockSpec((d, LANE), lambda j, src: (0, jnp.minimum(j, N_NAT_BLK - 1))),
                pl.BlockSpec((d, LANE), lambda j, src: (0, jnp.maximum(j - N_NAT_BLK, 0))),
            ],
        ),
        out_shape=[
            jax.ShapeDtypeStruct((d, NAT_COLS), BF16),
            jax.ShapeDtypeStruct((d, T_ROWS), BF16),
        ],
        compiler_params=_params(1),
        name="w_in_prep",
    )(src, w_in)


def kernel(x, meta_tokens, w_in, conv_w, conv_b, b_igate, b_fgate, attn_sinks, rel_bias,
           mh_norm, w_out, norm_mix, norm_mlp, w_up, w_down, norm_final):
    batch, seq, d = x.shape
    nblk = seq // BLOCK
    x2 = x.reshape(batch * seq, d)
    lead = jnp.concatenate([jnp.zeros((N_PAD, d), x.dtype), meta_tokens.astype(x.dtype)], axis=0)

    w_nat, w_qv = _win_prep(w_in)
    gate_bias = jnp.pad(jnp.concatenate([b_igate[0], b_fgate[0]]).astype(F32),
                        (0, GATE_LANES - 2 * ML_HEADS))[None]
    bias_band_t, bias_meta_t = _bias_tables(rel_bias, nblk)

    nat, proj_t, gates = _inproj(x2, lead, norm_mix[0][None].astype(F32), w_nat, w_qv)
    att_t, w_out_bf = _attention(attn_sinks[0].astype(F32), nat, proj_t, bias_band_t, bias_meta_t,
                                 w_out[0], batch, nblk)
    hm, w_up_bf, w_down_bf = _mlstm(nat, gates, gate_bias, conv_w[0].astype(F32),
                                    conv_b[0][None].astype(F32), mh_norm[0][None].astype(F32),
                                    w_up[0], w_down[0], batch, nblk)
    h2, u2 = _outproj(att_t, hm, x2, w_out_bf, norm_mlp[0][None].astype(F32))
    out = _mlp(u2, h2, w_up_bf, w_down_bf, norm_final[None].astype(F32))
    return out.reshape(batch, seq, d)
```

```python
import functools
import math

import jax
import jax.numpy as jnp
from jax import lax
from jax.experimental import pallas as pl
from jax.experimental.pallas import tpu as pltpu

F32 = jnp.float32
BF16 = jnp.bfloat16

D_MODEL = 2048
N_META = 16
BLOCK = 128
N_PAD = BLOCK - N_META
WINDOW = 128
ATT_HEADS = 16
ATT_KV_HEADS = 4
ATT_GROUP = ATT_HEADS // ATT_KV_HEADS
ATT_HEAD_DIM = 64
ATT_WIDTH = ATT_HEADS * ATT_HEAD_DIM
ATT_KV_WIDTH = ATT_KV_HEADS * ATT_HEAD_DIM
ML_HEADS = 4
ML_V_DIM = 256
ML_QK_DIM = 128
ML_WIDTH = ML_HEADS * ML_V_DIM
ML_QK_WIDTH = ML_HEADS * ML_QK_DIM
CONV_WIDTH = 4
GATE_SOFTCAP = 15.0
D_FF = 4 * D_MODEL
N_BUCKETS = 32
MAX_DISTANCE = 128
EPS = 1e-6
NEG = -1e30
LOG2E = math.log2(math.e)

GATE_LANES = 128
NAT_MQK = 0
NAT_MV = 2 * ML_QK_WIDTH
NAT_MO = NAT_MV + ML_WIDTH
NAT_K = NAT_MO + ML_WIDTH
NAT_COLS = NAT_K + ATT_KV_WIDTH + GATE_LANES
T_ROWS = ATT_WIDTH + ATT_KV_WIDTH

V7X_VMEM_LIMIT = 60 * 1024 * 1024


def _params(n_axes, vmem=V7X_VMEM_LIMIT):
    return pltpu.CompilerParams(dimension_semantics=("arbitrary",) * n_axes,
                                vmem_limit_bytes=vmem)


IN_TM = 1024
IN_NAT_STEPS = 3
IN_TN = NAT_COLS // IN_NAT_STEPS
MXU_COLS = 256


def _rms(x, g):
    return x * lax.rsqrt(jnp.mean(x * x, axis=-1, keepdims=True) + EPS) * g


def _inproj_kernel(x_ref, g_ref, wn_ref, wq_ref, *rest):
    nat_ref, t_ref, gate_ref, u_ref = rest[-4:]
    j = pl.program_id(1)

    @pl.when(j == 0)
    def _():
        u_ref[...] = _rms(x_ref[...], g_ref[...]).astype(BF16)

    contract_last = (((1,), (1,)), ((), ()))

    @pl.when(j < IN_NAT_STEPS)
    def _():
        r = lax.dot_general(u_ref[...], wn_ref[...], contract_last, preferred_element_type=F32)
        nat_ref[...] = r.astype(BF16)

        @pl.when(j == IN_NAT_STEPS - 1)
        def _():
            gate_ref[...] = r[:, IN_TN - GATE_LANES:]

    @pl.when(j == IN_NAT_STEPS)
    def _():
        t_ref[...] = lax.dot_general(wq_ref[...], u_ref[...], contract_last,
                                     preferred_element_type=F32).astype(BF16)


def _inproj_call(x_rows, g, w_nat, w_qv, tm, row_off, total_rows, prev=None):
    n_tiles = x_rows.shape[0] // tm
    off = row_off // tm
    last_nat = IN_NAT_STEPS - 1
    in_specs = [
        pl.BlockSpec((tm, D_MODEL), lambda i, j: (i, 0)),
        pl.BlockSpec((1, D_MODEL), lambda i, j: (0, 0)),
        pl.BlockSpec((IN_TN, D_MODEL), lambda i, j: (jnp.minimum(j, last_nat), 0)),
        pl.BlockSpec((T_ROWS, D_MODEL), lambda i, j: (0, 0)),
    ]
    args = [x_rows, g, w_nat, w_qv]
    aliases = {}
    if prev is not None:
        in_specs += [pl.BlockSpec(memory_space=pl.ANY)] * 3
        aliases = {len(args) + k: k for k in range(3)}
        args += list(prev)
    return pl.pallas_call(
        _inproj_kernel,
        grid=(n_tiles, IN_NAT_STEPS + 1),
        in_specs=in_specs,
        out_specs=[
            pl.BlockSpec((tm, IN_TN), lambda i, j: (off + i, jnp.minimum(j, last_nat))),
            pl.BlockSpec((T_ROWS, tm), lambda i, j: (0, off + i)),
            pl.BlockSpec((tm, GATE_LANES), lambda i, j: (off + i, 0)),
        ],
        out_shape=[
            jax.ShapeDtypeStruct((total_rows, NAT_COLS), BF16),
            jax.ShapeDtypeStruct((T_ROWS, total_rows), BF16),
            jax.ShapeDtypeStruct((total_rows, GATE_LANES), F32),
        ],
        scratch_shapes=[pltpu.VMEM((tm, D_MODEL), BF16)],
        input_output_aliases=aliases,
        compiler_params=_params(2),
        name="inproj_lead" if prev is not None else "inproj",
    )(*args)


def _inproj(x2, lead, g, w_nat, w_qv):
    n_real = x2.shape[0]
    total = n_real + BLOCK
    outs = _inproj_call(x2, g, w_nat, w_qv, IN_TM, 0, total)
    return _inproj_call(lead, g, w_nat, w_qv, BLOCK, n_real, total, prev=outs)


N_KEYS = 2 * BLOCK + N_META
GQ = ATT_GROUP * BLOCK


def _attn_kernel(sink_ref, qt_ref, kc_ref, kp_ref, km_ref, vc_ref, vp_ref, vl_ref,
                 bb_ref, bm_ref, wo_ref, o_ref, wo_bf_ref):
    n = pl.program_id(1)
    wo_bf_ref[...] = wo_ref[...].astype(BF16)
    c = lax.broadcasted_iota(jnp.int32, (2 * BLOCK, BLOCK), 0)
    r = lax.broadcasted_iota(jnp.int32, (2 * BLOCK, BLOCK), 1)
    dist = r + BLOCK - c
    band_ok = (dist >= 0) & (dist < WINDOW) & ((c >= BLOCK) | (n >= 1))
    mm = lax.broadcasted_iota(jnp.int32, (N_META, BLOCK), 0)
    rr = lax.broadcasted_iota(jnp.int32, (N_META, BLOCK), 1)
    meta_ok = (n + 1) * BLOCK + rr >= N_PAD + mm

    kall = jnp.concatenate([kp_ref[...], kc_ref[...], km_ref[...]], axis=0)
    zpad = jnp.zeros((N_PAD, BLOCK), BF16)
    for h in range(ATT_KV_HEADS):
        lo = (h // 2) * 128
        kpair = kall[:, lo:lo + 128]
        qh = jnp.concatenate(
            [qt_ref[(ATT_GROUP * h + g) * ATT_HEAD_DIM:(ATT_GROUP * h + g + 1) * ATT_HEAD_DIM, :]
             for g in range(ATT_GROUP)], axis=1)
        zq = jnp.zeros_like(qh)
        qz = jnp.concatenate([qh, zq] if h % 2 == 0 else [zq, qh], axis=0)
        st = jnp.dot(kpair, qz, preferred_element_type=F32)
        p_band, p_lead, inv = [], [], []
        for g in range(ATT_GROUP):
            sl = st[:, g * BLOCK:(g + 1) * BLOCK]
            sink = jnp.full((1, BLOCK), sink_ref[ATT_GROUP * h + g], F32) * LOG2E
            lb = jnp.where(band_ok, sl[:2 * BLOCK] + bb_ref[h, :, g * BLOCK:(g + 1) * BLOCK], NEG * LOG2E)
            lm = jnp.where(meta_ok, sl[2 * BLOCK:] + bm_ref[0, h, :, g * BLOCK:(g + 1) * BLOCK], NEG * LOG2E)
            m = jnp.maximum(jnp.max(lb, axis=0, keepdims=True), jnp.max(lm, axis=0, keepdims=True))
            m = jnp.maximum(m, sink)
            pb = jnp.exp2(lb - m)
            pm = jnp.exp2(lm - m)
            den = (jnp.sum(pb, axis=0, keepdims=True) + jnp.sum(pm, axis=0, keepdims=True)
                   + jnp.exp2(sink - m))
            inv.append(1.0 / den)
            p_band.append(pb.astype(BF16))
            p_lead.append(jnp.concatenate([zpad, pm.astype(BF16)], axis=0))
        pbt = jnp.concatenate(p_band, axis=1)
        plt = jnp.concatenate(p_lead, axis=1)
        hs = slice(h * ATT_HEAD_DIM, (h + 1) * ATT_HEAD_DIM)
        vband = jnp.concatenate([vp_ref[hs, :], vc_ref[hs, :]], axis=1)
        ot = (jnp.dot(vband, pbt, preferred_element_type=F32)
              + jnp.dot(vl_ref[hs, :], plt, preferred_element_type=F32))
        for g in range(ATT_GROUP):
            row = (ATT_GROUP * h + g) * ATT_HEAD_DIM
            o_ref[row:row + ATT_HEAD_DIM, :] = (
                ot[:, g * BLOCK:(g + 1) * BLOCK] * inv[g]).astype(BF16)


def _attention(sinks, nat, proj_t, bias_band_t, bias_meta_t, w_out, batch, nblk):
    lead_blk = batch * nblk
    wo_rows = w_out.shape[0] // (batch * nblk)
    cur = lambda b, n: b * nblk + n
    prev = lambda b, n: b * nblk + jnp.maximum(n - 1, 0)
    k_col = NAT_K // ATT_KV_WIDTH
    v_row = ATT_WIDTH // ATT_KV_WIDTH
    return pl.pallas_call(
        _attn_kernel,
        grid=(batch, nblk),
        in_specs=[
            pl.BlockSpec(memory_space=pltpu.SMEM),
            pl.BlockSpec((ATT_WIDTH, BLOCK), lambda b, n: (0, cur(b, n))),
            pl.BlockSpec((BLOCK, ATT_KV_WIDTH), lambda b, n: (cur(b, n), k_col)),
            pl.BlockSpec((BLOCK, ATT_KV_WIDTH), lambda b, n: (prev(b, n), k_col)),
            pl.BlockSpec((N_META, ATT_KV_WIDTH),
                         lambda b, n: (lead_blk * (BLOCK // N_META) + N_PAD // N_META, k_col)),
            pl.BlockSpec((ATT_KV_WIDTH, BLOCK), lambda b, n: (v_row, cur(b, n))),
            pl.BlockSpec((ATT_KV_WIDTH, BLOCK), lambda b, n: (v_row, prev(b, n))),
            pl.BlockSpec((ATT_KV_WIDTH, BLOCK), lambda b, n: (v_row, lead_blk)),
            pl.BlockSpec((ATT_KV_HEADS, 2 * BLOCK, GQ), lambda b, n: (0, 0, 0)),
            pl.BlockSpec((1, ATT_KV_HEADS, N_META, GQ), lambda b, n: (n, 0, 0, 0)),
            pl.BlockSpec((wo_rows, D_MODEL), lambda b, n: (cur(b, n), 0)),
        ],
        out_specs=[
            pl.BlockSpec((ATT_WIDTH, BLOCK), lambda b, n: (0, cur(b, n))),
            pl.BlockSpec((wo_rows, D_MODEL), lambda b, n: (cur(b, n), 0)),
        ],
        out_shape=[
            jax.ShapeDtypeStruct((ATT_WIDTH, batch * nblk * BLOCK), BF16),
            jax.ShapeDtypeStruct(w_out.shape, BF16),
        ],
        compiler_params=_params(2),
        name="attention",
    )(sinks, proj_t, nat, nat, nat, proj_t, proj_t, proj_t, bias_band_t, bias_meta_t, w_out)


TAIL = 8


def _mlstm_kernel(p_ref, gate_ref, gb_ref, cw_ref, cb_ref, mhn_ref, wu_ref, wd_ref,
                  o_ref, wu_bf_ref, wd_bf_ref, c_ref, n_ref, m_ref, tail_ref):
    n = pl.program_id(1)
    wu_bf_ref[...] = wu_ref[...].astype(BF16)
    wd_bf_ref[...] = wd_ref[...].astype(BF16)

    @pl.when(n == 0)
    def _():
        c_ref[...] = jnp.zeros_like(c_ref)
        n_ref[...] = jnp.zeros_like(n_ref)
        m_ref[...] = jnp.zeros_like(m_ref)
        tail_ref[...] = jnp.zeros_like(tail_ref)

    t_idx = lax.broadcasted_iota(jnp.int32, (BLOCK, GATE_LANES), 0)
    valid = (n > 0) | (t_idx >= N_PAD)
    pre = GATE_SOFTCAP * jnp.tanh((gate_ref[...] + gb_ref[...]) / GATE_SOFTCAP)
    log_i = jnp.where(valid, pre, NEG)
    log_sig = jnp.minimum(pre, 0.0) - jnp.log1p(jnp.exp(-jnp.abs(pre)))
    log_f = jnp.where(valid, log_sig, 0.0)
    row = lax.broadcasted_iota(jnp.int32, (BLOCK, BLOCK), 0)
    col = lax.broadcasted_iota(jnp.int32, (BLOCK, BLOCK), 1)
    causal = col <= row
    tril = jnp.where(causal, 1.0, 0.0).astype(F32)
    b_all = jnp.dot(tril, log_f, preferred_element_type=F32,
                    precision=lax.Precision.HIGHEST)
    log_i_t = log_i.T
    b_all_t = b_all.T

    xqk = p_ref[:, NAT_MQK:NAT_MV].astype(F32)
    tail = tail_ref[...]
    row8 = lax.broadcasted_iota(jnp.int32, (TAIL, 2 * ML_QK_WIDTH), 0)
    y = cb_ref[...] + cw_ref[CONV_WIDTH - 1:CONV_WIDTH, :] * xqk
    for k in range(1, CONV_WIDTH):
        rolled = pltpu.roll(xqk, k, 0)
        head = jnp.where(row8 < k, pltpu.roll(tail, k, 0), rolled[0:TAIL])
        shifted = jnp.concatenate([head, rolled[TAIL:]], axis=0)
        y = y + cw_ref[CONV_WIDTH - 1 - k:CONV_WIDTH - k, :] * shifted
    tail_ref[...] = xqk[BLOCK - TAIL:, :]
    act = y * jax.nn.sigmoid(y)
    q_all = act[:, :ML_QK_WIDTH] * (ML_QK_DIM ** -0.5)
    k_all = act[:, ML_QK_WIDTH:]

    for h in range(ML_HEADS):
        q_f = q_all[:, h * ML_QK_DIM:(h + 1) * ML_QK_DIM]
        k_f = k_all[:, h * ML_QK_DIM:(h + 1) * ML_QK_DIM]
        q_h = q_f.astype(BF16)
        k_h = k_f.astype(BF16)
        v_h = p_ref[:, NAT_MV + h * ML_V_DIM:NAT_MV + (h + 1) * ML_V_DIM]
        b_col = b_all[:, ML_HEADS + h:ML_HEADS + h + 1]
        li_col = log_i[:, h:h + 1]
        b_row = b_all_t[ML_HEADS + h:ML_HEADS + h + 1, :]
        li_row = log_i_t[h:h + 1, :]
        b_last = b_all[BLOCK - 1:BLOCK, ML_HEADS + h:ML_HEADS + h + 1]
        c_prev = c_ref[h]
        n_prev = n_ref[h][0:1, :]
        m_prev = m_ref[h][0:1, 0:1]

        dmat = jnp.where(causal, b_col - b_row + li_row, -jnp.inf)
        inter = b_col + m_prev
        m_t = jnp.maximum(inter, jnp.max(dmat, axis=1, keepdims=True))
        s = lax.dot_general(q_h, k_h, (((1,), (1,)), ((), ())),
                            preferred_element_type=F32) * jnp.exp(dmat - m_t)
        a_t = jnp.exp(inter - m_t)
        num = (jnp.dot(s.astype(BF16), v_h, preferred_element_type=F32)
               + a_t * jnp.dot(q_h, c_prev.astype(BF16), preferred_element_type=F32))
        den = (jnp.sum(s, axis=1, keepdims=True)
               + a_t * jnp.sum(q_f * n_prev, axis=1, keepdims=True))
        den = jnp.maximum(jnp.abs(den), jnp.exp(-m_t))
        hh = num / den
        hh = hh * lax.rsqrt(jnp.mean(hh * hh, axis=-1, keepdims=True) + EPS)
        vs = slice(h * ML_V_DIM, (h + 1) * ML_V_DIM)
        hh = hh * mhn_ref[:, vs]
        m_o = p_ref[:, NAT_MO + h * ML_V_DIM:NAT_MO + (h + 1) * ML_V_DIM].astype(F32)
        o_ref[:, vs] = (hh * jax.nn.sigmoid(m_o)).astype(BF16)

        g_col = b_last - b_col + li_col
        m_loc = jnp.max(g_col, axis=0, keepdims=True)
        wk = jnp.exp(g_col - m_loc) * k_f
        c_loc = lax.dot_general(wk.astype(BF16), v_h, (((0,), (0,)), ((), ())),
                                preferred_element_type=F32)
        n_loc = jnp.sum(wk, axis=0, keepdims=True)
        m_new = jnp.maximum(b_last + m_prev, m_loc)
        a = jnp.exp(b_last + m_prev - m_new)
        cc = jnp.exp(m_loc - m_new)
        c_ref[h] = a * c_prev + cc * c_loc
        n_ref[h] = jnp.broadcast_to(a * n_prev + cc * n_loc, (8, ML_QK_DIM))
        m_ref[h] = jnp.broadcast_to(m_new, (8, 128))


def _mlstm(nat, gates, gate_bias, conv_w, conv_b, mh_norm, w_up, w_down, batch, nblk):
    lead_blk = batch * nblk
    src = lambda b, n: jnp.where(n == 0, lead_blk, b * nblk + n - 1)
    n_slices = batch * nblk
    ff_slice = D_FF // n_slices
    wsl = lambda b, n: jnp.minimum(b * (nblk + 1) + n, n_slices - 1)
    return pl.pallas_call(
        _mlstm_kernel,
        grid=(batch, nblk + 1),
        in_specs=[
            pl.BlockSpec((BLOCK, NAT_K), lambda b, n: (src(b, n), 0)),
            pl.BlockSpec((BLOCK, GATE_LANES), lambda b, n: (src(b, n), 0)),
            pl.BlockSpec((1, GATE_LANES), lambda b, n: (0, 0)),
            pl.BlockSpec((CONV_WIDTH, 2 * ML_QK_WIDTH), lambda b, n: (0, 0)),
            pl.BlockSpec((1, 2 * ML_QK_WIDTH), lambda b, n: (0, 0)),
            pl.BlockSpec((1, ML_WIDTH), lambda b, n: (0, 0)),
            pl.BlockSpec((D_MODEL, ff_slice), lambda b, n: (0, wsl(b, n))),
            pl.BlockSpec((ff_slice, D_MODEL), lambda b, n: (wsl(b, n), 0)),
        ],
        out_specs=[
            pl.BlockSpec((BLOCK, ML_WIDTH), lambda b, n: (b * nblk + jnp.maximum(n - 1, 0), 0)),
            pl.BlockSpec((D_MODEL, ff_slice), lambda b, n: (0, wsl(b, n))),
            pl.BlockSpec((ff_slice, D_MODEL), lambda b, n: (wsl(b, n), 0)),
        ],
        out_shape=[
            jax.ShapeDtypeStruct((batch * nblk * BLOCK, ML_WIDTH), BF16),
            jax.ShapeDtypeStruct(w_up.shape, BF16),
            jax.ShapeDtypeStruct(w_down.shape, BF16),
        ],
        scratch_shapes=[
            pltpu.VMEM((ML_HEADS, ML_QK_DIM, ML_V_DIM), F32),
            pltpu.VMEM((ML_HEADS, 8, ML_QK_DIM), F32),
            pltpu.VMEM((ML_HEADS, 8, 128), F32),
            pltpu.VMEM((TAIL, 2 * ML_QK_WIDTH), F32),
        ],
        compiler_params=_params(2),
        name="mlstm",
    )(nat, gates, gate_bias, conv_w, conv_b, mh_norm, w_up, w_down)


OUT_TM = 512


def _outproj_kernel(att_ref, hm_ref, x_ref, wa_ref, wm_ref, g_ref, h_ref, u_ref):
    h = (x_ref[...]
         + lax.dot_general(att_ref[...], wa_ref[...], (((0,), (0,)), ((), ())),
                           preferred_element_type=F32)
         + jnp.dot(hm_ref[...], wm_ref[...], preferred_element_type=F32))
    h_ref[...] = h
    u_ref[...] = _rms(h, g_ref[...]).astype(BF16)


def _outproj(att_t, hm, x2, w_out, g):
    rows = x2.shape[0]
    w_att = w_ml = w_out
    return pl.pallas_call(
        _outproj_kernel,
        grid=(rows // OUT_TM,),
        in_specs=[
            pl.BlockSpec((ATT_WIDTH, OUT_TM), lambda i: (0, i)),
            pl.BlockSpec((OUT_TM, ML_WIDTH), lambda i: (i, 0)),
            pl.BlockSpec((OUT_TM, D_MODEL), lambda i: (i, 0)),
            pl.BlockSpec((ATT_WIDTH, D_MODEL), lambda i: (0, 0)),
            pl.BlockSpec((ML_WIDTH, D_MODEL), lambda i: (ATT_WIDTH // ML_WIDTH, 0)),
            pl.BlockSpec((1, D_MODEL), lambda i: (0, 0)),
        ],
        out_specs=[
            pl.BlockSpec((OUT_TM, D_MODEL), lambda i: (i, 0)),
            pl.BlockSpec((OUT_TM, D_MODEL), lambda i: (i, 0)),
        ],
        out_shape=[
            jax.ShapeDtypeStruct((rows, D_MODEL), F32),
            jax.ShapeDtypeStruct((rows, D_MODEL), BF16),
        ],
        compiler_params=_params(1),
        name="outproj",
    )(att_t, hm, x2, w_att, w_ml, g)


MLP_TM = 1024
MLP_TF = 512


def _mlp_kernel(u_ref, h_ref, wu_ref, wd_ref, g_ref, o_ref):
    j = pl.program_id(1)

    @pl.when(j == 0)
    def _():
        o_ref[...] = h_ref[...]

    a = jnp.dot(u_ref[...], wu_ref[...], preferred_element_type=F32)
    a = jnp.square(jnp.maximum(a, 0.0)).astype(BF16)
    o_ref[...] += jnp.dot(a, wd_ref[...], preferred_element_type=F32)

    @pl.when(j == pl.num_programs(1) - 1)
    def _():
        o_ref[...] = _rms(o_ref[...], g_ref[...])


def _mlp(u, h, w_up, w_down, g):
    rows = u.shape[0]
    return pl.pallas_call(
        _mlp_kernel,
        grid=(rows // MLP_TM, D_FF // MLP_TF),
        in_specs=[
            pl.BlockSpec((MLP_TM, D_MODEL), lambda i, j: (i, 0)),
            pl.BlockSpec((MLP_TM, D_MODEL), lambda i, j: (i, 0)),
            pl.BlockSpec((D_MODEL, MLP_TF), lambda i, j: (0, j)),
            pl.BlockSpec((MLP_TF, D_MODEL), lambda i, j: (j, 0)),
            pl.BlockSpec((1, D_MODEL), lambda i, j: (0, 0)),
        ],
        out_specs=pl.BlockSpec((MLP_TM, D_MODEL), lambda i, j: (i, 0)),
        out_shape=jax.ShapeDtypeStruct((rows, D_MODEL), F32),
        compiler_params=_params(2),
        name="mlp",
    )(u, h, w_up, w_down, g)


def _t5_bucket(dist):
    max_exact = N_BUCKETS // 2
    d = jnp.maximum(dist, 0)
    ratio = jnp.maximum(d, max_exact).astype(F32) / max_exact
    large = max_exact + (jnp.log(ratio) / math.log(MAX_DISTANCE / max_exact)
                         * (N_BUCKETS - max_exact)).astype(jnp.int32)
    large = jnp.minimum(large, N_BUCKETS - 1)
    return jnp.where(d < max_exact, d, large)


BIAS_ROWS = 16


def _bias_kernel(rb_ref, bkt_band_ref, bkt_meta_ref, band_ref, meta_ref):
    def lookup(bkt):
        def body(b, accs):
            hit = bkt == b
            return tuple(jnp.where(hit, rb_ref[b, hd], a) for hd, a in enumerate(accs))
        init = tuple(jnp.zeros(bkt.shape, F32) for _ in range(ATT_HEADS))
        return lax.fori_loop(0, N_BUCKETS, body, init)

    def band_rows(ci, carry):
        r0 = pl.multiple_of(ci * BIAS_ROWS, BIAS_ROWS)
        accs = lookup(bkt_band_ref[pl.ds(r0, BIAS_ROWS), :])
        for hd in range(ATT_HEADS):
            h, g = divmod(hd, ATT_GROUP)
            band_ref[h, pl.ds(r0, BIAS_ROWS), g * BLOCK:(g + 1) * BLOCK] = accs[hd] * LOG2E
        return carry

    lax.fori_loop(0, 2 * BLOCK // BIAS_ROWS, band_rows, 0)

    def meta_block(n, carry):
        accs = lookup(bkt_meta_ref[n])
        for hd in range(ATT_HEADS):
            h, g = divmod(hd, ATT_GROUP)
            meta_ref[n, h, :, g * BLOCK:(g + 1) * BLOCK] = accs[hd] * LOG2E
        return carry

    lax.fori_loop(0, meta_ref.shape[0], meta_block, 0)


def _bias_tables(rel_bias, nblk):
    c = jnp.arange(2 * BLOCK)[:, None]
    r = jnp.arange(BLOCK)[None, :]
    bkt_band = _t5_bucket(r + BLOCK - c).astype(jnp.int32)
    q_pos = BLOCK + jnp.arange(nblk * BLOCK)[None, :]
    m_pos = N_PAD + jnp.arange(N_META)[:, None]
    bkt_meta = _t5_bucket(q_pos - m_pos).astype(jnp.int32)
    bkt_meta = bkt_meta.reshape(N_META, nblk, BLOCK).transpose(1, 0, 2)
    vmem = pl.BlockSpec(memory_space=pltpu.VMEM)
    return pl.pallas_call(
        _bias_kernel,
        in_specs=[pl.BlockSpec(memory_space=pltpu.SMEM), vmem, vmem],
        out_specs=[vmem, vmem],
        out_shape=[
            jax.ShapeDtypeStruct((ATT_KV_HEADS, 2 * BLOCK, GQ), F32),
            jax.ShapeDtypeStruct((nblk, ATT_KV_HEADS, N_META, GQ), F32),
        ],
        name="bias_tables",
    )(rel_bias.astype(F32), bkt_band, bkt_meta)


LANE = 128
W_IN_OFFS = dict(q=0, k=1024, v=1280, mq=1536, mk=2048, mv=2560, mo=3584, gates=4608)
NAT_SRC = ([W_IN_OFFS["mq"] // LANE + t for t in range((W_IN_OFFS["gates"] - W_IN_OFFS["mq"]) // LANE)]
           + [W_IN_OFFS["k"] // LANE + t for t in range(ATT_KV_WIDTH // LANE)]
           + [W_IN_OFFS["gates"] // LANE])
QV_SRC = ([W_IN_OFFS["q"] // LANE + t for t in range(ATT_WIDTH // LANE)]
          + [W_IN_OFFS["v"] // LANE + t for t in range(ATT_KV_WIDTH // LANE)])
N_NAT_BLK = len(NAT_SRC)
N_Q_BLK = ATT_WIDTH // LANE


def _win_prep_kernel(src_ref, w_ref, nat_ref, qv_ref):
    j = pl.program_id(0)
    wv = w_ref[0]

    @pl.when(j < N_NAT_BLK - 1)
    def _():
        nat_ref[...] = wv.astype(BF16)

    @pl.when(j == N_NAT_BLK - 1)
    def _():
        row = lax.broadcasted_iota(jnp.int32, wv.shape, 0)
        nat_ref[...] = jnp.where(row < 2 * ML_HEADS, wv, 0.0).astype(BF16)

    @pl.when((j >= N_NAT_BLK) & (j < N_NAT_BLK + N_Q_BLK))
    def _():
        qv_ref[...] = (wv * (ATT_HEAD_DIM ** -0.5 * LOG2E)).astype(BF16)

    @pl.when(j >= N_NAT_BLK + N_Q_BLK)
    def _():
        qv_ref[...] = wv.astype(BF16)


def _win_prep(w_in_t):
    src = jnp.asarray(NAT_SRC + QV_SRC, jnp.int32)
    d = w_in_t.shape[2]
    return pl.pallas_call(
        _win_prep_kernel,
        grid_spec=pltpu.PrefetchScalarGridSpec(
            num_scalar_prefetch=1,
            grid=(len(NAT_SRC) + len(QV_SRC),),
            in_specs=[pl.BlockSpec((1, LANE, d), lambda j, src: (0, src[j], 0))],
            out_specs=[
                pl.BlockSpec((LANE, d), lambda j, src: (jnp.minimum(j, N_NAT_BLK - 1), 0)),
                pl.BlockSpec((LANE, d), lambda j, src: (jnp.maximum(j - N_NAT_BLK, 0), 0)),
            ],
        ),
        out_shape=[
            jax.ShapeDtypeStruct((NAT_COLS, d), BF16),
            jax.ShapeDtypeStruct((T_ROWS, d), BF16),
        ],
        compiler_params=_params(1),
        name="w_in_prep",
    )(src, w_in_t)


def kernel(x, meta_tokens, w_in, conv_w, conv_b, b_igate, b_fgate, attn_sinks, rel_bias,
           mh_norm, w_out, norm_mix, norm_mlp, w_up, w_down, norm_final):
    batch, seq, d = x.shape
    nblk = seq // BLOCK
    x2 = x.reshape(batch * seq, d)
    lead = jnp.concatenate([jnp.zeros((N_PAD, d), x.dtype), meta_tokens.astype(x.dtype)], axis=0)

    w_nat, w_qv = _win_prep(jnp.swapaxes(w_in, 1, 2))
    gate_bias = jnp.pad(jnp.concatenate([b_igate[0], b_fgate[0]]).astype(F32),
                        (0, GATE_LANES - 2 * ML_HEADS))[None]
    bias_band_t, bias_meta_t = _bias_tables(rel_bias, nblk)

    nat, proj_t, gates = _inproj(x2, lead, norm_mix[0][None].astype(F32), w_nat, w_qv)
    att_t, w_out_bf = _attention(attn_sinks[0].astype(F32), nat, proj_t, bias_band_t, bias_meta_t,
                                 w_out[0], batch, nblk)
    hm, w_up_bf, w_down_bf = _mlstm(nat, gates, gate_bias, conv_w[0].astype(F32),
                                    conv_b[0][None].astype(F32), mh_norm[0][None].astype(F32),
                                    w_up[0], w_down[0], batch, nblk)
    h2, u2 = _outproj(att_t, hm, x2, w_out_bf, norm_mlp[0][None].astype(F32))
    out = _mlp(u2, h2, w_up_bf, w_down_bf, norm_final[None].astype(F32))
    return out.reshape(batch, seq, d)
```

```python
import functools
import math

import jax
import jax.numpy as jnp
from jax import lax
from jax.experimental import pallas as pl
from jax.experimental.pallas import tpu as pltpu

F32 = jnp.float32
BF16 = jnp.bfloat16

D_MODEL = 2048
N_META = 16
BLOCK = 128
N_PAD = BLOCK - N_META
WINDOW = 128
ATT_HEADS = 16
ATT_KV_HEADS = 4
ATT_GROUP = ATT_HEADS // ATT_KV_HEADS
ATT_HEAD_DIM = 64
ATT_WIDTH = ATT_HEADS * ATT_HEAD_DIM
ATT_KV_WIDTH = ATT_KV_HEADS * ATT_HEAD_DIM
ML_HEADS = 4
ML_V_DIM = 256
ML_QK_DIM = 128
ML_WIDTH = ML_HEADS * ML_V_DIM
ML_QK_WIDTH = ML_HEADS * ML_QK_DIM
CONV_WIDTH = 4
GATE_SOFTCAP = 15.0
D_FF = 4 * D_MODEL
N_BUCKETS = 32
MAX_DISTANCE = 128
EPS = 1e-6
NEG = -1e30
LOG2E = math.log2(math.e)

GATE_LANES = 128
NAT_MQK = 0
NAT_MV = 2 * ML_QK_WIDTH
NAT_MO = NAT_MV + ML_WIDTH
NAT_K = NAT_MO + ML_WIDTH
NAT_COLS = NAT_K + ATT_KV_WIDTH + GATE_LANES
T_ROWS = ATT_WIDTH + ATT_KV_WIDTH

V7X_VMEM_LIMIT = 60 * 1024 * 1024


def _params(n_axes, vmem=V7X_VMEM_LIMIT):
    return pltpu.CompilerParams(dimension_semantics=("arbitrary",) * n_axes,
                                vmem_limit_bytes=vmem)


IN_TM = 1024
IN_NAT_STEPS = 3
IN_TN = NAT_COLS // IN_NAT_STEPS
MXU_COLS = 256


def _rms(x, g):
    return x * lax.rsqrt(jnp.mean(x * x, axis=-1, keepdims=True) + EPS) * g


def _inproj_kernel(x_ref, g_ref, wn_ref, wq_ref, *rest):
    nat_ref, t_ref, gate_ref, u_ref = rest[-4:]
    j = pl.program_id(1)

    @pl.when(j == 0)
    def _():
        u_ref[...] = _rms(x_ref[...], g_ref[...]).astype(BF16)

    contract_last = (((1,), (1,)), ((), ()))

    @pl.when(j < IN_NAT_STEPS)
    def _():
        r = lax.dot_general(u_ref[...], wn_ref[...], contract_last, preferred_element_type=F32)
        nat_ref[...] = r.astype(BF16)

        @pl.when(j == IN_NAT_STEPS - 1)
        def _():
            gate_ref[...] = r[:, IN_TN - GATE_LANES:]

    @pl.when(j == IN_NAT_STEPS)
    def _():
        t_ref[...] = lax.dot_general(wq_ref[...], u_ref[...], contract_last,
                                     preferred_element_type=F32).astype(BF16)


def _inproj_call(x_rows, g, w_nat, w_qv, tm, row_off, total_rows, prev=None):
    n_tiles = x_rows.shape[0] // tm
    off = row_off // tm
    last_nat = IN_NAT_STEPS - 1
    in_specs = [
        pl.BlockSpec((tm, D_MODEL), lambda i, j: (i, 0)),
        pl.BlockSpec((1, D_MODEL), lambda i, j: (0, 0)),
        pl.BlockSpec((IN_TN, D_MODEL), lambda i, j: (jnp.minimum(j, last_nat), 0)),
        pl.BlockSpec((T_ROWS, D_MODEL), lambda i, j: (0, 0)),
    ]
    args = [x_rows, g, w_nat, w_qv]
    aliases = {}
    if prev is not None:
        in_specs += [pl.BlockSpec(memory_space=pl.ANY)] * 3
        aliases = {len(args) + k: k for k in range(3)}
        args += list(prev)
    return pl.pallas_call(
        _inproj_kernel,
        grid=(n_tiles, IN_NAT_STEPS + 1),
        in_specs=in_specs,
        out_specs=[
            pl.BlockSpec((tm, IN_TN), lambda i, j: (off + i, jnp.minimum(j, last_nat))),
            pl.BlockSpec((T_ROWS, tm), lambda i, j: (0, off + i)),
            pl.BlockSpec((tm, GATE_LANES), lambda i, j: (off + i, 0)),
        ],
        out_shape=[
            jax.ShapeDtypeStruct((total_rows, NAT_COLS), BF16),
            jax.ShapeDtypeStruct((T_ROWS, total_rows), BF16),
            jax.ShapeDtypeStruct((total_rows, GATE_LANES), F32),
        ],
        scratch_shapes=[pltpu.VMEM((tm, D_MODEL), BF16)],
        input_output_aliases=aliases,
        compiler_params=_params(2),
        name="inproj_lead" if prev is not None else "inproj",
    )(*args)


def _inproj(x2, lead, g, w_nat, w_qv):
    n_real = x2.shape[0]
    total = n_real + BLOCK
    outs = _inproj_call(x2, g, w_nat, w_qv, IN_TM, 0, total)
    return _inproj_call(lead, g, w_nat, w_qv, BLOCK, n_real, total, prev=outs)


N_KEYS = 2 * BLOCK + N_META
GQ = ATT_GROUP * BLOCK


def _attn_body(n, sink_ref, qt_ref, kc_ref, kp_ref, km_ref, vc_ref, vp_ref, vl_ref,
               bb_ref, bm_ref, o_ref):
    c = lax.broadcasted_iota(jnp.int32, (2 * BLOCK, BLOCK), 0)
    r = lax.broadcasted_iota(jnp.int32, (2 * BLOCK, BLOCK), 1)
    dist = r + BLOCK - c
    band_ok = (dist >= 0) & (dist < WINDOW) & ((c >= BLOCK) | (n >= 1))
    mm = lax.broadcasted_iota(jnp.int32, (N_META, BLOCK), 0)
    rr = lax.broadcasted_iota(jnp.int32, (N_META, BLOCK), 1)
    meta_ok = (n + 1) * BLOCK + rr >= N_PAD + mm

    kall = jnp.concatenate([kp_ref[...], kc_ref[...], km_ref[...]], axis=0)
    zpad = jnp.zeros((N_PAD, BLOCK), BF16)
    for h in range(ATT_KV_HEADS):
        lo = (h // 2) * 128
        kpair = kall[:, lo:lo + 128]
        qh = jnp.concatenate(
            [qt_ref[(ATT_GROUP * h + g) * ATT_HEAD_DIM:(ATT_GROUP * h + g + 1) * ATT_HEAD_DIM, :]
             for g in range(ATT_GROUP)], axis=1)
        zq = jnp.zeros_like(qh)
        qz = jnp.concatenate([qh, zq] if h % 2 == 0 else [zq, qh], axis=0)
        st = jnp.dot(kpair, qz, preferred_element_type=F32)
        p_band, p_lead, inv = [], [], []
        for g in range(ATT_GROUP):
            sl = st[:, g * BLOCK:(g + 1) * BLOCK]
            sink = jnp.full((1, BLOCK), sink_ref[ATT_GROUP * h + g], F32) * LOG2E
            lb = jnp.where(band_ok, sl[:2 * BLOCK] + bb_ref[h, :, g * BLOCK:(g + 1) * BLOCK], NEG * LOG2E)
            lm = jnp.where(meta_ok, sl[2 * BLOCK:] + bm_ref[0, h, :, g * BLOCK:(g + 1) * BLOCK], NEG * LOG2E)
            m = jnp.maximum(jnp.max(lb, axis=0, keepdims=True), jnp.max(lm, axis=0, keepdims=True))
            m = jnp.maximum(m, sink)
            pb = jnp.exp2(lb - m)
            pm = jnp.exp2(lm - m)
            den = (jnp.sum(pb, axis=0, keepdims=True) + jnp.sum(pm, axis=0, keepdims=True)
                   + jnp.exp2(sink - m))
            inv.append(1.0 / den)
            p_band.append(pb.astype(BF16))
            p_lead.append(jnp.concatenate([zpad, pm.astype(BF16)], axis=0))
        pbt = jnp.concatenate(p_band, axis=1)
        plt = jnp.concatenate(p_lead, axis=1)
        hs = slice(h * ATT_HEAD_DIM, (h + 1) * ATT_HEAD_DIM)
        vband = jnp.concatenate([vp_ref[hs, :], vc_ref[hs, :]], axis=1)
        ot = (jnp.dot(vband, pbt, preferred_element_type=F32)
              + jnp.dot(vl_ref[hs, :], plt, preferred_element_type=F32))
        for g in range(ATT_GROUP):
            row = (ATT_GROUP * h + g) * ATT_HEAD_DIM
            o_ref[row:row + ATT_HEAD_DIM, :] = (
                ot[:, g * BLOCK:(g + 1) * BLOCK] * inv[g]).astype(BF16)


TAIL = 8


def _mlstm_body(n, p_ref, gate_ref, gb_ref, cw_ref, cb_ref, mhn_ref, o_ref,
                c_ref, n_ref, m_ref, tail_ref):
    @pl.when(n == 0)
    def _():
        c_ref[...] = jnp.zeros_like(c_ref)
        n_ref[...] = jnp.zeros_like(n_ref)
        m_ref[...] = jnp.zeros_like(m_ref)
        tail_ref[...] = jnp.zeros_like(tail_ref)

    t_idx = lax.broadcasted_iota(jnp.int32, (BLOCK, GATE_LANES), 0)
    valid = (n > 0) | (t_idx >= N_PAD)
    pre = GATE_SOFTCAP * jnp.tanh((gate_ref[...] + gb_ref[...]) / GATE_SOFTCAP)
    log_i = jnp.where(valid, pre, NEG)
    log_sig = jnp.minimum(pre, 0.0) - jnp.log1p(jnp.exp(-jnp.abs(pre)))
    log_f = jnp.where(valid, log_sig, 0.0)
    row = lax.broadcasted_iota(jnp.int32, (BLOCK, BLOCK), 0)
    col = lax.broadcasted_iota(jnp.int32, (BLOCK, BLOCK), 1)
    causal = col <= row
    tril = jnp.where(causal, 1.0, 0.0).astype(F32)
    b_all = jnp.dot(tril, log_f, preferred_element_type=F32,
                    precision=lax.Precision.HIGHEST)
    log_i_t = log_i.T
    b_all_t = b_all.T

    xqk = p_ref[:, NAT_MQK:NAT_MV].astype(F32)
    tail = tail_ref[...]
    row8 = lax.broadcasted_iota(jnp.int32, (TAIL, 2 * ML_QK_WIDTH), 0)
    y = cb_ref[...] + cw_ref[CONV_WIDTH - 1:CONV_WIDTH, :] * xqk
    for k in range(1, CONV_WIDTH):
        rolled = pltpu.roll(xqk, k, 0)
        head = jnp.where(row8 < k, pltpu.roll(tail, k, 0), rolled[0:TAIL])
        shifted = jnp.concatenate([head, rolled[TAIL:]], axis=0)
        y = y + cw_ref[CONV_WIDTH - 1 - k:CONV_WIDTH - k, :] * shifted
    tail_ref[...] = xqk[BLOCK - TAIL:, :]
    act = y * jax.nn.sigmoid(y)
    q_all = act[:, :ML_QK_WIDTH] * (ML_QK_DIM ** -0.5)
    k_all = act[:, ML_QK_WIDTH:]

    for h in range(ML_HEADS):
        q_f = q_all[:, h * ML_QK_DIM:(h + 1) * ML_QK_DIM]
        k_f = k_all[:, h * ML_QK_DIM:(h + 1) * ML_QK_DIM]
        q_h = q_f.astype(BF16)
        k_h = k_f.astype(BF16)
        v_h = p_ref[:, NAT_MV + h * ML_V_DIM:NAT_MV + (h + 1) * ML_V_DIM]
        b_col = b_all[:, ML_HEADS + h:ML_HEADS + h + 1]
        li_col = log_i[:, h:h + 1]
        b_row = b_all_t[ML_HEADS + h:ML_HEADS + h + 1, :]
        li_row = log_i_t[h:h + 1, :]
        b_last = b_all[BLOCK - 1:BLOCK, ML_HEADS + h:ML_HEADS + h + 1]
        c_prev = c_ref[h]
        n_prev = n_ref[h][0:1, :]
        m_prev = m_ref[h][0:1, 0:1]

        dmat = jnp.where(causal, b_col - b_row + li_row, -jnp.inf)
        inter = b_col + m_prev
        m_t = jnp.maximum(inter, jnp.max(dmat, axis=1, keepdims=True))
        s = lax.dot_general(q_h, k_h, (((1,), (1,)), ((), ())),
                            preferred_element_type=F32) * jnp.exp(dmat - m_t)
        a_t = jnp.exp(inter - m_t)
        num = (jnp.dot(s.astype(BF16), v_h, preferred_element_type=F32)
               + a_t * jnp.dot(q_h, c_prev.astype(BF16), preferred_element_type=F32))
        den = (jnp.sum(s, axis=1, keepdims=True)
               + a_t * jnp.sum(q_f * n_prev, axis=1, keepdims=True))
        den = jnp.maximum(jnp.abs(den), jnp.exp(-m_t))
        hh = num / den
        hh = hh * lax.rsqrt(jnp.mean(hh * hh, axis=-1, keepdims=True) + EPS)
        vs = slice(h * ML_V_DIM, (h + 1) * ML_V_DIM)
        hh = hh * mhn_ref[:, vs]
        m_o = p_ref[:, NAT_MO + h * ML_V_DIM:NAT_MO + (h + 1) * ML_V_DIM].astype(F32)
        o_ref[:, vs] = (hh * jax.nn.sigmoid(m_o)).astype(BF16)

        g_col = b_last - b_col + li_col
        m_loc = jnp.max(g_col, axis=0, keepdims=True)
        wk = jnp.exp(g_col - m_loc) * k_f
        c_loc = lax.dot_general(wk.astype(BF16), v_h, (((0,), (0,)), ((), ())),
                                preferred_element_type=F32)
        n_loc = jnp.sum(wk, axis=0, keepdims=True)
        m_new = jnp.maximum(b_last + m_prev, m_loc)
        a = jnp.exp(b_last + m_prev - m_new)
        cc = jnp.exp(m_loc - m_new)
        c_ref[h] = a * c_prev + cc * c_loc
        n_ref[h] = jnp.broadcast_to(a * n_prev + cc * n_loc, (8, ML_QK_DIM))
        m_ref[h] = jnp.broadcast_to(m_new, (8, 128))


N_ATT_IN = 10
N_ML_IN = 6


def _mixer_kernel(*refs):
    att_in = refs[:N_ATT_IN]
    ml_in = refs[N_ATT_IN:N_ATT_IN + N_ML_IN]
    wo_ref, wu_ref, wd_ref = refs[N_ATT_IN + N_ML_IN:N_ATT_IN + N_ML_IN + 3]
    att_ref, hm_ref, wo_bf_ref, wu_bf_ref, wd_bf_ref = refs[N_ATT_IN + N_ML_IN + 3:-4]
    scratch = refs[-4:]
    n = pl.program_id(1)
    wo_bf_ref[...] = wo_ref[...].astype(BF16)
    wu_bf_ref[...] = wu_ref[...].astype(BF16)
    wd_bf_ref[...] = wd_ref[...].astype(BF16)
    _mlstm_body(n, *ml_in, hm_ref, *scratch)
    _attn_body(jnp.maximum(n - 1, 0), *att_in, att_ref)


def _mixer(sinks, nat, proj_t, gates, bias_band_t, bias_meta_t, gate_bias, conv_w, conv_b, mh_norm,
           w_out, w_up, w_down, batch, nblk):
    lead_blk = batch * nblk
    acur = lambda b, n: b * nblk + jnp.maximum(n - 1, 0)
    aprev = lambda b, n: b * nblk + jnp.maximum(n - 2, 0)
    k_col = NAT_K // ATT_KV_WIDTH
    v_row = ATT_WIDTH // ATT_KV_WIDTH
    src = lambda b, n: jnp.where(n == 0, lead_blk, b * nblk + n - 1)
    n_slices = batch * nblk
    ff_slice = D_FF // n_slices
    wo_slice = w_out.shape[0] // n_slices
    wsl = lambda b, n: jnp.minimum(b * (nblk + 1) + n, n_slices - 1)
    return pl.pallas_call(
        _mixer_kernel,
        grid=(batch, nblk + 1),
        in_specs=[
            pl.BlockSpec(memory_space=pltpu.SMEM),
            pl.BlockSpec((ATT_WIDTH, BLOCK), lambda b, n: (0, acur(b, n))),
            pl.BlockSpec((BLOCK, ATT_KV_WIDTH), lambda b, n: (acur(b, n), k_col)),
            pl.BlockSpec((BLOCK, ATT_KV_WIDTH), lambda b, n: (aprev(b, n), k_col)),
            pl.BlockSpec((N_META, ATT_KV_WIDTH),
                         lambda b, n: (lead_blk * (BLOCK // N_META) + N_PAD // N_META, k_col)),
            pl.BlockSpec((ATT_KV_WIDTH, BLOCK), lambda b, n: (v_row, acur(b, n))),
            pl.BlockSpec((ATT_KV_WIDTH, BLOCK), lambda b, n: (v_row, aprev(b, n))),
            pl.BlockSpec((ATT_KV_WIDTH, BLOCK), lambda b, n: (v_row, lead_blk)),
            pl.BlockSpec((ATT_KV_HEADS, 2 * BLOCK, GQ), lambda b, n: (0, 0, 0)),
            pl.BlockSpec((1, ATT_KV_HEADS, N_META, GQ), lambda b, n: (jnp.maximum(n - 1, 0), 0, 0, 0)),
            pl.BlockSpec((BLOCK, NAT_K), lambda b, n: (src(b, n), 0)),
            pl.BlockSpec((BLOCK, GATE_LANES), lambda b, n: (src(b, n), 0)),
            pl.BlockSpec((1, GATE_LANES), lambda b, n: (0, 0)),
            pl.BlockSpec((CONV_WIDTH, 2 * ML_QK_WIDTH), lambda b, n: (0, 0)),
            pl.BlockSpec((1, 2 * ML_QK_WIDTH), lambda b, n: (0, 0)),
            pl.BlockSpec((1, ML_WIDTH), lambda b, n: (0, 0)),
            pl.BlockSpec((wo_slice, D_MODEL), lambda b, n: (wsl(b, n), 0)),
            pl.BlockSpec((D_MODEL, ff_slice), lambda b, n: (0, wsl(b, n))),
            pl.BlockSpec((ff_slice, D_MODEL), lambda b, n: (wsl(b, n), 0)),
        ],
        out_specs=[
            pl.BlockSpec((ATT_WIDTH, BLOCK), lambda b, n: (0, acur(b, n))),
            pl.BlockSpec((BLOCK, ML_WIDTH), lambda b, n: (acur(b, n), 0)),
            pl.BlockSpec((wo_slice, D_MODEL), lambda b, n: (wsl(b, n), 0)),
            pl.BlockSpec((D_MODEL, ff_slice), lambda b, n: (0, wsl(b, n))),
            pl.BlockSpec((ff_slice, D_MODEL), lambda b, n: (wsl(b, n), 0)),
        ],
        out_shape=[
            jax.ShapeDtypeStruct((ATT_WIDTH, batch * nblk * BLOCK), BF16),
            jax.ShapeDtypeStruct((batch * nblk * BLOCK, ML_WIDTH), BF16),
            jax.ShapeDtypeStruct(w_out.shape, BF16),
            jax.ShapeDtypeStruct(w_up.shape, BF16),
            jax.ShapeDtypeStruct(w_down.shape, BF16),
        ],
        scratch_shapes=[
            pltpu.VMEM((ML_HEADS, ML_QK_DIM, ML_V_DIM), F32),
            pltpu.VMEM((ML_HEADS, 8, ML_QK_DIM), F32),
            pltpu.VMEM((ML_HEADS, 8, 128), F32),
            pltpu.VMEM((TAIL, 2 * ML_QK_WIDTH), F32),
        ],
        compiler_params=_params(2),
        name="mixer",
    )(sinks, proj_t, nat, nat, nat, proj_t, proj_t, proj_t, bias_band_t, bias_meta_t,
      nat, gates, gate_bias, conv_w, conv_b, mh_norm, w_out, w_up, w_down)


OUT_TM = 512


def _outproj_kernel(att_ref, hm_ref, x_ref, wa_ref, wm_ref, g_ref, h_ref, u_ref):
    h = (x_ref[...]
         + lax.dot_general(att_ref[...], wa_ref[...], (((0,), (0,)), ((), ())),
                           preferred_element_type=F32)
         + jnp.dot(hm_ref[...], wm_ref[...], preferred_element_type=F32))
    h_ref[...] = h
    u_ref[...] = _rms(h, g_ref[...]).astype(BF16)


def _outproj(att_t, hm, x2, w_out, g):
    rows = x2.shape[0]
    w_att = w_ml = w_out
    return pl.pallas_call(
        _outproj_kernel,
        grid=(rows // OUT_TM,),
        in_specs=[
            pl.BlockSpec((ATT_WIDTH, OUT_TM), lambda i: (0, i)),
            pl.BlockSpec((OUT_TM, ML_WIDTH), lambda i: (i, 0)),
            pl.BlockSpec((OUT_TM, D_MODEL), lambda i: (i, 0)),
            pl.BlockSpec((ATT_WIDTH, D_MODEL), lambda i: (0, 0)),
            pl.BlockSpec((ML_WIDTH, D_MODEL), lambda i: (ATT_WIDTH // ML_WIDTH, 0)),
            pl.BlockSpec((1, D_MODEL), lambda i: (0, 0)),
        ],
        out_specs=[
            pl.BlockSpec((OUT_TM, D_MODEL), lambda i: (i, 0)),
            pl.BlockSpec((OUT_TM, D_MODEL), lambda i: (i, 0)),
        ],
        out_shape=[
            jax.ShapeDtypeStruct((rows, D_MODEL), F32),
            jax.ShapeDtypeStruct((rows, D_MODEL), BF16),
        ],
        compiler_params=_params(1),
        name="outproj",
    )(att_t, hm, x2, w_att, w_ml, g)


MLP_TM = 1024
MLP_TF = 512


def _mlp_kernel(u_ref, h_ref, wu_ref, wd_ref, g_ref, o_ref):
    j = pl.program_id(1)

    @pl.when(j == 0)
    def _():
        o_ref[...] = h_ref[...]

    a = jnp.dot(u_ref[...], wu_ref[...], preferred_element_type=F32)
    a = jnp.square(jnp.maximum(a, 0.0)).astype(BF16)
    o_ref[...] += jnp.dot(a, wd_ref[...], preferred_element_type=F32)

    @pl.when(j == pl.num_programs(1) - 1)
    def _():
        o_ref[...] = _rms(o_ref[...], g_ref[...])


def _mlp(u, h, w_up, w_down, g):
    rows = u.shape[0]
    return pl.pallas_call(
        _mlp_kernel,
        grid=(rows // MLP_TM, D_FF // MLP_TF),
        in_specs=[
            pl.BlockSpec((MLP_TM, D_MODEL), lambda i, j: (i, 0)),
            pl.BlockSpec((MLP_TM, D_MODEL), lambda i, j: (i, 0)),
            pl.BlockSpec((D_MODEL, MLP_TF), lambda i, j: (0, j)),
            pl.BlockSpec((MLP_TF, D_MODEL), lambda i, j: (j, 0)),
            pl.BlockSpec((1, D_MODEL), lambda i, j: (0, 0)),
        ],
        out_specs=pl.BlockSpec((MLP_TM, D_MODEL), lambda i, j: (i, 0)),
        out_shape=jax.ShapeDtypeStruct((rows, D_MODEL), F32),
        compiler_params=_params(2),
        name="mlp",
    )(u, h, w_up, w_down, g)


def _t5_bucket(dist):
    max_exact = N_BUCKETS // 2
    d = jnp.maximum(dist, 0)
    ratio = jnp.maximum(d, max_exact).astype(F32) / max_exact
    large = max_exact + (jnp.log(ratio) / math.log(MAX_DISTANCE / max_exact)
                         * (N_BUCKETS - max_exact)).astype(jnp.int32)
    large = jnp.minimum(large, N_BUCKETS - 1)
    return jnp.where(d < max_exact, d, large)


BIAS_ROWS = 16


def _bias_kernel(rb_ref, bkt_band_ref, bkt_meta_ref, band_ref, meta_ref, splat_ref):
    def fill(b, carry):
        for hd in range(ATT_HEADS):
            splat_ref[b * ATT_HEADS + hd] = jnp.full(splat_ref.shape[1:], rb_ref[b, hd], F32) * LOG2E
        return carry

    lax.fori_loop(0, N_BUCKETS, fill, 0)

    def lookup(bkt):
        def body(b, accs):
            hit = bkt == b
            out = []
            for hd, a in enumerate(accs):
                sp = splat_ref[b * ATT_HEADS + hd]
                out.append(jnp.where(hit, jnp.concatenate([sp] * (BIAS_ROWS // sp.shape[0]), axis=0), a))
            return tuple(out)
        init = tuple(jnp.zeros(bkt.shape, F32) for _ in range(ATT_HEADS))
        return lax.fori_loop(0, N_BUCKETS, body, init, unroll=4)

    def band_rows(ci, carry):
        r0 = pl.multiple_of(ci * BIAS_ROWS, BIAS_ROWS)
        accs = lookup(bkt_band_ref[pl.ds(r0, BIAS_ROWS), :])
        for hd in range(ATT_HEADS):
            h, g = divmod(hd, ATT_GROUP)
            band_ref[h, pl.ds(r0, BIAS_ROWS), g * BLOCK:(g + 1) * BLOCK] = accs[hd]
        return carry

    lax.fori_loop(0, 2 * BLOCK // BIAS_ROWS, band_rows, 0)

    def meta_block(n, carry):
        accs = lookup(bkt_meta_ref[n])
        for hd in range(ATT_HEADS):
            h, g = divmod(hd, ATT_GROUP)
            meta_ref[n, h, :, g * BLOCK:(g + 1) * BLOCK] = accs[hd]
        return carry

    lax.fori_loop(0, meta_ref.shape[0], meta_block, 0)


def _bias_tables(rel_bias, nblk):
    c = jnp.arange(2 * BLOCK)[:, None]
    r = jnp.arange(BLOCK)[None, :]
    bkt_band = _t5_bucket(r + BLOCK - c).astype(jnp.int32)
    q_pos = BLOCK + jnp.arange(nblk * BLOCK)[None, :]
    m_pos = N_PAD + jnp.arange(N_META)[:, None]
    bkt_meta = _t5_bucket(q_pos - m_pos).astype(jnp.int32)
    bkt_meta = bkt_meta.reshape(N_META, nblk, BLOCK).transpose(1, 0, 2)
    vmem = pl.BlockSpec(memory_space=pltpu.VMEM)
    return pl.pallas_call(
        _bias_kernel,
        in_specs=[pl.BlockSpec(memory_space=pltpu.SMEM), vmem, vmem],
        out_specs=[vmem, vmem],
        out_shape=[
            jax.ShapeDtypeStruct((ATT_KV_HEADS, 2 * BLOCK, GQ), F32),
            jax.ShapeDtypeStruct((nblk, ATT_KV_HEADS, N_META, GQ), F32),
        ],
        scratch_shapes=[pltpu.VMEM((N_BUCKETS * ATT_HEADS, 8, BLOCK), F32)],
        name="bias_tables",
    )(rel_bias.astype(F32), bkt_band, bkt_meta)


LANE = 128
W_IN_OFFS = dict(q=0, k=1024, v=1280, mq=1536, mk=2048, mv=2560, mo=3584, gates=4608)
NAT_SRC = ([W_IN_OFFS["mq"] // LANE + t for t in range((W_IN_OFFS["gates"] - W_IN_OFFS["mq"]) // LANE)]
           + [W_IN_OFFS["k"] // LANE + t for t in range(ATT_KV_WIDTH // LANE)]
           + [W_IN_OFFS["gates"] // LANE])
QV_SRC = ([W_IN_OFFS["q"] // LANE + t for t in range(ATT_WIDTH // LANE)]
          + [W_IN_OFFS["v"] // LANE + t for t in range(ATT_KV_WIDTH // LANE)])
N_NAT_BLK = len(NAT_SRC)
N_Q_BLK = ATT_WIDTH // LANE


def _win_prep_kernel(src_ref, w_ref, nat_ref, qv_ref):
    j = pl.program_id(0)
    wv = w_ref[0]

    @pl.when(j < N_NAT_BLK - 1)
    def _():
        nat_ref[...] = wv.astype(BF16)

    @pl.when(j == N_NAT_BLK - 1)
    def _():
        row = lax.broadcasted_iota(jnp.int32, wv.shape, 0)
        nat_ref[...] = jnp.where(row < 2 * ML_HEADS, wv, 0.0).astype(BF16)

    @pl.when((j >= N_NAT_BLK) & (j < N_NAT_BLK + N_Q_BLK))
    def _():
        qv_ref[...] = (wv * (ATT_HEAD_DIM ** -0.5 * LOG2E)).astype(BF16)

    @pl.when(j >= N_NAT_BLK + N_Q_BLK)
    def _():
        qv_ref[...] = wv.astype(BF16)


def _win_prep(w_in_t):
    src = jnp.asarray(NAT_SRC + QV_SRC, jnp.int32)
    d = w_in_t.shape[2]
    return pl.pallas_call(
        _win_prep_kernel,
        grid_spec=pltpu.PrefetchScalarGridSpec(
            num_scalar_prefetch=1,
            grid=(len(NAT_SRC) + len(QV_SRC),),
            in_specs=[pl.BlockSpec((1, LANE, d), lambda j, src: (0, src[j], 0))],
            out_specs=[
                pl.BlockSpec((LANE, d), lambda j, src: (jnp.minimum(j, N_NAT_BLK - 1), 0)),
                pl.BlockSpec((LANE, d), lambda j, src: (jnp.maximum(j - N_NAT_BLK, 0), 0)),
            ],
        ),
        out_shape=[
            jax.ShapeDtypeStruct((NAT_COLS, d), BF16),
            jax.ShapeDtypeStruct((T_ROWS, d), BF16),
        ],
        compiler_params=_params(1),
        name="w_in_prep",
    )(src, w_in_t)


def kernel(x, meta_tokens, w_in, conv_w, conv_b, b_igate, b_fgate, attn_sinks, rel_bias,
           mh_norm, w_out, norm_mix, norm_mlp, w_up, w_down, norm_final):
    batch, seq, d = x.shape
    nblk = seq // BLOCK
    x2 = x.reshape(batch * seq, d)
    lead = jnp.concatenate([jnp.zeros((N_PAD, d), x.dtype), meta_tokens.astype(x.dtype)], axis=0)

    w_nat, w_qv = _win_prep(jnp.swapaxes(w_in, 1, 2))
    gate_bias = jnp.pad(jnp.concatenate([b_igate[0], b_fgate[0]]).astype(F32),
                        (0, GATE_LANES - 2 * ML_HEADS))[None]
    bias_band_t, bias_meta_t = _bias_tables(rel_bias, nblk)

    nat, proj_t, gates = _inproj(x2, lead, norm_mix[0][None].astype(F32), w_nat, w_qv)
    att_t, hm, w_out_bf, w_up_bf, w_down_bf = _mixer(
        attn_sinks[0].astype(F32), nat, proj_t, gates, bias_band_t, bias_meta_t, gate_bias,
        conv_w[0].astype(F32), conv_b[0][None].astype(F32), mh_norm[0][None].astype(F32),
        w_out[0], w_up[0], w_down[0], batch, nblk)
    h2, u2 = _outproj(att_t, hm, x2, w_out_bf, norm_mlp[0][None].astype(F32))
    out = _mlp(u2, h2, w_up_bf, w_down_bf, norm_final[None].astype(F32))
    return out.reshape(batch, seq, d)
```

```python
import math

import numpy as np
import jax
import jax.numpy as jnp
from jax import lax
from jax.experimental import pallas as pl
from jax.experimental.pallas import tpu as pltpu

F32 = jnp.float32
BF16 = jnp.bfloat16

D_MODEL = 2048
N_META = 16
BLOCK = 128
N_PAD = BLOCK - N_META
WINDOW = 128
ATT_HEADS = 16
ATT_KV_HEADS = 4
ATT_GROUP = ATT_HEADS // ATT_KV_HEADS
ATT_HEAD_DIM = 64
ATT_WIDTH = ATT_HEADS * ATT_HEAD_DIM
ATT_KV_WIDTH = ATT_KV_HEADS * ATT_HEAD_DIM
ML_HEADS = 4
ML_V_DIM = 256
ML_QK_DIM = 128
ML_WIDTH = ML_HEADS * ML_V_DIM
ML_QK_WIDTH = ML_HEADS * ML_QK_DIM
CONV_WIDTH = 4
GATE_SOFTCAP = 15.0
D_FF = 4 * D_MODEL
N_BUCKETS = 32
MAX_DISTANCE = 128
EPS = 1e-6
NEG = -1e30
LOG2E = math.log2(math.e)

GATE_LANES = 128
NAT_MQK = 0
NAT_MV = 2 * ML_QK_WIDTH
NAT_MO = NAT_MV + ML_WIDTH
NAT_K = NAT_MO + ML_WIDTH
NAT_COLS = NAT_K + ATT_KV_WIDTH + GATE_LANES
T_ROWS = ATT_WIDTH + ATT_KV_WIDTH

V7X_VMEM_LIMIT = 60 * 1024 * 1024
LANE = 128
CONTRACT_LAST = (((1,), (1,)), ((), ()))
CONTRACT_FIRST = (((0,), (0,)), ((), ()))


def _params(n_axes, vmem=V7X_VMEM_LIMIT):
    return pltpu.CompilerParams(dimension_semantics=("arbitrary",) * n_axes,
                                vmem_limit_bytes=vmem)


def _rms(x, g):
    return x * lax.rsqrt(jnp.mean(x * x, axis=-1, keepdims=True) + EPS) * g


IN_TM = 1024
IN_NAT_STEPS = 3
IN_TN = NAT_COLS // IN_NAT_STEPS


def _inproj_kernel(x_ref, lead_ref, g_ref, wn_ref, wq_ref, nat_ref, t_ref, gate_ref, u_ref):
    i = pl.program_id(0)
    j = pl.program_id(1)
    is_real = i < pl.num_programs(0) - 1
    is_lead = jnp.logical_not(is_real)

    @pl.when((j == 0) & is_real)
    def _():
        u_ref[...] = _rms(x_ref[...], g_ref[...]).astype(BF16)

    @pl.when((j == 0) & is_lead)
    def _():
        u_ref[0:BLOCK, :] = _rms(lead_ref[...], g_ref[...]).astype(BF16)

    def project(rows):
        u = u_ref[0:rows, :]

        @pl.when(j < IN_NAT_STEPS)
        def _():
            r = lax.dot_general(u, wn_ref[...], CONTRACT_LAST, preferred_element_type=F32)
            nat_ref[0:rows, :] = r.astype(BF16)

            @pl.when(j == IN_NAT_STEPS - 1)
            def _():
                gate_ref[0:rows, :] = r[:, IN_TN - GATE_LANES:]

        @pl.when(j == IN_NAT_STEPS)
        def _():
            t_ref[:, 0:rows] = lax.dot_general(wq_ref[...], u, CONTRACT_LAST,
                                               preferred_element_type=F32).astype(BF16)

    @pl.when(is_real)
    def _():
        project(IN_TM)

    @pl.when(is_lead)
    def _():
        @pl.when(j < IN_NAT_STEPS)
        def _():
            nat_ref[BLOCK:, :] = jnp.zeros((IN_TM - BLOCK, IN_TN), BF16)

        @pl.when(j == IN_NAT_STEPS - 1)
        def _():
            gate_ref[BLOCK:, :] = jnp.zeros((IN_TM - BLOCK, GATE_LANES), F32)

        @pl.when(j == IN_NAT_STEPS)
        def _():
            t_ref[:, BLOCK:] = jnp.zeros((T_ROWS, IN_TM - BLOCK), BF16)

        project(BLOCK)


def _inproj(x2, lead, g, w_nat, w_qv):
    n_real_tiles = x2.shape[0] // IN_TM
    rows = (n_real_tiles + 1) * IN_TM
    last_nat = IN_NAT_STEPS - 1
    return pl.pallas_call(
        _inproj_kernel,
        grid=(n_real_tiles + 1, IN_NAT_STEPS + 1),
        in_specs=[
            pl.BlockSpec((IN_TM, D_MODEL), lambda i, j: (jnp.minimum(i, n_real_tiles - 1), 0)),
            pl.BlockSpec((BLOCK, D_MODEL), lambda i, j: (0, 0)),
            pl.BlockSpec((1, D_MODEL), lambda i, j: (0, 0)),
            pl.BlockSpec((IN_TN, D_MODEL), lambda i, j: (jnp.minimum(j, last_nat), 0)),
            pl.BlockSpec((T_ROWS, D_MODEL), lambda i, j: (0, 0)),
        ],
        out_specs=[
            pl.BlockSpec((IN_TM, IN_TN), lambda i, j: (i, jnp.minimum(j, last_nat))),
            pl.BlockSpec((T_ROWS, IN_TM), lambda i, j: (0, i)),
            pl.BlockSpec((IN_TM, GATE_LANES), lambda i, j: (i, 0)),
        ],
        out_shape=[
            jax.ShapeDtypeStruct((rows, NAT_COLS), BF16),
            jax.ShapeDtypeStruct((T_ROWS, rows), BF16),
            jax.ShapeDtypeStruct((rows, GATE_LANES), F32),
        ],
        scratch_shapes=[pltpu.VMEM((IN_TM, D_MODEL), BF16)],
        compiler_params=_params(2),
        name="inproj",
    )(x2, lead, g, w_nat, w_qv)


GQ = ATT_GROUP * BLOCK
ATT_SLABS_PER_PHASE = 2


def _attn_prologue(n, kc_ref, kp_ref, km_ref):
    c = lax.broadcasted_iota(jnp.int32, (2 * BLOCK, BLOCK), 0)
    r = lax.broadcasted_iota(jnp.int32, (2 * BLOCK, BLOCK), 1)
    dist = r + BLOCK - c
    band_ok = (dist >= 0) & (dist < WINDOW) & ((c >= BLOCK) | (n >= 1))
    mm = lax.broadcasted_iota(jnp.int32, (N_META, BLOCK), 0)
    rr = lax.broadcasted_iota(jnp.int32, (N_META, BLOCK), 1)
    meta_ok = (n + 1) * BLOCK + rr >= N_PAD + mm
    kall = jnp.concatenate([kp_ref[...], kc_ref[...], km_ref[...]], axis=0)
    return band_ok, meta_ok, kall


def _attn_head(h, pro, sink_ref, qt_ref, vc_ref, vp_ref, vl_ref, bb_ref, bm_ref, o_ref):
    band_ok, meta_ok, kall = pro
    lo = (h // 2) * LANE
    kpair = kall[:, lo:lo + LANE]
    qh = jnp.concatenate(
        [qt_ref[(ATT_GROUP * h + g) * ATT_HEAD_DIM:(ATT_GROUP * h + g + 1) * ATT_HEAD_DIM, :]
         for g in range(ATT_GROUP)], axis=1)
    zq = jnp.zeros_like(qh)
    qz = jnp.concatenate([qh, zq] if h % 2 == 0 else [zq, qh], axis=0)
    st = jnp.dot(kpair, qz, preferred_element_type=F32)
    yield
    zpad = jnp.zeros((N_PAD, BLOCK), BF16)
    p_band, p_lead, inv = [], [], []
    for g in range(ATT_GROUP):
        gs = slice(g * BLOCK, (g + 1) * BLOCK)
        sl = st[:, gs]
        sink = jnp.full((1, BLOCK), sink_ref[ATT_GROUP * h + g], F32) * LOG2E
        lb = jnp.where(band_ok, sl[:2 * BLOCK] + bb_ref[h, :, gs], NEG * LOG2E)
        lm = jnp.where(meta_ok, sl[2 * BLOCK:] + bm_ref[0, h, :, gs], NEG * LOG2E)
        m = jnp.maximum(jnp.max(lb, axis=0, keepdims=True), jnp.max(lm, axis=0, keepdims=True))
        m = jnp.maximum(m, sink)
        pb = jnp.exp2(lb - m)
        pm = jnp.exp2(lm - m)
        den = (jnp.sum(pb, axis=0, keepdims=True) + jnp.sum(pm, axis=0, keepdims=True)
               + jnp.exp2(sink - m))
        inv.append(1.0 / den)
        p_band.append(pb.astype(BF16))
        p_lead.append(jnp.concatenate([zpad, pm.astype(BF16)], axis=0))
        if g % ATT_SLABS_PER_PHASE == ATT_SLABS_PER_PHASE - 1:
            yield
    pbt = jnp.concatenate(p_band, axis=1)
    plt = jnp.concatenate(p_lead, axis=1)
    hs = slice(h * ATT_HEAD_DIM, (h + 1) * ATT_HEAD_DIM)
    vband = jnp.concatenate([vp_ref[hs, :], vc_ref[hs, :]], axis=1)
    ot = (jnp.dot(vband, pbt, preferred_element_type=F32)
          + jnp.dot(vl_ref[hs, :], plt, preferred_element_type=F32))
    for g in range(ATT_GROUP):
        row = (ATT_GROUP * h + g) * ATT_HEAD_DIM
        o_ref[row:row + ATT_HEAD_DIM, :] = (
            ot[:, g * BLOCK:(g + 1) * BLOCK] * inv[g]).astype(BF16)


def _conv_shift_matrix():
    s = np.zeros((CONV_WIDTH * BLOCK, 2 * BLOCK), np.float32)
    for k in range(CONV_WIDTH):
        t = np.arange(BLOCK)
        s[k * BLOCK + t, BLOCK + t - k] = 1.0
    return jnp.asarray(s, BF16)


def _mlstm_gates(n, gate_ref, gb_ref):
    t_idx = lax.broadcasted_iota(jnp.int32, (BLOCK, GATE_LANES), 0)
    valid = (n > 0) | (t_idx >= N_PAD)
    pre = GATE_SOFTCAP * jnp.tanh((gate_ref[...] + gb_ref[...]) / GATE_SOFTCAP)
    log_i = jnp.where(valid, pre, NEG)
    log_sig = jnp.minimum(pre, 0.0) - jnp.log1p(jnp.exp(-jnp.abs(pre)))
    log_f = jnp.where(valid, log_sig, 0.0)
    row = lax.broadcasted_iota(jnp.int32, (BLOCK, BLOCK), 0)
    col = lax.broadcasted_iota(jnp.int32, (BLOCK, BLOCK), 1)
    causal = col <= row
    tril = jnp.where(causal, 1.0, 0.0).astype(F32)
    b_all = jnp.dot(tril, log_f, preferred_element_type=F32,
                    precision=lax.Precision.HIGHEST)
    return log_i, b_all, log_i.T, b_all.T, causal


def _mlstm_conv(p_ref, shift_ref, cw_ref, cb_ref, xprev_ref):
    xcur = p_ref[:, NAT_MQK:NAT_MV]
    xcat = jnp.concatenate([xprev_ref[...], xcur], axis=0)
    sh = jnp.dot(shift_ref[...], xcat, preferred_element_type=F32)
    xprev_ref[...] = xcur
    y = cb_ref[...]
    for k in range(CONV_WIDTH):
        y = y + cw_ref[CONV_WIDTH - 1 - k:CONV_WIDTH - k, :] * sh[k * BLOCK:(k + 1) * BLOCK]
    act = y * jax.nn.sigmoid(y)
    return act[:, :ML_QK_WIDTH] * (ML_QK_DIM ** -0.5), act[:, ML_QK_WIDTH:]


def _mlstm_head(h, gates, q_all, k_all, p_ref, mhn_ref, o_ref, c_ref, n_ref, m_ref):
    log_i, b_all, log_i_t, b_all_t, causal = gates
    q_f = q_all[:, h * ML_QK_DIM:(h + 1) * ML_QK_DIM]
    k_f = k_all[:, h * ML_QK_DIM:(h + 1) * ML_QK_DIM]
    q_h = q_f.astype(BF16)
    k_h = k_f.astype(BF16)
    v_h = p_ref[:, NAT_MV + h * ML_V_DIM:NAT_MV + (h + 1) * ML_V_DIM]
    b_col = b_all[:, ML_HEADS + h:ML_HEADS + h + 1]
    li_col = log_i[:, h:h + 1]
    b_row = b_all_t[ML_HEADS + h:ML_HEADS + h + 1, :]
    li_row = log_i_t[h:h + 1, :]
    b_last = b_all[BLOCK - 1:BLOCK, ML_HEADS + h:ML_HEADS + h + 1]
    c_prev = c_ref[h]
    n_prev = n_ref[h][0:1, :]
    m_prev = m_ref[h][0:1, 0:1]

    dmat = jnp.where(causal, b_col - b_row + li_row, -jnp.inf)
    inter = b_col + m_prev
    m_t = jnp.maximum(inter, jnp.max(dmat, axis=1, keepdims=True))
    s = lax.dot_general(q_h, k_h, CONTRACT_LAST, preferred_element_type=F32) * jnp.exp(dmat - m_t)
    yield
    a_t = jnp.exp(inter - m_t)
    num = (jnp.dot(s.astype(BF16), v_h, preferred_element_type=F32)
           + a_t * jnp.dot(q_h, c_prev.astype(BF16), preferred_element_type=F32))
    den = (jnp.sum(s, axis=1, keepdims=True)
           + a_t * jnp.sum(q_f * n_prev, axis=1, keepdims=True))
    den = jnp.maximum(jnp.abs(den), jnp.exp(-m_t))
    hh = num / den
    hh = hh * lax.rsqrt(jnp.mean(hh * hh, axis=-1, keepdims=True) + EPS)
    vs = slice(h * ML_V_DIM, (h + 1) * ML_V_DIM)
    hh = hh * mhn_ref[:, vs]
    m_o = p_ref[:, NAT_MO + h * ML_V_DIM:NAT_MO + (h + 1) * ML_V_DIM].astype(F32)
    o_ref[:, vs] = (hh * jax.nn.sigmoid(m_o)).astype(BF16)
    yield

    g_col = b_last - b_col + li_col
    m_loc = jnp.max(g_col, axis=0, keepdims=True)
    wk = jnp.exp(g_col - m_loc) * k_f
    c_loc = lax.dot_general(wk.astype(BF16), v_h, CONTRACT_FIRST,
                            preferred_element_type=F32)
    n_loc = jnp.sum(wk, axis=0, keepdims=True)
    m_new = jnp.maximum(b_last + m_prev, m_loc)
    a = jnp.exp(b_last + m_prev - m_new)
    cc = jnp.exp(m_loc - m_new)
    c_ref[h] = a * c_prev + cc * c_loc
    n_ref[h] = jnp.broadcast_to(a * n_prev + cc * n_loc, (8, ML_QK_DIM))
    m_ref[h] = jnp.broadcast_to(m_new, (8, LANE))


def _mixer_kernel(sink_ref, qt_ref, kc_ref, kp_ref, km_ref, vc_ref, vp_ref, vl_ref, bb_ref, bm_ref,
                  p_ref, gate_ref, gb_ref, shift_ref, cw_ref, cb_ref, mhn_ref,
                  wo_ref, wu_ref, wd_ref,
                  att_ref, hm_ref, wo_bf_ref, wu_bf_ref, wd_bf_ref,
                  c_ref, n_ref, m_ref, xprev_ref):
    n = pl.program_id(1)

    @pl.when(n == 0)
    def _():
        c_ref[...] = jnp.zeros_like(c_ref)
        n_ref[...] = jnp.zeros_like(n_ref)
        m_ref[...] = jnp.zeros_like(m_ref)
        xprev_ref[...] = jnp.zeros_like(xprev_ref)

    wo_bf_ref[...] = wo_ref[...].astype(BF16)
    wu_bf_ref[...] = wu_ref[...].astype(BF16)
    wd_bf_ref[...] = wd_ref[...].astype(BF16)

    att_args = (sink_ref, qt_ref, vc_ref, vp_ref, vl_ref, bb_ref, bm_ref, att_ref)
    def attention_phases():
        pro = _attn_prologue(jnp.maximum(n - 1, 0), kc_ref, kp_ref, km_ref)
        for h in range(ATT_KV_HEADS):
            yield from _attn_head(h, pro, *att_args)

    def mlstm_phases():
        gates = _mlstm_gates(n, gate_ref, gb_ref)
        yield
        q_all, k_all = _mlstm_conv(p_ref, shift_ref, cw_ref, cb_ref, xprev_ref)
        yield
        for h in range(ML_HEADS):
            yield from _mlstm_head(h, gates, q_all, k_all, p_ref, mhn_ref, hm_ref, c_ref, n_ref, m_ref)

    streams = [mlstm_phases(), attention_phases()]
    while streams:
        for s in list(streams):
            if next(s, StopIteration) is StopIteration:
                streams.remove(s)


def _mixer(sinks, nat, proj_t, gates, bias_band_t, bias_meta_t, gate_bias, conv_w, conv_b, mh_norm,
           w_out, w_up, w_down, batch, nblk):
    lead_blk = batch * nblk
    acur = lambda b, n: b * nblk + jnp.maximum(n - 1, 0)
    aprev = lambda b, n: b * nblk + jnp.maximum(n - 2, 0)
    k_col = NAT_K // ATT_KV_WIDTH
    v_row = ATT_WIDTH // ATT_KV_WIDTH
    src = lambda b, n: jnp.where(n == 0, lead_blk, b * nblk + n - 1)
    n_slices = batch * nblk
    ff_slice = D_FF // n_slices
    wo_slice = w_out.shape[0] // n_slices
    wsl = lambda b, n: jnp.minimum(b * (nblk + 1) + n, n_slices - 1)
    const2 = lambda b, n: (0, 0)
    return pl.pallas_call(
        _mixer_kernel,
        grid=(batch, nblk + 1),
        in_specs=[
            pl.BlockSpec(memory_space=pltpu.SMEM),
            pl.BlockSpec((ATT_WIDTH, BLOCK), lambda b, n: (0, acur(b, n))),
            pl.BlockSpec((BLOCK, ATT_KV_WIDTH), lambda b, n: (acur(b, n), k_col)),
            pl.BlockSpec((BLOCK, ATT_KV_WIDTH), lambda b, n: (aprev(b, n), k_col)),
            pl.BlockSpec((N_META, ATT_KV_WIDTH),
                         lambda b, n: (lead_blk * (BLOCK // N_META) + N_PAD // N_META, k_col)),
            pl.BlockSpec((ATT_KV_WIDTH, BLOCK), lambda b, n: (v_row, acur(b, n))),
            pl.BlockSpec((ATT_KV_WIDTH, BLOCK), lambda b, n: (v_row, aprev(b, n))),
            pl.BlockSpec((ATT_KV_WIDTH, BLOCK), lambda b, n: (v_row, lead_blk)),
            pl.BlockSpec((ATT_KV_HEADS, 2 * BLOCK, GQ), lambda b, n: (0, 0, 0)),
            pl.BlockSpec((1, ATT_KV_HEADS, N_META, GQ), lambda b, n: (jnp.maximum(n - 1, 0), 0, 0, 0)),
            pl.BlockSpec((BLOCK, NAT_K), lambda b, n: (src(b, n), 0)),
            pl.BlockSpec((BLOCK, GATE_LANES), lambda b, n: (src(b, n), 0)),
            pl.BlockSpec((1, GATE_LANES), const2),
            pl.BlockSpec((CONV_WIDTH * BLOCK, 2 * BLOCK), const2),
            pl.BlockSpec((CONV_WIDTH, 2 * ML_QK_WIDTH), const2),
            pl.BlockSpec((1, 2 * ML_QK_WIDTH), const2),
            pl.BlockSpec((1, ML_WIDTH), const2),
            pl.BlockSpec((wo_slice, D_MODEL), lambda b, n: (wsl(b, n), 0)),
            pl.BlockSpec((D_MODEL, ff_slice), lambda b, n: (0, wsl(b, n))),
            pl.BlockSpec((ff_slice, D_MODEL), lambda b, n: (wsl(b, n), 0)),
        ],
        out_specs=[
            pl.BlockSpec((ATT_WIDTH, BLOCK), lambda b, n: (0, acur(b, n))),
            pl.BlockSpec((BLOCK, ML_WIDTH), lambda b, n: (acur(b, n), 0)),
            pl.BlockSpec((wo_slice, D_MODEL), lambda b, n: (wsl(b, n), 0)),
            pl.BlockSpec((D_MODEL, ff_slice), lambda b, n: (0, wsl(b, n))),
            pl.BlockSpec((ff_slice, D_MODEL), lambda b, n: (wsl(b, n), 0)),
        ],
        out_shape=[
            jax.ShapeDtypeStruct((ATT_WIDTH, batch * nblk * BLOCK), BF16),
            jax.ShapeDtypeStruct((batch * nblk * BLOCK, ML_WIDTH), BF16),
            jax.ShapeDtypeStruct(w_out.shape, BF16),
            jax.ShapeDtypeStruct(w_up.shape, BF16),
            jax.ShapeDtypeStruct(w_down.shape, BF16),
        ],
        scratch_shapes=[
            pltpu.VMEM((ML_HEADS, ML_QK_DIM, ML_V_DIM), F32),
            pltpu.VMEM((ML_HEADS, 8, ML_QK_DIM), F32),
            pltpu.VMEM((ML_HEADS, 8, LANE), F32),
            pltpu.VMEM((BLOCK, 2 * ML_QK_WIDTH), BF16),
        ],
        compiler_params=_params(2),
        name="mixer",
    )(sinks, proj_t, nat, nat, nat, proj_t, proj_t, proj_t, bias_band_t, bias_meta_t,
      nat, gates, gate_bias, _conv_shift_matrix(), conv_w, conv_b, mh_norm, w_out, w_up, w_down)


OUT_TM = 512


def _outproj_kernel(att_ref, hm_ref, x_ref, wa_ref, wm_ref, g_ref, h_ref, u_ref):
    h = (x_ref[...]
         + lax.dot_general(att_ref[...], wa_ref[...], CONTRACT_FIRST, preferred_element_type=F32)
         + jnp.dot(hm_ref[...], wm_ref[...], preferred_element_type=F32))
    h_ref[...] = h
    u_ref[...] = _rms(h, g_ref[...]).astype(BF16)


def _outproj(att_t, hm, x2, w_out, g):
    rows = x2.shape[0]
    return pl.pallas_call(
        _outproj_kernel,
        grid=(rows // OUT_TM,),
        in_specs=[
            pl.BlockSpec((ATT_WIDTH, OUT_TM), lambda i: (0, i)),
            pl.BlockSpec((OUT_TM, ML_WIDTH), lambda i: (i, 0)),
            pl.BlockSpec((OUT_TM, D_MODEL), lambda i: (i, 0)),
            pl.BlockSpec((ATT_WIDTH, D_MODEL), lambda i: (0, 0)),
            pl.BlockSpec((ML_WIDTH, D_MODEL), lambda i: (ATT_WIDTH // ML_WIDTH, 0)),
            pl.BlockSpec((1, D_MODEL), lambda i: (0, 0)),
        ],
        out_specs=[
            pl.BlockSpec((OUT_TM, D_MODEL), lambda i: (i, 0)),
            pl.BlockSpec((OUT_TM, D_MODEL), lambda i: (i, 0)),
        ],
        out_shape=[
            jax.ShapeDtypeStruct((rows, D_MODEL), F32),
            jax.ShapeDtypeStruct((rows, D_MODEL), BF16),
        ],
        compiler_params=_params(1),
        name="outproj",
    )(att_t, hm, x2, w_out, w_out, g)


MLP_TM = 1024
MLP_TF = 512


def _mlp_kernel(u_ref, h_ref, wu_ref, wd_ref, g_ref, o_ref):
    j = pl.program_id(1)

    @pl.when(j == 0)
    def _():
        o_ref[...] = h_ref[...]

    a = jnp.dot(u_ref[...], wu_ref[...], preferred_element_type=F32)
    a = jnp.square(jnp.maximum(a, 0.0)).astype(BF16)
    o_ref[...] += jnp.dot(a, wd_ref[...], preferred_element_type=F32)

    @pl.when(j == pl.num_programs(1) - 1)
    def _():
        o_ref[...] = _rms(o_ref[...], g_ref[...])


def _mlp(u, h, w_up, w_down, g):
    rows = u.shape[0]
    return pl.pallas_call(
        _mlp_kernel,
        grid=(rows // MLP_TM, D_FF // MLP_TF),
        in_specs=[
            pl.BlockSpec((MLP_TM, D_MODEL), lambda i, j: (i, 0)),
            pl.BlockSpec((MLP_TM, D_MODEL), lambda i, j: (i, 0)),
            pl.BlockSpec((D_MODEL, MLP_TF), lambda i, j: (0, j)),
            pl.BlockSpec((MLP_TF, D_MODEL), lambda i, j: (j, 0)),
            pl.BlockSpec((1, D_MODEL), lambda i, j: (0, 0)),
        ],
        out_specs=pl.BlockSpec((MLP_TM, D_MODEL), lambda i, j: (i, 0)),
        out_shape=jax.ShapeDtypeStruct((rows, D_MODEL), F32),
        compiler_params=_params(2),
        name="mlp",
    )(u, h, w_up, w_down, g)


def _t5_bucket(dist):
    max_exact = N_BUCKETS // 2
    d = jnp.maximum(dist, 0)
    ratio = jnp.maximum(d, max_exact).astype(F32) / max_exact
    large = max_exact + (jnp.log(ratio) / math.log(MAX_DISTANCE / max_exact)
                         * (N_BUCKETS - max_exact)).astype(jnp.int32)
    large = jnp.minimum(large, N_BUCKETS - 1)
    return jnp.where(d < max_exact, d, large)


BIAS_ROWS = 16
SUBLANES = 8


def _bias_kernel(rb_ref, bkt_band_ref, bkt_meta_ref, band_ref, meta_ref, splat_ref):
    def fill(b, carry):
        for hd in range(ATT_HEADS):
            splat_ref[b * ATT_HEADS + hd] = jnp.full((SUBLANES, BLOCK), rb_ref[b, hd], F32) * LOG2E
        return carry

    lax.fori_loop(0, N_BUCKETS, fill, 0)

    def lookup(bkt):
        def body(b, accs):
            hit = bkt == b
            out = []
            for hd, a in enumerate(accs):
                sp = splat_ref[b * ATT_HEADS + hd]
                out.append(jnp.where(hit, jnp.concatenate([sp] * (BIAS_ROWS // SUBLANES), axis=0), a))
            return tuple(out)
        init = tuple(jnp.zeros(bkt.shape, F32) for _ in range(ATT_HEADS))
        return lax.fori_loop(0, N_BUCKETS, body, init, unroll=4)

    def band_rows(ci, carry):
        r0 = pl.multiple_of(ci * BIAS_ROWS, BIAS_ROWS)
        accs = lookup(bkt_band_ref[pl.ds(r0, BIAS_ROWS), :])
        for hd in range(ATT_HEADS):
            h, g = divmod(hd, ATT_GROUP)
            band_ref[h, pl.ds(r0, BIAS_ROWS), g * BLOCK:(g + 1) * BLOCK] = accs[hd]
        return carry

    lax.fori_loop(0, 2 * BLOCK // BIAS_ROWS, band_rows, 0)

    def meta_block(n, carry):
        accs = lookup(bkt_meta_ref[n])
        for hd in range(ATT_HEADS):
            h, g = divmod(hd, ATT_GROUP)
            meta_ref[n, h, :, g * BLOCK:(g + 1) * BLOCK] = accs[hd]
        return carry

    lax.fori_loop(0, meta_ref.shape[0], meta_block, 0)


def _bias_tables(rel_bias, nblk):
    c = jnp.arange(2 * BLOCK)[:, None]
    r = jnp.arange(BLOCK)[None, :]
    bkt_band = _t5_bucket(r + BLOCK - c).astype(jnp.int32)
    q_pos = BLOCK + jnp.arange(nblk * BLOCK)[None, :]
    m_pos = N_PAD + jnp.arange(N_META)[:, None]
    bkt_meta = _t5_bucket(q_pos - m_pos).astype(jnp.int32)
    bkt_meta = bkt_meta.reshape(N_META, nblk, BLOCK).transpose(1, 0, 2)
    vmem = pl.BlockSpec(memory_space=pltpu.VMEM)
    return pl.pallas_call(
        _bias_kernel,
        in_specs=[pl.BlockSpec(memory_space=pltpu.SMEM), vmem, vmem],
        out_specs=[vmem, vmem],
        out_shape=[
            jax.ShapeDtypeStruct((ATT_KV_HEADS, 2 * BLOCK, GQ), F32),
            jax.ShapeDtypeStruct((nblk, ATT_KV_HEADS, N_META, GQ), F32),
        ],
        scratch_shapes=[pltpu.VMEM((N_BUCKETS * ATT_HEADS, SUBLANES, BLOCK), F32)],
        name="bias_tables",
    )(rel_bias.astype(F32), bkt_band, bkt_meta)


W_IN_OFFS = dict(q=0, k=1024, v=1280, mq=1536, mk=2048, mv=2560, mo=3584, gates=4608)
NAT_SRC = ([W_IN_OFFS["mq"] // LANE + t for t in range((W_IN_OFFS["gates"] - W_IN_OFFS["mq"]) // LANE)]
           + [W_IN_OFFS["k"] // LANE + t for t in range(ATT_KV_WIDTH // LANE)]
           + [W_IN_OFFS["gates"] // LANE])
QV_SRC = ([W_IN_OFFS["q"] // LANE + t for t in range(ATT_WIDTH // LANE)]
          + [W_IN_OFFS["v"] // LANE + t for t in range(ATT_KV_WIDTH // LANE)])
N_NAT_BLK = len(NAT_SRC)
N_Q_BLK = ATT_WIDTH // LANE


def _win_prep_kernel(src_ref, w_ref, nat_ref, qv_ref):
    j = pl.program_id(0)
    wv = w_ref[0]

    @pl.when(j < N_NAT_BLK - 1)
    def _():
        nat_ref[...] = wv.astype(BF16)

    @pl.when(j == N_NAT_BLK - 1)
    def _():
        row = lax.broadcasted_iota(jnp.int32, wv.shape, 0)
        nat_ref[...] = jnp.where(row < 2 * ML_HEADS, wv, 0.0).astype(BF16)

    @pl.when((j >= N_NAT_BLK) & (j < N_NAT_BLK + N_Q_BLK))
    def _():
        qv_ref[...] = (wv * (ATT_HEAD_DIM ** -0.5 * LOG2E)).astype(BF16)

    @pl.when(j >= N_NAT_BLK + N_Q_BLK)
    def _():
        qv_ref[...] = wv.astype(BF16)


def _win_prep(w_in_t):
    src = jnp.asarray(NAT_SRC + QV_SRC, jnp.int32)
    d = w_in_t.shape[2]
    return pl.pallas_call(
        _win_prep_kernel,
        grid_spec=pltpu.PrefetchScalarGridSpec(
            num_scalar_prefetch=1,
            grid=(len(NAT_SRC) + len(QV_SRC),),
            in_specs=[pl.BlockSpec((1, LANE, d), lambda j, src: (0, src[j], 0))],
            out_specs=[
                pl.BlockSpec((LANE, d), lambda j, src: (jnp.minimum(j, N_NAT_BLK - 1), 0)),
                pl.BlockSpec((LANE, d), lambda j, src: (jnp.maximum(j - N_NAT_BLK, 0), 0)),
            ],
        ),
        out_shape=[
            jax.ShapeDtypeStruct((NAT_COLS, d), BF16),
            jax.ShapeDtypeStruct((T_ROWS, d), BF16),
        ],
        compiler_params=_params(1),
        name="w_in_prep",
    )(src, w_in_t)


def kernel(x, meta_tokens, w_in, conv_w, conv_b, b_igate, b_fgate, attn_sinks, rel_bias,
           mh_norm, w_out, norm_mix, norm_mlp, w_up, w_down, norm_final):
    batch, seq, d = x.shape
    nblk = seq // BLOCK
    x2 = x.reshape(batch * seq, d)
    lead = jnp.concatenate([jnp.zeros((N_PAD, d), x.dtype), meta_tokens.astype(x.dtype)], axis=0)

    w_nat, w_qv = _win_prep(jnp.swapaxes(w_in, 1, 2))
    gate_bias = jnp.pad(jnp.concatenate([b_igate[0], b_fgate[0]]).astype(F32),
                        (0, GATE_LANES - 2 * ML_HEADS))[None]
    bias_band_t, bias_meta_t = _bias_tables(rel_bias, nblk)

    nat, proj_t, gates = _inproj(x2, lead, norm_mix[0][None].astype(F32), w_nat, w_qv)
    att_t, hm, w_out_bf, w_up_bf, w_down_bf = _mixer(
        attn_sinks[0].astype(F32), nat, proj_t, gates, bias_band_t, bias_meta_t, gate_bias,
        conv_w[0].astype(F32), conv_b[0][None].astype(F32), mh_norm[0][None].astype(F32),
        w_out[0], w_up[0], w_down[0], batch, nblk)
    h2, u2 = _outproj(att_t, hm, x2, w_out_bf, norm_mlp[0][None].astype(F32))
    out = _mlp(u2, h2, w_up_bf, w_down_bf, norm_final[None].astype(F32))
    return out.reshape(batch, seq, d)
```

```python
import math

import numpy as np
import jax
import jax.numpy as jnp
from jax import lax
from jax.experimental import pallas as pl
from jax.experimental.pallas import tpu as pltpu

F32 = jnp.float32
BF16 = jnp.bfloat16

D_MODEL = 2048
N_META = 16
BLOCK = 128
N_PAD = BLOCK - N_META
WINDOW = 128
ATT_HEADS = 16
ATT_KV_HEADS = 4
ATT_GROUP = ATT_HEADS // ATT_KV_HEADS
ATT_HEAD_DIM = 64
ATT_WIDTH = ATT_HEADS * ATT_HEAD_DIM
ATT_KV_WIDTH = ATT_KV_HEADS * ATT_HEAD_DIM
ML_HEADS = 4
ML_V_DIM = 256
ML_QK_DIM = 128
ML_WIDTH = ML_HEADS * ML_V_DIM
ML_QK_WIDTH = ML_HEADS * ML_QK_DIM
CONV_WIDTH = 4
GATE_SOFTCAP = 15.0
D_FF = 4 * D_MODEL
N_BUCKETS = 32
MAX_DISTANCE = 128
EPS = 1e-6
NEG = -1e30
LOG2E = math.log2(math.e)

GATE_LANES = 128
NAT_MQK = 0
NAT_MV = 2 * ML_QK_WIDTH
NAT_MO = NAT_MV + ML_WIDTH
NAT_K = NAT_MO + ML_WIDTH
NAT_COLS = NAT_K + ATT_KV_WIDTH + GATE_LANES
T_ROWS = ATT_WIDTH + ATT_KV_WIDTH

V7X_VMEM_LIMIT = 60 * 1024 * 1024
LANE = 128
CONTRACT_LAST = (((1,), (1,)), ((), ()))
CONTRACT_FIRST = (((0,), (0,)), ((), ()))


def _params(n_axes, vmem=V7X_VMEM_LIMIT):
    return pltpu.CompilerParams(dimension_semantics=("arbitrary",) * n_axes,
                                vmem_limit_bytes=vmem)


def _rms(x, g):
    return x * lax.rsqrt(jnp.mean(x * x, axis=-1, keepdims=True) + EPS) * g


IN_TM = 1024
IN_NAT_STEPS = 3
IN_TN = NAT_COLS // IN_NAT_STEPS


def _inproj_kernel(x_ref, lead_ref, g_ref, wn_ref, wq_ref, nat_ref, t_ref, gate_ref, u_ref):
    i = pl.program_id(0)
    j = pl.program_id(1)
    is_real = i < pl.num_programs(0) - 1
    is_lead = jnp.logical_not(is_real)

    @pl.when((j == 0) & is_real)
    def _():
        u_ref[...] = _rms(x_ref[...], g_ref[...]).astype(BF16)

    @pl.when((j == 0) & is_lead)
    def _():
        u_ref[0:BLOCK, :] = _rms(lead_ref[...], g_ref[...]).astype(BF16)

    def project(rows):
        u = u_ref[0:rows, :]

        @pl.when(j < IN_NAT_STEPS)
        def _():
            r = lax.dot_general(u, wn_ref[...], CONTRACT_LAST, preferred_element_type=F32)
            nat_ref[0:rows, :] = r.astype(BF16)

            @pl.when(j == IN_NAT_STEPS - 1)
            def _():
                gate_ref[0:rows, :] = r[:, IN_TN - GATE_LANES:]

        @pl.when(j == IN_NAT_STEPS)
        def _():
            t_ref[:, 0:rows] = lax.dot_general(wq_ref[...], u, CONTRACT_LAST,
                                               preferred_element_type=F32).astype(BF16)

    @pl.when(is_real)
    def _():
        project(IN_TM)

    @pl.when(is_lead)
    def _():
        @pl.when(j < IN_NAT_STEPS)
        def _():
            nat_ref[BLOCK:, :] = jnp.zeros((IN_TM - BLOCK, IN_TN), BF16)

        @pl.when(j == IN_NAT_STEPS - 1)
        def _():
            gate_ref[BLOCK:, :] = jnp.zeros((IN_TM - BLOCK, GATE_LANES), F32)

        @pl.when(j == IN_NAT_STEPS)
        def _():
            t_ref[:, BLOCK:] = jnp.zeros((T_ROWS, IN_TM - BLOCK), BF16)

        project(BLOCK)


def _inproj(x2, lead, g, w_nat, w_qv):
    n_real_tiles = x2.shape[0] // IN_TM
    rows = (n_real_tiles + 1) * IN_TM
    last_nat = IN_NAT_STEPS - 1
    return pl.pallas_call(
        _inproj_kernel,
        grid=(n_real_tiles + 1, IN_NAT_STEPS + 1),
        in_specs=[
            pl.BlockSpec((IN_TM, D_MODEL), lambda i, j: (jnp.minimum(i, n_real_tiles - 1), 0)),
            pl.BlockSpec((BLOCK, D_MODEL), lambda i, j: (0, 0)),
            pl.BlockSpec((1, D_MODEL), lambda i, j: (0, 0)),
            pl.BlockSpec((IN_TN, D_MODEL), lambda i, j: (jnp.minimum(j, last_nat), 0)),
            pl.BlockSpec((T_ROWS, D_MODEL), lambda i, j: (0, 0)),
        ],
        out_specs=[
            pl.BlockSpec((IN_TM, IN_TN), lambda i, j: (i, jnp.minimum(j, last_nat))),
            pl.BlockSpec((T_ROWS, IN_TM), lambda i, j: (0, i)),
            pl.BlockSpec((IN_TM, GATE_LANES), lambda i, j: (i, 0)),
        ],
        out_shape=[
            jax.ShapeDtypeStruct((rows, NAT_COLS), BF16),
            jax.ShapeDtypeStruct((T_ROWS, rows), BF16),
            jax.ShapeDtypeStruct((rows, GATE_LANES), F32),
        ],
        scratch_shapes=[pltpu.VMEM((IN_TM, D_MODEL), BF16)],
        compiler_params=_params(2),
        name="inproj",
    )(x2, lead, g, w_nat, w_qv)


GQ = ATT_GROUP * BLOCK
ATT_SLABS_PER_PHASE = 2


def _attn_prologue(n, kc_ref, kp_ref, km_ref):
    c = lax.broadcasted_iota(jnp.int32, (2 * BLOCK, BLOCK), 0)
    r = lax.broadcasted_iota(jnp.int32, (2 * BLOCK, BLOCK), 1)
    dist = r + BLOCK - c
    band_ok = (dist >= 0) & (dist < WINDOW) & ((c >= BLOCK) | (n >= 1))
    mm = lax.broadcasted_iota(jnp.int32, (N_META, BLOCK), 0)
    rr = lax.broadcasted_iota(jnp.int32, (N_META, BLOCK), 1)
    meta_ok = (n + 1) * BLOCK + rr >= N_PAD + mm
    kall = jnp.concatenate([kp_ref[...], kc_ref[...], km_ref[...]], axis=0)
    return band_ok, meta_ok, kall


def _attn_head(h, pro, sink_ref, qt_ref, vc_ref, vp_ref, vl_ref, bb_ref, bm_ref, o_ref):
    band_ok, meta_ok, kall = pro
    lo = (h // 2) * LANE
    kpair = kall[:, lo:lo + LANE]
    qh = jnp.concatenate(
        [qt_ref[(ATT_GROUP * h + g) * ATT_HEAD_DIM:(ATT_GROUP * h + g + 1) * ATT_HEAD_DIM, :]
         for g in range(ATT_GROUP)], axis=1)
    zq = jnp.zeros_like(qh)
    qz = jnp.concatenate([qh, zq] if h % 2 == 0 else [zq, qh], axis=0)
    st = jnp.dot(kpair, qz, preferred_element_type=F32)
    yield
    zpad = jnp.zeros((N_PAD, BLOCK), BF16)
    p_band, p_lead, inv = [], [], []
    for g in range(ATT_GROUP):
        gs = slice(g * BLOCK, (g + 1) * BLOCK)
        sl = st[:, gs]
        sink = jnp.full((1, BLOCK), sink_ref[ATT_GROUP * h + g], F32) * LOG2E
        lb = jnp.where(band_ok, sl[:2 * BLOCK] + bb_ref[h, :, gs], NEG * LOG2E)
        lm = jnp.where(meta_ok, sl[2 * BLOCK:] + bm_ref[0, h, :, gs], NEG * LOG2E)
        m = jnp.maximum(jnp.max(lb, axis=0, keepdims=True), jnp.max(lm, axis=0, keepdims=True))
        m = jnp.maximum(m, sink)
        pb = jnp.exp2(lb - m)
        pm = jnp.exp2(lm - m)
        den = (jnp.sum(pb, axis=0, keepdims=True) + jnp.sum(pm, axis=0, keepdims=True)
               + jnp.exp2(sink - m))
        inv.append(1.0 / den)
        p_band.append(pb.astype(BF16))
        p_lead.append(jnp.concatenate([zpad, pm.astype(BF16)], axis=0))
        if g % ATT_SLABS_PER_PHASE == ATT_SLABS_PER_PHASE - 1:
            yield
    pbt = jnp.concatenate(p_band, axis=1)
    plt = jnp.concatenate(p_lead, axis=1)
    hs = slice(h * ATT_HEAD_DIM, (h + 1) * ATT_HEAD_DIM)
    vband = jnp.concatenate([vp_ref[hs, :], vc_ref[hs, :]], axis=1)
    ot = (jnp.dot(vband, pbt, preferred_element_type=F32)
          + jnp.dot(vl_ref[hs, :], plt, preferred_element_type=F32))
    for g in range(ATT_GROUP):
        row = (ATT_GROUP * h + g) * ATT_HEAD_DIM
        o_ref[row:row + ATT_HEAD_DIM, :] = (
            ot[:, g * BLOCK:(g + 1) * BLOCK] * inv[g]).astype(BF16)


def _conv_shift_matrix():
    s = np.zeros((CONV_WIDTH * BLOCK, 2 * BLOCK), np.float32)
    for k in range(CONV_WIDTH):
        t = np.arange(BLOCK)
        s[k * BLOCK + t, BLOCK + t - k] = 1.0
    return jnp.asarray(s, BF16)


def _mlstm_gates(n, gate_ref, gb_ref):
    t_idx = lax.broadcasted_iota(jnp.int32, (BLOCK, GATE_LANES), 0)
    valid = (n > 0) | (t_idx >= N_PAD)
    pre = GATE_SOFTCAP * jnp.tanh((gate_ref[...] + gb_ref[...]) / GATE_SOFTCAP)
    log_i = jnp.where(valid, pre, NEG)
    log_sig = jnp.minimum(pre, 0.0) - jnp.log1p(jnp.exp(-jnp.abs(pre)))
    log_f = jnp.where(valid, log_sig, 0.0)
    row = lax.broadcasted_iota(jnp.int32, (BLOCK, BLOCK), 0)
    col = lax.broadcasted_iota(jnp.int32, (BLOCK, BLOCK), 1)
    causal = col <= row
    tril = jnp.where(causal, 1.0, 0.0).astype(F32)
    b_all = jnp.dot(tril, log_f, preferred_element_type=F32,
                    precision=lax.Precision.HIGHEST)
    return log_i, b_all, log_i.T, b_all.T, causal


def _mlstm_conv(p_ref, shift_ref, cw_ref, cb_ref, xprev_ref):
    xcur = p_ref[:, NAT_MQK:NAT_MV]
    xcat = jnp.concatenate([xprev_ref[...], xcur], axis=0)
    sh = jnp.dot(shift_ref[...], xcat, preferred_element_type=F32)
    xprev_ref[...] = xcur
    y = cb_ref[...]
    for k in range(CONV_WIDTH):
        y = y + cw_ref[CONV_WIDTH - 1 - k:CONV_WIDTH - k, :] * sh[k * BLOCK:(k + 1) * BLOCK]
    act = y * jax.nn.sigmoid(y)
    return act[:, :ML_QK_WIDTH] * (ML_QK_DIM ** -0.5), act[:, ML_QK_WIDTH:]


def _mlstm_head(h, gates, q_all, k_all, p_ref, mhn_ref, o_ref, c_ref, n_ref, m_ref):
    log_i, b_all, log_i_t, b_all_t, causal = gates
    q_f = q_all[:, h * ML_QK_DIM:(h + 1) * ML_QK_DIM]
    k_f = k_all[:, h * ML_QK_DIM:(h + 1) * ML_QK_DIM]
    q_h = q_f.astype(BF16)
    k_h = k_f.astype(BF16)
    v_h = p_ref[:, NAT_MV + h * ML_V_DIM:NAT_MV + (h + 1) * ML_V_DIM]
    b_col = b_all[:, ML_HEADS + h:ML_HEADS + h + 1]
    li_col = log_i[:, h:h + 1]
    b_row = b_all_t[ML_HEADS + h:ML_HEADS + h + 1, :]
    li_row = log_i_t[h:h + 1, :]
    b_last = b_all[BLOCK - 1:BLOCK, ML_HEADS + h:ML_HEADS + h + 1]
    c_prev = c_ref[h]
    n_prev = n_ref[h][0:1, :]
    m_prev = m_ref[h][0:1, 0:1]

    dmat = jnp.where(causal, b_col - b_row + li_row, -jnp.inf)
    inter = b_col + m_prev
    m_t = jnp.maximum(inter, jnp.max(dmat, axis=1, keepdims=True))
    s = lax.dot_general(q_h, k_h, CONTRACT_LAST, preferred_element_type=F32) * jnp.exp(dmat - m_t)
    yield
    a_t = jnp.exp(inter - m_t)
    num = (jnp.dot(s.astype(BF16), v_h, preferred_element_type=F32)
           + a_t * jnp.dot(q_h, c_prev.astype(BF16), preferred_element_type=F32))
    den = (jnp.sum(s, axis=1, keepdims=True)
           + a_t * jnp.sum(q_f * n_prev, axis=1, keepdims=True))
    den = jnp.maximum(jnp.abs(den), jnp.exp(-m_t))
    hh = num / den
    hh = hh * lax.rsqrt(jnp.mean(hh * hh, axis=-1, keepdims=True) + EPS)
    vs = slice(h * ML_V_DIM, (h + 1) * ML_V_DIM)
    hh = hh * mhn_ref[:, vs]
    m_o = p_ref[:, NAT_MO + h * ML_V_DIM:NAT_MO + (h + 1) * ML_V_DIM].astype(F32)
    o_ref[:, vs] = (hh * jax.nn.sigmoid(m_o)).astype(BF16)
    yield

    g_col = b_last - b_col + li_col
    m_loc = jnp.max(g_col, axis=0, keepdims=True)
    wk = jnp.exp(g_col - m_loc) * k_f
    c_loc = lax.dot_general(wk.astype(BF16), v_h, CONTRACT_FIRST,
                            preferred_element_type=F32)
    n_loc = jnp.sum(wk, axis=0, keepdims=True)
    m_new = jnp.maximum(b_last + m_prev, m_loc)
    a = jnp.exp(b_last + m_prev - m_new)
    cc = jnp.exp(m_loc - m_new)
    c_ref[h] = a * c_prev + cc * c_loc
    n_ref[h] = jnp.broadcast_to(a * n_prev + cc * n_loc, (8, ML_QK_DIM))
    m_ref[h] = jnp.broadcast_to(m_new, (8, LANE))


def _mixer_kernel(sink_ref, qt_ref, kc_ref, kp_ref, km_ref, vc_ref, vp_ref, vl_ref, bb_ref, bm_ref,
                  p_ref, gate_ref, gb_ref, shift_ref, cw_ref, cb_ref, mhn_ref,
                  wo_ref, wu_ref, wd_ref,
                  att_ref, hm_ref, wo_bf_ref, wu_bf_ref, wd_bf_ref,
                  c_ref, n_ref, m_ref, xprev_ref):
    n = pl.program_id(1)

    @pl.when(n == 0)
    def _():
        c_ref[...] = jnp.zeros_like(c_ref)
        n_ref[...] = jnp.zeros_like(n_ref)
        m_ref[...] = jnp.zeros_like(m_ref)
        xprev_ref[...] = jnp.zeros_like(xprev_ref)

    wo_bf_ref[...] = wo_ref[...].astype(BF16)
    wu_bf_ref[...] = wu_ref[...].astype(BF16)
    wd_bf_ref[...] = wd_ref[...].astype(BF16)

    att_args = (sink_ref, qt_ref, vc_ref, vp_ref, vl_ref, bb_ref, bm_ref, att_ref)
    def attention_phases():
        pro = _attn_prologue(jnp.maximum(n - 1, 0), kc_ref, kp_ref, km_ref)
        for h in range(ATT_KV_HEADS):
            yield from _attn_head(h, pro, *att_args)

    def mlstm_phases():
        gates = _mlstm_gates(n, gate_ref, gb_ref)
        yield
        q_all, k_all = _mlstm_conv(p_ref, shift_ref, cw_ref, cb_ref, xprev_ref)
        yield
        for h in range(ML_HEADS):
            yield from _mlstm_head(h, gates, q_all, k_all, p_ref, mhn_ref, hm_ref, c_ref, n_ref, m_ref)

    streams = [mlstm_phases(), attention_phases()]
    while streams:
        for s in list(streams):
            if next(s, StopIteration) is StopIteration:
                streams.remove(s)


def _mixer(sinks, nat, proj_t, gates, bias_band_t, bias_meta_t, gate_bias, conv_w, conv_b, mh_norm,
           w_out, w_up, w_down, batch, nblk):
    lead_blk = batch * nblk
    acur = lambda b, n: b * nblk + jnp.maximum(n - 1, 0)
    aprev = lambda b, n: b * nblk + jnp.maximum(n - 2, 0)
    k_col = NAT_K // ATT_KV_WIDTH
    v_row = ATT_WIDTH // ATT_KV_WIDTH
    src = lambda b, n: jnp.where(n == 0, lead_blk, b * nblk + n - 1)
    n_slices = batch * nblk
    ff_slice = D_FF // n_slices
    wo_slice = w_out.shape[0] // n_slices
    wsl = lambda b, n: jnp.minimum(b * (nblk + 1) + n, n_slices - 1)
    const2 = lambda b, n: (0, 0)
    return pl.pallas_call(
        _mixer_kernel,
        grid=(batch, nblk + 1),
        in_specs=[
            pl.BlockSpec(memory_space=pltpu.SMEM),
            pl.BlockSpec((ATT_WIDTH, BLOCK), lambda b, n: (0, acur(b, n))),
            pl.BlockSpec((BLOCK, ATT_KV_WIDTH), lambda b, n: (acur(b, n), k_col)),
            pl.BlockSpec((BLOCK, ATT_KV_WIDTH), lambda b, n: (aprev(b, n), k_col)),
            pl.BlockSpec((N_META, ATT_KV_WIDTH),
                         lambda b, n: (lead_blk * (BLOCK // N_META) + N_PAD // N_META, k_col)),
            pl.BlockSpec((ATT_KV_WIDTH, BLOCK), lambda b, n: (v_row, acur(b, n))),
            pl.BlockSpec((ATT_KV_WIDTH, BLOCK), lambda b, n: (v_row, aprev(b, n))),
            pl.BlockSpec((ATT_KV_WIDTH, BLOCK), lambda b, n: (v_row, lead_blk)),
            pl.BlockSpec((ATT_KV_HEADS, 2 * BLOCK, GQ), lambda b, n: (0, 0, 0)),
            pl.BlockSpec((1, ATT_KV_HEADS, N_META, GQ), lambda b, n: (jnp.maximum(n - 1, 0), 0, 0, 0)),
            pl.BlockSpec((BLOCK, NAT_K), lambda b, n: (src(b, n), 0)),
            pl.BlockSpec((BLOCK, GATE_LANES), lambda b, n: (src(b, n), 0)),
            pl.BlockSpec((1, GATE_LANES), const2),
            pl.BlockSpec((CONV_WIDTH * BLOCK, 2 * BLOCK), const2),
            pl.BlockSpec((CONV_WIDTH, 2 * ML_QK_WIDTH), const2),
            pl.BlockSpec((1, 2 * ML_QK_WIDTH), const2),
            pl.BlockSpec((1, ML_WIDTH), const2),
            pl.BlockSpec((wo_slice, D_MODEL), lambda b, n: (wsl(b, n), 0)),
            pl.BlockSpec((D_MODEL, ff_slice), lambda b, n: (0, wsl(b, n))),
            pl.BlockSpec((ff_slice, D_MODEL), lambda b, n: (wsl(b, n), 0)),
        ],
        out_specs=[
            pl.BlockSpec((ATT_WIDTH, BLOCK), lambda b, n: (0, acur(b, n))),
            pl.BlockSpec((BLOCK, ML_WIDTH), lambda b, n: (acur(b, n), 0)),
            pl.BlockSpec((wo_slice, D_MODEL), lambda b, n: (wsl(b, n), 0)),
            pl.BlockSpec((D_MODEL, ff_slice), lambda b, n: (0, wsl(b, n))),
            pl.BlockSpec((ff_slice, D_MODEL), lambda b, n: (wsl(b, n), 0)),
        ],
        out_shape=[
            jax.ShapeDtypeStruct((ATT_WIDTH, batch * nblk * BLOCK), BF16),
            jax.ShapeDtypeStruct((batch * nblk * BLOCK, ML_WIDTH), BF16),
            jax.ShapeDtypeStruct(w_out.shape, BF16),
            jax.ShapeDtypeStruct(w_up.shape, BF16),
            jax.ShapeDtypeStruct(w_down.shape, BF16),
        ],
        scratch_shapes=[
            pltpu.VMEM((ML_HEADS, ML_QK_DIM, ML_V_DIM), F32),
            pltpu.VMEM((ML_HEADS, 8, ML_QK_DIM), F32),
            pltpu.VMEM((ML_HEADS, 8, LANE), F32),
            pltpu.VMEM((BLOCK, 2 * ML_QK_WIDTH), BF16),
        ],
        compiler_params=_params(2),
        name="mixer",
    )(sinks, proj_t, nat, nat, nat, proj_t, proj_t, proj_t, bias_band_t, bias_meta_t,
      nat, gates, gate_bias, _conv_shift_matrix(), conv_w, conv_b, mh_norm, w_out, w_up, w_down)


OUT_TM = 512


def _outproj_kernel(att_ref, hm_ref, x_ref, wa_ref, wm_ref, g_ref, h_ref, u_ref):
    h = (x_ref[...]
         + lax.dot_general(att_ref[...], wa_ref[...], CONTRACT_FIRST, preferred_element_type=F32)
         + jnp.dot(hm_ref[...], wm_ref[...], preferred_element_type=F32))
    h_ref[...] = h
    u_ref[...] = _rms(h, g_ref[...]).astype(BF16)


def _outproj(att_t, hm, x2, w_out, g):
    rows = x2.shape[0]
    return pl.pallas_call(
        _outproj_kernel,
        grid=(rows // OUT_TM,),
        in_specs=[
            pl.BlockSpec((ATT_WIDTH, OUT_TM), lambda i: (0, i)),
            pl.BlockSpec((OUT_TM, ML_WIDTH), lambda i: (i, 0)),
            pl.BlockSpec((OUT_TM, D_MODEL), lambda i: (i, 0)),
            pl.BlockSpec((ATT_WIDTH, D_MODEL), lambda i: (0, 0)),
            pl.BlockSpec((ML_WIDTH, D_MODEL), lambda i: (ATT_WIDTH // ML_WIDTH, 0)),
            pl.BlockSpec((1, D_MODEL), lambda i: (0, 0)),
        ],
        out_specs=[
            pl.BlockSpec((OUT_TM, D_MODEL), lambda i: (i, 0)),
            pl.BlockSpec((OUT_TM, D_MODEL), lambda i: (i, 0)),
        ],
        out_shape=[
            jax.ShapeDtypeStruct((rows, D_MODEL), F32),
            jax.ShapeDtypeStruct((rows, D_MODEL), BF16),
        ],
        compiler_params=_params(1),
        name="outproj",
    )(att_t, hm, x2, w_out, w_out, g)


MLP_TM = 1024
MLP_TF = 512
MXU_COLS = 256


def _mlp_step(first, u_ref, wu_ref, wd_ref, o_ref):
    acts = []
    for c in range(MLP_TF // MXU_COLS):
        cs = slice(c * MXU_COLS, (c + 1) * MXU_COLS)
        a = jnp.dot(u_ref[...], wu_ref[:, cs], preferred_element_type=F32)
        acts.append(jnp.square(jnp.maximum(a, 0.0)).astype(BF16))
        yield
    act = jnp.concatenate(acts, axis=1)
    for c in range(D_MODEL // MXU_COLS):
        cs = slice(c * MXU_COLS, (c + 1) * MXU_COLS)
        d = jnp.dot(act, wd_ref[:, cs], preferred_element_type=F32)
        if first:
            o_ref[:, cs] = d
        else:
            o_ref[:, cs] += d
        yield


def _mlp_kernel(u_ref, h_hbm, wu_ref, wd_ref, g_ref, o_ref, hbuf_ref, hsem):
    i = pl.program_id(0)
    j = pl.program_id(1)
    r0 = pl.multiple_of(i * MLP_TM, MLP_TM)
    h_copy = pltpu.make_async_copy(h_hbm.at[pl.ds(r0, MLP_TM), :], hbuf_ref, hsem)

    @pl.when(j == 0)
    def _():
        h_copy.start()
        for _ in _mlp_step(True, u_ref, wu_ref, wd_ref, o_ref):
            pass

    @pl.when(j > 0)
    def _():
        for _ in _mlp_step(False, u_ref, wu_ref, wd_ref, o_ref):
            pass

    @pl.when(j == pl.num_programs(1) - 1)
    def _():
        h_copy.wait()
        o_ref[...] = _rms(o_ref[...] + hbuf_ref[...], g_ref[...])


def _mlp(u, h, w_up, w_down, g):
    rows = u.shape[0]
    return pl.pallas_call(
        _mlp_kernel,
        grid=(rows // MLP_TM, D_FF // MLP_TF),
        in_specs=[
            pl.BlockSpec((MLP_TM, D_MODEL), lambda i, j: (i, 0)),
            pl.BlockSpec(memory_space=pl.ANY),
            pl.BlockSpec((D_MODEL, MLP_TF), lambda i, j: (0, j)),
            pl.BlockSpec((MLP_TF, D_MODEL), lambda i, j: (j, 0)),
            pl.BlockSpec((1, D_MODEL), lambda i, j: (0, 0)),
        ],
        out_specs=pl.BlockSpec((MLP_TM, D_MODEL), lambda i, j: (i, 0)),
        out_shape=jax.ShapeDtypeStruct((rows, D_MODEL), F32),
        scratch_shapes=[pltpu.VMEM((MLP_TM, D_MODEL), F32), pltpu.SemaphoreType.DMA(())],
        compiler_params=_params(2),
        name="mlp",
    )(u, h, w_up, w_down, g)


def _t5_bucket(dist):
    max_exact = N_BUCKETS // 2
    d = jnp.maximum(dist, 0)
    ratio = jnp.maximum(d, max_exact).astype(F32) / max_exact
    large = max_exact + (jnp.log(ratio) / math.log(MAX_DISTANCE / max_exact)
                         * (N_BUCKETS - max_exact)).astype(jnp.int32)
    large = jnp.minimum(large, N_BUCKETS - 1)
    return jnp.where(d < max_exact, d, large)


BIAS_ROWS = 16
SUBLANES = 8


def _bias_kernel(rb_ref, bkt_band_ref, bkt_meta_ref, band_ref, meta_ref, splat_ref):
    def fill(b, carry):
        for hd in range(ATT_HEADS):
            splat_ref[b * ATT_HEADS + hd] = jnp.full((SUBLANES, BLOCK), rb_ref[b, hd], F32) * LOG2E
        return carry

    lax.fori_loop(0, N_BUCKETS, fill, 0)

    def lookup(bkt):
        def body(b, accs):
            hit = bkt == b
            out = []
            for hd, a in enumerate(accs):
                sp = splat_ref[b * ATT_HEADS + hd]
                out.append(jnp.where(hit, jnp.concatenate([sp] * (BIAS_ROWS // SUBLANES), axis=0), a))
            return tuple(out)
        init = tuple(jnp.zeros(bkt.shape, F32) for _ in range(ATT_HEADS))
        return lax.fori_loop(0, N_BUCKETS, body, init, unroll=4)

    def band_rows(ci, carry):
        r0 = pl.multiple_of(ci * BIAS_ROWS, BIAS_ROWS)
        accs = lookup(bkt_band_ref[pl.ds(r0, BIAS_ROWS), :])
        for hd in range(ATT_HEADS):
            h, g = divmod(hd, ATT_GROUP)
            band_ref[h, pl.ds(r0, BIAS_ROWS), g * BLOCK:(g + 1) * BLOCK] = accs[hd]
        return carry

    lax.fori_loop(0, 2 * BLOCK // BIAS_ROWS, band_rows, 0)

    def meta_block(n, carry):
        accs = lookup(bkt_meta_ref[n])
        for hd in range(ATT_HEADS):
            h, g = divmod(hd, ATT_GROUP)
            meta_ref[n, h, :, g * BLOCK:(g + 1) * BLOCK] = accs[hd]
        return carry

    lax.fori_loop(0, meta_ref.shape[0], meta_block, 0)


def _bias_tables(rel_bias, nblk):
    c = jnp.arange(2 * BLOCK)[:, None]
    r = jnp.arange(BLOCK)[None, :]
    bkt_band = _t5_bucket(r + BLOCK - c).astype(jnp.int32)
    q_pos = BLOCK + jnp.arange(nblk * BLOCK)[None, :]
    m_pos = N_PAD + jnp.arange(N_META)[:, None]
    bkt_meta = _t5_bucket(q_pos - m_pos).astype(jnp.int32)
    bkt_meta = bkt_meta.reshape(N_META, nblk, BLOCK).transpose(1, 0, 2)
    vmem = pl.BlockSpec(memory_space=pltpu.VMEM)
    return pl.pallas_call(
        _bias_kernel,
        in_specs=[pl.BlockSpec(memory_space=pltpu.SMEM), vmem, vmem],
        out_specs=[vmem, vmem],
        out_shape=[
            jax.ShapeDtypeStruct((ATT_KV_HEADS, 2 * BLOCK, GQ), F32),
            jax.ShapeDtypeStruct((nblk, ATT_KV_HEADS, N_META, GQ), F32),
        ],
        scratch_shapes=[pltpu.VMEM((N_BUCKETS * ATT_HEADS, SUBLANES, BLOCK), F32)],
        name="bias_tables",
    )(rel_bias.astype(F32), bkt_band, bkt_meta)


W_IN_OFFS = dict(q=0, k=1024, v=1280, mq=1536, mk=2048, mv=2560, mo=3584, gates=4608)
NAT_SRC = ([W_IN_OFFS["mq"] // LANE + t for t in range((W_IN_OFFS["gates"] - W_IN_OFFS["mq"]) // LANE)]
           + [W_IN_OFFS["k"] // LANE + t for t in range(ATT_KV_WIDTH // LANE)]
           + [W_IN_OFFS["gates"] // LANE])
QV_SRC = ([W_IN_OFFS["q"] // LANE + t for t in range(ATT_WIDTH // LANE)]
          + [W_IN_OFFS["v"] // LANE + t for t in range(ATT_KV_WIDTH // LANE)])
N_NAT_BLK = len(NAT_SRC)
N_Q_BLK = ATT_WIDTH // LANE


def _win_prep_kernel(src_ref, w_ref, nat_ref, qv_ref):
    j = pl.program_id(0)
    wv = w_ref[0]

    @pl.when(j < N_NAT_BLK - 1)
    def _():
        nat_ref[...] = wv.astype(BF16)

    @pl.when(j == N_NAT_BLK - 1)
    def _():
        row = lax.broadcasted_iota(jnp.int32, wv.shape, 0)
        nat_ref[...] = jnp.where(row < 2 * ML_HEADS, wv, 0.0).astype(BF16)

    @pl.when((j >= N_NAT_BLK) & (j < N_NAT_BLK + N_Q_BLK))
    def _():
        qv_ref[...] = (wv * (ATT_HEAD_DIM ** -0.5 * LOG2E)).astype(BF16)

    @pl.when(j >= N_NAT_BLK + N_Q_BLK)
    def _():
        qv_ref[...] = wv.astype(BF16)


def _win_prep(w_in_t):
    src = jnp.asarray(NAT_SRC + QV_SRC, jnp.int32)
    d = w_in_t.shape[2]
    return pl.pallas_call(
        _win_prep_kernel,
        grid_spec=pltpu.PrefetchScalarGridSpec(
            num_scalar_prefetch=1,
            grid=(len(NAT_SRC) + len(QV_SRC),),
            in_specs=[pl.BlockSpec((1, LANE, d), lambda j, src: (0, src[j], 0))],
            out_specs=[
                pl.BlockSpec((LANE, d), lambda j, src: (jnp.minimum(j, N_NAT_BLK - 1), 0)),
                pl.BlockSpec((LANE, d), lambda j, src: (jnp.maximum(j - N_NAT_BLK, 0), 0)),
            ],
        ),
        out_shape=[
            jax.ShapeDtypeStruct((NAT_COLS, d), BF16),
            jax.ShapeDtypeStruct((T_ROWS, d), BF16),
        ],
        compiler_params=_params(1),
        name="w_in_prep",
    )(src, w_in_t)


def kernel(x, meta_tokens, w_in, conv_w, conv_b, b_igate, b_fgate, attn_sinks, rel_bias,
           mh_norm, w_out, norm_mix, norm_mlp, w_up, w_down, norm_final):
    batch, seq, d = x.shape
    nblk = seq // BLOCK
    x2 = x.reshape(batch * seq, d)
    lead = jnp.concatenate([jnp.zeros((N_PAD, d), x.dtype), meta_tokens.astype(x.dtype)], axis=0)

    w_nat, w_qv = _win_prep(jnp.swapaxes(w_in, 1, 2))
    gate_bias = jnp.pad(jnp.concatenate([b_igate[0], b_fgate[0]]).astype(F32),
                        (0, GATE_LANES - 2 * ML_HEADS))[None]
    bias_band_t, bias_meta_t = _bias_tables(rel_bias, nblk)

    nat, proj_t, gates = _inproj(x2, lead, norm_mix[0][None].astype(F32), w_nat, w_qv)
    att_t, hm, w_out_bf, w_up_bf, w_down_bf = _mixer(
        attn_sinks[0].astype(F32), nat, proj_t, gates, bias_band_t, bias_meta_t, gate_bias,
        conv_w[0].astype(F32), conv_b[0][None].astype(F32), mh_norm[0][None].astype(F32),
        w_out[0], w_up[0], w_down[0], batch, nblk)
    h2, u2 = _outproj(att_t, hm, x2, w_out_bf, norm_mlp[0][None].astype(F32))
    out = _mlp(u2, h2, w_up_bf, w_down_bf, norm_final[None].astype(F32))
    return out.reshape(batch, seq, d)
```

```python
import math

import numpy as np
import jax
import jax.numpy as jnp
from jax import lax
from jax.experimental import pallas as pl
from jax.experimental.pallas import tpu as pltpu

F32 = jnp.float32
BF16 = jnp.bfloat16

D_MODEL = 2048
N_META = 16
BLOCK = 128
N_PAD = BLOCK - N_META
WINDOW = 128
ATT_HEADS = 16
ATT_KV_HEADS = 4
ATT_GROUP = ATT_HEADS // ATT_KV_HEADS
ATT_HEAD_DIM = 64
ATT_WIDTH = ATT_HEADS * ATT_HEAD_DIM
ATT_KV_WIDTH = ATT_KV_HEADS * ATT_HEAD_DIM
ML_HEADS = 4
ML_V_DIM = 256
ML_QK_DIM = 128
ML_WIDTH = ML_HEADS * ML_V_DIM
ML_QK_WIDTH = ML_HEADS * ML_QK_DIM
CONV_WIDTH = 4
GATE_SOFTCAP = 15.0
D_FF = 4 * D_MODEL
N_BUCKETS = 32
MAX_DISTANCE = 128
EPS = 1e-6
NEG = -1e30
LOG2E = math.log2(math.e)

GATE_LANES = 128
NAT_MQK = 0
NAT_MV = 2 * ML_QK_WIDTH
NAT_MO = NAT_MV + ML_WIDTH
NAT_K = NAT_MO + ML_WIDTH
NAT_COLS = NAT_K + ATT_KV_WIDTH + GATE_LANES
T_ROWS = ATT_WIDTH + ATT_KV_WIDTH

V7X_VMEM_LIMIT = 60 * 1024 * 1024
LANE = 128
SUBLANES = 8
CONTRACT_LAST = (((1,), (1,)), ((), ()))
CONTRACT_FIRST = (((0,), (0,)), ((), ()))


def _params(n_axes, vmem=V7X_VMEM_LIMIT):
    return pltpu.CompilerParams(dimension_semantics=("arbitrary",) * n_axes,
                                vmem_limit_bytes=vmem)


def _rms(x, g):
    return x * lax.rsqrt(jnp.mean(x * x, axis=-1, keepdims=True) + EPS) * g


_DONE = object()


def _drain(phases):
    for _ in phases:
        pass


def _interleave(streams, weights):
    live = list(zip(streams, weights))
    while live:
        for item in list(live):
            stream, weight = item
            for _ in range(weight):
                if next(stream, _DONE) is _DONE:
                    live.remove(item)
                    break
        yield


IN_TM = 1024
IN_NAT_STEPS = 3
IN_TN = NAT_COLS // IN_NAT_STEPS
MXU_COLS = 256


def _inproj_nat_phases(j, rows, u_ref, wn_ref, nat_ref, gate_ref):
    u = u_ref[0:rows, :]
    for c0 in range(0, IN_TN, MXU_COLS):
        cs = slice(c0, min(c0 + MXU_COLS, IN_TN))
        r = lax.dot_general(u, wn_ref[cs, :], CONTRACT_LAST, preferred_element_type=F32)
        nat_ref[0:rows, cs] = r.astype(BF16)
        if cs.stop == IN_TN:
            assert cs.stop - cs.start == GATE_LANES

            @pl.when(j == IN_NAT_STEPS - 1)
            def _():
                gate_ref[0:rows, :] = r
        yield


def _inproj_t_phases(rows, u_ref, wq_ref, t_ref):
    u = u_ref[0:rows, :]
    for c0 in range(0, T_ROWS, MXU_COLS):
        cs = slice(c0, c0 + MXU_COLS)
        t_ref[cs, 0:rows] = lax.dot_general(wq_ref[cs, :], u, CONTRACT_LAST,
                                            preferred_element_type=F32).astype(BF16)
        yield


def _inproj_kernel(x_ref, lead_ref, g_ref, wn_ref, wq_ref, nat_ref, t_ref, gate_ref, u_ref):
    i = pl.program_id(0)
    j = pl.program_id(1)
    is_real = i < pl.num_programs(0) - 1
    is_lead = jnp.logical_not(is_real)

    @pl.when((j == 0) & is_real)
    def _():
        u_ref[...] = _rms(x_ref[...], g_ref[...]).astype(BF16)

    @pl.when((j == 0) & is_lead)
    def _():
        u_ref[0:BLOCK, :] = _rms(lead_ref[...], g_ref[...]).astype(BF16)

    def project(rows):
        @pl.when(j < IN_NAT_STEPS)
        def _():
            _drain(_inproj_nat_phases(j, rows, u_ref, wn_ref, nat_ref, gate_ref))

        @pl.when(j == IN_NAT_STEPS)
        def _():
            _drain(_inproj_t_phases(rows, u_ref, wq_ref, t_ref))

    @pl.when(is_real)
    def _():
        project(IN_TM)

    @pl.when(is_lead)
    def _():
        @pl.when(j < IN_NAT_STEPS)
        def _():
            nat_ref[BLOCK:, :] = jnp.zeros((IN_TM - BLOCK, IN_TN), BF16)

        @pl.when(j == IN_NAT_STEPS - 1)
        def _():
            gate_ref[BLOCK:, :] = jnp.zeros((IN_TM - BLOCK, GATE_LANES), F32)

        @pl.when(j == IN_NAT_STEPS)
        def _():
            t_ref[:, BLOCK:] = jnp.zeros((T_ROWS, IN_TM - BLOCK), BF16)

        project(BLOCK)


def _inproj_weight_specs():
    last_nat = IN_NAT_STEPS - 1
    return [
        pl.BlockSpec((IN_TN, D_MODEL), lambda i, j: (jnp.minimum(j, last_nat), 0)),
        pl.BlockSpec((T_ROWS, D_MODEL), lambda i, j: (0, 0)),
    ]


def _inproj_out(tm, rows):
    last_nat = IN_NAT_STEPS - 1
    specs = [
        pl.BlockSpec((tm, IN_TN), lambda i, j: (i, jnp.minimum(j, last_nat))),
        pl.BlockSpec((T_ROWS, tm), lambda i, j: (0, i)),
        pl.BlockSpec((tm, GATE_LANES), lambda i, j: (i, 0)),
    ]
    shapes = [
        jax.ShapeDtypeStruct((rows, NAT_COLS), BF16),
        jax.ShapeDtypeStruct((T_ROWS, rows), BF16),
        jax.ShapeDtypeStruct((rows, GATE_LANES), F32),
    ]
    return specs, shapes


def _inproj(x2, lead, g, w_nat, w_qv, n_real_tiles):
    out_specs, out_shape = _inproj_out(IN_TM, (n_real_tiles + 1) * IN_TM)
    return pl.pallas_call(
        _inproj_kernel,
        grid=(n_real_tiles + 1, IN_NAT_STEPS + 1),
        in_specs=[
            pl.BlockSpec((IN_TM, D_MODEL), lambda i, j: (jnp.minimum(i, n_real_tiles - 1), 0)),
            pl.BlockSpec((BLOCK, D_MODEL), lambda i, j: (0, 0)),
            pl.BlockSpec((1, D_MODEL), lambda i, j: (0, 0)),
        ] + _inproj_weight_specs(),
        out_specs=out_specs,
        out_shape=out_shape,
        scratch_shapes=[pltpu.VMEM((IN_TM, D_MODEL), BF16)],
        compiler_params=_params(2),
        name="inproj",
    )(x2, lead, g, w_nat, w_qv)


GQ = ATT_GROUP * BLOCK
ATT_SLABS_PER_PHASE = 2


def _attn_prologue(n, kc_ref, kp_ref, km_ref):
    c = lax.broadcasted_iota(jnp.int32, (2 * BLOCK, BLOCK), 0)
    r = lax.broadcasted_iota(jnp.int32, (2 * BLOCK, BLOCK), 1)
    dist = r + BLOCK - c
    band_ok = (dist >= 0) & (dist < WINDOW) & ((c >= BLOCK) | (n >= 1))
    mm = lax.broadcasted_iota(jnp.int32, (N_META, BLOCK), 0)
    rr = lax.broadcasted_iota(jnp.int32, (N_META, BLOCK), 1)
    meta_ok = (n + 1) * BLOCK + rr >= N_PAD + mm
    kall = jnp.concatenate([kp_ref[...], kc_ref[...], km_ref[...]], axis=0)
    return band_ok, meta_ok, kall


def _attn_head(h, pro, sink_ref, qt_ref, vc_ref, vp_ref, vl_ref, bb_ref, bm_ref, o_ref):
    band_ok, meta_ok, kall = pro
    lo = (h // 2) * LANE
    kpair = kall[:, lo:lo + LANE]
    qh = jnp.concatenate(
        [qt_ref[(ATT_GROUP * h + g) * ATT_HEAD_DIM:(ATT_GROUP * h + g + 1) * ATT_HEAD_DIM, :]
         for g in range(ATT_GROUP)], axis=1)
    zq = jnp.zeros_like(qh)
    qz = jnp.concatenate([qh, zq] if h % 2 == 0 else [zq, qh], axis=0)
    st = jnp.dot(kpair, qz, preferred_element_type=F32)
    yield
    zpad = jnp.zeros((N_PAD, BLOCK), BF16)
    p_band, p_lead, inv = [], [], []
    for g in range(ATT_GROUP):
        gs = slice(g * BLOCK, (g + 1) * BLOCK)
        sl = st[:, gs]
        sink = jnp.full((1, BLOCK), sink_ref[ATT_GROUP * h + g], F32) * LOG2E
        lb = jnp.where(band_ok, sl[:2 * BLOCK] + bb_ref[h, :, gs], NEG * LOG2E)
        lm = jnp.where(meta_ok, sl[2 * BLOCK:] + bm_ref[0, h, :, gs], NEG * LOG2E)
        m = jnp.maximum(jnp.max(lb, axis=0, keepdims=True), jnp.max(lm, axis=0, keepdims=True))
        m = jnp.maximum(m, sink)
        pb = jnp.exp2(lb - m)
        pm = jnp.exp2(lm - m)
        den = (jnp.sum(pb, axis=0, keepdims=True) + jnp.sum(pm, axis=0, keepdims=True)
               + jnp.exp2(sink - m))
        inv.append(1.0 / den)
        p_band.append(pb.astype(BF16))
        p_lead.append(jnp.concatenate([zpad, pm.astype(BF16)], axis=0))
        if g % ATT_SLABS_PER_PHASE == ATT_SLABS_PER_PHASE - 1:
            yield
    pbt = jnp.concatenate(p_band, axis=1)
    plt = jnp.concatenate(p_lead, axis=1)
    hs = slice(h * ATT_HEAD_DIM, (h + 1) * ATT_HEAD_DIM)
    vband = jnp.concatenate([vp_ref[hs, :], vc_ref[hs, :]], axis=1)
    ot = (jnp.dot(vband, pbt, preferred_element_type=F32)
          + jnp.dot(vl_ref[hs, :], plt, preferred_element_type=F32))
    for g in range(ATT_GROUP):
        row = (ATT_GROUP * h + g) * ATT_HEAD_DIM
        o_ref[row:row + ATT_HEAD_DIM, :] = (
            ot[:, g * BLOCK:(g + 1) * BLOCK] * inv[g]).astype(BF16)


def _conv_shift_matrix():
    s = np.zeros((CONV_WIDTH * BLOCK, 2 * BLOCK), np.float32)
    for k in range(CONV_WIDTH):
        t = np.arange(BLOCK)
        s[k * BLOCK + t, BLOCK + t - k] = 1.0
    return jnp.asarray(s, BF16)


def _mlstm_gates(n, gate_ref, gb_ref):
    t_idx = lax.broadcasted_iota(jnp.int32, (BLOCK, GATE_LANES), 0)
    valid = (n > 0) | (t_idx >= N_PAD)
    pre = GATE_SOFTCAP * jnp.tanh((gate_ref[...] + gb_ref[...]) / GATE_SOFTCAP)
    log_i = jnp.where(valid, pre, NEG)
    log_sig = jnp.minimum(pre, 0.0) - jnp.log1p(jnp.exp(-jnp.abs(pre)))
    log_f = jnp.where(valid, log_sig, 0.0)
    row = lax.broadcasted_iota(jnp.int32, (BLOCK, BLOCK), 0)
    col = lax.broadcasted_iota(jnp.int32, (BLOCK, BLOCK), 1)
    causal = col <= row
    tril = jnp.where(causal, 1.0, 0.0).astype(F32)
    b_all = jnp.dot(tril, log_f, preferred_element_type=F32,
                    precision=lax.Precision.HIGHEST)
    return log_i, b_all, log_i.T, b_all.T, causal


def _mlstm_conv(p_ref, shift_ref, cw_ref, cb_ref, xprev_ref):
    xcur = p_ref[:, NAT_MQK:NAT_MV]
    xcat = jnp.concatenate([xprev_ref[...], xcur], axis=0)
    sh = jnp.dot(shift_ref[...], xcat, preferred_element_type=F32)
    xprev_ref[...] = xcur
    y = cb_ref[...]
    for k in range(CONV_WIDTH):
        y = y + cw_ref[CONV_WIDTH - 1 - k:CONV_WIDTH - k, :] * sh[k * BLOCK:(k + 1) * BLOCK]
    act = y * jax.nn.sigmoid(y)
    return act[:, :ML_QK_WIDTH] * (ML_QK_DIM ** -0.5), act[:, ML_QK_WIDTH:]


def _mlstm_head(h, gates, q_all, k_all, p_ref, mhn_ref, o_ref, c_ref, n_ref, m_ref):
    log_i, b_all, log_i_t, b_all_t, causal = gates
    q_f = q_all[:, h * ML_QK_DIM:(h + 1) * ML_QK_DIM]
    k_f = k_all[:, h * ML_QK_DIM:(h + 1) * ML_QK_DIM]
    q_h = q_f.astype(BF16)
    k_h = k_f.astype(BF16)
    v_h = p_ref[:, NAT_MV + h * ML_V_DIM:NAT_MV + (h + 1) * ML_V_DIM]
    b_col = b_all[:, ML_HEADS + h:ML_HEADS + h + 1]
    li_col = log_i[:, h:h + 1]
    b_row = b_all_t[ML_HEADS + h:ML_HEADS + h + 1, :]
    li_row = log_i_t[h:h + 1, :]
    b_last = b_all[BLOCK - 1:BLOCK, ML_HEADS + h:ML_HEADS + h + 1]
    c_prev = c_ref[h]
    n_prev = n_ref[h][0:1, :]
    m_prev = m_ref[h][0:1, 0:1]

    dmat = jnp.where(causal, b_col - b_row + li_row, -jnp.inf)
    inter = b_col + m_prev
    m_t = jnp.maximum(inter, jnp.max(dmat, axis=1, keepdims=True))
    s = lax.dot_general(q_h, k_h, CONTRACT_LAST, preferred_element_type=F32) * jnp.exp(dmat - m_t)
    yield
    a_t = jnp.exp(inter - m_t)
    num = (jnp.dot(s.astype(BF16), v_h, preferred_element_type=F32)
           + a_t * jnp.dot(q_h, c_prev.astype(BF16), preferred_element_type=F32))
    den = (jnp.sum(s, axis=1, keepdims=True)
           + a_t * jnp.sum(q_f * n_prev, axis=1, keepdims=True))
    den = jnp.maximum(jnp.abs(den), jnp.exp(-m_t))
    hh = num / den
    hh = hh * lax.rsqrt(jnp.mean(hh * hh, axis=-1, keepdims=True) + EPS)
    vs = slice(h * ML_V_DIM, (h + 1) * ML_V_DIM)
    hh = hh * mhn_ref[:, vs]
    m_o = p_ref[:, NAT_MO + h * ML_V_DIM:NAT_MO + (h + 1) * ML_V_DIM].astype(F32)
    o_ref[:, vs] = (hh * jax.nn.sigmoid(m_o)).astype(BF16)
    yield

    g_col = b_last - b_col + li_col
    m_loc = jnp.max(g_col, axis=0, keepdims=True)
    wk = jnp.exp(g_col - m_loc) * k_f
    c_loc = lax.dot_general(wk.astype(BF16), v_h, CONTRACT_FIRST,
                            preferred_element_type=F32)
    n_loc = jnp.sum(wk, axis=0, keepdims=True)
    m_new = jnp.maximum(b_last + m_prev, m_loc)
    a = jnp.exp(b_last + m_prev - m_new)
    cc = jnp.exp(m_loc - m_new)
    c_ref[h] = a * c_prev + cc * c_loc
    n_ref[h] = jnp.broadcast_to(a * n_prev + cc * n_loc, (8, ML_QK_DIM))
    m_ref[h] = jnp.broadcast_to(m_new, (8, LANE))


N_MIX_IN = 17
N_STATE = 4
INPROJ_PHASES_PER_ROUND = 1
MIXER_PHASES_PER_ROUND = 3


def _mixer_phases(n, mix_in, att_ref, hm_ref, state):
    (sink_ref, qt_ref, kc_ref, kp_ref, km_ref, vc_ref, vp_ref, vl_ref, bb_ref, bm_ref,
     p_ref, gate_ref, gb_ref, shift_ref, cw_ref, cb_ref, mhn_ref) = mix_in
    c_ref, n_ref, m_ref, xprev_ref = state

    def attention_phases():
        pro = _attn_prologue(n - 1, kc_ref, kp_ref, km_ref)
        for h in range(ATT_KV_HEADS):
            yield from _attn_head(h, pro, sink_ref, qt_ref, vc_ref, vp_ref, vl_ref, bb_ref, bm_ref,
                                  att_ref)

    def mlstm_phases():
        gates = _mlstm_gates(n, gate_ref, gb_ref)
        yield
        q_all, k_all = _mlstm_conv(p_ref, shift_ref, cw_ref, cb_ref, xprev_ref)
        yield
        for h in range(ML_HEADS):
            yield from _mlstm_head(h, gates, q_all, k_all, p_ref, mhn_ref, hm_ref, c_ref, n_ref, m_ref)

    return _interleave([mlstm_phases(), attention_phases()], [1, 1])


def _load_state(state, state_in):
    for dst, src in zip(state, state_in):
        dst[...] = src[...]


def _state_shapes():
    return [
        jax.ShapeDtypeStruct((ML_HEADS, ML_QK_DIM, ML_V_DIM), F32),
        jax.ShapeDtypeStruct((ML_HEADS, SUBLANES, ML_QK_DIM), F32),
        jax.ShapeDtypeStruct((ML_HEADS, SUBLANES, LANE), F32),
        jax.ShapeDtypeStruct((BLOCK, 2 * ML_QK_WIDTH), BF16),
    ]


def _state_scratch():
    return [pltpu.VMEM(s.shape, s.dtype) for s in _state_shapes()]


def _whole(shape):
    return pl.BlockSpec(shape, lambda *ids: (0,) * len(shape))


def _mixer_in_specs(step, lead_blk):
    cur = lambda *ids: step(*ids)
    prev = lambda *ids: jnp.maximum(step(*ids) - 1, 0)
    k_col = NAT_K // ATT_KV_WIDTH
    v_row = ATT_WIDTH // ATT_KV_WIDTH
    meta_rows = lead_blk * (BLOCK // N_META) + N_PAD // N_META
    return [
        pl.BlockSpec(memory_space=pltpu.SMEM),
        pl.BlockSpec((ATT_WIDTH, BLOCK), lambda *ids: (0, cur(*ids))),
        pl.BlockSpec((BLOCK, ATT_KV_WIDTH), lambda *ids: (cur(*ids), k_col)),
        pl.BlockSpec((BLOCK, ATT_KV_WIDTH), lambda *ids: (prev(*ids), k_col)),
        pl.BlockSpec((N_META, ATT_KV_WIDTH), lambda *ids: (meta_rows, k_col)),
        pl.BlockSpec((ATT_KV_WIDTH, BLOCK), lambda *ids: (v_row, cur(*ids))),
        pl.BlockSpec((ATT_KV_WIDTH, BLOCK), lambda *ids: (v_row, prev(*ids))),
        pl.BlockSpec((ATT_KV_WIDTH, BLOCK), lambda *ids: (v_row, lead_blk)),
        _whole((ATT_KV_HEADS, 2 * BLOCK, GQ)),
        pl.BlockSpec((1, ATT_KV_HEADS, N_META, GQ), lambda *ids: (cur(*ids), 0, 0, 0)),
        pl.BlockSpec((BLOCK, NAT_K), lambda *ids: (cur(*ids), 0)),
        pl.BlockSpec((BLOCK, GATE_LANES), lambda *ids: (cur(*ids), 0)),
        _whole((1, GATE_LANES)),
        _whole((CONV_WIDTH * BLOCK, 2 * BLOCK)),
        _whole((CONV_WIDTH, 2 * ML_QK_WIDTH)),
        _whole((1, 2 * ML_QK_WIDTH)),
        _whole((1, ML_WIDTH)),
    ] + [_whole(s.shape) for s in _state_shapes()]


def _mixer_args(consts, nat_b, projt_b, gates_b, nat_lead, projt_lead, state):
    sinks, bias_band_t, bias_meta_t, gate_bias, shift, conv_w, conv_b, mh_norm = consts
    return [sinks, projt_b, nat_b, nat_b, nat_lead, projt_b, projt_b, projt_lead, bias_band_t,
            bias_meta_t, nat_b, gates_b, gate_bias, shift, conv_w, conv_b, mh_norm] + list(state)


def _mixer_out(step, n_rows):
    specs = [
        pl.BlockSpec((ATT_WIDTH, BLOCK), lambda *ids: (0, step(*ids))),
        pl.BlockSpec((BLOCK, ML_WIDTH), lambda *ids: (step(*ids), 0)),
    ]
    shapes = [
        jax.ShapeDtypeStruct((ATT_WIDTH, n_rows), BF16),
        jax.ShapeDtypeStruct((n_rows, ML_WIDTH), BF16),
    ]
    return specs, shapes


def _lead_state_kernel(p_ref, gate_ref, gb_ref, shift_ref, cw_ref, cb_ref, mhn_ref,
                       c_ref, n_ref, m_ref, xprev_ref, hm_ref):
    for ref in (c_ref, n_ref, m_ref, xprev_ref):
        ref[...] = jnp.zeros_like(ref)
    gates = _mlstm_gates(0, gate_ref, gb_ref)
    q_all, k_all = _mlstm_conv(p_ref, shift_ref, cw_ref, cb_ref, xprev_ref)
    for h in range(ML_HEADS):
        _drain(_mlstm_head(h, gates, q_all, k_all, p_ref, mhn_ref, hm_ref, c_ref, n_ref, m_ref))


def _lead_state(nat_lead, gates_lead, lead_blk, consts):
    _, _, _, gate_bias, shift, conv_w, conv_b, mh_norm = consts
    shapes = _state_shapes()
    return pl.pallas_call(
        _lead_state_kernel,
        grid=(1,),
        in_specs=[
            pl.BlockSpec((BLOCK, NAT_K), lambda i: (lead_blk, 0)),
            pl.BlockSpec((BLOCK, GATE_LANES), lambda i: (lead_blk, 0)),
            _whole((1, GATE_LANES)),
            _whole((CONV_WIDTH * BLOCK, 2 * BLOCK)),
            _whole((CONV_WIDTH, 2 * ML_QK_WIDTH)),
            _whole((1, 2 * ML_QK_WIDTH)),
            _whole((1, ML_WIDTH)),
        ],
        out_specs=[_whole(s.shape) for s in shapes],
        out_shape=shapes,
        scratch_shapes=[pltpu.VMEM((BLOCK, ML_WIDTH), BF16)],
        compiler_params=_params(1),
        name="lead_state",
    )(nat_lead, gates_lead, gate_bias, shift, conv_w, conv_b, mh_norm)


FUSED_TM = 512


def _inproj_mixer_kernel(*refs):
    x_ref, g_ref, wn_ref, wq_ref = refs[:4]
    mix_in = refs[4:4 + N_MIX_IN]
    state_in = refs[4 + N_MIX_IN:4 + N_MIX_IN + N_STATE]
    nat_ref, t_ref, gate_ref, att_ref, hm_ref = refs[4 + N_MIX_IN + N_STATE:-(1 + N_STATE)]
    u_ref = refs[-(1 + N_STATE)]
    state = refs[-N_STATE:]
    i = pl.program_id(0)
    j = pl.program_id(1)
    step = i * (IN_NAT_STEPS + 1) + j

    @pl.when(j == 0)
    def _():
        u_ref[...] = _rms(x_ref[...], g_ref[...]).astype(BF16)

    @pl.when(step == 0)
    def _():
        _load_state(state, state_in)

    weights = [INPROJ_PHASES_PER_ROUND, MIXER_PHASES_PER_ROUND]

    @pl.when(j < IN_NAT_STEPS)
    def _():
        _drain(_interleave([_inproj_nat_phases(j, FUSED_TM, u_ref, wn_ref, nat_ref, gate_ref),
                            _mixer_phases(step + 1, mix_in, att_ref, hm_ref, state)], weights))

    @pl.when(j == IN_NAT_STEPS)
    def _():
        _drain(_interleave([_inproj_t_phases(FUSED_TM, u_ref, wq_ref, t_ref),
                            _mixer_phases(step + 1, mix_in, att_ref, hm_ref, state)], weights))


def _inproj_mixer(x2, row_tile0, g, w_nat, w_qv, mixer_args, lead_blk, nblk):
    n_tiles = nblk * BLOCK // FUSED_TM
    assert n_tiles * (IN_NAT_STEPS + 1) == nblk
    step = lambda i, j: i * (IN_NAT_STEPS + 1) + j
    proj_specs, proj_shapes = _inproj_out(FUSED_TM, n_tiles * FUSED_TM)
    mix_specs, mix_shapes = _mixer_out(step, nblk * BLOCK)
    return pl.pallas_call(
        _inproj_mixer_kernel,
        grid=(n_tiles, IN_NAT_STEPS + 1),
        in_specs=[
            pl.BlockSpec((FUSED_TM, D_MODEL), lambda i, j: (row_tile0 + i, 0)),
            pl.BlockSpec((1, D_MODEL), lambda i, j: (0, 0)),
        ] + _inproj_weight_specs() + _mixer_in_specs(step, lead_blk),
        out_specs=proj_specs + mix_specs,
        out_shape=proj_shapes + mix_shapes,
        scratch_shapes=[pltpu.VMEM((FUSED_TM, D_MODEL), BF16)] + _state_scratch(),
        compiler_params=_params(2),
        name="inproj_mixer",
    )(x2, g, w_nat, w_qv, *mixer_args)


def _mixer_kernel(*refs):
    mix_in = refs[:N_MIX_IN]
    state_in = refs[N_MIX_IN:N_MIX_IN + N_STATE]
    wo_ref, wu_ref, wd_ref = refs[N_MIX_IN + N_STATE:N_MIX_IN + N_STATE + 3]
    att_ref, hm_ref, wo_bf_ref, wu_bf_ref, wd_bf_ref = refs[N_MIX_IN + N_STATE + 3:-N_STATE]
    state = refs[-N_STATE:]
    step = pl.program_id(0)

    @pl.when(step == 0)
    def _():
        _load_state(state, state_in)

    wo_bf_ref[...] = wo_ref[...].astype(BF16)
    wu_bf_ref[...] = wu_ref[...].astype(BF16)
    wd_bf_ref[...] = wd_ref[...].astype(BF16)
    _drain(_mixer_phases(step + 1, mix_in, att_ref, hm_ref, state))


def _mixer(mixer_args, w_out, w_up, w_down, lead_blk, nblk):
    step = lambda s: s
    ff_slice = D_FF // nblk
    wo_slice = w_out.shape[0] // nblk
    mix_specs, mix_shapes = _mixer_out(step, nblk * BLOCK)
    w_specs = [
        pl.BlockSpec((wo_slice, D_MODEL), lambda s: (s, 0)),
        pl.BlockSpec((D_MODEL, ff_slice), lambda s: (0, s)),
        pl.BlockSpec((ff_slice, D_MODEL), lambda s: (s, 0)),
    ]
    return pl.pallas_call(
        _mixer_kernel,
        grid=(nblk,),
        in_specs=_mixer_in_specs(step, lead_blk) + w_specs,
        out_specs=mix_specs + w_specs,
        out_shape=mix_shapes + [jax.ShapeDtypeStruct(w.shape, BF16) for w in (w_out, w_up, w_down)],
        scratch_shapes=_state_scratch(),
        compiler_params=_params(1),
        name="mixer",
    )(*mixer_args, w_out, w_up, w_down)


OUT_TM = 512


def _outproj_kernel(att0_ref, att1_ref, hm0_ref, hm1_ref, x_ref, wa_ref, wm_ref, g_ref,
                    h_ref, u_ref):
    i = pl.program_id(0)
    half = pl.num_programs(0) // 2

    def body(att_ref, hm_ref):
        h = (x_ref[...]
             + lax.dot_general(att_ref[...], wa_ref[...], CONTRACT_FIRST, preferred_element_type=F32)
             + jnp.dot(hm_ref[...], wm_ref[...], preferred_element_type=F32))
        h_ref[...] = h
        u_ref[...] = _rms(h, g_ref[...]).astype(BF16)

    @pl.when(i < half)
    def _():
        body(att0_ref, hm0_ref)

    @pl.when(i >= half)
    def _():
        body(att1_ref, hm1_ref)


def _outproj(att0, att1, hm0, hm1, x2, w_out, g):
    rows = x2.shape[0]
    n = rows // OUT_TM
    half = n // 2
    first = lambda i: jnp.minimum(i, half - 1)
    second = lambda i: jnp.maximum(i - half, 0)
    return pl.pallas_call(
        _outproj_kernel,
        grid=(n,),
        in_specs=[
            pl.BlockSpec((ATT_WIDTH, OUT_TM), lambda i: (0, first(i))),
            pl.BlockSpec((ATT_WIDTH, OUT_TM), lambda i: (0, second(i))),
            pl.BlockSpec((OUT_TM, ML_WIDTH), lambda i: (first(i), 0)),
            pl.BlockSpec((OUT_TM, ML_WIDTH), lambda i: (second(i), 0)),
            pl.BlockSpec((OUT_TM, D_MODEL), lambda i: (i, 0)),
            pl.BlockSpec((ATT_WIDTH, D_MODEL), lambda i: (0, 0)),
            pl.BlockSpec((ML_WIDTH, D_MODEL), lambda i: (ATT_WIDTH // ML_WIDTH, 0)),
            pl.BlockSpec((1, D_MODEL), lambda i: (0, 0)),
        ],
        out_specs=[
            pl.BlockSpec((OUT_TM, D_MODEL), lambda i: (i, 0)),
            pl.BlockSpec((OUT_TM, D_MODEL), lambda i: (i, 0)),
        ],
        out_shape=[
            jax.ShapeDtypeStruct((rows, D_MODEL), F32),
            jax.ShapeDtypeStruct((rows, D_MODEL), BF16),
        ],
        compiler_params=_params(1),
        name="outproj",
    )(att0, att1, hm0, hm1, x2, w_out, w_out, g)


MLP_TM = 1024
MLP_TF = 512


def _mlp_step(first, u_ref, wu_ref, wd_ref, o_ref):
    acts = []
    for c in range(MLP_TF // MXU_COLS):
        cs = slice(c * MXU_COLS, (c + 1) * MXU_COLS)
        a = jnp.dot(u_ref[...], wu_ref[:, cs], preferred_element_type=F32)
        acts.append(jnp.square(jnp.maximum(a, 0.0)).astype(BF16))
        yield
    act = jnp.concatenate(acts, axis=1)
    for c in range(D_MODEL // MXU_COLS):
        cs = slice(c * MXU_COLS, (c + 1) * MXU_COLS)
        d = jnp.dot(act, wd_ref[:, cs], preferred_element_type=F32)
        if first:
            o_ref[:, cs] = d
        else:
            o_ref[:, cs] += d
        yield


def _mlp_kernel(u_ref, h_hbm, wu_ref, wd_ref, g_ref, o_ref, hbuf_ref, hsem):
    i = pl.program_id(0)
    j = pl.program_id(1)
    r0 = pl.multiple_of(i * MLP_TM, MLP_TM)
    h_copy = pltpu.make_async_copy(h_hbm.at[pl.ds(r0, MLP_TM), :], hbuf_ref, hsem)

    @pl.when(j == 0)
    def _():
        h_copy.start()
        for _ in _mlp_step(True, u_ref, wu_ref, wd_ref, o_ref):
            pass

    @pl.when(j > 0)
    def _():
        for _ in _mlp_step(False, u_ref, wu_ref, wd_ref, o_ref):
            pass

    @pl.when(j == pl.num_programs(1) - 1)
    def _():
        h_copy.wait()
        o_ref[...] = _rms(o_ref[...] + hbuf_ref[...], g_ref[...])


def _mlp(u, h, w_up, w_down, g):
    rows = u.shape[0]
    return pl.pallas_call(
        _mlp_kernel,
        grid=(rows // MLP_TM, D_FF // MLP_TF),
        in_specs=[
            pl.BlockSpec((MLP_TM, D_MODEL), lambda i, j: (i, 0)),
            pl.BlockSpec(memory_space=pl.ANY),
            pl.BlockSpec((D_MODEL, MLP_TF), lambda i, j: (0, j)),
            pl.BlockSpec((MLP_TF, D_MODEL), lambda i, j: (j, 0)),
            pl.BlockSpec((1, D_MODEL), lambda i, j: (0, 0)),
        ],
        out_specs=pl.BlockSpec((MLP_TM, D_MODEL), lambda i, j: (i, 0)),
        out_shape=jax.ShapeDtypeStruct((rows, D_MODEL), F32),
        scratch_shapes=[pltpu.VMEM((MLP_TM, D_MODEL), F32), pltpu.SemaphoreType.DMA(())],
        compiler_params=_params(2),
        name="mlp",
    )(u, h, w_up, w_down, g)


def _t5_bucket(dist):
    max_exact = N_BUCKETS // 2
    d = jnp.maximum(dist, 0)
    ratio = jnp.maximum(d, max_exact).astype(F32) / max_exact
    large = max_exact + (jnp.log(ratio) / math.log(MAX_DISTANCE / max_exact)
                         * (N_BUCKETS - max_exact)).astype(jnp.int32)
    large = jnp.minimum(large, N_BUCKETS - 1)
    return jnp.where(d < max_exact, d, large)


BIAS_ROWS = 16


def _bias_kernel(rb_ref, bkt_band_ref, bkt_meta_ref, band_ref, meta_ref, splat_ref):
    def fill(b, carry):
        for hd in range(ATT_HEADS):
            splat_ref[b * ATT_HEADS + hd] = jnp.full((SUBLANES, BLOCK), rb_ref[b, hd], F32) * LOG2E
        return carry

    lax.fori_loop(0, N_BUCKETS, fill, 0)

    def lookup(bkt):
        def body(b, accs):
            hit = bkt == b
            out = []
            for hd, a in enumerate(accs):
                sp = splat_ref[b * ATT_HEADS + hd]
                out.append(jnp.where(hit, jnp.concatenate([sp] * (BIAS_ROWS // SUBLANES), axis=0), a))
            return tuple(out)
        init = tuple(jnp.zeros(bkt.shape, F32) for _ in range(ATT_HEADS))
        return lax.fori_loop(0, N_BUCKETS, body, init, unroll=4)

    def band_rows(ci, carry):
        r0 = pl.multiple_of(ci * BIAS_ROWS, BIAS_ROWS)
        accs = lookup(bkt_band_ref[pl.ds(r0, BIAS_ROWS), :])
        for hd in range(ATT_HEADS):
            h, g = divmod(hd, ATT_GROUP)
            band_ref[h, pl.ds(r0, BIAS_ROWS), g * BLOCK:(g + 1) * BLOCK] = accs[hd]
        return carry

    lax.fori_loop(0, 2 * BLOCK // BIAS_ROWS, band_rows, 0)

    def meta_block(n, carry):
        accs = lookup(bkt_meta_ref[n])
        for hd in range(ATT_HEADS):
            h, g = divmod(hd, ATT_GROUP)
            meta_ref[n, h, :, g * BLOCK:(g + 1) * BLOCK] = accs[hd]
        return carry

    lax.fori_loop(0, meta_ref.shape[0], meta_block, 0)


def _bias_tables(rel_bias, nblk):
    c = jnp.arange(2 * BLOCK)[:, None]
    r = jnp.arange(BLOCK)[None, :]
    bkt_band = _t5_bucket(r + BLOCK - c).astype(jnp.int32)
    q_pos = BLOCK + jnp.arange(nblk * BLOCK)[None, :]
    m_pos = N_PAD + jnp.arange(N_META)[:, None]
    bkt_meta = _t5_bucket(q_pos - m_pos).astype(jnp.int32)
    bkt_meta = bkt_meta.reshape(N_META, nblk, BLOCK).transpose(1, 0, 2)
    vmem = pl.BlockSpec(memory_space=pltpu.VMEM)
    return pl.pallas_call(
        _bias_kernel,
        in_specs=[pl.BlockSpec(memory_space=pltpu.SMEM), vmem, vmem],
        out_specs=[vmem, vmem],
        out_shape=[
            jax.ShapeDtypeStruct((ATT_KV_HEADS, 2 * BLOCK, GQ), F32),
            jax.ShapeDtypeStruct((nblk, ATT_KV_HEADS, N_META, GQ), F32),
        ],
        scratch_shapes=[pltpu.VMEM((N_BUCKETS * ATT_HEADS, SUBLANES, BLOCK), F32)],
        name="bias_tables",
    )(rel_bias.astype(F32), bkt_band, bkt_meta)


W_IN_OFFS = dict(q=0, k=1024, v=1280, mq=1536, mk=2048, mv=2560, mo=3584, gates=4608)
NAT_SRC = ([W_IN_OFFS["mq"] // LANE + t for t in range((W_IN_OFFS["gates"] - W_IN_OFFS["mq"]) // LANE)]
           + [W_IN_OFFS["k"] // LANE + t for t in range(ATT_KV_WIDTH // LANE)]
           + [W_IN_OFFS["gates"] // LANE])
QV_SRC = ([W_IN_OFFS["q"] // LANE + t for t in range(ATT_WIDTH // LANE)]
          + [W_IN_OFFS["v"] // LANE + t for t in range(ATT_KV_WIDTH // LANE)])
N_NAT_BLK = len(NAT_SRC)
N_Q_BLK = ATT_WIDTH // LANE


def _win_prep_kernel(src_ref, w_ref, nat_ref, qv_ref):
    j = pl.program_id(0)
    wv = w_ref[0]

    @pl.when(j < N_NAT_BLK - 1)
    def _():
        nat_ref[...] = wv.astype(BF16)

    @pl.when(j == N_NAT_BLK - 1)
    def _():
        row = lax.broadcasted_iota(jnp.int32, wv.shape, 0)
        nat_ref[...] = jnp.where(row < 2 * ML_HEADS, wv, 0.0).astype(BF16)

    @pl.when((j >= N_NAT_BLK) & (j < N_NAT_BLK + N_Q_BLK))
    def _():
        qv_ref[...] = (wv * (ATT_HEAD_DIM ** -0.5 * LOG2E)).astype(BF16)

    @pl.when(j >= N_NAT_BLK + N_Q_BLK)
    def _():
        qv_ref[...] = wv.astype(BF16)


def _win_prep(w_in_t):
    src = jnp.asarray(NAT_SRC + QV_SRC, jnp.int32)
    d = w_in_t.shape[2]
    return pl.pallas_call(
        _win_prep_kernel,
        grid_spec=pltpu.PrefetchScalarGridSpec(
            num_scalar_prefetch=1,
            grid=(len(NAT_SRC) + len(QV_SRC),),
            in_specs=[pl.BlockSpec((1, LANE, d), lambda j, src: (0, src[j], 0))],
            out_specs=[
                pl.BlockSpec((LANE, d), lambda j, src: (jnp.minimum(j, N_NAT_BLK - 1), 0)),
                pl.BlockSpec((LANE, d), lambda j, src: (jnp.maximum(j - N_NAT_BLK, 0), 0)),
            ],
        ),
        out_shape=[
            jax.ShapeDtypeStruct((NAT_COLS, d), BF16),
            jax.ShapeDtypeStruct((T_ROWS, d), BF16),
        ],
        compiler_params=_params(1),
        name="w_in_prep",
    )(src, w_in_t)


def kernel(x, meta_tokens, w_in, conv_w, conv_b, b_igate, b_fgate, attn_sinks, rel_bias,
           mh_norm, w_out, norm_mix, norm_mlp, w_up, w_down, norm_final):
    batch, seq, d = x.shape
    assert batch == 2, "the two batches are processed one projection / mixer stage apart"
    nblk = seq // BLOCK
    x2 = x.reshape(batch * seq, d)
    lead = jnp.concatenate([jnp.zeros((N_PAD, d), x.dtype), meta_tokens.astype(x.dtype)], axis=0)
    g_mix = norm_mix[0][None].astype(F32)

    w_nat, w_qv = _win_prep(jnp.swapaxes(w_in, 1, 2))
    gate_bias = jnp.pad(jnp.concatenate([b_igate[0], b_fgate[0]]).astype(F32),
                        (0, GATE_LANES - 2 * ML_HEADS))[None]
    bias_band_t, bias_meta_t = _bias_tables(rel_bias, nblk)
    consts = (attn_sinks[0].astype(F32), bias_band_t, bias_meta_t, gate_bias, _conv_shift_matrix(),
              conv_w[0].astype(F32), conv_b[0][None].astype(F32), mh_norm[0][None].astype(F32))

    nat0, projt0, gates0 = _inproj(x2, lead, g_mix, w_nat, w_qv, seq // IN_TM)
    lead_blk = seq // BLOCK
    state = _lead_state(nat0, gates0, lead_blk, consts)
    args0 = _mixer_args(consts, nat0, projt0, gates0, nat0, projt0, state)
    nat1, projt1, gates1, att0, hm0 = _inproj_mixer(
        x2, seq // FUSED_TM, g_mix, w_nat, w_qv, args0, lead_blk, nblk)
    args1 = _mixer_args(consts, nat1, projt1, gates1, nat0, projt0, state)
    att1, hm1, w_out_bf, w_up_bf, w_down_bf = _mixer(args1, w_out[0], w_up[0], w_down[0], lead_blk, nblk)

    h2, u2 = _outproj(att0, att1, hm0, hm1, x2, w_out_bf, norm_mlp[0][None].astype(F32))
    out = _mlp(u2, h2, w_up_bf, w_down_bf, norm_final[None].astype(F32))
    return out.reshape(batch, seq, d)
```

```python
import math

import numpy as np
import jax
import jax.numpy as jnp
from jax import lax
from jax.experimental import pallas as pl
from jax.experimental.pallas import tpu as pltpu

F32 = jnp.float32
BF16 = jnp.bfloat16

D_MODEL = 2048
N_META = 16
BLOCK = 128
N_PAD = BLOCK - N_META
WINDOW = 128
ATT_HEADS = 16
ATT_KV_HEADS = 4
ATT_GROUP = ATT_HEADS // ATT_KV_HEADS
ATT_HEAD_DIM = 64
ATT_WIDTH = ATT_HEADS * ATT_HEAD_DIM
ATT_KV_WIDTH = ATT_KV_HEADS * ATT_HEAD_DIM
ML_HEADS = 4
ML_V_DIM = 256
ML_QK_DIM = 128
ML_WIDTH = ML_HEADS * ML_V_DIM
ML_QK_WIDTH = ML_HEADS * ML_QK_DIM
CONV_WIDTH = 4
GATE_SOFTCAP = 15.0
D_FF = 4 * D_MODEL
N_BUCKETS = 32
MAX_DISTANCE = 128
EPS = 1e-6
NEG = -1e30
LOG2E = math.log2(math.e)

GATE_LANES = 128
NAT_MQK = 0
NAT_MV = 2 * ML_QK_WIDTH
NAT_MO = NAT_MV + ML_WIDTH
NAT_K = NAT_MO + ML_WIDTH
NAT_COLS = NAT_K + ATT_KV_WIDTH + GATE_LANES
T_ROWS = ATT_WIDTH + ATT_KV_WIDTH

V7X_VMEM_LIMIT = 60 * 1024 * 1024
LANE = 128
SUBLANES = 8
CONTRACT_LAST = (((1,), (1,)), ((), ()))
CONTRACT_FIRST = (((0,), (0,)), ((), ()))


def _params(n_axes, vmem=V7X_VMEM_LIMIT):
    return pltpu.CompilerParams(dimension_semantics=("arbitrary",) * n_axes,
                                vmem_limit_bytes=vmem)


def _rms(x, g):
    return x * lax.rsqrt(jnp.mean(x * x, axis=-1, keepdims=True) + EPS) * g


_DONE = object()


def _drain(phases):
    for _ in phases:
        pass


def _interleave(streams, weights):
    live = list(zip(streams, weights))
    while live:
        for item in list(live):
            stream, weight = item
            for _ in range(weight):
                if next(stream, _DONE) is _DONE:
                    live.remove(item)
                    break
        yield


IN_TM = 1024
IN_NAT_STEPS = 3
IN_TN = NAT_COLS // IN_NAT_STEPS
MXU_COLS = 256


def _inproj_nat_phases(j, rows, u_ref, wn_ref, nat_ref, gate_ref):
    u = u_ref[0:rows, :]
    for c0 in range(0, IN_TN, MXU_COLS):
        cs = slice(c0, min(c0 + MXU_COLS, IN_TN))
        r = lax.dot_general(u, wn_ref[cs, :], CONTRACT_LAST, preferred_element_type=F32)
        nat_ref[0:rows, cs] = r.astype(BF16)
        if cs.stop == IN_TN:
            assert cs.stop - cs.start == GATE_LANES

            @pl.when(j == IN_NAT_STEPS - 1)
            def _():
                gate_ref[0:rows, :] = r
        yield


def _inproj_t_phases(rows, u_ref, wq_ref, t_ref):
    u = u_ref[0:rows, :]
    for c0 in range(0, T_ROWS, MXU_COLS):
        cs = slice(c0, c0 + MXU_COLS)
        t_ref[cs, 0:rows] = lax.dot_general(wq_ref[cs, :], u, CONTRACT_LAST,
                                            preferred_element_type=F32).astype(BF16)
        yield


def _inproj_kernel(x_ref, lead_ref, g_ref, wn_ref, wq_ref, nat_ref, t_ref, gate_ref, u_ref):
    i = pl.program_id(0)
    j = pl.program_id(1)
    is_real = i < pl.num_programs(0) - 1
    is_lead = jnp.logical_not(is_real)

    @pl.when((j == 0) & is_real)
    def _():
        u_ref[...] = _rms(x_ref[...], g_ref[...]).astype(BF16)

    @pl.when((j == 0) & is_lead)
    def _():
        u_ref[0:BLOCK, :] = _rms(lead_ref[...], g_ref[...]).astype(BF16)

    def project(rows):
        @pl.when(j < IN_NAT_STEPS)
        def _():
            _drain(_inproj_nat_phases(j, rows, u_ref, wn_ref, nat_ref, gate_ref))

        @pl.when(j == IN_NAT_STEPS)
        def _():
            _drain(_inproj_t_phases(rows, u_ref, wq_ref, t_ref))

    @pl.when(is_real)
    def _():
        project(IN_TM)

    @pl.when(is_lead)
    def _():
        @pl.when(j < IN_NAT_STEPS)
        def _():
            nat_ref[BLOCK:, :] = jnp.zeros((IN_TM - BLOCK, IN_TN), BF16)

        @pl.when(j == IN_NAT_STEPS - 1)
        def _():
            gate_ref[BLOCK:, :] = jnp.zeros((IN_TM - BLOCK, GATE_LANES), F32)

        @pl.when(j == IN_NAT_STEPS)
        def _():
            t_ref[:, BLOCK:] = jnp.zeros((T_ROWS, IN_TM - BLOCK), BF16)

        project(BLOCK)


def _inproj_weight_specs():
    last_nat = IN_NAT_STEPS - 1
    return [
        pl.BlockSpec((IN_TN, D_MODEL), lambda i, j: (jnp.minimum(j, last_nat), 0)),
        pl.BlockSpec((T_ROWS, D_MODEL), lambda i, j: (0, 0)),
    ]


def _inproj_out(tm, rows):
    last_nat = IN_NAT_STEPS - 1
    specs = [
        pl.BlockSpec((tm, IN_TN), lambda i, j: (i, jnp.minimum(j, last_nat))),
        pl.BlockSpec((T_ROWS, tm), lambda i, j: (0, i)),
        pl.BlockSpec((tm, GATE_LANES), lambda i, j: (i, 0)),
    ]
    shapes = [
        jax.ShapeDtypeStruct((rows, NAT_COLS), BF16),
        jax.ShapeDtypeStruct((T_ROWS, rows), BF16),
        jax.ShapeDtypeStruct((rows, GATE_LANES), F32),
    ]
    return specs, shapes


def _inproj(x2, lead, g, w_nat, w_qv, n_real_tiles):
    out_specs, out_shape = _inproj_out(IN_TM, (n_real_tiles + 1) * IN_TM)
    return pl.pallas_call(
        _inproj_kernel,
        grid=(n_real_tiles + 1, IN_NAT_STEPS + 1),
        in_specs=[
            pl.BlockSpec((IN_TM, D_MODEL), lambda i, j: (jnp.minimum(i, n_real_tiles - 1), 0)),
            pl.BlockSpec((BLOCK, D_MODEL), lambda i, j: (0, 0)),
            pl.BlockSpec((1, D_MODEL), lambda i, j: (0, 0)),
        ] + _inproj_weight_specs(),
        out_specs=out_specs,
        out_shape=out_shape,
        scratch_shapes=[pltpu.VMEM((IN_TM, D_MODEL), BF16)],
        compiler_params=_params(2),
        name="inproj",
    )(x2, lead, g, w_nat, w_qv)


GQ = ATT_GROUP * BLOCK
ATT_SLABS_PER_PHASE = 2


def _attn_prologue(n, kc_ref, kp_ref, km_ref):
    c = lax.broadcasted_iota(jnp.int32, (2 * BLOCK, BLOCK), 0)
    r = lax.broadcasted_iota(jnp.int32, (2 * BLOCK, BLOCK), 1)
    dist = r + BLOCK - c
    band_ok = (dist >= 0) & (dist < WINDOW) & ((c >= BLOCK) | (n >= 1))
    mm = lax.broadcasted_iota(jnp.int32, (N_META, BLOCK), 0)
    rr = lax.broadcasted_iota(jnp.int32, (N_META, BLOCK), 1)
    meta_ok = (n + 1) * BLOCK + rr >= N_PAD + mm
    kall = jnp.concatenate([kp_ref[...], kc_ref[...], km_ref[...]], axis=0)
    return band_ok, meta_ok, kall


def _attn_head(h, pro, sink_ref, qt_ref, vc_ref, vp_ref, vl_ref, bb_ref, bm_ref, o_ref):
    band_ok, meta_ok, kall = pro
    lo = (h // 2) * LANE
    kpair = kall[:, lo:lo + LANE]
    qh = jnp.concatenate(
        [qt_ref[(ATT_GROUP * h + g) * ATT_HEAD_DIM:(ATT_GROUP * h + g + 1) * ATT_HEAD_DIM, :]
         for g in range(ATT_GROUP)], axis=1)
    zq = jnp.zeros_like(qh)
    qz = jnp.concatenate([qh, zq] if h % 2 == 0 else [zq, qh], axis=0)
    st = jnp.dot(kpair, qz, preferred_element_type=F32)
    yield
    zpad = jnp.zeros((N_PAD, BLOCK), BF16)
    p_band, p_lead, inv = [], [], []
    for g in range(ATT_GROUP):
        gs = slice(g * BLOCK, (g + 1) * BLOCK)
        sl = st[:, gs]
        sink = jnp.full((1, BLOCK), sink_ref[ATT_GROUP * h + g], F32) * LOG2E
        lb = jnp.where(band_ok, sl[:2 * BLOCK] + bb_ref[h, :, gs], NEG * LOG2E)
        lm = jnp.where(meta_ok, sl[2 * BLOCK:] + bm_ref[0, h, :, gs], NEG * LOG2E)
        m = jnp.maximum(jnp.max(lb, axis=0, keepdims=True), jnp.max(lm, axis=0, keepdims=True))
        m = jnp.maximum(m, sink)
        pb = jnp.exp2(lb - m)
        pm = jnp.exp2(lm - m)
        den = (jnp.sum(pb, axis=0, keepdims=True) + jnp.sum(pm, axis=0, keepdims=True)
               + jnp.exp2(sink - m))
        inv.append(1.0 / den)
        p_band.append(pb.astype(BF16))
        p_lead.append(jnp.concatenate([zpad, pm.astype(BF16)], axis=0))
        if g % ATT_SLABS_PER_PHASE == ATT_SLABS_PER_PHASE - 1:
            yield
    pbt = jnp.concatenate(p_band, axis=1)
    plt = jnp.concatenate(p_lead, axis=1)
    hs = slice(h * ATT_HEAD_DIM, (h + 1) * ATT_HEAD_DIM)
    vband = jnp.concatenate([vp_ref[hs, :], vc_ref[hs, :]], axis=1)
    ot = (jnp.dot(vband, pbt, preferred_element_type=F32)
          + jnp.dot(vl_ref[hs, :], plt, preferred_element_type=F32))
    for g in range(ATT_GROUP):
        row = (ATT_GROUP * h + g) * ATT_HEAD_DIM
        o_ref[row:row + ATT_HEAD_DIM, :] = (
            ot[:, g * BLOCK:(g + 1) * BLOCK] * inv[g]).astype(BF16)


def _conv_shift_matrix():
    s = np.zeros((CONV_WIDTH * BLOCK, 2 * BLOCK), np.float32)
    for k in range(CONV_WIDTH):
        t = np.arange(BLOCK)
        s[k * BLOCK + t, BLOCK + t - k] = 1.0
    return jnp.asarray(s, BF16)


def _mlstm_gates(n, gate_ref, gb_ref):
    t_idx = lax.broadcasted_iota(jnp.int32, (BLOCK, GATE_LANES), 0)
    valid = (n > 0) | (t_idx >= N_PAD)
    pre = GATE_SOFTCAP * jnp.tanh((gate_ref[...] + gb_ref[...]) / GATE_SOFTCAP)
    log_i = jnp.where(valid, pre, NEG)
    log_sig = jnp.minimum(pre, 0.0) - jnp.log1p(jnp.exp(-jnp.abs(pre)))
    log_f = jnp.where(valid, log_sig, 0.0)
    row = lax.broadcasted_iota(jnp.int32, (BLOCK, BLOCK), 0)
    col = lax.broadcasted_iota(jnp.int32, (BLOCK, BLOCK), 1)
    causal = col <= row
    tril = jnp.where(causal, 1.0, 0.0).astype(F32)
    b_all = jnp.dot(tril, log_f, preferred_element_type=F32,
                    precision=lax.Precision.HIGHEST)
    return log_i, b_all, log_i.T, b_all.T, causal


def _mlstm_conv(p_ref, shift_ref, cw_ref, cb_ref, xprev_ref):
    xcur = p_ref[:, NAT_MQK:NAT_MV]
    xcat = jnp.concatenate([xprev_ref[...], xcur], axis=0)
    sh = jnp.dot(shift_ref[...], xcat, preferred_element_type=F32)
    xprev_ref[...] = xcur
    y = cb_ref[...]
    for k in range(CONV_WIDTH):
        y = y + cw_ref[CONV_WIDTH - 1 - k:CONV_WIDTH - k, :] * sh[k * BLOCK:(k + 1) * BLOCK]
    act = y * jax.nn.sigmoid(y)
    return act[:, :ML_QK_WIDTH] * (ML_QK_DIM ** -0.5), act[:, ML_QK_WIDTH:]


def _mlstm_head(h, gates, q_all, k_all, p_ref, mhn_ref, o_ref, c_ref, n_ref, m_ref):
    log_i, b_all, log_i_t, b_all_t, causal = gates
    q_f = q_all[:, h * ML_QK_DIM:(h + 1) * ML_QK_DIM]
    k_f = k_all[:, h * ML_QK_DIM:(h + 1) * ML_QK_DIM]
    q_h = q_f.astype(BF16)
    k_h = k_f.astype(BF16)
    v_h = p_ref[:, NAT_MV + h * ML_V_DIM:NAT_MV + (h + 1) * ML_V_DIM]
    b_col = b_all[:, ML_HEADS + h:ML_HEADS + h + 1]
    li_col = log_i[:, h:h + 1]
    b_row = b_all_t[ML_HEADS + h:ML_HEADS + h + 1, :]
    li_row = log_i_t[h:h + 1, :]
    b_last = b_all[BLOCK - 1:BLOCK, ML_HEADS + h:ML_HEADS + h + 1]
    c_prev = c_ref[h]
    n_prev = n_ref[h][0:1, :]
    m_prev = m_ref[h][0:1, 0:1]

    dmat = jnp.where(causal, b_col - b_row + li_row, -jnp.inf)
    inter = b_col + m_prev
    m_t = jnp.maximum(inter, jnp.max(dmat, axis=1, keepdims=True))
    s = lax.dot_general(q_h, k_h, CONTRACT_LAST, preferred_element_type=F32) * jnp.exp(dmat - m_t)
    yield
    a_t = jnp.exp(inter - m_t)
    num = (jnp.dot(s.astype(BF16), v_h, preferred_element_type=F32)
           + a_t * jnp.dot(q_h, c_prev.astype(BF16), preferred_element_type=F32))
    den = (jnp.sum(s, axis=1, keepdims=True)
           + a_t * jnp.sum(q_f * n_prev, axis=1, keepdims=True))
    den = jnp.maximum(jnp.abs(den), jnp.exp(-m_t))
    hh = num / den
    hh = hh * lax.rsqrt(jnp.mean(hh * hh, axis=-1, keepdims=True) + EPS)
    vs = slice(h * ML_V_DIM, (h + 1) * ML_V_DIM)
    hh = hh * mhn_ref[:, vs]
    m_o = p_ref[:, NAT_MO + h * ML_V_DIM:NAT_MO + (h + 1) * ML_V_DIM].astype(F32)
    o_ref[:, vs] = (hh * jax.nn.sigmoid(m_o)).astype(BF16)
    yield

    g_col = b_last - b_col + li_col
    m_loc = jnp.max(g_col, axis=0, keepdims=True)
    wk = jnp.exp(g_col - m_loc) * k_f
    c_loc = lax.dot_general(wk.astype(BF16), v_h, CONTRACT_FIRST,
                            preferred_element_type=F32)
    n_loc = jnp.sum(wk, axis=0, keepdims=True)
    m_new = jnp.maximum(b_last + m_prev, m_loc)
    a = jnp.exp(b_last + m_prev - m_new)
    cc = jnp.exp(m_loc - m_new)
    c_ref[h] = a * c_prev + cc * c_loc
    n_ref[h] = jnp.broadcast_to(a * n_prev + cc * n_loc, (8, ML_QK_DIM))
    m_ref[h] = jnp.broadcast_to(m_new, (8, LANE))


N_MIX_IN = 17
N_STATE = 4
INPROJ_PHASES_PER_ROUND = 1
MIXER_PHASES_PER_ROUND = 3


def _mixer_phases(n, mix_in, att_ref, hm_ref, state):
    (sink_ref, qt_ref, kc_ref, kp_ref, km_ref, vc_ref, vp_ref, vl_ref, bb_ref, bm_ref,
     p_ref, gate_ref, gb_ref, shift_ref, cw_ref, cb_ref, mhn_ref) = mix_in
    c_ref, n_ref, m_ref, xprev_ref = state

    def attention_phases():
        pro = _attn_prologue(n - 1, kc_ref, kp_ref, km_ref)
        for h in range(ATT_KV_HEADS):
            yield from _attn_head(h, pro, sink_ref, qt_ref, vc_ref, vp_ref, vl_ref, bb_ref, bm_ref,
                                  att_ref)

    def mlstm_phases():
        gates = _mlstm_gates(n, gate_ref, gb_ref)
        yield
        q_all, k_all = _mlstm_conv(p_ref, shift_ref, cw_ref, cb_ref, xprev_ref)
        yield
        for h in range(ML_HEADS):
            yield from _mlstm_head(h, gates, q_all, k_all, p_ref, mhn_ref, hm_ref, c_ref, n_ref, m_ref)

    return _interleave([mlstm_phases(), attention_phases()], [1, 1])


def _load_state(state, state_in):
    for dst, src in zip(state, state_in):
        dst[...] = src[...]


def _state_shapes():
    return [
        jax.ShapeDtypeStruct((ML_HEADS, ML_QK_DIM, ML_V_DIM), F32),
        jax.ShapeDtypeStruct((ML_HEADS, SUBLANES, ML_QK_DIM), F32),
        jax.ShapeDtypeStruct((ML_HEADS, SUBLANES, LANE), F32),
        jax.ShapeDtypeStruct((BLOCK, 2 * ML_QK_WIDTH), BF16),
    ]


def _state_scratch():
    return [pltpu.VMEM(s.shape, s.dtype) for s in _state_shapes()]


def _whole(shape):
    return pl.BlockSpec(shape, lambda *ids: (0,) * len(shape))


def _mixer_in_specs(step, lead_blk):
    cur = lambda *ids: step(*ids)
    prev = lambda *ids: jnp.maximum(step(*ids) - 1, 0)
    k_col = NAT_K // ATT_KV_WIDTH
    v_row = ATT_WIDTH // ATT_KV_WIDTH
    meta_rows = lead_blk * (BLOCK // N_META) + N_PAD // N_META
    return [
        pl.BlockSpec(memory_space=pltpu.SMEM),
        pl.BlockSpec((ATT_WIDTH, BLOCK), lambda *ids: (0, cur(*ids))),
        pl.BlockSpec((BLOCK, ATT_KV_WIDTH), lambda *ids: (cur(*ids), k_col)),
        pl.BlockSpec((BLOCK, ATT_KV_WIDTH), lambda *ids: (prev(*ids), k_col)),
        pl.BlockSpec((N_META, ATT_KV_WIDTH), lambda *ids: (meta_rows, k_col)),
        pl.BlockSpec((ATT_KV_WIDTH, BLOCK), lambda *ids: (v_row, cur(*ids))),
        pl.BlockSpec((ATT_KV_WIDTH, BLOCK), lambda *ids: (v_row, prev(*ids))),
        pl.BlockSpec((ATT_KV_WIDTH, BLOCK), lambda *ids: (v_row, lead_blk)),
        _whole((ATT_KV_HEADS, 2 * BLOCK, GQ)),
        pl.BlockSpec((1, ATT_KV_HEADS, N_META, GQ), lambda *ids: (cur(*ids), 0, 0, 0)),
        pl.BlockSpec((BLOCK, NAT_K), lambda *ids: (cur(*ids), 0)),
        pl.BlockSpec((BLOCK, GATE_LANES), lambda *ids: (cur(*ids), 0)),
        _whole((1, GATE_LANES)),
        _whole((CONV_WIDTH * BLOCK, 2 * BLOCK)),
        _whole((CONV_WIDTH, 2 * ML_QK_WIDTH)),
        _whole((1, 2 * ML_QK_WIDTH)),
        _whole((1, ML_WIDTH)),
    ] + [_whole(s.shape) for s in _state_shapes()]


def _mixer_args(consts, nat_b, projt_b, gates_b, nat_lead, projt_lead, state):
    sinks, bias_band_t, bias_meta_t, gate_bias, shift, conv_w, conv_b, mh_norm = consts
    return [sinks, projt_b, nat_b, nat_b, nat_lead, projt_b, projt_b, projt_lead, bias_band_t,
            bias_meta_t, nat_b, gates_b, gate_bias, shift, conv_w, conv_b, mh_norm] + list(state)


def _mixer_out(step, n_rows):
    specs = [
        pl.BlockSpec((ATT_WIDTH, BLOCK), lambda *ids: (0, step(*ids))),
        pl.BlockSpec((BLOCK, ML_WIDTH), lambda *ids: (step(*ids), 0)),
    ]
    shapes = [
        jax.ShapeDtypeStruct((ATT_WIDTH, n_rows), BF16),
        jax.ShapeDtypeStruct((n_rows, ML_WIDTH), BF16),
    ]
    return specs, shapes


def _lead_state_kernel(p_ref, gate_ref, gb_ref, shift_ref, cw_ref, cb_ref, mhn_ref,
                       c_ref, n_ref, m_ref, xprev_ref, hm_ref):
    for ref in (c_ref, n_ref, m_ref, xprev_ref):
        ref[...] = jnp.zeros_like(ref)
    gates = _mlstm_gates(0, gate_ref, gb_ref)
    q_all, k_all = _mlstm_conv(p_ref, shift_ref, cw_ref, cb_ref, xprev_ref)
    for h in range(ML_HEADS):
        _drain(_mlstm_head(h, gates, q_all, k_all, p_ref, mhn_ref, hm_ref, c_ref, n_ref, m_ref))


def _lead_state(nat_lead, gates_lead, lead_blk, consts):
    _, _, _, gate_bias, shift, conv_w, conv_b, mh_norm = consts
    shapes = _state_shapes()
    return pl.pallas_call(
        _lead_state_kernel,
        grid=(1,),
        in_specs=[
            pl.BlockSpec((BLOCK, NAT_K), lambda i: (lead_blk, 0)),
            pl.BlockSpec((BLOCK, GATE_LANES), lambda i: (lead_blk, 0)),
            _whole((1, GATE_LANES)),
            _whole((CONV_WIDTH * BLOCK, 2 * BLOCK)),
            _whole((CONV_WIDTH, 2 * ML_QK_WIDTH)),
            _whole((1, 2 * ML_QK_WIDTH)),
            _whole((1, ML_WIDTH)),
        ],
        out_specs=[_whole(s.shape) for s in shapes],
        out_shape=shapes,
        scratch_shapes=[pltpu.VMEM((BLOCK, ML_WIDTH), BF16)],
        compiler_params=_params(1),
        name="lead_state",
    )(nat_lead, gates_lead, gate_bias, shift, conv_w, conv_b, mh_norm)


FUSED_TM = 512


def _inproj_mixer_kernel(*refs):
    x_ref, g_ref, wn_ref, wq_ref = refs[:4]
    mix_in = refs[4:4 + N_MIX_IN]
    state_in = refs[4 + N_MIX_IN:4 + N_MIX_IN + N_STATE]
    wu_ref = refs[4 + N_MIX_IN + N_STATE]
    nat_ref, t_ref, gate_ref, att_ref, hm_ref, wu_bf_ref = refs[5 + N_MIX_IN + N_STATE:-(1 + N_STATE)]
    u_ref = refs[-(1 + N_STATE)]
    state = refs[-N_STATE:]
    i = pl.program_id(0)
    j = pl.program_id(1)
    step = i * (IN_NAT_STEPS + 1) + j
    wu_bf_ref[...] = wu_ref[...].astype(BF16)

    @pl.when(j == 0)
    def _():
        u_ref[...] = _rms(x_ref[...], g_ref[...]).astype(BF16)

    @pl.when(step == 0)
    def _():
        _load_state(state, state_in)

    weights = [INPROJ_PHASES_PER_ROUND, MIXER_PHASES_PER_ROUND]

    @pl.when(j < IN_NAT_STEPS)
    def _():
        _drain(_interleave([_inproj_nat_phases(j, FUSED_TM, u_ref, wn_ref, nat_ref, gate_ref),
                            _mixer_phases(step + 1, mix_in, att_ref, hm_ref, state)], weights))

    @pl.when(j == IN_NAT_STEPS)
    def _():
        _drain(_interleave([_inproj_t_phases(FUSED_TM, u_ref, wq_ref, t_ref),
                            _mixer_phases(step + 1, mix_in, att_ref, hm_ref, state)], weights))


def _inproj_mixer(x2, row_tile0, g, w_nat, w_qv, mixer_args, w_up, lead_blk, nblk):
    n_tiles = nblk * BLOCK // FUSED_TM
    assert n_tiles * (IN_NAT_STEPS + 1) == nblk
    step = lambda i, j: i * (IN_NAT_STEPS + 1) + j
    proj_specs, proj_shapes = _inproj_out(FUSED_TM, n_tiles * FUSED_TM)
    mix_specs, mix_shapes = _mixer_out(step, nblk * BLOCK)
    wu_spec = pl.BlockSpec((D_MODEL, D_FF // nblk), lambda i, j: (0, step(i, j)))
    return pl.pallas_call(
        _inproj_mixer_kernel,
        grid=(n_tiles, IN_NAT_STEPS + 1),
        in_specs=[
            pl.BlockSpec((FUSED_TM, D_MODEL), lambda i, j: (row_tile0 + i, 0)),
            pl.BlockSpec((1, D_MODEL), lambda i, j: (0, 0)),
        ] + _inproj_weight_specs() + _mixer_in_specs(step, lead_blk) + [wu_spec],
        out_specs=proj_specs + mix_specs + [wu_spec],
        out_shape=proj_shapes + mix_shapes + [jax.ShapeDtypeStruct(w_up.shape, BF16)],
        scratch_shapes=[pltpu.VMEM((FUSED_TM, D_MODEL), BF16)] + _state_scratch(),
        compiler_params=_params(2),
        name="inproj_mixer",
    )(x2, g, w_nat, w_qv, *mixer_args, w_up)


def _mixer_kernel(*refs):
    mix_in = refs[:N_MIX_IN]
    state_in = refs[N_MIX_IN:N_MIX_IN + N_STATE]
    wo_ref, wd_ref = refs[N_MIX_IN + N_STATE:N_MIX_IN + N_STATE + 2]
    att_ref, hm_ref, wo_bf_ref, wd_bf_ref = refs[N_MIX_IN + N_STATE + 2:-N_STATE]
    state = refs[-N_STATE:]
    step = pl.program_id(0)

    @pl.when(step == 0)
    def _():
        _load_state(state, state_in)

    wo_bf_ref[...] = wo_ref[...].astype(BF16)
    wd_bf_ref[...] = wd_ref[...].astype(BF16)
    _drain(_mixer_phases(step + 1, mix_in, att_ref, hm_ref, state))


def _mixer(mixer_args, w_out, w_down, lead_blk, nblk):
    step = lambda s: s
    mix_specs, mix_shapes = _mixer_out(step, nblk * BLOCK)
    w_specs = [
        pl.BlockSpec((w_out.shape[0] // nblk, D_MODEL), lambda s: (s, 0)),
        pl.BlockSpec((D_FF // nblk, D_MODEL), lambda s: (s, 0)),
    ]
    return pl.pallas_call(
        _mixer_kernel,
        grid=(nblk,),
        in_specs=_mixer_in_specs(step, lead_blk) + w_specs,
        out_specs=mix_specs + w_specs,
        out_shape=mix_shapes + [jax.ShapeDtypeStruct(w.shape, BF16) for w in (w_out, w_down)],
        scratch_shapes=_state_scratch(),
        compiler_params=_params(1),
        name="mixer",
    )(*mixer_args, w_out, w_down)


OUT_TM = 512


def _outproj_kernel(att0_ref, att1_ref, hm0_ref, hm1_ref, x_ref, wa_ref, wm_ref, g_ref,
                    h_ref, u_ref):
    i = pl.program_id(0)
    half = pl.num_programs(0) // 2

    def body(att_ref, hm_ref):
        h = (x_ref[...]
             + lax.dot_general(att_ref[...], wa_ref[...], CONTRACT_FIRST, preferred_element_type=F32)
             + jnp.dot(hm_ref[...], wm_ref[...], preferred_element_type=F32))
        h_ref[...] = h
        u_ref[...] = _rms(h, g_ref[...]).astype(BF16)

    @pl.when(i < half)
    def _():
        body(att0_ref, hm0_ref)

    @pl.when(i >= half)
    def _():
        body(att1_ref, hm1_ref)


def _outproj(att0, att1, hm0, hm1, x2, w_out, g):
    rows = x2.shape[0]
    n = rows // OUT_TM
    half = n // 2
    first = lambda i: jnp.minimum(i, half - 1)
    second = lambda i: jnp.maximum(i - half, 0)
    return pl.pallas_call(
        _outproj_kernel,
        grid=(n,),
        in_specs=[
            pl.BlockSpec((ATT_WIDTH, OUT_TM), lambda i: (0, first(i))),
            pl.BlockSpec((ATT_WIDTH, OUT_TM), lambda i: (0, second(i))),
            pl.BlockSpec((OUT_TM, ML_WIDTH), lambda i: (first(i), 0)),
            pl.BlockSpec((OUT_TM, ML_WIDTH), lambda i: (second(i), 0)),
            pl.BlockSpec((OUT_TM, D_MODEL), lambda i: (i, 0)),
            pl.BlockSpec((ATT_WIDTH, D_MODEL), lambda i: (0, 0)),
            pl.BlockSpec((ML_WIDTH, D_MODEL), lambda i: (ATT_WIDTH // ML_WIDTH, 0)),
            pl.BlockSpec((1, D_MODEL), lambda i: (0, 0)),
        ],
        out_specs=[
            pl.BlockSpec((OUT_TM, D_MODEL), lambda i: (i, 0)),
            pl.BlockSpec((OUT_TM, D_MODEL), lambda i: (i, 0)),
        ],
        out_shape=[
            jax.ShapeDtypeStruct((rows, D_MODEL), F32),
            jax.ShapeDtypeStruct((rows, D_MODEL), BF16),
        ],
        compiler_params=_params(1),
        name="outproj",
    )(att0, att1, hm0, hm1, x2, w_out, w_out, g)


MLP_TM = 1024
MLP_TF = 1024


def _mlp_step(first, u_ref, wu_ref, wd_ref, o_ref):
    acts = []
    for c in range(MLP_TF // MXU_COLS):
        cs = slice(c * MXU_COLS, (c + 1) * MXU_COLS)
        a = jnp.dot(u_ref[...], wu_ref[:, cs], preferred_element_type=F32)
        acts.append(jnp.square(jnp.maximum(a, 0.0)).astype(BF16))
        yield
    act = jnp.concatenate(acts, axis=1)
    for c in range(D_MODEL // MXU_COLS):
        cs = slice(c * MXU_COLS, (c + 1) * MXU_COLS)
        d = jnp.dot(act, wd_ref[:, cs], preferred_element_type=F32)
        if first:
            o_ref[:, cs] = d
        else:
            o_ref[:, cs] += d
        yield


def _mlp_kernel(u_ref, h_hbm, wu_ref, wd_ref, g_ref, o_ref, hbuf_ref, hsem):
    i = pl.program_id(0)
    j = pl.program_id(1)
    r0 = pl.multiple_of(i * MLP_TM, MLP_TM)
    h_copy = pltpu.make_async_copy(h_hbm.at[pl.ds(r0, MLP_TM), :], hbuf_ref, hsem)

    @pl.when(j == 0)
    def _():
        h_copy.start()
        for _ in _mlp_step(True, u_ref, wu_ref, wd_ref, o_ref):
            pass

    @pl.when(j > 0)
    def _():
        for _ in _mlp_step(False, u_ref, wu_ref, wd_ref, o_ref):
            pass

    @pl.when(j == pl.num_programs(1) - 1)
    def _():
        h_copy.wait()
        o_ref[...] = _rms(o_ref[...] + hbuf_ref[...], g_ref[...])


def _mlp(u, h, w_up, w_down, g):
    rows = u.shape[0]
    return pl.pallas_call(
        _mlp_kernel,
        grid=(rows // MLP_TM, D_FF // MLP_TF),
        in_specs=[
            pl.BlockSpec((MLP_TM, D_MODEL), lambda i, j: (i, 0)),
            pl.BlockSpec(memory_space=pl.ANY),
            pl.BlockSpec((D_MODEL, MLP_TF), lambda i, j: (0, j)),
            pl.BlockSpec((MLP_TF, D_MODEL), lambda i, j: (j, 0)),
            pl.BlockSpec((1, D_MODEL), lambda i, j: (0, 0)),
        ],
        out_specs=pl.BlockSpec((MLP_TM, D_MODEL), lambda i, j: (i, 0)),
        out_shape=jax.ShapeDtypeStruct((rows, D_MODEL), F32),
        scratch_shapes=[pltpu.VMEM((MLP_TM, D_MODEL), F32), pltpu.SemaphoreType.DMA(())],
        compiler_params=_params(2),
        name="mlp",
    )(u, h, w_up, w_down, g)


def _t5_bucket(dist):
    max_exact = N_BUCKETS // 2
    d = jnp.maximum(dist, 0)
    ratio = jnp.maximum(d, max_exact).astype(F32) / max_exact
    large = max_exact + (jnp.log(ratio) / math.log(MAX_DISTANCE / max_exact)
                         * (N_BUCKETS - max_exact)).astype(jnp.int32)
    large = jnp.minimum(large, N_BUCKETS - 1)
    return jnp.where(d < max_exact, d, large)


BIAS_ROWS = 16


def _bias_kernel(rb_ref, bkt_band_ref, bkt_meta_ref, band_ref, meta_ref, splat_ref):
    def fill(b, carry):
        for hd in range(ATT_HEADS):
            splat_ref[b * ATT_HEADS + hd] = jnp.full((SUBLANES, BLOCK), rb_ref[b, hd], F32) * LOG2E
        return carry

    lax.fori_loop(0, N_BUCKETS, fill, 0)

    def lookup(bkt):
        def body(b, accs):
            hit = bkt == b
            out = []
            for hd, a in enumerate(accs):
                sp = splat_ref[b * ATT_HEADS + hd]
                out.append(jnp.where(hit, jnp.concatenate([sp] * (BIAS_ROWS // SUBLANES), axis=0), a))
            return tuple(out)
        init = tuple(jnp.zeros(bkt.shape, F32) for _ in range(ATT_HEADS))
        return lax.fori_loop(0, N_BUCKETS, body, init, unroll=4)

    def band_rows(ci, carry):
        r0 = pl.multiple_of(ci * BIAS_ROWS, BIAS_ROWS)
        accs = lookup(bkt_band_ref[pl.ds(r0, BIAS_ROWS), :])
        for hd in range(ATT_HEADS):
            h, g = divmod(hd, ATT_GROUP)
            band_ref[h, pl.ds(r0, BIAS_ROWS), g * BLOCK:(g + 1) * BLOCK] = accs[hd]
        return carry

    lax.fori_loop(0, 2 * BLOCK // BIAS_ROWS, band_rows, 0)

    def meta_block(n, carry):
        accs = lookup(bkt_meta_ref[n])
        for hd in range(ATT_HEADS):
            h, g = divmod(hd, ATT_GROUP)
            meta_ref[n, h, :, g * BLOCK:(g + 1) * BLOCK] = accs[hd]
        return carry

    lax.fori_loop(0, meta_ref.shape[0], meta_block, 0)


def _bias_tables(rel_bias, nblk):
    c = jnp.arange(2 * BLOCK)[:, None]
    r = jnp.arange(BLOCK)[None, :]
    bkt_band = _t5_bucket(r + BLOCK - c).astype(jnp.int32)
    q_pos = BLOCK + jnp.arange(nblk * BLOCK)[None, :]
    m_pos = N_PAD + jnp.arange(N_META)[:, None]
    bkt_meta = _t5_bucket(q_pos - m_pos).astype(jnp.int32)
    bkt_meta = bkt_meta.reshape(N_META, nblk, BLOCK).transpose(1, 0, 2)
    vmem = pl.BlockSpec(memory_space=pltpu.VMEM)
    return pl.pallas_call(
        _bias_kernel,
        in_specs=[pl.BlockSpec(memory_space=pltpu.SMEM), vmem, vmem],
        out_specs=[vmem, vmem],
        out_shape=[
            jax.ShapeDtypeStruct((ATT_KV_HEADS, 2 * BLOCK, GQ), F32),
            jax.ShapeDtypeStruct((nblk, ATT_KV_HEADS, N_META, GQ), F32),
        ],
        scratch_shapes=[pltpu.VMEM((N_BUCKETS * ATT_HEADS, SUBLANES, BLOCK), F32)],
        name="bias_tables",
    )(rel_bias.astype(F32), bkt_band, bkt_meta)


W_IN_OFFS = dict(q=0, k=1024, v=1280, mq=1536, mk=2048, mv=2560, mo=3584, gates=4608)
NAT_SRC = ([W_IN_OFFS["mq"] // LANE + t for t in range((W_IN_OFFS["gates"] - W_IN_OFFS["mq"]) // LANE)]
           + [W_IN_OFFS["k"] // LANE + t for t in range(ATT_KV_WIDTH // LANE)]
           + [W_IN_OFFS["gates"] // LANE])
QV_SRC = ([W_IN_OFFS["q"] // LANE + t for t in range(ATT_WIDTH // LANE)]
          + [W_IN_OFFS["v"] // LANE + t for t in range(ATT_KV_WIDTH // LANE)])
N_NAT_BLK = len(NAT_SRC)
N_Q_BLK = ATT_WIDTH // LANE


def _win_prep_kernel(src_ref, w_ref, nat_ref, qv_ref):
    j = pl.program_id(0)
    wv = w_ref[0]

    @pl.when(j < N_NAT_BLK - 1)
    def _():
        nat_ref[...] = wv.astype(BF16)

    @pl.when(j == N_NAT_BLK - 1)
    def _():
        row = lax.broadcasted_iota(jnp.int32, wv.shape, 0)
        nat_ref[...] = jnp.where(row < 2 * ML_HEADS, wv, 0.0).astype(BF16)

    @pl.when((j >= N_NAT_BLK) & (j < N_NAT_BLK + N_Q_BLK))
    def _():
        qv_ref[...] = (wv * (ATT_HEAD_DIM ** -0.5 * LOG2E)).astype(BF16)

    @pl.when(j >= N_NAT_BLK + N_Q_BLK)
    def _():
        qv_ref[...] = wv.astype(BF16)


def _win_prep(w_in_t):
    src = jnp.asarray(NAT_SRC + QV_SRC, jnp.int32)
    d = w_in_t.shape[2]
    return pl.pallas_call(
        _win_prep_kernel,
        grid_spec=pltpu.PrefetchScalarGridSpec(
            num_scalar_prefetch=1,
            grid=(len(NAT_SRC) + len(QV_SRC),),
            in_specs=[pl.BlockSpec((1, LANE, d), lambda j, src: (0, src[j], 0))],
            out_specs=[
                pl.BlockSpec((LANE, d), lambda j, src: (jnp.minimum(j, N_NAT_BLK - 1), 0)),
                pl.BlockSpec((LANE, d), lambda j, src: (jnp.maximum(j - N_NAT_BLK, 0), 0)),
            ],
        ),
        out_shape=[
            jax.ShapeDtypeStruct((NAT_COLS, d), BF16),
            jax.ShapeDtypeStruct((T_ROWS, d), BF16),
        ],
        compiler_params=_params(1),
        name="w_in_prep",
    )(src, w_in_t)


def kernel(x, meta_tokens, w_in, conv_w, conv_b, b_igate, b_fgate, attn_sinks, rel_bias,
           mh_norm, w_out, norm_mix, norm_mlp, w_up, w_down, norm_final):
    batch, seq, d = x.shape
    assert batch == 2, "the two batches are processed one projection / mixer stage apart"
    nblk = seq // BLOCK
    x2 = x.reshape(batch * seq, d)
    lead = jnp.concatenate([jnp.zeros((N_PAD, d), x.dtype), meta_tokens.astype(x.dtype)], axis=0)
    g_mix = norm_mix[0][None].astype(F32)

    w_nat, w_qv = _win_prep(jnp.swapaxes(w_in, 1, 2))
    gate_bias = jnp.pad(jnp.concatenate([b_igate[0], b_fgate[0]]).astype(F32),
                        (0, GATE_LANES - 2 * ML_HEADS))[None]
    bias_band_t, bias_meta_t = _bias_tables(rel_bias, nblk)
    consts = (attn_sinks[0].astype(F32), bias_band_t, bias_meta_t, gate_bias, _conv_shift_matrix(),
              conv_w[0].astype(F32), conv_b[0][None].astype(F32), mh_norm[0][None].astype(F32))

    nat0, projt0, gates0 = _inproj(x2, lead, g_mix, w_nat, w_qv, seq // IN_TM)
    lead_blk = seq // BLOCK
    state = _lead_state(nat0, gates0, lead_blk, consts)
    args0 = _mixer_args(consts, nat0, projt0, gates0, nat0, projt0, state)
    nat1, projt1, gates1, att0, hm0, w_up_bf = _inproj_mixer(
        x2, seq // FUSED_TM, g_mix, w_nat, w_qv, args0, w_up[0], lead_blk, nblk)
    args1 = _mixer_args(consts, nat1, projt1, gates1, nat0, projt0, state)
    att1, hm1, w_out_bf, w_down_bf = _mixer(args1, w_out[0], w_down[0], lead_blk, nblk)

    h2, u2 = _outproj(att0, att1, hm0, hm1, x2, w_out_bf, norm_mlp[0][None].astype(F32))
    out = _mlp(u2, h2, w_up_bf, w_down_bf, norm_final[None].astype(F32))
    return out.reshape(batch, seq, d)
```

```python
import math

import numpy as np
import jax
import jax.numpy as jnp
from jax import lax
from jax.experimental import pallas as pl
from jax.experimental.pallas import tpu as pltpu

F32 = jnp.float32
BF16 = jnp.bfloat16

D_MODEL = 2048
N_META = 16
BLOCK = 128
N_PAD = BLOCK - N_META
WINDOW = 128
ATT_HEADS = 16
ATT_KV_HEADS = 4
ATT_GROUP = ATT_HEADS // ATT_KV_HEADS
ATT_HEAD_DIM = 64
ATT_WIDTH = ATT_HEADS * ATT_HEAD_DIM
ATT_KV_WIDTH = ATT_KV_HEADS * ATT_HEAD_DIM
ML_HEADS = 4
ML_V_DIM = 256
ML_QK_DIM = 128
ML_WIDTH = ML_HEADS * ML_V_DIM
ML_QK_WIDTH = ML_HEADS * ML_QK_DIM
CONV_WIDTH = 4
GATE_SOFTCAP = 15.0
D_FF = 4 * D_MODEL
N_BUCKETS = 32
MAX_DISTANCE = 128
EPS = 1e-6
NEG = -1e30
LOG2E = math.log2(math.e)

GATE_LANES = 128
NAT_MQK = 0
NAT_MV = 2 * ML_QK_WIDTH
NAT_MO = NAT_MV + ML_WIDTH
NAT_K = NAT_MO + ML_WIDTH
NAT_COLS = NAT_K + ATT_KV_WIDTH + GATE_LANES
T_ROWS = ATT_WIDTH + ATT_KV_WIDTH

V7X_VMEM_LIMIT = 60 * 1024 * 1024
LANE = 128
SUBLANES = 8
CONTRACT_LAST = (((1,), (1,)), ((), ()))
CONTRACT_FIRST = (((0,), (0,)), ((), ()))


def _params(n_axes, vmem=V7X_VMEM_LIMIT):
    return pltpu.CompilerParams(dimension_semantics=("arbitrary",) * n_axes,
                                vmem_limit_bytes=vmem)


def _rms(x, g):
    return x * lax.rsqrt(jnp.mean(x * x, axis=-1, keepdims=True) + EPS) * g


_DONE = object()


def _drain(phases):
    for _ in phases:
        pass


def _interleave(streams, weights):
    live = list(zip(streams, weights))
    while live:
        for item in list(live):
            stream, weight = item
            for _ in range(weight):
                if next(stream, _DONE) is _DONE:
                    live.remove(item)
                    break
        yield


IN_TM = 1024
IN_NAT_STEPS = 3
IN_TN = NAT_COLS // IN_NAT_STEPS
MXU_COLS = 256


def _inproj_nat_phases(j, rows, u_ref, wn_ref, nat_ref, gate_ref):
    u = u_ref[0:rows, :]
    for c0 in range(0, IN_TN, MXU_COLS):
        cs = slice(c0, min(c0 + MXU_COLS, IN_TN))
        r = lax.dot_general(u, wn_ref[cs, :], CONTRACT_LAST, preferred_element_type=F32)
        nat_ref[0:rows, cs] = r.astype(BF16)
        if cs.stop == IN_TN:
            assert cs.stop - cs.start == GATE_LANES

            @pl.when(j == IN_NAT_STEPS - 1)
            def _():
                gate_ref[0:rows, :] = r
        yield


def _inproj_t_phases(rows, u_ref, wq_ref, t_ref):
    u = u_ref[0:rows, :]
    for c0 in range(0, T_ROWS, MXU_COLS):
        cs = slice(c0, c0 + MXU_COLS)
        t_ref[cs, 0:rows] = lax.dot_general(wq_ref[cs, :], u, CONTRACT_LAST,
                                            preferred_element_type=F32).astype(BF16)
        yield


def _inproj_kernel(x_ref, lead_ref, g_ref, wn_ref, wq_ref,
                   nat_ref, t_ref, gate_ref, nat_lead_ref, t_lead_ref, gate_lead_ref,
                   u_ref, u_lead_ref):
    i = pl.program_id(0)
    j = pl.program_id(1)

    @pl.when(j == 0)
    def _():
        u_ref[...] = _rms(x_ref[...], g_ref[...]).astype(BF16)

    @pl.when((j == 0) & (i == 0))
    def _():
        u_lead_ref[...] = _rms(lead_ref[...], g_ref[...]).astype(BF16)

    def project(rows, u, nat, t, gate):
        @pl.when(j < IN_NAT_STEPS)
        def _():
            _drain(_inproj_nat_phases(j, rows, u, wn_ref, nat, gate))

        @pl.when(j == IN_NAT_STEPS)
        def _():
            _drain(_inproj_t_phases(rows, u, wq_ref, t))

    project(IN_TM, u_ref, nat_ref, t_ref, gate_ref)

    @pl.when(i == 0)
    def _():
        project(BLOCK, u_lead_ref, nat_lead_ref, t_lead_ref, gate_lead_ref)


def _inproj_weight_specs():
    last_nat = IN_NAT_STEPS - 1
    return [
        pl.BlockSpec((IN_TN, D_MODEL), lambda i, j: (jnp.minimum(j, last_nat), 0)),
        pl.BlockSpec((T_ROWS, D_MODEL), lambda i, j: (0, 0)),
    ]


def _inproj_out(tm, rows):
    last_nat = IN_NAT_STEPS - 1
    specs = [
        pl.BlockSpec((tm, IN_TN), lambda i, j: (i, jnp.minimum(j, last_nat))),
        pl.BlockSpec((T_ROWS, tm), lambda i, j: (0, i)),
        pl.BlockSpec((tm, GATE_LANES), lambda i, j: (i, 0)),
    ]
    shapes = [
        jax.ShapeDtypeStruct((rows, NAT_COLS), BF16),
        jax.ShapeDtypeStruct((T_ROWS, rows), BF16),
        jax.ShapeDtypeStruct((rows, GATE_LANES), F32),
    ]
    return specs, shapes


def _inproj(x2, lead, g, w_nat, w_qv, n_tiles):
    out_specs, out_shape = _inproj_out(IN_TM, n_tiles * IN_TM)
    last_nat = IN_NAT_STEPS - 1
    lead_col = lambda i, j: jnp.where(i == 0, jnp.minimum(j, last_nat), last_nat)
    lead_specs = [
        pl.BlockSpec((BLOCK, IN_TN), lambda i, j: (0, lead_col(i, j))),
        pl.BlockSpec((T_ROWS, BLOCK), lambda i, j: (0, 0)),
        pl.BlockSpec((BLOCK, GATE_LANES), lambda i, j: (0, 0)),
    ]
    lead_shapes = [
        jax.ShapeDtypeStruct((BLOCK, NAT_COLS), BF16),
        jax.ShapeDtypeStruct((T_ROWS, BLOCK), BF16),
        jax.ShapeDtypeStruct((BLOCK, GATE_LANES), F32),
    ]
    outs = pl.pallas_call(
        _inproj_kernel,
        grid=(n_tiles, IN_NAT_STEPS + 1),
        in_specs=[
            pl.BlockSpec((IN_TM, D_MODEL), lambda i, j: (i, 0)),
            pl.BlockSpec((BLOCK, D_MODEL), lambda i, j: (0, 0)),
            pl.BlockSpec((1, D_MODEL), lambda i, j: (0, 0)),
        ] + _inproj_weight_specs(),
        out_specs=out_specs + lead_specs,
        out_shape=out_shape + lead_shapes,
        scratch_shapes=[pltpu.VMEM((IN_TM, D_MODEL), BF16), pltpu.VMEM((BLOCK, D_MODEL), BF16)],
        compiler_params=_params(2),
        name="inproj",
    )(x2, lead, g, w_nat, w_qv)
    return outs[:3], outs[3:]


GQ = ATT_GROUP * BLOCK
ATT_SLABS_PER_PHASE = 2


def _attn_prologue(n, kc_ref, kp_ref, km_ref):
    c = lax.broadcasted_iota(jnp.int32, (2 * BLOCK, BLOCK), 0)
    r = lax.broadcasted_iota(jnp.int32, (2 * BLOCK, BLOCK), 1)
    dist = r + BLOCK - c
    band_ok = (dist >= 0) & (dist < WINDOW) & ((c >= BLOCK) | (n >= 1))
    mm = lax.broadcasted_iota(jnp.int32, (N_META, BLOCK), 0)
    rr = lax.broadcasted_iota(jnp.int32, (N_META, BLOCK), 1)
    meta_ok = (n + 1) * BLOCK + rr >= N_PAD + mm
    kall = jnp.concatenate([kp_ref[...], kc_ref[...], km_ref[...]], axis=0)
    return band_ok, meta_ok, kall


def _attn_head(h, pro, sink_ref, qt_ref, vc_ref, vp_ref, vl_ref, bb_ref, bm_ref, o_ref):
    band_ok, meta_ok, kall = pro
    lo = (h // 2) * LANE
    kpair = kall[:, lo:lo + LANE]
    qh = jnp.concatenate(
        [qt_ref[(ATT_GROUP * h + g) * ATT_HEAD_DIM:(ATT_GROUP * h + g + 1) * ATT_HEAD_DIM, :]
         for g in range(ATT_GROUP)], axis=1)
    zq = jnp.zeros_like(qh)
    qz = jnp.concatenate([qh, zq] if h % 2 == 0 else [zq, qh], axis=0)
    st = jnp.dot(kpair, qz, preferred_element_type=F32)
    yield
    zpad = jnp.zeros((N_PAD, BLOCK), BF16)
    p_band, p_lead, inv = [], [], []
    for g in range(ATT_GROUP):
        gs = slice(g * BLOCK, (g + 1) * BLOCK)
        sl = st[:, gs]
        sink = jnp.full((1, BLOCK), sink_ref[ATT_GROUP * h + g], F32) * LOG2E
        lb = jnp.where(band_ok, sl[:2 * BLOCK] + bb_ref[h, :, gs], NEG * LOG2E)
        lm = jnp.where(meta_ok, sl[2 * BLOCK:] + bm_ref[0, h, :, gs], NEG * LOG2E)
        m = jnp.maximum(jnp.max(lb, axis=0, keepdims=True), jnp.max(lm, axis=0, keepdims=True))
        m = jnp.maximum(m, sink)
        pb = jnp.exp2(lb - m)
        pm = jnp.exp2(lm - m)
        den = (jnp.sum(pb, axis=0, keepdims=True) + jnp.sum(pm, axis=0, keepdims=True)
               + jnp.exp2(sink - m))
        inv.append(1.0 / den)
        p_band.append(pb.astype(BF16))
        p_lead.append(jnp.concatenate([zpad, pm.astype(BF16)], axis=0))
        if g % ATT_SLABS_PER_PHASE == ATT_SLABS_PER_PHASE - 1:
            yield
    pbt = jnp.concatenate(p_band, axis=1)
    plt = jnp.concatenate(p_lead, axis=1)
    hs = slice(h * ATT_HEAD_DIM, (h + 1) * ATT_HEAD_DIM)
    vband = jnp.concatenate([vp_ref[hs, :], vc_ref[hs, :]], axis=1)
    ot = (jnp.dot(vband, pbt, preferred_element_type=F32)
          + jnp.dot(vl_ref[hs, :], plt, preferred_element_type=F32))
    for g in range(ATT_GROUP):
        row = (ATT_GROUP * h + g) * ATT_HEAD_DIM
        o_ref[row:row + ATT_HEAD_DIM, :] = (
            ot[:, g * BLOCK:(g + 1) * BLOCK] * inv[g]).astype(BF16)


def _conv_shift_matrix():
    s = np.zeros((CONV_WIDTH * BLOCK, 2 * BLOCK), np.float32)
    for k in range(CONV_WIDTH):
        t = np.arange(BLOCK)
        s[k * BLOCK + t, BLOCK + t - k] = 1.0
    return jnp.asarray(s, BF16)


def _mlstm_gates(n, gate_ref, gb_ref):
    t_idx = lax.broadcasted_iota(jnp.int32, (BLOCK, GATE_LANES), 0)
    valid = (n > 0) | (t_idx >= N_PAD)
    pre = GATE_SOFTCAP * jnp.tanh((gate_ref[...] + gb_ref[...]) / GATE_SOFTCAP)
    log_i = jnp.where(valid, pre, NEG)
    log_sig = jnp.minimum(pre, 0.0) - jnp.log1p(jnp.exp(-jnp.abs(pre)))
    log_f = jnp.where(valid, log_sig, 0.0)
    row = lax.broadcasted_iota(jnp.int32, (BLOCK, BLOCK), 0)
    col = lax.broadcasted_iota(jnp.int32, (BLOCK, BLOCK), 1)
    causal = col <= row
    tril = jnp.where(causal, 1.0, 0.0).astype(F32)
    b_all = jnp.dot(tril, log_f, preferred_element_type=F32,
                    precision=lax.Precision.HIGHEST)
    return log_i, b_all, log_i.T, b_all.T, causal


def _mlstm_conv(p_ref, shift_ref, cw_ref, cb_ref, xprev_ref):
    xcur = p_ref[:, NAT_MQK:NAT_MV]
    xcat = jnp.concatenate([xprev_ref[...], xcur], axis=0)
    sh = jnp.dot(shift_ref[...], xcat, preferred_element_type=F32)
    xprev_ref[...] = xcur
    y = cb_ref[...]
    for k in range(CONV_WIDTH):
        y = y + cw_ref[CONV_WIDTH - 1 - k:CONV_WIDTH - k, :] * sh[k * BLOCK:(k + 1) * BLOCK]
    act = y * jax.nn.sigmoid(y)
    return act[:, :ML_QK_WIDTH] * (ML_QK_DIM ** -0.5), act[:, ML_QK_WIDTH:]


def _mlstm_head(h, gates, q_all, k_all, p_ref, mhn_ref, o_ref, c_ref, n_ref, m_ref):
    log_i, b_all, log_i_t, b_all_t, causal = gates
    q_f = q_all[:, h * ML_QK_DIM:(h + 1) * ML_QK_DIM]
    k_f = k_all[:, h * ML_QK_DIM:(h + 1) * ML_QK_DIM]
    q_h = q_f.astype(BF16)
    k_h = k_f.astype(BF16)
    v_h = p_ref[:, NAT_MV + h * ML_V_DIM:NAT_MV + (h + 1) * ML_V_DIM]
    b_col = b_all[:, ML_HEADS + h:ML_HEADS + h + 1]
    li_col = log_i[:, h:h + 1]
    b_row = b_all_t[ML_HEADS + h:ML_HEADS + h + 1, :]
    li_row = log_i_t[h:h + 1, :]
    b_last = b_all[BLOCK - 1:BLOCK, ML_HEADS + h:ML_HEADS + h + 1]
    c_prev = c_ref[h]
    n_prev = n_ref[h][0:1, :]
    m_prev = m_ref[h][0:1, 0:1]

    dmat = jnp.where(causal, b_col - b_row + li_row, -jnp.inf)
    inter = b_col + m_prev
    m_t = jnp.maximum(inter, jnp.max(dmat, axis=1, keepdims=True))
    s = lax.dot_general(q_h, k_h, CONTRACT_LAST, preferred_element_type=F32) * jnp.exp(dmat - m_t)
    yield
    a_t = jnp.exp(inter - m_t)
    num = (jnp.dot(s.astype(BF16), v_h, preferred_element_type=F32)
           + a_t * jnp.dot(q_h, c_prev.astype(BF16), preferred_element_type=F32))
    den = (jnp.sum(s, axis=1, keepdims=True)
           + a_t * jnp.sum(q_f * n_prev, axis=1, keepdims=True))
    den = jnp.maximum(jnp.abs(den), jnp.exp(-m_t))
    hh = num / den
    hh = hh * lax.rsqrt(jnp.mean(hh * hh, axis=-1, keepdims=True) + EPS)
    vs = slice(h * ML_V_DIM, (h + 1) * ML_V_DIM)
    hh = hh * mhn_ref[:, vs]
    m_o = p_ref[:, NAT_MO + h * ML_V_DIM:NAT_MO + (h + 1) * ML_V_DIM].astype(F32)
    o_ref[:, vs] = (hh * jax.nn.sigmoid(m_o)).astype(BF16)
    yield

    g_col = b_last - b_col + li_col
    m_loc = jnp.max(g_col, axis=0, keepdims=True)
    wk = jnp.exp(g_col - m_loc) * k_f
    c_loc = lax.dot_general(wk.astype(BF16), v_h, CONTRACT_FIRST,
                            preferred_element_type=F32)
    n_loc = jnp.sum(wk, axis=0, keepdims=True)
    m_new = jnp.maximum(b_last + m_prev, m_loc)
    a = jnp.exp(b_last + m_prev - m_new)
    cc = jnp.exp(m_loc - m_new)
    c_ref[h] = a * c_prev + cc * c_loc
    n_ref[h] = jnp.broadcast_to(a * n_prev + cc * n_loc, (8, ML_QK_DIM))
    m_ref[h] = jnp.broadcast_to(m_new, (8, LANE))


N_MIX_IN = 17
N_STATE = 4
INPROJ_PHASES_PER_ROUND = 1
MIXER_PHASES_PER_ROUND = 4


def _mixer_phases(n, mix_in, att_ref, hm_ref, state):
    (sink_ref, qt_ref, kc_ref, kp_ref, km_ref, vc_ref, vp_ref, vl_ref, bb_ref, bm_ref,
     p_ref, gate_ref, gb_ref, shift_ref, cw_ref, cb_ref, mhn_ref) = mix_in
    c_ref, n_ref, m_ref, xprev_ref = state

    def attention_phases():
        pro = _attn_prologue(n - 1, kc_ref, kp_ref, km_ref)
        for h in range(ATT_KV_HEADS):
            yield from _attn_head(h, pro, sink_ref, qt_ref, vc_ref, vp_ref, vl_ref, bb_ref, bm_ref,
                                  att_ref)

    def mlstm_phases():
        gates = _mlstm_gates(n, gate_ref, gb_ref)
        yield
        q_all, k_all = _mlstm_conv(p_ref, shift_ref, cw_ref, cb_ref, xprev_ref)
        yield
        for h in range(ML_HEADS):
            yield from _mlstm_head(h, gates, q_all, k_all, p_ref, mhn_ref, hm_ref, c_ref, n_ref, m_ref)

    return _interleave([mlstm_phases(), attention_phases()], [1, 1])


def _load_state(state, state_in):
    for dst, src in zip(state, state_in):
        dst[...] = src[...]


def _state_shapes():
    return [
        jax.ShapeDtypeStruct((ML_HEADS, ML_QK_DIM, ML_V_DIM), F32),
        jax.ShapeDtypeStruct((ML_HEADS, SUBLANES, ML_QK_DIM), F32),
        jax.ShapeDtypeStruct((ML_HEADS, SUBLANES, LANE), F32),
        jax.ShapeDtypeStruct((BLOCK, 2 * ML_QK_WIDTH), BF16),
    ]


def _state_scratch():
    return [pltpu.VMEM(s.shape, s.dtype) for s in _state_shapes()]


def _whole(shape):
    return pl.BlockSpec(shape, lambda *ids: (0,) * len(shape))


def _mixer_in_specs(step, lead_blk):
    cur = lambda *ids: step(*ids)
    prev = lambda *ids: jnp.maximum(step(*ids) - 1, 0)
    k_col = NAT_K // ATT_KV_WIDTH
    v_row = ATT_WIDTH // ATT_KV_WIDTH
    meta_rows = lead_blk * (BLOCK // N_META) + N_PAD // N_META
    return [
        pl.BlockSpec(memory_space=pltpu.SMEM),
        pl.BlockSpec((ATT_WIDTH, BLOCK), lambda *ids: (0, cur(*ids))),
        pl.BlockSpec((BLOCK, ATT_KV_WIDTH), lambda *ids: (cur(*ids), k_col)),
        pl.BlockSpec((BLOCK, ATT_KV_WIDTH), lambda *ids: (prev(*ids), k_col)),
        pl.BlockSpec((N_META, ATT_KV_WIDTH), lambda *ids: (meta_rows, k_col)),
        pl.BlockSpec((ATT_KV_WIDTH, BLOCK), lambda *ids: (v_row, cur(*ids))),
        pl.BlockSpec((ATT_KV_WIDTH, BLOCK), lambda *ids: (v_row, prev(*ids))),
        pl.BlockSpec((ATT_KV_WIDTH, BLOCK), lambda *ids: (v_row, lead_blk)),
        _whole((ATT_KV_HEADS, 2 * BLOCK, GQ)),
        pl.BlockSpec((1, ATT_KV_HEADS, N_META, GQ), lambda *ids: (cur(*ids), 0, 0, 0)),
        pl.BlockSpec((BLOCK, NAT_K), lambda *ids: (cur(*ids), 0)),
        pl.BlockSpec((BLOCK, GATE_LANES), lambda *ids: (cur(*ids), 0)),
        _whole((1, GATE_LANES)),
        _whole((CONV_WIDTH * BLOCK, 2 * BLOCK)),
        _whole((CONV_WIDTH, 2 * ML_QK_WIDTH)),
        _whole((1, 2 * ML_QK_WIDTH)),
        _whole((1, ML_WIDTH)),
    ] + [_whole(s.shape) for s in _state_shapes()]


def _mixer_args(consts, nat_b, projt_b, gates_b, nat_lead, projt_lead, state):
    sinks, bias_band_t, bias_meta_t, gate_bias, shift, conv_w, conv_b, mh_norm = consts
    return [sinks, projt_b, nat_b, nat_b, nat_lead, projt_b, projt_b, projt_lead, bias_band_t,
            bias_meta_t, nat_b, gates_b, gate_bias, shift, conv_w, conv_b, mh_norm] + list(state)


def _mixer_out(step, n_rows):
    specs = [
        pl.BlockSpec((ATT_WIDTH, BLOCK), lambda *ids: (0, step(*ids))),
        pl.BlockSpec((BLOCK, ML_WIDTH), lambda *ids: (step(*ids), 0)),
    ]
    shapes = [
        jax.ShapeDtypeStruct((ATT_WIDTH, n_rows), BF16),
        jax.ShapeDtypeStruct((n_rows, ML_WIDTH), BF16),
    ]
    return specs, shapes


def _lead_state_kernel(p_ref, gate_ref, gb_ref, shift_ref, cw_ref, cb_ref, mhn_ref,
                       c_ref, n_ref, m_ref, xprev_ref, hm_ref):
    for ref in (c_ref, n_ref, m_ref, xprev_ref):
        ref[...] = jnp.zeros_like(ref)
    gates = _mlstm_gates(0, gate_ref, gb_ref)
    q_all, k_all = _mlstm_conv(p_ref, shift_ref, cw_ref, cb_ref, xprev_ref)
    for h in range(ML_HEADS):
        _drain(_mlstm_head(h, gates, q_all, k_all, p_ref, mhn_ref, hm_ref, c_ref, n_ref, m_ref))


def _lead_state(nat_lead, gates_lead, lead_blk, consts):
    _, _, _, gate_bias, shift, conv_w, conv_b, mh_norm = consts
    shapes = _state_shapes()
    return pl.pallas_call(
        _lead_state_kernel,
        grid=(1,),
        in_specs=[
            pl.BlockSpec((BLOCK, NAT_K), lambda i: (lead_blk, 0)),
            pl.BlockSpec((BLOCK, GATE_LANES), lambda i: (lead_blk, 0)),
            _whole((1, GATE_LANES)),
            _whole((CONV_WIDTH * BLOCK, 2 * BLOCK)),
            _whole((CONV_WIDTH, 2 * ML_QK_WIDTH)),
            _whole((1, 2 * ML_QK_WIDTH)),
            _whole((1, ML_WIDTH)),
        ],
        out_specs=[_whole(s.shape) for s in shapes],
        out_shape=shapes,
        scratch_shapes=[pltpu.VMEM((BLOCK, ML_WIDTH), BF16)],
        compiler_params=_params(1),
        name="lead_state",
    )(nat_lead, gates_lead, gate_bias, shift, conv_w, conv_b, mh_norm)


FUSED_TM = 512


def _inproj_mixer_kernel(*refs):
    x_ref, g_ref, wn_ref, wq_ref = refs[:4]
    mix_in = refs[4:4 + N_MIX_IN]
    state_in = refs[4 + N_MIX_IN:4 + N_MIX_IN + N_STATE]
    wu_ref = refs[4 + N_MIX_IN + N_STATE]
    nat_ref, t_ref, gate_ref, att_ref, hm_ref, wu_bf_ref = refs[5 + N_MIX_IN + N_STATE:-(1 + N_STATE)]
    u_ref = refs[-(1 + N_STATE)]
    state = refs[-N_STATE:]
    i = pl.program_id(0)
    j = pl.program_id(1)
    step = i * (IN_NAT_STEPS + 1) + j
    wu_bf_ref[...] = wu_ref[...].astype(BF16)

    @pl.when(j == 0)
    def _():
        u_ref[...] = _rms(x_ref[...], g_ref[...]).astype(BF16)

    @pl.when(step == 0)
    def _():
        _load_state(state, state_in)

    weights = [INPROJ_PHASES_PER_ROUND, MIXER_PHASES_PER_ROUND]

    @pl.when(j < IN_NAT_STEPS)
    def _():
        _drain(_interleave([_inproj_nat_phases(j, FUSED_TM, u_ref, wn_ref, nat_ref, gate_ref),
                            _mixer_phases(step + 1, mix_in, att_ref, hm_ref, state)], weights))

    @pl.when(j == IN_NAT_STEPS)
    def _():
        _drain(_interleave([_inproj_t_phases(FUSED_TM, u_ref, wq_ref, t_ref),
                            _mixer_phases(step + 1, mix_in, att_ref, hm_ref, state)], weights))


def _inproj_mixer(x2, row_tile0, g, w_nat, w_qv, mixer_args, w_up, lead_blk, nblk):
    n_tiles = nblk * BLOCK // FUSED_TM
    assert n_tiles * (IN_NAT_STEPS + 1) == nblk
    step = lambda i, j: i * (IN_NAT_STEPS + 1) + j
    proj_specs, proj_shapes = _inproj_out(FUSED_TM, n_tiles * FUSED_TM)
    mix_specs, mix_shapes = _mixer_out(step, nblk * BLOCK)
    wu_spec = pl.BlockSpec((D_MODEL, D_FF // nblk), lambda i, j: (0, step(i, j)))
    return pl.pallas_call(
        _inproj_mixer_kernel,
        grid=(n_tiles, IN_NAT_STEPS + 1),
        in_specs=[
            pl.BlockSpec((FUSED_TM, D_MODEL), lambda i, j: (row_tile0 + i, 0)),
            pl.BlockSpec((1, D_MODEL), lambda i, j: (0, 0)),
        ] + _inproj_weight_specs() + _mixer_in_specs(step, lead_blk) + [wu_spec],
        out_specs=proj_specs + mix_specs + [wu_spec],
        out_shape=proj_shapes + mix_shapes + [jax.ShapeDtypeStruct(w_up.shape, BF16)],
        scratch_shapes=[pltpu.VMEM((FUSED_TM, D_MODEL), BF16)] + _state_scratch(),
        compiler_params=_params(2),
        name="inproj_mixer",
    )(x2, g, w_nat, w_qv, *mixer_args, w_up)


def _mixer_kernel(*refs):
    mix_in = refs[:N_MIX_IN]
    state_in = refs[N_MIX_IN:N_MIX_IN + N_STATE]
    wo_ref, wd_ref = refs[N_MIX_IN + N_STATE:N_MIX_IN + N_STATE + 2]
    att_ref, hm_ref, wo_bf_ref, wd_bf_ref = refs[N_MIX_IN + N_STATE + 2:-N_STATE]
    state = refs[-N_STATE:]
    step = pl.program_id(0)

    @pl.when(step == 0)
    def _():
        _load_state(state, state_in)

    wo_bf_ref[...] = wo_ref[...].astype(BF16)
    wd_bf_ref[...] = wd_ref[...].astype(BF16)
    _drain(_mixer_phases(step + 1, mix_in, att_ref, hm_ref, state))


def _mixer(mixer_args, w_out, w_down, lead_blk, nblk):
    step = lambda s: s
    mix_specs, mix_shapes = _mixer_out(step, nblk * BLOCK)
    w_specs = [
        pl.BlockSpec((w_out.shape[0] // nblk, D_MODEL), lambda s: (s, 0)),
        pl.BlockSpec((D_FF // nblk, D_MODEL), lambda s: (s, 0)),
    ]
    return pl.pallas_call(
        _mixer_kernel,
        grid=(nblk,),
        in_specs=_mixer_in_specs(step, lead_blk) + w_specs,
        out_specs=mix_specs + w_specs,
        out_shape=mix_shapes + [jax.ShapeDtypeStruct(w.shape, BF16) for w in (w_out, w_down)],
        scratch_shapes=_state_scratch(),
        compiler_params=_params(1),
        name="mixer",
    )(*mixer_args, w_out, w_down)


OUT_TM = 512


def _outproj_kernel(att0_ref, att1_ref, hm0_ref, hm1_ref, x_ref, wa_ref, wm_ref, g_ref,
                    h_ref, u_ref):
    i = pl.program_id(0)
    half = pl.num_programs(0) // 2

    def body(att_ref, hm_ref):
        h = (x_ref[...]
             + lax.dot_general(att_ref[...], wa_ref[...], CONTRACT_FIRST, preferred_element_type=F32)
             + jnp.dot(hm_ref[...], wm_ref[...], preferred_element_type=F32))
        h_ref[...] = h
        u_ref[...] = _rms(h, g_ref[...]).astype(BF16)

    @pl.when(i < half)
    def _():
        body(att0_ref, hm0_ref)

    @pl.when(i >= half)
    def _():
        body(att1_ref, hm1_ref)


def _outproj(att0, att1, hm0, hm1, x2, w_out, g):
    rows = x2.shape[0]
    n = rows // OUT_TM
    half = n // 2
    first = lambda i: jnp.minimum(i, half - 1)
    second = lambda i: jnp.maximum(i - half, 0)
    return pl.pallas_call(
        _outproj_kernel,
        grid=(n,),
        in_specs=[
            pl.BlockSpec((ATT_WIDTH, OUT_TM), lambda i: (0, first(i))),
            pl.BlockSpec((ATT_WIDTH, OUT_TM), lambda i: (0, second(i))),
            pl.BlockSpec((OUT_TM, ML_WIDTH), lambda i: (first(i), 0)),
            pl.BlockSpec((OUT_TM, ML_WIDTH), lambda i: (second(i), 0)),
            pl.BlockSpec((OUT_TM, D_MODEL), lambda i: (i, 0)),
            pl.BlockSpec((ATT_WIDTH, D_MODEL), lambda i: (0, 0)),
            pl.BlockSpec((ML_WIDTH, D_MODEL), lambda i: (ATT_WIDTH // ML_WIDTH, 0)),
            pl.BlockSpec((1, D_MODEL), lambda i: (0, 0)),
        ],
        out_specs=[
            pl.BlockSpec((OUT_TM, D_MODEL), lambda i: (i, 0)),
            pl.BlockSpec((OUT_TM, D_MODEL), lambda i: (i, 0)),
        ],
        out_shape=[
            jax.ShapeDtypeStruct((rows, D_MODEL), F32),
            jax.ShapeDtypeStruct((rows, D_MODEL), BF16),
        ],
        compiler_params=_params(1),
        name="outproj",
    )(att0, att1, hm0, hm1, x2, w_out, w_out, g)


MLP_TM = 1024
MLP_TF = 1024


def _mlp_step(first, u_ref, wu_ref, wd_ref, o_ref):
    acts = []
    for c in range(MLP_TF // MXU_COLS):
        cs = slice(c * MXU_COLS, (c + 1) * MXU_COLS)
        a = jnp.dot(u_ref[...], wu_ref[:, cs], preferred_element_type=F32)
        acts.append(jnp.square(jnp.maximum(a, 0.0)).astype(BF16))
        yield
    act = jnp.concatenate(acts, axis=1)
    for c in range(D_MODEL // MXU_COLS):
        cs = slice(c * MXU_COLS, (c + 1) * MXU_COLS)
        d = jnp.dot(act, wd_ref[:, cs], preferred_element_type=F32)
        if first:
            o_ref[:, cs] = d
        else:
            o_ref[:, cs] += d
        yield


def _mlp_kernel(u_ref, h_hbm, wu_ref, wd_ref, g_ref, o_ref, hbuf_ref, hsem):
    i = pl.program_id(0)
    j = pl.program_id(1)
    r0 = pl.multiple_of(i * MLP_TM, MLP_TM)
    h_copy = pltpu.make_async_copy(h_hbm.at[pl.ds(r0, MLP_TM), :], hbuf_ref, hsem)

    @pl.when(j == 0)
    def _():
        h_copy.start()
        for _ in _mlp_step(True, u_ref, wu_ref, wd_ref, o_ref):
            pass

    @pl.when(j > 0)
    def _():
        for _ in _mlp_step(False, u_ref, wu_ref, wd_ref, o_ref):
            pass

    @pl.when(j == pl.num_programs(1) - 1)
    def _():
        h_copy.wait()
        o_ref[...] = _rms(o_ref[...] + hbuf_ref[...], g_ref[...])


def _mlp(u, h, w_up, w_down, g):
    rows = u.shape[0]
    return pl.pallas_call(
        _mlp_kernel,
        grid=(rows // MLP_TM, D_FF // MLP_TF),
        in_specs=[
            pl.BlockSpec((MLP_TM, D_MODEL), lambda i, j: (i, 0)),
            pl.BlockSpec(memory_space=pl.ANY),
            pl.BlockSpec((D_MODEL, MLP_TF), lambda i, j: (0, j)),
            pl.BlockSpec((MLP_TF, D_MODEL), lambda i, j: (j, 0)),
            pl.BlockSpec((1, D_MODEL), lambda i, j: (0, 0)),
        ],
        out_specs=pl.BlockSpec((MLP_TM, D_MODEL), lambda i, j: (i, 0)),
        out_shape=jax.ShapeDtypeStruct((rows, D_MODEL), F32),
        scratch_shapes=[pltpu.VMEM((MLP_TM, D_MODEL), F32), pltpu.SemaphoreType.DMA(())],
        compiler_params=_params(2),
        name="mlp",
    )(u, h, w_up, w_down, g)


def _t5_bucket(dist):
    max_exact = N_BUCKETS // 2
    d = jnp.maximum(dist, 0)
    ratio = jnp.maximum(d, max_exact).astype(F32) / max_exact
    large = max_exact + (jnp.log(ratio) / math.log(MAX_DISTANCE / max_exact)
                         * (N_BUCKETS - max_exact)).astype(jnp.int32)
    large = jnp.minimum(large, N_BUCKETS - 1)
    return jnp.where(d < max_exact, d, large)


BIAS_ROWS = 16


def _bias_kernel(rb_ref, bkt_band_ref, bkt_meta_ref, band_ref, meta_ref, splat_ref):
    def fill(b, carry):
        for hd in range(ATT_HEADS):
            splat_ref[b * ATT_HEADS + hd] = jnp.full((SUBLANES, BLOCK), rb_ref[b, hd], F32) * LOG2E
        return carry

    lax.fori_loop(0, N_BUCKETS, fill, 0)

    def lookup(bkt):
        def body(b, accs):
            hit = bkt == b
            out = []
            for hd, a in enumerate(accs):
                sp = splat_ref[b * ATT_HEADS + hd]
                out.append(jnp.where(hit, jnp.concatenate([sp] * (BIAS_ROWS // SUBLANES), axis=0), a))
            return tuple(out)
        init = tuple(jnp.zeros(bkt.shape, F32) for _ in range(ATT_HEADS))
        return lax.fori_loop(0, N_BUCKETS, body, init, unroll=4)

    def band_rows(ci, carry):
        r0 = pl.multiple_of(ci * BIAS_ROWS, BIAS_ROWS)
        accs = lookup(bkt_band_ref[pl.ds(r0, BIAS_ROWS), :])
        for hd in range(ATT_HEADS):
            h, g = divmod(hd, ATT_GROUP)
            band_ref[h, pl.ds(r0, BIAS_ROWS), g * BLOCK:(g + 1) * BLOCK] = accs[hd]
        return carry

    lax.fori_loop(0, 2 * BLOCK // BIAS_ROWS, band_rows, 0)

    def meta_block(n, carry):
        accs = lookup(bkt_meta_ref[n])
        for hd in range(ATT_HEADS):
            h, g = divmod(hd, ATT_GROUP)
            meta_ref[n, h, :, g * BLOCK:(g + 1) * BLOCK] = accs[hd]
        return carry

    lax.fori_loop(0, meta_ref.shape[0], meta_block, 0)


def _bias_tables(rel_bias, nblk):
    c = jnp.arange(2 * BLOCK)[:, None]
    r = jnp.arange(BLOCK)[None, :]
    bkt_band = _t5_bucket(r + BLOCK - c).astype(jnp.int32)
    q_pos = BLOCK + jnp.arange(nblk * BLOCK)[None, :]
    m_pos = N_PAD + jnp.arange(N_META)[:, None]
    bkt_meta = _t5_bucket(q_pos - m_pos).astype(jnp.int32)
    bkt_meta = bkt_meta.reshape(N_META, nblk, BLOCK).transpose(1, 0, 2)
    vmem = pl.BlockSpec(memory_space=pltpu.VMEM)
    return pl.pallas_call(
        _bias_kernel,
        in_specs=[pl.BlockSpec(memory_space=pltpu.SMEM), vmem, vmem],
        out_specs=[vmem, vmem],
        out_shape=[
            jax.ShapeDtypeStruct((ATT_KV_HEADS, 2 * BLOCK, GQ), F32),
            jax.ShapeDtypeStruct((nblk, ATT_KV_HEADS, N_META, GQ), F32),
        ],
        scratch_shapes=[pltpu.VMEM((N_BUCKETS * ATT_HEADS, SUBLANES, BLOCK), F32)],
        name="bias_tables",
    )(rel_bias.astype(F32), bkt_band, bkt_meta)


W_IN_OFFS = dict(q=0, k=1024, v=1280, mq=1536, mk=2048, mv=2560, mo=3584, gates=4608)
NAT_SRC = ([W_IN_OFFS["mq"] // LANE + t for t in range((W_IN_OFFS["gates"] - W_IN_OFFS["mq"]) // LANE)]
           + [W_IN_OFFS["k"] // LANE + t for t in range(ATT_KV_WIDTH // LANE)]
           + [W_IN_OFFS["gates"] // LANE])
QV_SRC = ([W_IN_OFFS["q"] // LANE + t for t in range(ATT_WIDTH // LANE)]
          + [W_IN_OFFS["v"] // LANE + t for t in range(ATT_KV_WIDTH // LANE)])
N_NAT_BLK = len(NAT_SRC)
N_Q_BLK = ATT_WIDTH // LANE


def _win_prep_kernel(src_ref, w_ref, nat_ref, qv_ref):
    j = pl.program_id(0)
    wv = w_ref[0]

    @pl.when(j < N_NAT_BLK - 1)
    def _():
        nat_ref[...] = wv.astype(BF16)

    @pl.when(j == N_NAT_BLK - 1)
    def _():
        row = lax.broadcasted_iota(jnp.int32, wv.shape, 0)
        nat_ref[...] = jnp.where(row < 2 * ML_HEADS, wv, 0.0).astype(BF16)

    @pl.when((j >= N_NAT_BLK) & (j < N_NAT_BLK + N_Q_BLK))
    def _():
        qv_ref[...] = (wv * (ATT_HEAD_DIM ** -0.5 * LOG2E)).astype(BF16)

    @pl.when(j >= N_NAT_BLK + N_Q_BLK)
    def _():
        qv_ref[...] = wv.astype(BF16)


def _win_prep(w_in_t):
    src = jnp.asarray(NAT_SRC + QV_SRC, jnp.int32)
    d = w_in_t.shape[2]
    return pl.pallas_call(
        _win_prep_kernel,
        grid_spec=pltpu.PrefetchScalarGridSpec(
            num_scalar_prefetch=1,
            grid=(len(NAT_SRC) + len(QV_SRC),),
            in_specs=[pl.BlockSpec((1, LANE, d), lambda j, src: (0, src[j], 0))],
            out_specs=[
                pl.BlockSpec((LANE, d), lambda j, src: (jnp.minimum(j, N_NAT_BLK - 1), 0)),
                pl.BlockSpec((LANE, d), lambda j, src: (jnp.maximum(j - N_NAT_BLK, 0), 0)),
            ],
        ),
        out_shape=[
            jax.ShapeDtypeStruct((NAT_COLS, d), BF16),
            jax.ShapeDtypeStruct((T_ROWS, d), BF16),
        ],
        compiler_params=_params(1),
        name="w_in_prep",
    )(src, w_in_t)


def kernel(x, meta_tokens, w_in, conv_w, conv_b, b_igate, b_fgate, attn_sinks, rel_bias,
           mh_norm, w_out, norm_mix, norm_mlp, w_up, w_down, norm_final):
    batch, seq, d = x.shape
    assert batch == 2, "the two batches are processed one projection / mixer stage apart"
    nblk = seq // BLOCK
    x2 = x.reshape(batch * seq, d)
    lead = jnp.concatenate([jnp.zeros((N_PAD, d), x.dtype), meta_tokens.astype(x.dtype)], axis=0)
    g_mix = norm_mix[0][None].astype(F32)

    w_nat, w_qv = _win_prep(jnp.swapaxes(w_in, 1, 2))
    gate_bias = jnp.pad(jnp.concatenate([b_igate[0], b_fgate[0]]).astype(F32),
                        (0, GATE_LANES - 2 * ML_HEADS))[None]
    bias_band_t, bias_meta_t = _bias_tables(rel_bias, nblk)
    consts = (attn_sinks[0].astype(F32), bias_band_t, bias_meta_t, gate_bias, _conv_shift_matrix(),
              conv_w[0].astype(F32), conv_b[0][None].astype(F32), mh_norm[0][None].astype(F32))

    (nat0, projt0, gates0), (nat_l, projt_l, gates_l) = _inproj(x2, lead, g_mix, w_nat, w_qv,
                                                                seq // IN_TM)
    lead_blk = 0
    state = _lead_state(nat_l, gates_l, lead_blk, consts)
    args0 = _mixer_args(consts, nat0, projt0, gates0, nat_l, projt_l, state)
    nat1, projt1, gates1, att0, hm0, w_up_bf = _inproj_mixer(
        x2, seq // FUSED_TM, g_mix, w_nat, w_qv, args0, w_up[0], lead_blk, nblk)
    args1 = _mixer_args(consts, nat1, projt1, gates1, nat_l, projt_l, state)
    att1, hm1, w_out_bf, w_down_bf = _mixer(args1, w_out[0], w_down[0], lead_blk, nblk)

    h2, u2 = _outproj(att0, att1, hm0, hm1, x2, w_out_bf, norm_mlp[0][None].astype(F32))
    out = _mlp(u2, h2, w_up_bf, w_down_bf, norm_final[None].astype(F32))
    return out.reshape(batch, seq, d)
```

```python
import math

import numpy as np
import jax
import jax.numpy as jnp
from jax import lax
from jax.experimental import pallas as pl
from jax.experimental.pallas import tpu as pltpu

F32 = jnp.float32
BF16 = jnp.bfloat16

D_MODEL = 2048
N_META = 16
BLOCK = 128
N_PAD = BLOCK - N_META
WINDOW = 128
ATT_HEADS = 16
ATT_KV_HEADS = 4
ATT_GROUP = ATT_HEADS // ATT_KV_HEADS
ATT_HEAD_DIM = 64
ATT_WIDTH = ATT_HEADS * ATT_HEAD_DIM
ATT_KV_WIDTH = ATT_KV_HEADS * ATT_HEAD_DIM
ML_HEADS = 4
ML_V_DIM = 256
ML_QK_DIM = 128
ML_WIDTH = ML_HEADS * ML_V_DIM
ML_QK_WIDTH = ML_HEADS * ML_QK_DIM
CONV_WIDTH = 4
GATE_SOFTCAP = 15.0
D_FF = 4 * D_MODEL
N_BUCKETS = 32
MAX_DISTANCE = 128
EPS = 1e-6
NEG = -1e30
LOG2E = math.log2(math.e)

GATE_LANES = 128
NAT_MQK = 0
NAT_MV = 2 * ML_QK_WIDTH
NAT_MO = NAT_MV + ML_WIDTH
NAT_K = NAT_MO + ML_WIDTH
NAT_COLS = NAT_K + ATT_KV_WIDTH + GATE_LANES
T_ROWS = ATT_WIDTH + ATT_KV_WIDTH

V7X_VMEM_LIMIT = 60 * 1024 * 1024
LANE = 128
SUBLANES = 8
CONTRACT_LAST = (((1,), (1,)), ((), ()))
CONTRACT_FIRST = (((0,), (0,)), ((), ()))


def _params(n_axes, vmem=V7X_VMEM_LIMIT):
    return pltpu.CompilerParams(dimension_semantics=("arbitrary",) * n_axes,
                                vmem_limit_bytes=vmem)


def _rms(x, g):
    return x * lax.rsqrt(jnp.mean(x * x, axis=-1, keepdims=True) + EPS) * g


_DONE = object()


def _drain(phases):
    for _ in phases:
        pass


def _interleave(streams, weights):
    live = list(zip(streams, weights))
    while live:
        for item in list(live):
            stream, weight = item
            for _ in range(weight):
                if next(stream, _DONE) is _DONE:
                    live.remove(item)
                    break
        yield


IN_TM = 1024
IN_NAT_STEPS = 3
IN_TN = NAT_COLS // IN_NAT_STEPS
MXU_COLS = 256


def _inproj_nat_phases(j, rows, u_ref, wn_ref, nat_ref, gate_ref):
    u = u_ref[0:rows, :]
    for c0 in range(0, IN_TN, MXU_COLS):
        cs = slice(c0, min(c0 + MXU_COLS, IN_TN))
        r = lax.dot_general(u, wn_ref[cs, :], CONTRACT_LAST, preferred_element_type=F32)
        nat_ref[0:rows, cs] = r.astype(BF16)
        if cs.stop == IN_TN:
            assert cs.stop - cs.start == GATE_LANES

            @pl.when(j == IN_NAT_STEPS - 1)
            def _():
                gate_ref[0:rows, :] = r
        yield


def _inproj_t_phases(rows, u_ref, wq_ref, t_ref):
    u = u_ref[0:rows, :]
    for c0 in range(0, T_ROWS, MXU_COLS):
        cs = slice(c0, c0 + MXU_COLS)
        t_ref[cs, 0:rows] = lax.dot_general(wq_ref[cs, :], u, CONTRACT_LAST,
                                            preferred_element_type=F32).astype(BF16)
        yield


BIAS_PHASES_PER_ROUND = 3


def _inproj_kernel(x_ref, lead_ref, g_ref, wn_ref, wq_ref, rb_ref, bkt_band_ref, bkt_meta_ref,
                   nat_ref, t_ref, gate_ref, nat_lead_ref, t_lead_ref, gate_lead_ref,
                   band_ref, meta_ref, u_ref, u_lead_ref, splat_ref):
    i = pl.program_id(0)
    j = pl.program_id(1)

    @pl.when(j == 0)
    def _():
        u_ref[...] = _rms(x_ref[...], g_ref[...]).astype(BF16)

    @pl.when((j == 0) & (i == 0))
    def _():
        u_lead_ref[...] = _rms(lead_ref[...], g_ref[...]).astype(BF16)
        _bias_fill_splats(rb_ref, splat_ref)

    def with_bias(phases):
        bias = _bias_step_phases(bkt_band_ref, bkt_meta_ref, band_ref, meta_ref, splat_ref)
        _drain(_interleave([phases, bias], [1, BIAS_PHASES_PER_ROUND]))

    @pl.when(j < IN_NAT_STEPS)
    def _():
        with_bias(_inproj_nat_phases(j, IN_TM, u_ref, wn_ref, nat_ref, gate_ref))

    @pl.when(j == IN_NAT_STEPS)
    def _():
        with_bias(_inproj_t_phases(IN_TM, u_ref, wq_ref, t_ref))

    @pl.when((i == 0) & (j < IN_NAT_STEPS))
    def _():
        _drain(_inproj_nat_phases(j, BLOCK, u_lead_ref, wn_ref, nat_lead_ref, gate_lead_ref))

    @pl.when((i == 0) & (j == IN_NAT_STEPS))
    def _():
        _drain(_inproj_t_phases(BLOCK, u_lead_ref, wq_ref, t_lead_ref))


def _inproj_weight_specs():
    last_nat = IN_NAT_STEPS - 1
    return [
        pl.BlockSpec((IN_TN, D_MODEL), lambda i, j: (jnp.minimum(j, last_nat), 0)),
        pl.BlockSpec((T_ROWS, D_MODEL), lambda i, j: (0, 0)),
    ]


def _inproj_out(tm, rows):
    last_nat = IN_NAT_STEPS - 1
    specs = [
        pl.BlockSpec((tm, IN_TN), lambda i, j: (i, jnp.minimum(j, last_nat))),
        pl.BlockSpec((T_ROWS, tm), lambda i, j: (0, i)),
        pl.BlockSpec((tm, GATE_LANES), lambda i, j: (i, 0)),
    ]
    shapes = [
        jax.ShapeDtypeStruct((rows, NAT_COLS), BF16),
        jax.ShapeDtypeStruct((T_ROWS, rows), BF16),
        jax.ShapeDtypeStruct((rows, GATE_LANES), F32),
    ]
    return specs, shapes


def _inproj(x2, lead, g, w_nat, w_qv, rel_bias, n_tiles, nblk):
    out_specs, out_shape = _inproj_out(IN_TM, n_tiles * IN_TM)
    last_nat = IN_NAT_STEPS - 1
    n_steps = n_tiles * (IN_NAT_STEPS + 1)
    assert n_steps * BIAS_ROWS == 2 * BLOCK and nblk % n_steps == 0
    meta_per_step = nblk // n_steps
    bkt_band, bkt_meta = _bias_buckets(nblk)
    step = lambda i, j: i * (IN_NAT_STEPS + 1) + j
    bias_in_specs = [
        pl.BlockSpec(memory_space=pltpu.SMEM),
        pl.BlockSpec((BIAS_ROWS, BLOCK), lambda i, j: (step(i, j), 0)),
        pl.BlockSpec((meta_per_step, N_META, BLOCK), lambda i, j: (step(i, j), 0, 0)),
    ]
    bias_out_specs = [
        pl.BlockSpec((ATT_KV_HEADS, BIAS_ROWS, GQ), lambda i, j: (0, step(i, j), 0)),
        pl.BlockSpec((meta_per_step, ATT_KV_HEADS, N_META, GQ), lambda i, j: (step(i, j), 0, 0, 0)),
    ]
    bias_shapes = [
        jax.ShapeDtypeStruct((ATT_KV_HEADS, 2 * BLOCK, GQ), F32),
        jax.ShapeDtypeStruct((nblk, ATT_KV_HEADS, N_META, GQ), F32),
    ]
    lead_col = lambda i, j: jnp.where(i == 0, jnp.minimum(j, last_nat), last_nat)
    lead_specs = [
        pl.BlockSpec((BLOCK, IN_TN), lambda i, j: (0, lead_col(i, j))),
        pl.BlockSpec((T_ROWS, BLOCK), lambda i, j: (0, 0)),
        pl.BlockSpec((BLOCK, GATE_LANES), lambda i, j: (0, 0)),
    ]
    lead_shapes = [
        jax.ShapeDtypeStruct((BLOCK, NAT_COLS), BF16),
        jax.ShapeDtypeStruct((T_ROWS, BLOCK), BF16),
        jax.ShapeDtypeStruct((BLOCK, GATE_LANES), F32),
    ]
    outs = pl.pallas_call(
        _inproj_kernel,
        grid=(n_tiles, IN_NAT_STEPS + 1),
        in_specs=[
            pl.BlockSpec((IN_TM, D_MODEL), lambda i, j: (i, 0)),
            pl.BlockSpec((BLOCK, D_MODEL), lambda i, j: (0, 0)),
            pl.BlockSpec((1, D_MODEL), lambda i, j: (0, 0)),
        ] + _inproj_weight_specs() + bias_in_specs,
        out_specs=out_specs + lead_specs + bias_out_specs,
        out_shape=out_shape + lead_shapes + bias_shapes,
        scratch_shapes=[pltpu.VMEM((IN_TM, D_MODEL), BF16), pltpu.VMEM((BLOCK, D_MODEL), BF16),
                        pltpu.VMEM((N_BUCKETS * ATT_HEADS, SUBLANES, BLOCK), F32)],
        compiler_params=_params(2),
        name="inproj",
    )(x2, lead, g, w_nat, w_qv, rel_bias.astype(F32), bkt_band, bkt_meta)
    return outs[:3], outs[3:6], outs[6:]


GQ = ATT_GROUP * BLOCK
ATT_SLABS_PER_PHASE = 2


def _attn_prologue(n, kc_ref, kp_ref, km_ref):
    c = lax.broadcasted_iota(jnp.int32, (2 * BLOCK, BLOCK), 0)
    r = lax.broadcasted_iota(jnp.int32, (2 * BLOCK, BLOCK), 1)
    dist = r + BLOCK - c
    band_ok = (dist >= 0) & (dist < WINDOW) & ((c >= BLOCK) | (n >= 1))
    mm = lax.broadcasted_iota(jnp.int32, (N_META, BLOCK), 0)
    rr = lax.broadcasted_iota(jnp.int32, (N_META, BLOCK), 1)
    meta_ok = (n + 1) * BLOCK + rr >= N_PAD + mm
    kall = jnp.concatenate([kp_ref[...], kc_ref[...], km_ref[...]], axis=0)
    return band_ok, meta_ok, kall


def _attn_head(h, pro, sink_ref, qt_ref, vc_ref, vp_ref, vl_ref, bb_ref, bm_ref, o_ref):
    band_ok, meta_ok, kall = pro
    lo = (h // 2) * LANE
    kpair = kall[:, lo:lo + LANE]
    qh = jnp.concatenate(
        [qt_ref[(ATT_GROUP * h + g) * ATT_HEAD_DIM:(ATT_GROUP * h + g + 1) * ATT_HEAD_DIM, :]
         for g in range(ATT_GROUP)], axis=1)
    zq = jnp.zeros_like(qh)
    qz = jnp.concatenate([qh, zq] if h % 2 == 0 else [zq, qh], axis=0)
    st = jnp.dot(kpair, qz, preferred_element_type=F32)
    yield
    zpad = jnp.zeros((N_PAD, BLOCK), BF16)
    p_band, p_lead, inv = [], [], []
    for g in range(ATT_GROUP):
        gs = slice(g * BLOCK, (g + 1) * BLOCK)
        sl = st[:, gs]
        sink = jnp.full((1, BLOCK), sink_ref[ATT_GROUP * h + g], F32) * LOG2E
        lb = jnp.where(band_ok, sl[:2 * BLOCK] + bb_ref[h, :, gs], NEG * LOG2E)
        lm = jnp.where(meta_ok, sl[2 * BLOCK:] + bm_ref[0, h, :, gs], NEG * LOG2E)
        m = jnp.maximum(jnp.max(lb, axis=0, keepdims=True), jnp.max(lm, axis=0, keepdims=True))
        m = jnp.maximum(m, sink)
        pb = jnp.exp2(lb - m)
        pm = jnp.exp2(lm - m)
        den = (jnp.sum(pb, axis=0, keepdims=True) + jnp.sum(pm, axis=0, keepdims=True)
               + jnp.exp2(sink - m))
        inv.append(1.0 / den)
        p_band.append(pb.astype(BF16))
        p_lead.append(jnp.concatenate([zpad, pm.astype(BF16)], axis=0))
        if g % ATT_SLABS_PER_PHASE == ATT_SLABS_PER_PHASE - 1:
            yield
    pbt = jnp.concatenate(p_band, axis=1)
    plt = jnp.concatenate(p_lead, axis=1)
    hs = slice(h * ATT_HEAD_DIM, (h + 1) * ATT_HEAD_DIM)
    vband = jnp.concatenate([vp_ref[hs, :], vc_ref[hs, :]], axis=1)
    ot = (jnp.dot(vband, pbt, preferred_element_type=F32)
          + jnp.dot(vl_ref[hs, :], plt, preferred_element_type=F32))
    for g in range(ATT_GROUP):
        row = (ATT_GROUP * h + g) * ATT_HEAD_DIM
        o_ref[row:row + ATT_HEAD_DIM, :] = (
            ot[:, g * BLOCK:(g + 1) * BLOCK] * inv[g]).astype(BF16)


def _conv_shift_matrix():
    s = np.zeros((CONV_WIDTH * BLOCK, 2 * BLOCK), np.float32)
    for k in range(CONV_WIDTH):
        t = np.arange(BLOCK)
        s[k * BLOCK + t, BLOCK + t - k] = 1.0
    return jnp.asarray(s, BF16)


def _mlstm_gates(n, gate_ref, gb_ref):
    t_idx = lax.broadcasted_iota(jnp.int32, (BLOCK, GATE_LANES), 0)
    valid = (n > 0) | (t_idx >= N_PAD)
    pre = GATE_SOFTCAP * jnp.tanh((gate_ref[...] + gb_ref[...]) / GATE_SOFTCAP)
    log_i = jnp.where(valid, pre, NEG)
    log_sig = jnp.minimum(pre, 0.0) - jnp.log1p(jnp.exp(-jnp.abs(pre)))
    log_f = jnp.where(valid, log_sig, 0.0)
    row = lax.broadcasted_iota(jnp.int32, (BLOCK, BLOCK), 0)
    col = lax.broadcasted_iota(jnp.int32, (BLOCK, BLOCK), 1)
    causal = col <= row
    tril = jnp.where(causal, 1.0, 0.0).astype(F32)
    b_all = jnp.dot(tril, log_f, preferred_element_type=F32,
                    precision=lax.Precision.HIGHEST)
    return log_i, b_all, log_i.T, b_all.T, causal


def _mlstm_conv(p_ref, shift_ref, cw_ref, cb_ref, xprev_ref):
    xcur = p_ref[:, NAT_MQK:NAT_MV]
    xcat = jnp.concatenate([xprev_ref[...], xcur], axis=0)
    sh = jnp.dot(shift_ref[...], xcat, preferred_element_type=F32)
    xprev_ref[...] = xcur
    y = cb_ref[...]
    for k in range(CONV_WIDTH):
        y = y + cw_ref[CONV_WIDTH - 1 - k:CONV_WIDTH - k, :] * sh[k * BLOCK:(k + 1) * BLOCK]
    act = y * jax.nn.sigmoid(y)
    return act[:, :ML_QK_WIDTH] * (ML_QK_DIM ** -0.5), act[:, ML_QK_WIDTH:]


def _mlstm_head(h, gates, q_all, k_all, p_ref, mhn_ref, o_ref, c_ref, n_ref, m_ref):
    log_i, b_all, log_i_t, b_all_t, causal = gates
    q_f = q_all[:, h * ML_QK_DIM:(h + 1) * ML_QK_DIM]
    k_f = k_all[:, h * ML_QK_DIM:(h + 1) * ML_QK_DIM]
    q_h = q_f.astype(BF16)
    k_h = k_f.astype(BF16)
    v_h = p_ref[:, NAT_MV + h * ML_V_DIM:NAT_MV + (h + 1) * ML_V_DIM]
    b_col = b_all[:, ML_HEADS + h:ML_HEADS + h + 1]
    li_col = log_i[:, h:h + 1]
    b_row = b_all_t[ML_HEADS + h:ML_HEADS + h + 1, :]
    li_row = log_i_t[h:h + 1, :]
    b_last = b_all[BLOCK - 1:BLOCK, ML_HEADS + h:ML_HEADS + h + 1]
    c_prev = c_ref[h]
    n_prev = n_ref[h][0:1, :]
    m_prev = m_ref[h][0:1, 0:1]

    dmat = jnp.where(causal, b_col - b_row + li_row, -jnp.inf)
    inter = b_col + m_prev
    m_t = jnp.maximum(inter, jnp.max(dmat, axis=1, keepdims=True))
    s = lax.dot_general(q_h, k_h, CONTRACT_LAST, preferred_element_type=F32) * jnp.exp(dmat - m_t)
    yield
    a_t = jnp.exp(inter - m_t)
    num = (jnp.dot(s.astype(BF16), v_h, preferred_element_type=F32)
           + a_t * jnp.dot(q_h, c_prev.astype(BF16), preferred_element_type=F32))
    den = (jnp.sum(s, axis=1, keepdims=True)
           + a_t * jnp.sum(q_f * n_prev, axis=1, keepdims=True))
    den = jnp.maximum(jnp.abs(den), jnp.exp(-m_t))
    hh = num / den
    hh = hh * lax.rsqrt(jnp.mean(hh * hh, axis=-1, keepdims=True) + EPS)
    vs = slice(h * ML_V_DIM, (h + 1) * ML_V_DIM)
    hh = hh * mhn_ref[:, vs]
    m_o = p_ref[:, NAT_MO + h * ML_V_DIM:NAT_MO + (h + 1) * ML_V_DIM].astype(F32)
    o_ref[:, vs] = (hh * jax.nn.sigmoid(m_o)).astype(BF16)
    yield

    g_col = b_last - b_col + li_col
    m_loc = jnp.max(g_col, axis=0, keepdims=True)
    wk = jnp.exp(g_col - m_loc) * k_f
    c_loc = lax.dot_general(wk.astype(BF16), v_h, CONTRACT_FIRST,
                            preferred_element_type=F32)
    n_loc = jnp.sum(wk, axis=0, keepdims=True)
    m_new = jnp.maximum(b_last + m_prev, m_loc)
    a = jnp.exp(b_last + m_prev - m_new)
    cc = jnp.exp(m_loc - m_new)
    c_ref[h] = a * c_prev + cc * c_loc
    n_ref[h] = jnp.broadcast_to(a * n_prev + cc * n_loc, (8, ML_QK_DIM))
    m_ref[h] = jnp.broadcast_to(m_new, (8, LANE))


N_MIX_IN = 17
N_STATE = 4
INPROJ_PHASES_PER_ROUND = 1
MIXER_PHASES_PER_ROUND = 4


def _mixer_phases(n, mix_in, att_ref, hm_ref, state):
    (sink_ref, qt_ref, kc_ref, kp_ref, km_ref, vc_ref, vp_ref, vl_ref, bb_ref, bm_ref,
     p_ref, gate_ref, gb_ref, shift_ref, cw_ref, cb_ref, mhn_ref) = mix_in
    c_ref, n_ref, m_ref, xprev_ref = state

    def attention_phases():
        pro = _attn_prologue(n - 1, kc_ref, kp_ref, km_ref)
        for h in range(ATT_KV_HEADS):
            yield from _attn_head(h, pro, sink_ref, qt_ref, vc_ref, vp_ref, vl_ref, bb_ref, bm_ref,
                                  att_ref)

    def mlstm_phases():
        gates = _mlstm_gates(n, gate_ref, gb_ref)
        yield
        q_all, k_all = _mlstm_conv(p_ref, shift_ref, cw_ref, cb_ref, xprev_ref)
        yield
        for h in range(ML_HEADS):
            yield from _mlstm_head(h, gates, q_all, k_all, p_ref, mhn_ref, hm_ref, c_ref, n_ref, m_ref)

    return _interleave([mlstm_phases(), attention_phases()], [1, 1])


def _load_state(state, state_in):
    for dst, src in zip(state, state_in):
        dst[...] = src[...]


def _state_shapes():
    return [
        jax.ShapeDtypeStruct((ML_HEADS, ML_QK_DIM, ML_V_DIM), F32),
        jax.ShapeDtypeStruct((ML_HEADS, SUBLANES, ML_QK_DIM), F32),
        jax.ShapeDtypeStruct((ML_HEADS, SUBLANES, LANE), F32),
        jax.ShapeDtypeStruct((BLOCK, 2 * ML_QK_WIDTH), BF16),
    ]


def _state_scratch():
    return [pltpu.VMEM(s.shape, s.dtype) for s in _state_shapes()]


def _whole(shape):
    return pl.BlockSpec(shape, lambda *ids: (0,) * len(shape))


def _mixer_in_specs(step, lead_blk):
    cur = lambda *ids: step(*ids)
    prev = lambda *ids: jnp.maximum(step(*ids) - 1, 0)
    k_col = NAT_K // ATT_KV_WIDTH
    v_row = ATT_WIDTH // ATT_KV_WIDTH
    meta_rows = lead_blk * (BLOCK // N_META) + N_PAD // N_META
    return [
        pl.BlockSpec(memory_space=pltpu.SMEM),
        pl.BlockSpec((ATT_WIDTH, BLOCK), lambda *ids: (0, cur(*ids))),
        pl.BlockSpec((BLOCK, ATT_KV_WIDTH), lambda *ids: (cur(*ids), k_col)),
        pl.BlockSpec((BLOCK, ATT_KV_WIDTH), lambda *ids: (prev(*ids), k_col)),
        pl.BlockSpec((N_META, ATT_KV_WIDTH), lambda *ids: (meta_rows, k_col)),
        pl.BlockSpec((ATT_KV_WIDTH, BLOCK), lambda *ids: (v_row, cur(*ids))),
        pl.BlockSpec((ATT_KV_WIDTH, BLOCK), lambda *ids: (v_row, prev(*ids))),
        pl.BlockSpec((ATT_KV_WIDTH, BLOCK), lambda *ids: (v_row, lead_blk)),
        _whole((ATT_KV_HEADS, 2 * BLOCK, GQ)),
        pl.BlockSpec((1, ATT_KV_HEADS, N_META, GQ), lambda *ids: (cur(*ids), 0, 0, 0)),
        pl.BlockSpec((BLOCK, NAT_K), lambda *ids: (cur(*ids), 0)),
        pl.BlockSpec((BLOCK, GATE_LANES), lambda *ids: (cur(*ids), 0)),
        _whole((1, GATE_LANES)),
        _whole((CONV_WIDTH * BLOCK, 2 * BLOCK)),
        _whole((CONV_WIDTH, 2 * ML_QK_WIDTH)),
        _whole((1, 2 * ML_QK_WIDTH)),
        _whole((1, ML_WIDTH)),
    ] + [_whole(s.shape) for s in _state_shapes()]


def _mixer_args(consts, nat_b, projt_b, gates_b, nat_lead, projt_lead, state):
    sinks, bias_band_t, bias_meta_t, gate_bias, shift, conv_w, conv_b, mh_norm = consts
    return [sinks, projt_b, nat_b, nat_b, nat_lead, projt_b, projt_b, projt_lead, bias_band_t,
            bias_meta_t, nat_b, gates_b, gate_bias, shift, conv_w, conv_b, mh_norm] + list(state)


def _mixer_out(step, n_rows):
    specs = [
        pl.BlockSpec((ATT_WIDTH, BLOCK), lambda *ids: (0, step(*ids))),
        pl.BlockSpec((BLOCK, ML_WIDTH), lambda *ids: (step(*ids), 0)),
    ]
    shapes = [
        jax.ShapeDtypeStruct((ATT_WIDTH, n_rows), BF16),
        jax.ShapeDtypeStruct((n_rows, ML_WIDTH), BF16),
    ]
    return specs, shapes


def _lead_state_kernel(p_ref, gate_ref, gb_ref, shift_ref, cw_ref, cb_ref, mhn_ref,
                       c_ref, n_ref, m_ref, xprev_ref, hm_ref):
    for ref in (c_ref, n_ref, m_ref, xprev_ref):
        ref[...] = jnp.zeros_like(ref)
    gates = _mlstm_gates(0, gate_ref, gb_ref)
    q_all, k_all = _mlstm_conv(p_ref, shift_ref, cw_ref, cb_ref, xprev_ref)
    for h in range(ML_HEADS):
        _drain(_mlstm_head(h, gates, q_all, k_all, p_ref, mhn_ref, hm_ref, c_ref, n_ref, m_ref))


def _lead_state(nat_lead, gates_lead, lead_blk, consts):
    _, _, _, gate_bias, shift, conv_w, conv_b, mh_norm = consts
    shapes = _state_shapes()
    return pl.pallas_call(
        _lead_state_kernel,
        grid=(1,),
        in_specs=[
            pl.BlockSpec((BLOCK, NAT_K), lambda i: (lead_blk, 0)),
            pl.BlockSpec((BLOCK, GATE_LANES), lambda i: (lead_blk, 0)),
            _whole((1, GATE_LANES)),
            _whole((CONV_WIDTH * BLOCK, 2 * BLOCK)),
            _whole((CONV_WIDTH, 2 * ML_QK_WIDTH)),
            _whole((1, 2 * ML_QK_WIDTH)),
            _whole((1, ML_WIDTH)),
        ],
        out_specs=[_whole(s.shape) for s in shapes],
        out_shape=shapes,
        scratch_shapes=[pltpu.VMEM((BLOCK, ML_WIDTH), BF16)],
        compiler_params=_params(1),
        name="lead_state",
    )(nat_lead, gates_lead, gate_bias, shift, conv_w, conv_b, mh_norm)


FUSED_TM = 512


def _inproj_mixer_kernel(*refs):
    x_ref, g_ref, wn_ref, wq_ref = refs[:4]
    mix_in = refs[4:4 + N_MIX_IN]
    state_in = refs[4 + N_MIX_IN:4 + N_MIX_IN + N_STATE]
    wu_ref = refs[4 + N_MIX_IN + N_STATE]
    nat_ref, t_ref, gate_ref, att_ref, hm_ref, wu_bf_ref = refs[5 + N_MIX_IN + N_STATE:-(1 + N_STATE)]
    u_ref = refs[-(1 + N_STATE)]
    state = refs[-N_STATE:]
    i = pl.program_id(0)
    j = pl.program_id(1)
    step = i * (IN_NAT_STEPS + 1) + j
    wu_bf_ref[...] = wu_ref[...].astype(BF16)

    @pl.when(j == 0)
    def _():
        u_ref[...] = _rms(x_ref[...], g_ref[...]).astype(BF16)

    @pl.when(step == 0)
    def _():
        _load_state(state, state_in)

    weights = [INPROJ_PHASES_PER_ROUND, MIXER_PHASES_PER_ROUND]

    @pl.when(j < IN_NAT_STEPS)
    def _():
        _drain(_interleave([_inproj_nat_phases(j, FUSED_TM, u_ref, wn_ref, nat_ref, gate_ref),
                            _mixer_phases(step + 1, mix_in, att_ref, hm_ref, state)], weights))

    @pl.when(j == IN_NAT_STEPS)
    def _():
        _drain(_interleave([_inproj_t_phases(FUSED_TM, u_ref, wq_ref, t_ref),
                            _mixer_phases(step + 1, mix_in, att_ref, hm_ref, state)], weights))


def _inproj_mixer(x2, row_tile0, g, w_nat, w_qv, mixer_args, w_up, lead_blk, nblk):
    n_tiles = nblk * BLOCK // FUSED_TM
    assert n_tiles * (IN_NAT_STEPS + 1) == nblk
    step = lambda i, j: i * (IN_NAT_STEPS + 1) + j
    proj_specs, proj_shapes = _inproj_out(FUSED_TM, n_tiles * FUSED_TM)
    mix_specs, mix_shapes = _mixer_out(step, nblk * BLOCK)
    wu_spec = pl.BlockSpec((D_MODEL, D_FF // nblk), lambda i, j: (0, step(i, j)))
    return pl.pallas_call(
        _inproj_mixer_kernel,
        grid=(n_tiles, IN_NAT_STEPS + 1),
        in_specs=[
            pl.BlockSpec((FUSED_TM, D_MODEL), lambda i, j: (row_tile0 + i, 0)),
            pl.BlockSpec((1, D_MODEL), lambda i, j: (0, 0)),
        ] + _inproj_weight_specs() + _mixer_in_specs(step, lead_blk) + [wu_spec],
        out_specs=proj_specs + mix_specs + [wu_spec],
        out_shape=proj_shapes + mix_shapes + [jax.ShapeDtypeStruct(w_up.shape, BF16)],
        scratch_shapes=[pltpu.VMEM((FUSED_TM, D_MODEL), BF16)] + _state_scratch(),
        compiler_params=_params(2),
        name="inproj_mixer",
    )(x2, g, w_nat, w_qv, *mixer_args, w_up)


def _mixer_kernel(*refs):
    mix_in = refs[:N_MIX_IN]
    state_in = refs[N_MIX_IN:N_MIX_IN + N_STATE]
    wo_ref, wd_ref = refs[N_MIX_IN + N_STATE:N_MIX_IN + N_STATE + 2]
    att_ref, hm_ref, wo_bf_ref, wd_bf_ref = refs[N_MIX_IN + N_STATE + 2:-N_STATE]
    state = refs[-N_STATE:]
    step = pl.program_id(0)

    @pl.when(step == 0)
    def _():
        _load_state(state, state_in)

    wo_bf_ref[...] = wo_ref[...].astype(BF16)
    wd_bf_ref[...] = wd_ref[...].astype(BF16)
    _drain(_mixer_phases(step + 1, mix_in, att_ref, hm_ref, state))


def _mixer(mixer_args, w_out, w_down, lead_blk, nblk):
    step = lambda s: s
    mix_specs, mix_shapes = _mixer_out(step, nblk * BLOCK)
    w_specs = [
        pl.BlockSpec((w_out.shape[0] // nblk, D_MODEL), lambda s: (s, 0)),
        pl.BlockSpec((D_FF // nblk, D_MODEL), lambda s: (s, 0)),
    ]
    return pl.pallas_call(
        _mixer_kernel,
        grid=(nblk,),
        in_specs=_mixer_in_specs(step, lead_blk) + w_specs,
        out_specs=mix_specs + w_specs,
        out_shape=mix_shapes + [jax.ShapeDtypeStruct(w.shape, BF16) for w in (w_out, w_down)],
        scratch_shapes=_state_scratch(),
        compiler_params=_params(1),
        name="mixer",
    )(*mixer_args, w_out, w_down)


OUT_TM = 512


def _outproj_kernel(att0_ref, att1_ref, hm0_ref, hm1_ref, x_ref, wa_ref, wm_ref, g_ref,
                    h_ref, u_ref):
    i = pl.program_id(0)
    half = pl.num_programs(0) // 2

    def body(att_ref, hm_ref):
        h = (x_ref[...]
             + lax.dot_general(att_ref[...], wa_ref[...], CONTRACT_FIRST, preferred_element_type=F32)
             + jnp.dot(hm_ref[...], wm_ref[...], preferred_element_type=F32))
        h_ref[...] = h
        u_ref[...] = _rms(h, g_ref[...]).astype(BF16)

    @pl.when(i < half)
    def _():
        body(att0_ref, hm0_ref)

    @pl.when(i >= half)
    def _():
        body(att1_ref, hm1_ref)


def _outproj(att0, att1, hm0, hm1, x2, w_out, g):
    rows = x2.shape[0]
    n = rows // OUT_TM
    half = n // 2
    first = lambda i: jnp.minimum(i, half - 1)
    second = lambda i: jnp.maximum(i - half, 0)
    return pl.pallas_call(
        _outproj_kernel,
        grid=(n,),
        in_specs=[
            pl.BlockSpec((ATT_WIDTH, OUT_TM), lambda i: (0, first(i))),
            pl.BlockSpec((ATT_WIDTH, OUT_TM), lambda i: (0, second(i))),
            pl.BlockSpec((OUT_TM, ML_WIDTH), lambda i: (first(i), 0)),
            pl.BlockSpec((OUT_TM, ML_WIDTH), lambda i: (second(i), 0)),
            pl.BlockSpec((OUT_TM, D_MODEL), lambda i: (i, 0)),
            pl.BlockSpec((ATT_WIDTH, D_MODEL), lambda i: (0, 0)),
            pl.BlockSpec((ML_WIDTH, D_MODEL), lambda i: (ATT_WIDTH // ML_WIDTH, 0)),
            pl.BlockSpec((1, D_MODEL), lambda i: (0, 0)),
        ],
        out_specs=[
            pl.BlockSpec((OUT_TM, D_MODEL), lambda i: (i, 0)),
            pl.BlockSpec((OUT_TM, D_MODEL), lambda i: (i, 0)),
        ],
        out_shape=[
            jax.ShapeDtypeStruct((rows, D_MODEL), F32),
            jax.ShapeDtypeStruct((rows, D_MODEL), BF16),
        ],
        compiler_params=_params(1),
        name="outproj",
    )(att0, att1, hm0, hm1, x2, w_out, w_out, g)


MLP_TM = 1024
MLP_TF = 1024


def _mlp_step(first, u_ref, wu_ref, wd_ref, o_ref):
    acts = []
    for c in range(MLP_TF // MXU_COLS):
        cs = slice(c * MXU_COLS, (c + 1) * MXU_COLS)
        a = jnp.dot(u_ref[...], wu_ref[:, cs], preferred_element_type=F32)
        acts.append(jnp.square(jnp.maximum(a, 0.0)).astype(BF16))
        yield
    act = jnp.concatenate(acts, axis=1)
    for c in range(D_MODEL // MXU_COLS):
        cs = slice(c * MXU_COLS, (c + 1) * MXU_COLS)
        d = jnp.dot(act, wd_ref[:, cs], preferred_element_type=F32)
        if first:
            o_ref[:, cs] = d
        else:
            o_ref[:, cs] += d
        yield


def _mlp_kernel(u_ref, h_hbm, wu_ref, wd_ref, g_ref, o_ref, hbuf_ref, hsem):
    i = pl.program_id(0)
    j = pl.program_id(1)
    r0 = pl.multiple_of(i * MLP_TM, MLP_TM)
    h_copy = pltpu.make_async_copy(h_hbm.at[pl.ds(r0, MLP_TM), :], hbuf_ref, hsem)

    @pl.when(j == 0)
    def _():
        h_copy.start()
        for _ in _mlp_step(True, u_ref, wu_ref, wd_ref, o_ref):
            pass

    @pl.when(j > 0)
    def _():
        for _ in _mlp_step(False, u_ref, wu_ref, wd_ref, o_ref):
            pass

    @pl.when(j == pl.num_programs(1) - 1)
    def _():
        h_copy.wait()
        o_ref[...] = _rms(o_ref[...] + hbuf_ref[...], g_ref[...])


def _mlp(u, h, w_up, w_down, g):
    rows = u.shape[0]
    return pl.pallas_call(
        _mlp_kernel,
        grid=(rows // MLP_TM, D_FF // MLP_TF),
        in_specs=[
            pl.BlockSpec((MLP_TM, D_MODEL), lambda i, j: (i, 0)),
            pl.BlockSpec(memory_space=pl.ANY),
            pl.BlockSpec((D_MODEL, MLP_TF), lambda i, j: (0, j)),
            pl.BlockSpec((MLP_TF, D_MODEL), lambda i, j: (j, 0)),
            pl.BlockSpec((1, D_MODEL), lambda i, j: (0, 0)),
        ],
        out_specs=pl.BlockSpec((MLP_TM, D_MODEL), lambda i, j: (i, 0)),
        out_shape=jax.ShapeDtypeStruct((rows, D_MODEL), F32),
        scratch_shapes=[pltpu.VMEM((MLP_TM, D_MODEL), F32), pltpu.SemaphoreType.DMA(())],
        compiler_params=_params(2),
        name="mlp",
    )(u, h, w_up, w_down, g)


def _t5_bucket(dist):
    max_exact = N_BUCKETS // 2
    d = jnp.maximum(dist, 0)
    ratio = jnp.maximum(d, max_exact).astype(F32) / max_exact
    large = max_exact + (jnp.log(ratio) / math.log(MAX_DISTANCE / max_exact)
                         * (N_BUCKETS - max_exact)).astype(jnp.int32)
    large = jnp.minimum(large, N_BUCKETS - 1)
    return jnp.where(d < max_exact, d, large)


BIAS_ROWS = 16


BIAS_BUCKETS_PER_PHASE = 8


def _bias_fill_splats(rb_ref, splat_ref):
    def fill(b, carry):
        for hd in range(ATT_HEADS):
            splat_ref[b * ATT_HEADS + hd] = jnp.full((SUBLANES, BLOCK), rb_ref[b, hd], F32) * LOG2E
        return carry

    lax.fori_loop(0, N_BUCKETS, fill, 0)


def _bias_lookup_phases(bkt, splat_ref, write):
    accs = [jnp.zeros(bkt.shape, F32)] * ATT_HEADS
    for b in range(N_BUCKETS):
        hit = bkt == b
        accs = [jnp.where(hit, jnp.concatenate([splat_ref[b * ATT_HEADS + hd]]
                                               * (BIAS_ROWS // SUBLANES), axis=0), a)
                for hd, a in enumerate(accs)]
        if b % BIAS_BUCKETS_PER_PHASE == BIAS_BUCKETS_PER_PHASE - 1:
            yield
    write(accs)


def _bias_step_phases(bkt_band_ref, bkt_meta_ref, band_ref, meta_ref, splat_ref):
    def write_band(tables):
        for hd, t in enumerate(tables):
            h, g = divmod(hd, ATT_GROUP)
            band_ref[h, :, g * BLOCK:(g + 1) * BLOCK] = t

    yield from _bias_lookup_phases(bkt_band_ref[...], splat_ref, write_band)
    for k in range(bkt_meta_ref.shape[0]):
        def write_meta(tables, k=k):
            for hd, t in enumerate(tables):
                h, g = divmod(hd, ATT_GROUP)
                meta_ref[k, h, :, g * BLOCK:(g + 1) * BLOCK] = t

        yield from _bias_lookup_phases(bkt_meta_ref[k], splat_ref, write_meta)


def _bias_buckets(nblk):
    c = jnp.arange(2 * BLOCK)[:, None]
    r = jnp.arange(BLOCK)[None, :]
    bkt_band = _t5_bucket(r + BLOCK - c).astype(jnp.int32)
    q_pos = BLOCK + jnp.arange(nblk * BLOCK)[None, :]
    m_pos = N_PAD + jnp.arange(N_META)[:, None]
    bkt_meta = _t5_bucket(q_pos - m_pos).astype(jnp.int32)
    return bkt_band, bkt_meta.reshape(N_META, nblk, BLOCK).transpose(1, 0, 2)


W_IN_OFFS = dict(q=0, k=1024, v=1280, mq=1536, mk=2048, mv=2560, mo=3584, gates=4608)
NAT_SRC = ([W_IN_OFFS["mq"] // LANE + t for t in range((W_IN_OFFS["gates"] - W_IN_OFFS["mq"]) // LANE)]
           + [W_IN_OFFS["k"] // LANE + t for t in range(ATT_KV_WIDTH // LANE)]
           + [W_IN_OFFS["gates"] // LANE])
QV_SRC = ([W_IN_OFFS["q"] // LANE + t for t in range(ATT_WIDTH // LANE)]
          + [W_IN_OFFS["v"] // LANE + t for t in range(ATT_KV_WIDTH // LANE)])
N_NAT_BLK = len(NAT_SRC)
N_Q_BLK = ATT_WIDTH // LANE


def _win_prep_kernel(src_ref, w_ref, nat_ref, qv_ref):
    j = pl.program_id(0)
    wv = w_ref[0]

    @pl.when(j < N_NAT_BLK - 1)
    def _():
        nat_ref[...] = wv.astype(BF16)

    @pl.when(j == N_NAT_BLK - 1)
    def _():
        row = lax.broadcasted_iota(jnp.int32, wv.shape, 0)
        nat_ref[...] = jnp.where(row < 2 * ML_HEADS, wv, 0.0).astype(BF16)

    @pl.when((j >= N_NAT_BLK) & (j < N_NAT_BLK + N_Q_BLK))
    def _():
        qv_ref[...] = (wv * (ATT_HEAD_DIM ** -0.5 * LOG2E)).astype(BF16)

    @pl.when(j >= N_NAT_BLK + N_Q_BLK)
    def _():
        qv_ref[...] = wv.astype(BF16)


def _win_prep(w_in_t):
    src = jnp.asarray(NAT_SRC + QV_SRC, jnp.int32)
    d = w_in_t.shape[2]
    return pl.pallas_call(
        _win_prep_kernel,
        grid_spec=pltpu.PrefetchScalarGridSpec(
            num_scalar_prefetch=1,
            grid=(len(NAT_SRC) + len(QV_SRC),),
            in_specs=[pl.BlockSpec((1, LANE, d), lambda j, src: (0, src[j], 0))],
            out_specs=[
                pl.BlockSpec((LANE, d), lambda j, src: (jnp.minimum(j, N_NAT_BLK - 1), 0)),
                pl.BlockSpec((LANE, d), lambda j, src: (jnp.maximum(j - N_NAT_BLK, 0), 0)),
            ],
        ),
        out_shape=[
            jax.ShapeDtypeStruct((NAT_COLS, d), BF16),
            jax.ShapeDtypeStruct((T_ROWS, d), BF16),
        ],
        compiler_params=_params(1),
        name="w_in_prep",
    )(src, w_in_t)


def kernel(x, meta_tokens, w_in, conv_w, conv_b, b_igate, b_fgate, attn_sinks, rel_bias,
           mh_norm, w_out, norm_mix, norm_mlp, w_up, w_down, norm_final):
    batch, seq, d = x.shape
    assert batch == 2, "the two batches are processed one projection / mixer stage apart"
    nblk = seq // BLOCK
    x2 = x.reshape(batch * seq, d)
    lead = jnp.concatenate([jnp.zeros((N_PAD, d), x.dtype), meta_tokens.astype(x.dtype)], axis=0)
    g_mix = norm_mix[0][None].astype(F32)

    w_nat, w_qv = _win_prep(jnp.swapaxes(w_in, 1, 2))
    gate_bias = jnp.pad(jnp.concatenate([b_igate[0], b_fgate[0]]).astype(F32),
                        (0, GATE_LANES - 2 * ML_HEADS))[None]

    (nat0, projt0, gates0), (nat_l, projt_l, gates_l), (bias_band_t, bias_meta_t) = _inproj(
        x2, lead, g_mix, w_nat, w_qv, rel_bias, seq // IN_TM, nblk)
    consts = (attn_sinks[0].astype(F32), bias_band_t, bias_meta_t, gate_bias, _conv_shift_matrix(),
              conv_w[0].astype(F32), conv_b[0][None].astype(F32), mh_norm[0][None].astype(F32))
    lead_blk = 0
    state = _lead_state(nat_l, gates_l, lead_blk, consts)
    args0 = _mixer_args(consts, nat0, projt0, gates0, nat_l, projt_l, state)
    nat1, projt1, gates1, att0, hm0, w_up_bf = _inproj_mixer(
        x2, seq // FUSED_TM, g_mix, w_nat, w_qv, args0, w_up[0], lead_blk, nblk)
    args1 = _mixer_args(consts, nat1, projt1, gates1, nat_l, projt_l, state)
    att1, hm1, w_out_bf, w_down_bf = _mixer(args1, w_out[0], w_down[0], lead_blk, nblk)

    h2, u2 = _outproj(att0, att1, hm0, hm1, x2, w_out_bf, norm_mlp[0][None].astype(F32))
    out = _mlp(u2, h2, w_up_bf, w_down_bf, norm_final[None].astype(F32))
    return out.reshape(batch, seq, d)
```

```python
import math

import numpy as np
import jax
import jax.numpy as jnp
from jax import lax
from jax.experimental import pallas as pl
from jax.experimental.pallas import tpu as pltpu

F32 = jnp.float32
BF16 = jnp.bfloat16

D_MODEL = 2048
N_META = 16
BLOCK = 128
N_PAD = BLOCK - N_META
WINDOW = 128
ATT_HEADS = 16
ATT_KV_HEADS = 4
ATT_GROUP = ATT_HEADS // ATT_KV_HEADS
ATT_HEAD_DIM = 64
ATT_WIDTH = ATT_HEADS * ATT_HEAD_DIM
ATT_KV_WIDTH = ATT_KV_HEADS * ATT_HEAD_DIM
ML_HEADS = 4
ML_V_DIM = 256
ML_QK_DIM = 128
ML_WIDTH = ML_HEADS * ML_V_DIM
ML_QK_WIDTH = ML_HEADS * ML_QK_DIM
CONV_WIDTH = 4
GATE_SOFTCAP = 15.0
D_FF = 4 * D_MODEL
N_BUCKETS = 32
MAX_DISTANCE = 128
EPS = 1e-6
NEG = -1e30
LOG2E = math.log2(math.e)

GATE_LANES = 128
NAT_MQK = 0
NAT_MV = 2 * ML_QK_WIDTH
NAT_MO = NAT_MV + ML_WIDTH
NAT_K = NAT_MO + ML_WIDTH
NAT_COLS = NAT_K + ATT_KV_WIDTH + GATE_LANES
T_ROWS = ATT_WIDTH + ATT_KV_WIDTH

V7X_VMEM_LIMIT = 60 * 1024 * 1024
LANE = 128
SUBLANES = 8
CONTRACT_LAST = (((1,), (1,)), ((), ()))
CONTRACT_FIRST = (((0,), (0,)), ((), ()))


def _params(n_axes, vmem=V7X_VMEM_LIMIT):
    return pltpu.CompilerParams(dimension_semantics=("arbitrary",) * n_axes,
                                vmem_limit_bytes=vmem)


def _rms(x, g):
    return x * lax.rsqrt(jnp.mean(x * x, axis=-1, keepdims=True) + EPS) * g


_DONE = object()


def _drain(phases):
    for _ in phases:
        pass


def _interleave(streams, weights):
    live = list(zip(streams, weights))
    while live:
        for item in list(live):
            stream, weight = item
            for _ in range(weight):
                if next(stream, _DONE) is _DONE:
                    live.remove(item)
                    break
        yield


IN_TM = 1024
IN_NAT_STEPS = 3
IN_TN = NAT_COLS // IN_NAT_STEPS
MXU_COLS = 256


def _inproj_nat_phases(j, rows, u_ref, wn_ref, nat_ref, gate_ref):
    u = u_ref[0:rows, :]
    for c0 in range(0, IN_TN, MXU_COLS):
        cs = slice(c0, min(c0 + MXU_COLS, IN_TN))
        r = lax.dot_general(u, wn_ref[cs, :], CONTRACT_LAST, preferred_element_type=F32)
        nat_ref[0:rows, cs] = r.astype(BF16)
        if cs.stop == IN_TN:
            assert cs.stop - cs.start == GATE_LANES

            @pl.when(j == IN_NAT_STEPS - 1)
            def _():
                gate_ref[0:rows, :] = r
        yield


def _inproj_t_phases(rows, u_ref, wq_ref, t_ref):
    u = u_ref[0:rows, :]
    for c0 in range(0, T_ROWS, MXU_COLS):
        cs = slice(c0, c0 + MXU_COLS)
        t_ref[cs, 0:rows] = lax.dot_general(wq_ref[cs, :], u, CONTRACT_LAST,
                                            preferred_element_type=F32).astype(BF16)
        yield


BIAS_PHASES_PER_ROUND = 3


def _inproj_kernel(x_ref, lead_ref, g_ref, wn_ref, wq_ref, rb_ref, bkt_band_ref, bkt_meta_ref,
                   nat_ref, t_ref, gate_ref, nat_lead_ref, t_lead_ref, gate_lead_ref,
                   band_ref, meta_ref, u_ref, u_lead_ref, splat_ref):
    i = pl.program_id(0)
    j = pl.program_id(1)

    @pl.when(j == 0)
    def _():
        u_ref[...] = _rms(x_ref[...], g_ref[...]).astype(BF16)

    @pl.when((j == 0) & (i == 0))
    def _():
        u_lead_ref[...] = _rms(lead_ref[...], g_ref[...]).astype(BF16)
        _bias_fill_splats(rb_ref, splat_ref)

    def with_bias(phases):
        bias = _bias_step_phases(bkt_band_ref, bkt_meta_ref, band_ref, meta_ref, splat_ref)
        _drain(_interleave([phases, bias], [1, BIAS_PHASES_PER_ROUND]))

    @pl.when(j < IN_NAT_STEPS)
    def _():
        with_bias(_inproj_nat_phases(j, IN_TM, u_ref, wn_ref, nat_ref, gate_ref))

    @pl.when(j == IN_NAT_STEPS)
    def _():
        with_bias(_inproj_t_phases(IN_TM, u_ref, wq_ref, t_ref))

    @pl.when((i == 0) & (j < IN_NAT_STEPS))
    def _():
        _drain(_inproj_nat_phases(j, BLOCK, u_lead_ref, wn_ref, nat_lead_ref, gate_lead_ref))

    @pl.when((i == 0) & (j == IN_NAT_STEPS))
    def _():
        _drain(_inproj_t_phases(BLOCK, u_lead_ref, wq_ref, t_lead_ref))


def _inproj_weight_specs():
    last_nat = IN_NAT_STEPS - 1
    return [
        pl.BlockSpec((IN_TN, D_MODEL), lambda i, j: (jnp.minimum(j, last_nat), 0)),
        pl.BlockSpec((T_ROWS, D_MODEL), lambda i, j: (0, 0)),
    ]


def _inproj_out(tm, rows):
    last_nat = IN_NAT_STEPS - 1
    specs = [
        pl.BlockSpec((tm, IN_TN), lambda i, j: (i, jnp.minimum(j, last_nat))),
        pl.BlockSpec((T_ROWS, tm), lambda i, j: (0, i)),
        pl.BlockSpec((tm, GATE_LANES), lambda i, j: (i, 0)),
    ]
    shapes = [
        jax.ShapeDtypeStruct((rows, NAT_COLS), BF16),
        jax.ShapeDtypeStruct((T_ROWS, rows), BF16),
        jax.ShapeDtypeStruct((rows, GATE_LANES), F32),
    ]
    return specs, shapes


def _inproj(x2, lead, g, w_nat, w_qv, rel_bias, n_tiles, nblk):
    out_specs, out_shape = _inproj_out(IN_TM, n_tiles * IN_TM)
    last_nat = IN_NAT_STEPS - 1
    n_steps = n_tiles * (IN_NAT_STEPS + 1)
    assert n_steps * BIAS_ROWS == 2 * BLOCK and nblk % n_steps == 0
    meta_per_step = nblk // n_steps
    bkt_band, bkt_meta = _bias_buckets(nblk)
    step = lambda i, j: i * (IN_NAT_STEPS + 1) + j
    bias_in_specs = [
        pl.BlockSpec(memory_space=pltpu.SMEM),
        pl.BlockSpec((BIAS_ROWS, BLOCK), lambda i, j: (step(i, j), 0)),
        pl.BlockSpec((meta_per_step, N_META, BLOCK), lambda i, j: (step(i, j), 0, 0)),
    ]
    bias_out_specs = [
        pl.BlockSpec((ATT_KV_HEADS, BIAS_ROWS, GQ), lambda i, j: (0, step(i, j), 0)),
        pl.BlockSpec((meta_per_step, ATT_KV_HEADS, N_META, GQ), lambda i, j: (step(i, j), 0, 0, 0)),
    ]
    bias_shapes = [
        jax.ShapeDtypeStruct((ATT_KV_HEADS, 2 * BLOCK, GQ), F32),
        jax.ShapeDtypeStruct((nblk, ATT_KV_HEADS, N_META, GQ), F32),
    ]
    lead_col = lambda i, j: jnp.where(i == 0, jnp.minimum(j, last_nat), last_nat)
    lead_specs = [
        pl.BlockSpec((BLOCK, IN_TN), lambda i, j: (0, lead_col(i, j))),
        pl.BlockSpec((T_ROWS, BLOCK), lambda i, j: (0, 0)),
        pl.BlockSpec((BLOCK, GATE_LANES), lambda i, j: (0, 0)),
    ]
    lead_shapes = [
        jax.ShapeDtypeStruct((BLOCK, NAT_COLS), BF16),
        jax.ShapeDtypeStruct((T_ROWS, BLOCK), BF16),
        jax.ShapeDtypeStruct((BLOCK, GATE_LANES), F32),
    ]
    outs = pl.pallas_call(
        _inproj_kernel,
        grid=(n_tiles, IN_NAT_STEPS + 1),
        in_specs=[
            pl.BlockSpec((IN_TM, D_MODEL), lambda i, j: (i, 0)),
            pl.BlockSpec((BLOCK, D_MODEL), lambda i, j: (0, 0)),
            pl.BlockSpec((1, D_MODEL), lambda i, j: (0, 0)),
        ] + _inproj_weight_specs() + bias_in_specs,
        out_specs=out_specs + lead_specs + bias_out_specs,
        out_shape=out_shape + lead_shapes + bias_shapes,
        scratch_shapes=[pltpu.VMEM((IN_TM, D_MODEL), BF16), pltpu.VMEM((BLOCK, D_MODEL), BF16),
                        pltpu.VMEM((N_BUCKETS * ATT_HEADS, SUBLANES, BLOCK), F32)],
        compiler_params=_params(2),
        name="inproj",
    )(x2, lead, g, w_nat, w_qv, rel_bias.astype(F32), bkt_band, bkt_meta)
    return outs[:3], outs[3:6], outs[6:]


GQ = ATT_GROUP * BLOCK
ATT_SLABS_PER_PHASE = 2


def _attn_prologue(n, kc_ref, kp_ref, km_ref):
    c = lax.broadcasted_iota(jnp.int32, (2 * BLOCK, BLOCK), 0)
    r = lax.broadcasted_iota(jnp.int32, (2 * BLOCK, BLOCK), 1)
    dist = r + BLOCK - c
    band_ok = (dist >= 0) & (dist < WINDOW) & ((c >= BLOCK) | (n >= 1))
    mm = lax.broadcasted_iota(jnp.int32, (N_META, BLOCK), 0)
    rr = lax.broadcasted_iota(jnp.int32, (N_META, BLOCK), 1)
    meta_ok = (n + 1) * BLOCK + rr >= N_PAD + mm
    kall = jnp.concatenate([kp_ref[...], kc_ref[...], km_ref[...]], axis=0)
    return band_ok, meta_ok, kall


BF16_ROWS = 16


def _ones_row(width):
    row = lax.broadcasted_iota(jnp.int32, (BF16_ROWS, width), 0)
    return jnp.where(row == 0, 1.0, 0.0).astype(BF16)


def _attn_head(h, pro, sink_ref, qt_ref, vc_ref, vp_ref, vl_ref, bb_ref, bm_ref, o_ref):
    band_ok, meta_ok, kall = pro
    lo = (h // 2) * LANE
    kpair = kall[:, lo:lo + LANE]
    qh = jnp.concatenate(
        [qt_ref[(ATT_GROUP * h + g) * ATT_HEAD_DIM:(ATT_GROUP * h + g + 1) * ATT_HEAD_DIM, :]
         for g in range(ATT_GROUP)], axis=1)
    zq = jnp.zeros_like(qh)
    qz = jnp.concatenate([qh, zq] if h % 2 == 0 else [zq, qh], axis=0)
    st = jnp.dot(kpair, qz, preferred_element_type=F32)
    yield
    zpad = jnp.zeros((N_PAD, BLOCK), BF16)
    p_band, p_lead, sink_p = [], [], []
    for g in range(ATT_GROUP):
        gs = slice(g * BLOCK, (g + 1) * BLOCK)
        sl = st[:, gs]
        sink = jnp.full((1, BLOCK), sink_ref[ATT_GROUP * h + g], F32) * LOG2E
        lb = jnp.where(band_ok, sl[:2 * BLOCK] + bb_ref[h, :, gs], NEG * LOG2E)
        lm = jnp.where(meta_ok, sl[2 * BLOCK:] + bm_ref[0, h, :, gs], NEG * LOG2E)
        m = jnp.maximum(jnp.max(lb, axis=0, keepdims=True), jnp.max(lm, axis=0, keepdims=True))
        m = jnp.maximum(m, sink)
        pb = jnp.exp2(lb - m)
        pm = jnp.exp2(lm - m)
        sink_p.append(jnp.exp2(sink - m))
        p_band.append(pb.astype(BF16))
        p_lead.append(jnp.concatenate([zpad, pm.astype(BF16)], axis=0))
        if g % ATT_SLABS_PER_PHASE == ATT_SLABS_PER_PHASE - 1:
            yield
    pbt = jnp.concatenate(p_band, axis=1)
    plt = jnp.concatenate(p_lead, axis=1)
    hs = slice(h * ATT_HEAD_DIM, (h + 1) * ATT_HEAD_DIM)
    vband = jnp.concatenate([vp_ref[hs, :], vc_ref[hs, :]], axis=1)
    ones_b = _ones_row(2 * BLOCK)
    ones_l = _ones_row(BLOCK)
    ot = (jnp.dot(jnp.concatenate([vband, ones_b], axis=0), pbt, preferred_element_type=F32)
          + jnp.dot(jnp.concatenate([vl_ref[hs, :], ones_l], axis=0), plt,
                    preferred_element_type=F32))
    for g in range(ATT_GROUP):
        gs = slice(g * BLOCK, (g + 1) * BLOCK)
        den = ot[ATT_HEAD_DIM:ATT_HEAD_DIM + 1, gs] + sink_p[g]
        row = (ATT_GROUP * h + g) * ATT_HEAD_DIM
        o_ref[row:row + ATT_HEAD_DIM, :] = (ot[:ATT_HEAD_DIM, gs] * (1.0 / den)).astype(BF16)


def _conv_shift_matrix():
    s = np.zeros((CONV_WIDTH * BLOCK, 2 * BLOCK), np.float32)
    for k in range(CONV_WIDTH):
        t = np.arange(BLOCK)
        s[k * BLOCK + t, BLOCK + t - k] = 1.0
    return jnp.asarray(s, BF16)


def _mlstm_gates(n, gate_ref, gb_ref):
    t_idx = lax.broadcasted_iota(jnp.int32, (BLOCK, GATE_LANES), 0)
    valid = (n > 0) | (t_idx >= N_PAD)
    pre = GATE_SOFTCAP * jnp.tanh((gate_ref[...] + gb_ref[...]) / GATE_SOFTCAP)
    log_i = jnp.where(valid, pre, NEG)
    log_sig = jnp.minimum(pre, 0.0) - jnp.log1p(jnp.exp(-jnp.abs(pre)))
    log_f = jnp.where(valid, log_sig, 0.0)
    row = lax.broadcasted_iota(jnp.int32, (BLOCK, BLOCK), 0)
    col = lax.broadcasted_iota(jnp.int32, (BLOCK, BLOCK), 1)
    causal = col <= row
    tril = jnp.where(causal, 1.0, 0.0).astype(F32)
    b_all = jnp.dot(tril, log_f, preferred_element_type=F32,
                    precision=lax.Precision.HIGHEST)
    return log_i, b_all, log_i.T, b_all.T, causal


def _mlstm_conv(p_ref, shift_ref, cw_ref, cb_ref, xprev_ref):
    xcur = p_ref[:, NAT_MQK:NAT_MV]
    xcat = jnp.concatenate([xprev_ref[...], xcur], axis=0)
    sh = jnp.dot(shift_ref[...], xcat, preferred_element_type=F32)
    xprev_ref[...] = xcur
    y = cb_ref[...]
    for k in range(CONV_WIDTH):
        y = y + cw_ref[CONV_WIDTH - 1 - k:CONV_WIDTH - k, :] * sh[k * BLOCK:(k + 1) * BLOCK]
    act = y * jax.nn.sigmoid(y)
    return act[:, :ML_QK_WIDTH] * (ML_QK_DIM ** -0.5), act[:, ML_QK_WIDTH:]


def _mlstm_head(h, gates, q_all, k_all, p_ref, mhn_ref, o_ref, c_ref, n_ref, m_ref):
    log_i, b_all, log_i_t, b_all_t, causal = gates
    q_f = q_all[:, h * ML_QK_DIM:(h + 1) * ML_QK_DIM]
    k_f = k_all[:, h * ML_QK_DIM:(h + 1) * ML_QK_DIM]
    q_h = q_f.astype(BF16)
    k_h = k_f.astype(BF16)
    v_h = p_ref[:, NAT_MV + h * ML_V_DIM:NAT_MV + (h + 1) * ML_V_DIM]
    b_col = b_all[:, ML_HEADS + h:ML_HEADS + h + 1]
    li_col = log_i[:, h:h + 1]
    b_row = b_all_t[ML_HEADS + h:ML_HEADS + h + 1, :]
    li_row = log_i_t[h:h + 1, :]
    b_last = b_all[BLOCK - 1:BLOCK, ML_HEADS + h:ML_HEADS + h + 1]
    c_prev = c_ref[h]
    n_prev = n_ref[h][0:1, :]
    m_prev = m_ref[h][0:1, 0:1]

    dmat = jnp.where(causal, b_col - b_row + li_row, -jnp.inf)
    inter = b_col + m_prev
    m_t = jnp.maximum(inter, jnp.max(dmat, axis=1, keepdims=True))
    s = lax.dot_general(q_h, k_h, CONTRACT_LAST, preferred_element_type=F32) * jnp.exp(dmat - m_t)
    yield
    a_t = jnp.exp(inter - m_t)
    num = (jnp.dot(s.astype(BF16), v_h, preferred_element_type=F32)
           + a_t * jnp.dot(q_h, c_prev.astype(BF16), preferred_element_type=F32))
    den = (jnp.sum(s, axis=1, keepdims=True)
           + a_t * jnp.sum(q_f * n_prev, axis=1, keepdims=True))
    den = jnp.maximum(jnp.abs(den), jnp.exp(-m_t))
    hh = num / den
    hh = hh * lax.rsqrt(jnp.mean(hh * hh, axis=-1, keepdims=True) + EPS)
    vs = slice(h * ML_V_DIM, (h + 1) * ML_V_DIM)
    hh = hh * mhn_ref[:, vs]
    m_o = p_ref[:, NAT_MO + h * ML_V_DIM:NAT_MO + (h + 1) * ML_V_DIM].astype(F32)
    o_ref[:, vs] = (hh * jax.nn.sigmoid(m_o)).astype(BF16)
    yield

    g_col = b_last - b_col + li_col
    m_loc = jnp.max(g_col, axis=0, keepdims=True)
    wk = jnp.exp(g_col - m_loc) * k_f
    c_loc = lax.dot_general(wk.astype(BF16), v_h, CONTRACT_FIRST,
                            preferred_element_type=F32)
    n_loc = jnp.sum(wk, axis=0, keepdims=True)
    m_new = jnp.maximum(b_last + m_prev, m_loc)
    a = jnp.exp(b_last + m_prev - m_new)
    cc = jnp.exp(m_loc - m_new)
    c_ref[h] = a * c_prev + cc * c_loc
    n_ref[h] = jnp.broadcast_to(a * n_prev + cc * n_loc, (8, ML_QK_DIM))
    m_ref[h] = jnp.broadcast_to(m_new, (8, LANE))


N_MIX_IN = 17
N_STATE = 4
INPROJ_PHASES_PER_ROUND = 1
MIXER_PHASES_PER_ROUND = 4


def _mixer_phases(n, mix_in, att_ref, hm_ref, state):
    (sink_ref, qt_ref, kc_ref, kp_ref, km_ref, vc_ref, vp_ref, vl_ref, bb_ref, bm_ref,
     p_ref, gate_ref, gb_ref, shift_ref, cw_ref, cb_ref, mhn_ref) = mix_in
    c_ref, n_ref, m_ref, xprev_ref = state

    def attention_phases():
        pro = _attn_prologue(n - 1, kc_ref, kp_ref, km_ref)
        for h in range(ATT_KV_HEADS):
            yield from _attn_head(h, pro, sink_ref, qt_ref, vc_ref, vp_ref, vl_ref, bb_ref, bm_ref,
                                  att_ref)

    def mlstm_phases():
        gates = _mlstm_gates(n, gate_ref, gb_ref)
        yield
        q_all, k_all = _mlstm_conv(p_ref, shift_ref, cw_ref, cb_ref, xprev_ref)
        yield
        for h in range(ML_HEADS):
            yield from _mlstm_head(h, gates, q_all, k_all, p_ref, mhn_ref, hm_ref, c_ref, n_ref, m_ref)

    return _interleave([mlstm_phases(), attention_phases()], [1, 1])


def _load_state(state, state_in):
    for dst, src in zip(state, state_in):
        dst[...] = src[...]


def _state_shapes():
    return [
        jax.ShapeDtypeStruct((ML_HEADS, ML_QK_DIM, ML_V_DIM), F32),
        jax.ShapeDtypeStruct((ML_HEADS, SUBLANES, ML_QK_DIM), F32),
        jax.ShapeDtypeStruct((ML_HEADS, SUBLANES, LANE), F32),
        jax.ShapeDtypeStruct((BLOCK, 2 * ML_QK_WIDTH), BF16),
    ]


def _state_scratch():
    return [pltpu.VMEM(s.shape, s.dtype) for s in _state_shapes()]


def _whole(shape):
    return pl.BlockSpec(shape, lambda *ids: (0,) * len(shape))


def _mixer_in_specs(step, lead_blk):
    cur = lambda *ids: step(*ids)
    prev = lambda *ids: jnp.maximum(step(*ids) - 1, 0)
    k_col = NAT_K // ATT_KV_WIDTH
    v_row = ATT_WIDTH // ATT_KV_WIDTH
    meta_rows = lead_blk * (BLOCK // N_META) + N_PAD // N_META
    return [
        pl.BlockSpec(memory_space=pltpu.SMEM),
        pl.BlockSpec((ATT_WIDTH, BLOCK), lambda *ids: (0, cur(*ids))),
        pl.BlockSpec((BLOCK, ATT_KV_WIDTH), lambda *ids: (cur(*ids), k_col)),
        pl.BlockSpec((BLOCK, ATT_KV_WIDTH), lambda *ids: (prev(*ids), k_col)),
        pl.BlockSpec((N_META, ATT_KV_WIDTH), lambda *ids: (meta_rows, k_col)),
        pl.BlockSpec((ATT_KV_WIDTH, BLOCK), lambda *ids: (v_row, cur(*ids))),
        pl.BlockSpec((ATT_KV_WIDTH, BLOCK), lambda *ids: (v_row, prev(*ids))),
        pl.BlockSpec((ATT_KV_WIDTH, BLOCK), lambda *ids: (v_row, lead_blk)),
        _whole((ATT_KV_HEADS, 2 * BLOCK, GQ)),
        pl.BlockSpec((1, ATT_KV_HEADS, N_META, GQ), lambda *ids: (cur(*ids), 0, 0, 0)),
        pl.BlockSpec((BLOCK, NAT_K), lambda *ids: (cur(*ids), 0)),
        pl.BlockSpec((BLOCK, GATE_LANES), lambda *ids: (cur(*ids), 0)),
        _whole((1, GATE_LANES)),
        _whole((CONV_WIDTH * BLOCK, 2 * BLOCK)),
        _whole((CONV_WIDTH, 2 * ML_QK_WIDTH)),
        _whole((1, 2 * ML_QK_WIDTH)),
        _whole((1, ML_WIDTH)),
    ] + [_whole(s.shape) for s in _state_shapes()]


def _mixer_args(consts, nat_b, projt_b, gates_b, nat_lead, projt_lead, state):
    sinks, bias_band_t, bias_meta_t, gate_bias, shift, conv_w, conv_b, mh_norm = consts
    return [sinks, projt_b, nat_b, nat_b, nat_lead, projt_b, projt_b, projt_lead, bias_band_t,
            bias_meta_t, nat_b, gates_b, gate_bias, shift, conv_w, conv_b, mh_norm] + list(state)


def _mixer_out(step, n_rows):
    specs = [
        pl.BlockSpec((ATT_WIDTH, BLOCK), lambda *ids: (0, step(*ids))),
        pl.BlockSpec((BLOCK, ML_WIDTH), lambda *ids: (step(*ids), 0)),
    ]
    shapes = [
        jax.ShapeDtypeStruct((ATT_WIDTH, n_rows), BF16),
        jax.ShapeDtypeStruct((n_rows, ML_WIDTH), BF16),
    ]
    return specs, shapes


def _lead_state_kernel(p_ref, gate_ref, gb_ref, shift_ref, cw_ref, cb_ref, mhn_ref,
                       c_ref, n_ref, m_ref, xprev_ref, hm_ref):
    for ref in (c_ref, n_ref, m_ref, xprev_ref):
        ref[...] = jnp.zeros_like(ref)
    gates = _mlstm_gates(0, gate_ref, gb_ref)
    q_all, k_all = _mlstm_conv(p_ref, shift_ref, cw_ref, cb_ref, xprev_ref)
    for h in range(ML_HEADS):
        _drain(_mlstm_head(h, gates, q_all, k_all, p_ref, mhn_ref, hm_ref, c_ref, n_ref, m_ref))


def _lead_state(nat_lead, gates_lead, lead_blk, consts):
    _, _, _, gate_bias, shift, conv_w, conv_b, mh_norm = consts
    shapes = _state_shapes()
    return pl.pallas_call(
        _lead_state_kernel,
        grid=(1,),
        in_specs=[
            pl.BlockSpec((BLOCK, NAT_K), lambda i: (lead_blk, 0)),
            pl.BlockSpec((BLOCK, GATE_LANES), lambda i: (lead_blk, 0)),
            _whole((1, GATE_LANES)),
            _whole((CONV_WIDTH * BLOCK, 2 * BLOCK)),
            _whole((CONV_WIDTH, 2 * ML_QK_WIDTH)),
            _whole((1, 2 * ML_QK_WIDTH)),
            _whole((1, ML_WIDTH)),
        ],
        out_specs=[_whole(s.shape) for s in shapes],
        out_shape=shapes,
        scratch_shapes=[pltpu.VMEM((BLOCK, ML_WIDTH), BF16)],
        compiler_params=_params(1),
        name="lead_state",
    )(nat_lead, gates_lead, gate_bias, shift, conv_w, conv_b, mh_norm)


FUSED_TM = 512


def _inproj_mixer_kernel(*refs):
    x_ref, g_ref, wn_ref, wq_ref = refs[:4]
    mix_in = refs[4:4 + N_MIX_IN]
    state_in = refs[4 + N_MIX_IN:4 + N_MIX_IN + N_STATE]
    wu_ref = refs[4 + N_MIX_IN + N_STATE]
    nat_ref, t_ref, gate_ref, att_ref, hm_ref, wu_bf_ref = refs[5 + N_MIX_IN + N_STATE:-(1 + N_STATE)]
    u_ref = refs[-(1 + N_STATE)]
    state = refs[-N_STATE:]
    i = pl.program_id(0)
    j = pl.program_id(1)
    step = i * (IN_NAT_STEPS + 1) + j
    wu_bf_ref[...] = wu_ref[...].astype(BF16)

    @pl.when(j == 0)
    def _():
        u_ref[...] = _rms(x_ref[...], g_ref[...]).astype(BF16)

    @pl.when(step == 0)
    def _():
        _load_state(state, state_in)

    weights = [INPROJ_PHASES_PER_ROUND, MIXER_PHASES_PER_ROUND]

    @pl.when(j < IN_NAT_STEPS)
    def _():
        _drain(_interleave([_inproj_nat_phases(j, FUSED_TM, u_ref, wn_ref, nat_ref, gate_ref),
                            _mixer_phases(step + 1, mix_in, att_ref, hm_ref, state)], weights))

    @pl.when(j == IN_NAT_STEPS)
    def _():
        _drain(_interleave([_inproj_t_phases(FUSED_TM, u_ref, wq_ref, t_ref),
                            _mixer_phases(step + 1, mix_in, att_ref, hm_ref, state)], weights))


def _inproj_mixer(x2, row_tile0, g, w_nat, w_qv, mixer_args, w_up, lead_blk, nblk):
    n_tiles = nblk * BLOCK // FUSED_TM
    assert n_tiles * (IN_NAT_STEPS + 1) == nblk
    step = lambda i, j: i * (IN_NAT_STEPS + 1) + j
    proj_specs, proj_shapes = _inproj_out(FUSED_TM, n_tiles * FUSED_TM)
    mix_specs, mix_shapes = _mixer_out(step, nblk * BLOCK)
    wu_spec = pl.BlockSpec((D_MODEL, D_FF // nblk), lambda i, j: (0, step(i, j)))
    return pl.pallas_call(
        _inproj_mixer_kernel,
        grid=(n_tiles, IN_NAT_STEPS + 1),
        in_specs=[
            pl.BlockSpec((FUSED_TM, D_MODEL), lambda i, j: (row_tile0 + i, 0)),
            pl.BlockSpec((1, D_MODEL), lambda i, j: (0, 0)),
        ] + _inproj_weight_specs() + _mixer_in_specs(step, lead_blk) + [wu_spec],
        out_specs=proj_specs + mix_specs + [wu_spec],
        out_shape=proj_shapes + mix_shapes + [jax.ShapeDtypeStruct(w_up.shape, BF16)],
        scratch_shapes=[pltpu.VMEM((FUSED_TM, D_MODEL), BF16)] + _state_scratch(),
        compiler_params=_params(2),
        name="inproj_mixer",
    )(x2, g, w_nat, w_qv, *mixer_args, w_up)


def _mixer_kernel(*refs):
    mix_in = refs[:N_MIX_IN]
    state_in = refs[N_MIX_IN:N_MIX_IN + N_STATE]
    wo_ref, wd_ref = refs[N_MIX_IN + N_STATE:N_MIX_IN + N_STATE + 2]
    att_ref, hm_ref, wo_bf_ref, wd_bf_ref = refs[N_MIX_IN + N_STATE + 2:-N_STATE]
    state = refs[-N_STATE:]
    step = pl.program_id(0)

    @pl.when(step == 0)
    def _():
        _load_state(state, state_in)

    wo_bf_ref[...] = wo_ref[...].astype(BF16)
    wd_bf_ref[...] = wd_ref[...].astype(BF16)
    _drain(_mixer_phases(step + 1, mix_in, att_ref, hm_ref, state))


def _mixer(mixer_args, w_out, w_down, lead_blk, nblk):
    step = lambda s: s
    mix_specs, mix_shapes = _mixer_out(step, nblk * BLOCK)
    w_specs = [
        pl.BlockSpec((w_out.shape[0] // nblk, D_MODEL), lambda s: (s, 0)),
        pl.BlockSpec((D_FF // nblk, D_MODEL), lambda s: (s, 0)),
    ]
    return pl.pallas_call(
        _mixer_kernel,
        grid=(nblk,),
        in_specs=_mixer_in_specs(step, lead_blk) + w_specs,
        out_specs=mix_specs + w_specs,
        out_shape=mix_shapes + [jax.ShapeDtypeStruct(w.shape, BF16) for w in (w_out, w_down)],
        scratch_shapes=_state_scratch(),
        compiler_params=_params(1),
        name="mixer",
    )(*mixer_args, w_out, w_down)


OUT_TM = 512


def _outproj_kernel(att0_ref, att1_ref, hm0_ref, hm1_ref, x_ref, wa_ref, wm_ref, g_ref,
                    h_ref, u_ref):
    i = pl.program_id(0)
    half = pl.num_programs(0) // 2

    def body(att_ref, hm_ref):
        h = (x_ref[...]
             + lax.dot_general(att_ref[...], wa_ref[...], CONTRACT_FIRST, preferred_element_type=F32)
             + jnp.dot(hm_ref[...], wm_ref[...], preferred_element_type=F32))
        h_ref[...] = h
        u_ref[...] = _rms(h, g_ref[...]).astype(BF16)

    @pl.when(i < half)
    def _():
        body(att0_ref, hm0_ref)

    @pl.when(i >= half)
    def _():
        body(att1_ref, hm1_ref)


def _outproj(att0, att1, hm0, hm1, x2, w_out, g):
    rows = x2.shape[0]
    n = rows // OUT_TM
    half = n // 2
    first = lambda i: jnp.minimum(i, half - 1)
    second = lambda i: jnp.maximum(i - half, 0)
    return pl.pallas_call(
        _outproj_kernel,
        grid=(n,),
        in_specs=[
            pl.BlockSpec((ATT_WIDTH, OUT_TM), lambda i: (0, first(i))),
            pl.BlockSpec((ATT_WIDTH, OUT_TM), lambda i: (0, second(i))),
            pl.BlockSpec((OUT_TM, ML_WIDTH), lambda i: (first(i), 0)),
            pl.BlockSpec((OUT_TM, ML_WIDTH), lambda i: (second(i), 0)),
            pl.BlockSpec((OUT_TM, D_MODEL), lambda i: (i, 0)),
            pl.BlockSpec((ATT_WIDTH, D_MODEL), lambda i: (0, 0)),
            pl.BlockSpec((ML_WIDTH, D_MODEL), lambda i: (ATT_WIDTH // ML_WIDTH, 0)),
            pl.BlockSpec((1, D_MODEL), lambda i: (0, 0)),
        ],
        out_specs=[
            pl.BlockSpec((OUT_TM, D_MODEL), lambda i: (i, 0)),
            pl.BlockSpec((OUT_TM, D_MODEL), lambda i: (i, 0)),
        ],
        out_shape=[
            jax.ShapeDtypeStruct((rows, D_MODEL), F32),
            jax.ShapeDtypeStruct((rows, D_MODEL), BF16),
        ],
        compiler_params=_params(1),
        name="outproj",
    )(att0, att1, hm0, hm1, x2, w_out, w_out, g)


MLP_TM = 1024
MLP_TF = 1024


def _mlp_step(mode, u_ref, wu_ref, wd_ref, o_ref, hbuf_ref=None, ssq_ref=None):
    acts = []
    for c in range(MLP_TF // MXU_COLS):
        cs = slice(c * MXU_COLS, (c + 1) * MXU_COLS)
        a = jnp.dot(u_ref[...], wu_ref[:, cs], preferred_element_type=F32)
        acts.append(jnp.square(jnp.maximum(a, 0.0)).astype(BF16))
        yield
    act = jnp.concatenate(acts, axis=1)
    for c in range(D_MODEL // MXU_COLS):
        cs = slice(c * MXU_COLS, (c + 1) * MXU_COLS)
        d = jnp.dot(act, wd_ref[:, cs], preferred_element_type=F32)
        if mode == "first":
            o_ref[:, cs] = d
        elif mode == "mid":
            o_ref[:, cs] += d
        else:
            v = o_ref[:, cs] + d + hbuf_ref[:, cs]
            o_ref[:, cs] = v
            sq = v * v
            part = sum(sq[:, k * LANE:(k + 1) * LANE] for k in range(MXU_COLS // LANE))
            if c == 0:
                ssq_ref[...] = part
            else:
                ssq_ref[...] += part
        yield


def _mlp_kernel(u_ref, h_hbm, wu_ref, wd_ref, g_ref, o_ref, hbuf_ref, ssq_ref, hsem):
    i = pl.program_id(0)
    j = pl.program_id(1)
    last = pl.num_programs(1) - 1
    r0 = pl.multiple_of(i * MLP_TM, MLP_TM)
    h_copy = pltpu.make_async_copy(h_hbm.at[pl.ds(r0, MLP_TM), :], hbuf_ref, hsem)

    @pl.when(j == 0)
    def _():
        h_copy.start()
        _drain(_mlp_step("first", u_ref, wu_ref, wd_ref, o_ref))

    @pl.when((j > 0) & (j < last))
    def _():
        _drain(_mlp_step("mid", u_ref, wu_ref, wd_ref, o_ref))

    @pl.when(j == last)
    def _():
        h_copy.wait()
        _drain(_mlp_step("last", u_ref, wu_ref, wd_ref, o_ref, hbuf_ref, ssq_ref))
        ms = jnp.sum(ssq_ref[...], axis=-1, keepdims=True) / D_MODEL
        o_ref[...] = o_ref[...] * lax.rsqrt(ms + EPS) * g_ref[...]


def _mlp(u, h, w_up, w_down, g):
    rows = u.shape[0]
    return pl.pallas_call(
        _mlp_kernel,
        grid=(rows // MLP_TM, D_FF // MLP_TF),
        in_specs=[
            pl.BlockSpec((MLP_TM, D_MODEL), lambda i, j: (i, 0)),
            pl.BlockSpec(memory_space=pl.ANY),
            pl.BlockSpec((D_MODEL, MLP_TF), lambda i, j: (0, j)),
            pl.BlockSpec((MLP_TF, D_MODEL), lambda i, j: (j, 0)),
            pl.BlockSpec((1, D_MODEL), lambda i, j: (0, 0)),
        ],
        out_specs=pl.BlockSpec((MLP_TM, D_MODEL), lambda i, j: (i, 0)),
        out_shape=jax.ShapeDtypeStruct((rows, D_MODEL), F32),
        scratch_shapes=[pltpu.VMEM((MLP_TM, D_MODEL), F32), pltpu.VMEM((MLP_TM, LANE), F32),
                        pltpu.SemaphoreType.DMA(())],
        compiler_params=_params(2),
        name="mlp",
    )(u, h, w_up, w_down, g)


def _t5_bucket(dist):
    max_exact = N_BUCKETS // 2
    d = jnp.maximum(dist, 0)
    ratio = jnp.maximum(d, max_exact).astype(F32) / max_exact
    large = max_exact + (jnp.log(ratio) / math.log(MAX_DISTANCE / max_exact)
                         * (N_BUCKETS - max_exact)).astype(jnp.int32)
    large = jnp.minimum(large, N_BUCKETS - 1)
    return jnp.where(d < max_exact, d, large)


BIAS_ROWS = 16


BIAS_BUCKETS_PER_PHASE = 8


def _bias_fill_splats(rb_ref, splat_ref):
    def fill(b, carry):
        for hd in range(ATT_HEADS):
            splat_ref[b * ATT_HEADS + hd] = jnp.full((SUBLANES, BLOCK), rb_ref[b, hd], F32) * LOG2E
        return carry

    lax.fori_loop(0, N_BUCKETS, fill, 0)


def _bias_lookup_phases(bkt, splat_ref, write):
    accs = [jnp.zeros(bkt.shape, F32)] * ATT_HEADS
    for b in range(N_BUCKETS):
        hit = (bkt >= b) & (bkt < b + 1)
        accs = [jnp.where(hit, jnp.concatenate([splat_ref[b * ATT_HEADS + hd]]
                                               * (BIAS_ROWS // SUBLANES), axis=0), a)
                for hd, a in enumerate(accs)]
        if b % BIAS_BUCKETS_PER_PHASE == BIAS_BUCKETS_PER_PHASE - 1:
            yield
    write(accs)


def _bias_step_phases(bkt_band_ref, bkt_meta_ref, band_ref, meta_ref, splat_ref):
    def write_band(tables):
        for hd, t in enumerate(tables):
            h, g = divmod(hd, ATT_GROUP)
            band_ref[h, :, g * BLOCK:(g + 1) * BLOCK] = t

    yield from _bias_lookup_phases(bkt_band_ref[...], splat_ref, write_band)
    for k in range(bkt_meta_ref.shape[0]):
        def write_meta(tables, k=k):
            for hd, t in enumerate(tables):
                h, g = divmod(hd, ATT_GROUP)
                meta_ref[k, h, :, g * BLOCK:(g + 1) * BLOCK] = t

        yield from _bias_lookup_phases(bkt_meta_ref[k], splat_ref, write_meta)


def _bias_buckets(nblk):
    c = jnp.arange(2 * BLOCK)[:, None]
    r = jnp.arange(BLOCK)[None, :]
    bkt_band = _t5_bucket(r + BLOCK - c).astype(jnp.int32)
    q_pos = BLOCK + jnp.arange(nblk * BLOCK)[None, :]
    m_pos = N_PAD + jnp.arange(N_META)[:, None]
    bkt_meta = _t5_bucket(q_pos - m_pos).astype(jnp.int32)
    return bkt_band, bkt_meta.reshape(N_META, nblk, BLOCK).transpose(1, 0, 2)


W_IN_OFFS = dict(q=0, k=1024, v=1280, mq=1536, mk=2048, mv=2560, mo=3584, gates=4608)
PREP_ROWS = 256
NAT_SRC = ([W_IN_OFFS["mq"] // PREP_ROWS + t
            for t in range((W_IN_OFFS["gates"] - W_IN_OFFS["mq"]) // PREP_ROWS)]
           + [W_IN_OFFS["k"] // PREP_ROWS + t for t in range(ATT_KV_WIDTH // PREP_ROWS)]
           + [W_IN_OFFS["gates"] // PREP_ROWS])
QV_SRC = ([W_IN_OFFS["q"] // PREP_ROWS + t for t in range(ATT_WIDTH // PREP_ROWS)]
          + [W_IN_OFFS["v"] // PREP_ROWS + t for t in range(ATT_KV_WIDTH // PREP_ROWS)])
N_NAT_BLK = len(NAT_SRC)
N_Q_BLK = ATT_WIDTH // PREP_ROWS
assert (N_NAT_BLK - 1) * PREP_ROWS + GATE_LANES == NAT_COLS


def _win_prep_kernel(src_ref, w_ref, nat_ref, qv_ref):
    j = pl.program_id(0)
    wv = w_ref[0]

    @pl.when(j < N_NAT_BLK - 1)
    def _():
        nat_ref[...] = wv.astype(BF16)

    @pl.when(j == N_NAT_BLK - 1)
    def _():
        row = lax.broadcasted_iota(jnp.int32, wv.shape, 0)
        nat_ref[...] = jnp.where(row < 2 * ML_HEADS, wv, 0.0).astype(BF16)

    @pl.when((j >= N_NAT_BLK) & (j < N_NAT_BLK + N_Q_BLK))
    def _():
        qv_ref[...] = (wv * (ATT_HEAD_DIM ** -0.5 * LOG2E)).astype(BF16)

    @pl.when(j >= N_NAT_BLK + N_Q_BLK)
    def _():
        qv_ref[...] = wv.astype(BF16)


def _win_prep(w_in_t):
    src = jnp.asarray(NAT_SRC + QV_SRC, jnp.int32)
    d = w_in_t.shape[2]
    return pl.pallas_call(
        _win_prep_kernel,
        grid_spec=pltpu.PrefetchScalarGridSpec(
            num_scalar_prefetch=1,
            grid=(len(NAT_SRC) + len(QV_SRC),),
            in_specs=[pl.BlockSpec((1, PREP_ROWS, d), lambda j, src: (0, src[j], 0))],
            out_specs=[
                pl.BlockSpec((PREP_ROWS, d), lambda j, src: (jnp.minimum(j, N_NAT_BLK - 1), 0)),
                pl.BlockSpec((PREP_ROWS, d), lambda j, src: (jnp.maximum(j - N_NAT_BLK, 0), 0)),
            ],
        ),
        out_shape=[
            jax.ShapeDtypeStruct((N_NAT_BLK * PREP_ROWS, d), BF16),
            jax.ShapeDtypeStruct((T_ROWS, d), BF16),
        ],
        compiler_params=_params(1),
        name="w_in_prep",
    )(src, w_in_t)


def kernel(x, meta_tokens, w_in, conv_w, conv_b, b_igate, b_fgate, attn_sinks, rel_bias,
           mh_norm, w_out, norm_mix, norm_mlp, w_up, w_down, norm_final):
    batch, seq, d = x.shape
    assert batch == 2, "the two batches are processed one projection / mixer stage apart"
    nblk = seq // BLOCK
    x2 = x.reshape(batch * seq, d)
    lead = jnp.concatenate([jnp.zeros((N_PAD, d), x.dtype), meta_tokens.astype(x.dtype)], axis=0)
    g_mix = norm_mix[0][None].astype(F32)

    w_nat, w_qv = _win_prep(jnp.swapaxes(w_in, 1, 2))
    gate_bias = jnp.pad(jnp.concatenate([b_igate[0], b_fgate[0]]).astype(F32),
                        (0, GATE_LANES - 2 * ML_HEADS))[None]

    (nat0, projt0, gates0), (nat_l, projt_l, gates_l), (bias_band_t, bias_meta_t) = _inproj(
        x2, lead, g_mix, w_nat, w_qv, rel_bias, seq // IN_TM, nblk)
    consts = (attn_sinks[0].astype(F32), bias_band_t, bias_meta_t, gate_bias, _conv_shift_matrix(),
              conv_w[0].astype(F32), conv_b[0][None].astype(F32), mh_norm[0][None].astype(F32))
    lead_blk = 0
    state = _lead_state(nat_l, gates_l, lead_blk, consts)
    args0 = _mixer_args(consts, nat0, projt0, gates0, nat_l, projt_l, state)
    nat1, projt1, gates1, att0, hm0, w_up_bf = _inproj_mixer(
        x2, seq // FUSED_TM, g_mix, w_nat, w_qv, args0, w_up[0], lead_blk, nblk)
    args1 = _mixer_args(consts, nat1, projt1, gates1, nat_l, projt_l, state)
    att1, hm1, w_out_bf, w_down_bf = _mixer(args1, w_out[0], w_down[0], lead_blk, nblk)

    h2, u2 = _outproj(att0, att1, hm0, hm1, x2, w_out_bf, norm_mlp[0][None].astype(F32))
    out = _mlp(u2, h2, w_up_bf, w_down_bf, norm_final[None].astype(F32))
    return out.reshape(batch, seq, d)
```

```python
import math

import numpy as np
import jax
import jax.numpy as jnp
from jax import lax
from jax.experimental import pallas as pl
from jax.experimental.pallas import tpu as pltpu

F32 = jnp.float32
BF16 = jnp.bfloat16

D_MODEL = 2048
N_META = 16
BLOCK = 128
N_PAD = BLOCK - N_META
WINDOW = 128
ATT_HEADS = 16
ATT_KV_HEADS = 4
ATT_GROUP = ATT_HEADS // ATT_KV_HEADS
ATT_HEAD_DIM = 64
ATT_WIDTH = ATT_HEADS * ATT_HEAD_DIM
ATT_KV_WIDTH = ATT_KV_HEADS * ATT_HEAD_DIM
ML_HEADS = 4
ML_V_DIM = 256
ML_QK_DIM = 128
ML_WIDTH = ML_HEADS * ML_V_DIM
ML_QK_WIDTH = ML_HEADS * ML_QK_DIM
CONV_WIDTH = 4
GATE_SOFTCAP = 15.0
D_FF = 4 * D_MODEL
N_BUCKETS = 32
MAX_DISTANCE = 128
EPS = 1e-6
NEG = -1e30
LOG2E = math.log2(math.e)

NAT_MQK = 0
NAT_MV = 2 * ML_QK_WIDTH
NAT_MO = NAT_MV + ML_WIDTH
NAT_COLS = NAT_MO + ML_WIDTH
T_ROWS = ATT_WIDTH + ATT_KV_WIDTH
GATE_LANES = 128
W3_K = T_ROWS
W3_G = W3_K + ATT_KV_WIDTH
W3_ROWS = W3_G + 256

V7X_VMEM_LIMIT = 60 * 1024 * 1024
LANE = 128
SUBLANES = 8
CONTRACT_LAST = (((1,), (1,)), ((), ()))
CONTRACT_FIRST = (((0,), (0,)), ((), ()))


def _params(n_axes, vmem=V7X_VMEM_LIMIT):
    return pltpu.CompilerParams(dimension_semantics=("arbitrary",) * n_axes,
                                vmem_limit_bytes=vmem)


def _rms(x, g):
    return x * lax.rsqrt(jnp.mean(x * x, axis=-1, keepdims=True) + EPS) * g


_DONE = object()


def _drain(phases):
    for _ in phases:
        pass


def _interleave(streams, weights):
    live = list(zip(streams, weights))
    while live:
        for item in list(live):
            stream, weight = item
            for _ in range(weight):
                if next(stream, _DONE) is _DONE:
                    live.remove(item)
                    break
        yield


IN_TM = 1024
IN_NAT_STEPS = 3
IN_TN = NAT_COLS // IN_NAT_STEPS
MXU_COLS = 256
N_PROJ_OUT = 4


def _inproj_nat_phases(rows, u_ref, wn_ref, nat_ref):
    u = u_ref[0:rows, :]
    for c0 in range(0, IN_TN, MXU_COLS):
        cs = slice(c0, c0 + MXU_COLS)
        nat_ref[0:rows, cs] = lax.dot_general(u, wn_ref[cs, :], CONTRACT_LAST,
                                              preferred_element_type=F32).astype(BF16)
        yield


def _inproj_last_phases(rows, u_ref, w3_ref, kn_ref, t_ref, gt_ref):
    u = u_ref[0:rows, :]
    for c0 in range(0, T_ROWS, MXU_COLS):
        cs = slice(c0, c0 + MXU_COLS)
        t_ref[cs, 0:rows] = lax.dot_general(w3_ref[cs, :], u, CONTRACT_LAST,
                                            preferred_element_type=F32).astype(BF16)
        yield
    kn_ref[0:rows, :] = lax.dot_general(u, w3_ref[W3_K:W3_G, :], CONTRACT_LAST,
                                        preferred_element_type=F32).astype(BF16)
    yield
    gt_ref[0:rows, :] = lax.dot_general(u, w3_ref[W3_G:W3_G + GATE_LANES, :], CONTRACT_LAST,
                                        preferred_element_type=F32)
    yield


BIAS_PHASES_PER_ROUND = 3


def _inproj_kernel(x_ref, lead_ref, g_ref, wn_ref, w3_ref, rb_ref, bkt_band_ref, bkt_meta_ref,
                   nat_ref, kn_ref, t_ref, gt_ref, nat_lead_ref, kn_lead_ref, t_lead_ref, gt_lead_ref,
                   band_ref, meta_ref, u_ref, u_lead_ref, splat_ref):
    i = pl.program_id(0)
    j = pl.program_id(1)

    @pl.when(j == 0)
    def _():
        u_ref[...] = _rms(x_ref[...], g_ref[...]).astype(BF16)

    @pl.when((j == 0) & (i == 0))
    def _():
        u_lead_ref[...] = _rms(lead_ref[...], g_ref[...]).astype(BF16)
        _bias_fill_splats(rb_ref, splat_ref)

    def with_bias(phases):
        bias = _bias_step_phases(bkt_band_ref, bkt_meta_ref, band_ref, meta_ref, splat_ref)
        _drain(_interleave([phases, bias], [1, BIAS_PHASES_PER_ROUND]))

    @pl.when(j < IN_NAT_STEPS)
    def _():
        with_bias(_inproj_nat_phases(IN_TM, u_ref, wn_ref, nat_ref))

    @pl.when(j == IN_NAT_STEPS)
    def _():
        with_bias(_inproj_last_phases(IN_TM, u_ref, w3_ref, kn_ref, t_ref, gt_ref))

    @pl.when((i == 0) & (j < IN_NAT_STEPS))
    def _():
        _drain(_inproj_nat_phases(BLOCK, u_lead_ref, wn_ref, nat_lead_ref))

    @pl.when((i == 0) & (j == IN_NAT_STEPS))
    def _():
        _drain(_inproj_last_phases(BLOCK, u_lead_ref, w3_ref, kn_lead_ref, t_lead_ref, gt_lead_ref))


def _inproj_weight_specs():
    last_nat = IN_NAT_STEPS - 1
    return [
        pl.BlockSpec((IN_TN, D_MODEL), lambda i, j: (jnp.minimum(j, last_nat), 0)),
        pl.BlockSpec((W3_ROWS, D_MODEL), lambda i, j: (0, 0)),
    ]


def _inproj_out(tm, rows):
    last_nat = IN_NAT_STEPS - 1
    specs = [
        pl.BlockSpec((tm, IN_TN), lambda i, j: (i, jnp.minimum(j, last_nat))),
        pl.BlockSpec((tm, ATT_KV_WIDTH), lambda i, j: (i, 0)),
        pl.BlockSpec((T_ROWS, tm), lambda i, j: (0, i)),
        pl.BlockSpec((tm, GATE_LANES), lambda i, j: (i, 0)),
    ]
    shapes = [
        jax.ShapeDtypeStruct((rows, NAT_COLS), BF16),
        jax.ShapeDtypeStruct((rows, ATT_KV_WIDTH), BF16),
        jax.ShapeDtypeStruct((T_ROWS, rows), BF16),
        jax.ShapeDtypeStruct((rows, GATE_LANES), F32),
    ]
    return specs, shapes


def _inproj(x2, lead, g, w_nat, w3, rel_bias, n_tiles, nblk):
    out_specs, out_shape = _inproj_out(IN_TM, n_tiles * IN_TM)
    last_nat = IN_NAT_STEPS - 1
    n_steps = n_tiles * (IN_NAT_STEPS + 1)
    assert n_steps * BIAS_ROWS == 2 * BLOCK and nblk % n_steps == 0
    meta_per_step = nblk // n_steps
    bkt_band, bkt_meta = _bias_buckets(nblk)
    step = lambda i, j: i * (IN_NAT_STEPS + 1) + j
    bias_in_specs = [
        pl.BlockSpec(memory_space=pltpu.SMEM),
        pl.BlockSpec((BIAS_ROWS, BLOCK), lambda i, j: (step(i, j), 0)),
        pl.BlockSpec((meta_per_step, N_META, BLOCK), lambda i, j: (step(i, j), 0, 0)),
    ]
    bias_out_specs = [
        pl.BlockSpec((ATT_KV_HEADS, BIAS_ROWS, GQ), lambda i, j: (0, step(i, j), 0)),
        pl.BlockSpec((meta_per_step, ATT_KV_HEADS, N_META, GQ), lambda i, j: (step(i, j), 0, 0, 0)),
    ]
    bias_shapes = [
        jax.ShapeDtypeStruct((ATT_KV_HEADS, 2 * BLOCK, GQ), F32),
        jax.ShapeDtypeStruct((nblk, ATT_KV_HEADS, N_META, GQ), F32),
    ]
    lead_col = lambda i, j: jnp.where(i == 0, jnp.minimum(j, last_nat), last_nat)
    lead_specs = [
        pl.BlockSpec((BLOCK, IN_TN), lambda i, j: (0, lead_col(i, j))),
        pl.BlockSpec((BLOCK, ATT_KV_WIDTH), lambda i, j: (0, 0)),
        pl.BlockSpec((T_ROWS, BLOCK), lambda i, j: (0, 0)),
        pl.BlockSpec((BLOCK, GATE_LANES), lambda i, j: (0, 0)),
    ]
    lead_shapes = [
        jax.ShapeDtypeStruct((BLOCK, NAT_COLS), BF16),
        jax.ShapeDtypeStruct((BLOCK, ATT_KV_WIDTH), BF16),
        jax.ShapeDtypeStruct((T_ROWS, BLOCK), BF16),
        jax.ShapeDtypeStruct((BLOCK, GATE_LANES), F32),
    ]
    outs = pl.pallas_call(
        _inproj_kernel,
        grid=(n_tiles, IN_NAT_STEPS + 1),
        in_specs=[
            pl.BlockSpec((IN_TM, D_MODEL), lambda i, j: (i, 0)),
            pl.BlockSpec((BLOCK, D_MODEL), lambda i, j: (0, 0)),
            pl.BlockSpec((1, D_MODEL), lambda i, j: (0, 0)),
        ] + _inproj_weight_specs() + bias_in_specs,
        out_specs=out_specs + lead_specs + bias_out_specs,
        out_shape=out_shape + lead_shapes + bias_shapes,
        scratch_shapes=[pltpu.VMEM((IN_TM, D_MODEL), BF16), pltpu.VMEM((BLOCK, D_MODEL), BF16),
                        pltpu.VMEM((N_BUCKETS * ATT_HEADS, SUBLANES, BLOCK), F32)],
        compiler_params=_params(2),
        name="inproj",
    )(x2, lead, g, w_nat, w3, rel_bias.astype(F32), bkt_band, bkt_meta)
    return outs[:N_PROJ_OUT], outs[N_PROJ_OUT:2 * N_PROJ_OUT], outs[2 * N_PROJ_OUT:]


GQ = ATT_GROUP * BLOCK
ATT_SLABS_PER_PHASE = 2


def _attn_prologue(n, kc_ref, kp_ref, km_ref):
    c = lax.broadcasted_iota(jnp.int32, (2 * BLOCK, BLOCK), 0)
    r = lax.broadcasted_iota(jnp.int32, (2 * BLOCK, BLOCK), 1)
    dist = r + BLOCK - c
    band_ok = (dist >= 0) & (dist < WINDOW) & ((c >= BLOCK) | (n >= 1))
    mm = lax.broadcasted_iota(jnp.int32, (N_META, BLOCK), 0)
    rr = lax.broadcasted_iota(jnp.int32, (N_META, BLOCK), 1)
    meta_ok = (n + 1) * BLOCK + rr >= N_PAD + mm
    kall = jnp.concatenate([kp_ref[...], kc_ref[...], km_ref[...]], axis=0)
    return band_ok, meta_ok, kall


BF16_ROWS = 16


def _ones_row(width):
    row = lax.broadcasted_iota(jnp.int32, (BF16_ROWS, width), 0)
    return jnp.where(row == 0, 1.0, 0.0).astype(BF16)


def _attn_head(h, pro, sink_ref, qt_ref, vc_ref, vp_ref, vl_ref, bb_ref, bm_ref, o_ref):
    band_ok, meta_ok, kall = pro
    lo = (h // 2) * LANE
    kpair = kall[:, lo:lo + LANE]
    qh = jnp.concatenate(
        [qt_ref[(ATT_GROUP * h + g) * ATT_HEAD_DIM:(ATT_GROUP * h + g + 1) * ATT_HEAD_DIM, :]
         for g in range(ATT_GROUP)], axis=1)
    zq = jnp.zeros_like(qh)
    qz = jnp.concatenate([qh, zq] if h % 2 == 0 else [zq, qh], axis=0)
    st = jnp.dot(kpair, qz, preferred_element_type=F32)
    yield
    zpad = jnp.zeros((N_PAD, BLOCK), BF16)
    p_band, p_lead, sink_p = [], [], []
    for g in range(ATT_GROUP):
        gs = slice(g * BLOCK, (g + 1) * BLOCK)
        sl = st[:, gs]
        sink = jnp.full((1, BLOCK), sink_ref[ATT_GROUP * h + g], F32) * LOG2E
        lb = jnp.where(band_ok, sl[:2 * BLOCK] + bb_ref[h, :, gs], NEG * LOG2E)
        lm = jnp.where(meta_ok, sl[2 * BLOCK:] + bm_ref[0, h, :, gs], NEG * LOG2E)
        m = jnp.maximum(jnp.max(lb, axis=0, keepdims=True), jnp.max(lm, axis=0, keepdims=True))
        m = jnp.maximum(m, sink)
        pb = jnp.exp2(lb - m)
        pm = jnp.exp2(lm - m)
        sink_p.append(jnp.exp2(sink - m))
        p_band.append(pb.astype(BF16))
        p_lead.append(jnp.concatenate([zpad, pm.astype(BF16)], axis=0))
        if g % ATT_SLABS_PER_PHASE == ATT_SLABS_PER_PHASE - 1:
            yield
    pbt = jnp.concatenate(p_band, axis=1)
    plt = jnp.concatenate(p_lead, axis=1)
    hs = slice(h * ATT_HEAD_DIM, (h + 1) * ATT_HEAD_DIM)
    vband = jnp.concatenate([vp_ref[hs, :], vc_ref[hs, :]], axis=1)
    ones_b = _ones_row(2 * BLOCK)
    ones_l = _ones_row(BLOCK)
    ot = (jnp.dot(jnp.concatenate([vband, ones_b], axis=0), pbt, preferred_element_type=F32)
          + jnp.dot(jnp.concatenate([vl_ref[hs, :], ones_l], axis=0), plt,
                    preferred_element_type=F32))
    for g in range(ATT_GROUP):
        gs = slice(g * BLOCK, (g + 1) * BLOCK)
        den = ot[ATT_HEAD_DIM:ATT_HEAD_DIM + 1, gs] + sink_p[g]
        row = (ATT_GROUP * h + g) * ATT_HEAD_DIM
        o_ref[row:row + ATT_HEAD_DIM, :] = (ot[:ATT_HEAD_DIM, gs] * (1.0 / den)).astype(BF16)


def _conv_shift_matrix():
    s = np.zeros((CONV_WIDTH * BLOCK, 2 * BLOCK), np.float32)
    for k in range(CONV_WIDTH):
        t = np.arange(BLOCK)
        s[k * BLOCK + t, BLOCK + t - k] = 1.0
    return jnp.asarray(s, BF16)


def _mlstm_gates(n, gate_ref, gb_ref):
    t_idx = lax.broadcasted_iota(jnp.int32, (BLOCK, GATE_LANES), 0)
    valid = (n > 0) | (t_idx >= N_PAD)
    pre = GATE_SOFTCAP * jnp.tanh((gate_ref[...] + gb_ref[...]) / GATE_SOFTCAP)
    log_i = jnp.where(valid, pre, NEG)
    log_sig = jnp.minimum(pre, 0.0) - jnp.log1p(jnp.exp(-jnp.abs(pre)))
    log_f = jnp.where(valid, log_sig, 0.0)
    row = lax.broadcasted_iota(jnp.int32, (BLOCK, BLOCK), 0)
    col = lax.broadcasted_iota(jnp.int32, (BLOCK, BLOCK), 1)
    causal = col <= row
    tril = jnp.where(causal, 1.0, 0.0).astype(F32)
    b_all = jnp.dot(tril, log_f, preferred_element_type=F32,
                    precision=lax.Precision.HIGHEST)
    return log_i, b_all, log_i.T, b_all.T, causal


def _mlstm_conv(p_ref, shift_ref, cw_ref, cb_ref, xprev_ref):
    xcur = p_ref[:, NAT_MQK:NAT_MV]
    xcat = jnp.concatenate([xprev_ref[...], xcur], axis=0)
    sh = jnp.dot(shift_ref[...], xcat, preferred_element_type=F32)
    xprev_ref[...] = xcur
    y = cb_ref[...]
    for k in range(CONV_WIDTH):
        y = y + cw_ref[CONV_WIDTH - 1 - k:CONV_WIDTH - k, :] * sh[k * BLOCK:(k + 1) * BLOCK]
    act = y * jax.nn.sigmoid(y)
    return act[:, :ML_QK_WIDTH] * (ML_QK_DIM ** -0.5), act[:, ML_QK_WIDTH:]


def _mlstm_head(h, gates, q_all, k_all, p_ref, mhn_ref, o_ref, c_ref, n_ref, m_ref):
    log_i, b_all, log_i_t, b_all_t, causal = gates
    q_f = q_all[:, h * ML_QK_DIM:(h + 1) * ML_QK_DIM]
    k_f = k_all[:, h * ML_QK_DIM:(h + 1) * ML_QK_DIM]
    q_h = q_f.astype(BF16)
    k_h = k_f.astype(BF16)
    v_h = p_ref[:, NAT_MV + h * ML_V_DIM:NAT_MV + (h + 1) * ML_V_DIM]
    b_col = b_all[:, ML_HEADS + h:ML_HEADS + h + 1]
    li_col = log_i[:, h:h + 1]
    b_row = b_all_t[ML_HEADS + h:ML_HEADS + h + 1, :]
    li_row = log_i_t[h:h + 1, :]
    b_last = b_all[BLOCK - 1:BLOCK, ML_HEADS + h:ML_HEADS + h + 1]
    c_prev = c_ref[h]
    n_prev = n_ref[h][0:1, :]
    m_prev = m_ref[h][0:1, 0:1]

    dmat = jnp.where(causal, b_col - b_row + li_row, -jnp.inf)
    inter = b_col + m_prev
    m_t = jnp.maximum(inter, jnp.max(dmat, axis=1, keepdims=True))
    s = lax.dot_general(q_h, k_h, CONTRACT_LAST, preferred_element_type=F32) * jnp.exp(dmat - m_t)
    yield
    a_t = jnp.exp(inter - m_t)
    num = (jnp.dot(s.astype(BF16), v_h, preferred_element_type=F32)
           + a_t * jnp.dot(q_h, c_prev.astype(BF16), preferred_element_type=F32))
    den = (jnp.sum(s, axis=1, keepdims=True)
           + a_t * jnp.sum(q_f * n_prev, axis=1, keepdims=True))
    den = jnp.maximum(jnp.abs(den), jnp.exp(-m_t))
    hh = num / den
    hh = hh * lax.rsqrt(jnp.mean(hh * hh, axis=-1, keepdims=True) + EPS)
    vs = slice(h * ML_V_DIM, (h + 1) * ML_V_DIM)
    hh = hh * mhn_ref[:, vs]
    m_o = p_ref[:, NAT_MO + h * ML_V_DIM:NAT_MO + (h + 1) * ML_V_DIM].astype(F32)
    o_ref[:, vs] = (hh * jax.nn.sigmoid(m_o)).astype(BF16)
    yield

    g_col = b_last - b_col + li_col
    m_loc = jnp.max(g_col, axis=0, keepdims=True)
    wk = jnp.exp(g_col - m_loc) * k_f
    c_loc = lax.dot_general(wk.astype(BF16), v_h, CONTRACT_FIRST,
                            preferred_element_type=F32)
    n_loc = jnp.sum(wk, axis=0, keepdims=True)
    m_new = jnp.maximum(b_last + m_prev, m_loc)
    a = jnp.exp(b_last + m_prev - m_new)
    cc = jnp.exp(m_loc - m_new)
    c_ref[h] = a * c_prev + cc * c_loc
    n_ref[h] = jnp.broadcast_to(a * n_prev + cc * n_loc, (8, ML_QK_DIM))
    m_ref[h] = jnp.broadcast_to(m_new, (8, LANE))


N_MIX_IN = 17
N_STATE = 4
MIXER_PHASES_PER_NAT_PHASE = 6
MIXER_PHASES_PER_LAST_PHASE = 3


def _mixer_phases(n, mix_in, att_ref, hm_ref, state):
    (sink_ref, qt_ref, kc_ref, kp_ref, km_ref, vc_ref, vp_ref, vl_ref, bb_ref, bm_ref,
     p_ref, gate_ref, gb_ref, shift_ref, cw_ref, cb_ref, mhn_ref) = mix_in
    c_ref, n_ref, m_ref, xprev_ref = state

    def attention_phases():
        pro = _attn_prologue(n - 1, kc_ref, kp_ref, km_ref)
        for h in range(ATT_KV_HEADS):
            yield from _attn_head(h, pro, sink_ref, qt_ref, vc_ref, vp_ref, vl_ref, bb_ref, bm_ref,
                                  att_ref)

    def mlstm_phases():
        gates = _mlstm_gates(n, gate_ref, gb_ref)
        yield
        q_all, k_all = _mlstm_conv(p_ref, shift_ref, cw_ref, cb_ref, xprev_ref)
        yield
        for h in range(ML_HEADS):
            yield from _mlstm_head(h, gates, q_all, k_all, p_ref, mhn_ref, hm_ref, c_ref, n_ref, m_ref)

    return _interleave([mlstm_phases(), attention_phases()], [1, 1])


def _load_state(state, state_in):
    for dst, src in zip(state, state_in):
        dst[...] = src[...]


def _state_shapes():
    return [
        jax.ShapeDtypeStruct((ML_HEADS, ML_QK_DIM, ML_V_DIM), F32),
        jax.ShapeDtypeStruct((ML_HEADS, SUBLANES, ML_QK_DIM), F32),
        jax.ShapeDtypeStruct((ML_HEADS, SUBLANES, LANE), F32),
        jax.ShapeDtypeStruct((BLOCK, 2 * ML_QK_WIDTH), BF16),
    ]


def _state_scratch():
    return [pltpu.VMEM(s.shape, s.dtype) for s in _state_shapes()]


def _whole(shape):
    return pl.BlockSpec(shape, lambda *ids: (0,) * len(shape))


def _mixer_in_specs(step):
    cur = lambda *ids: step(*ids)
    prev = lambda *ids: jnp.maximum(step(*ids) - 1, 0)
    v_row = ATT_WIDTH // ATT_KV_WIDTH
    meta_rows = N_PAD // N_META
    return [
        pl.BlockSpec(memory_space=pltpu.SMEM),
        pl.BlockSpec((ATT_WIDTH, BLOCK), lambda *ids: (0, cur(*ids))),
        pl.BlockSpec((BLOCK, ATT_KV_WIDTH), lambda *ids: (cur(*ids), 0)),
        pl.BlockSpec((BLOCK, ATT_KV_WIDTH), lambda *ids: (prev(*ids), 0)),
        pl.BlockSpec((N_META, ATT_KV_WIDTH), lambda *ids: (meta_rows, 0)),
        pl.BlockSpec((ATT_KV_WIDTH, BLOCK), lambda *ids: (v_row, cur(*ids))),
        pl.BlockSpec((ATT_KV_WIDTH, BLOCK), lambda *ids: (v_row, prev(*ids))),
        pl.BlockSpec((ATT_KV_WIDTH, BLOCK), lambda *ids: (v_row, 0)),
        _whole((ATT_KV_HEADS, 2 * BLOCK, GQ)),
        pl.BlockSpec((1, ATT_KV_HEADS, N_META, GQ), lambda *ids: (cur(*ids), 0, 0, 0)),
        pl.BlockSpec((BLOCK, NAT_COLS), lambda *ids: (cur(*ids), 0)),
        pl.BlockSpec((BLOCK, GATE_LANES), lambda *ids: (cur(*ids), 0)),
        _whole((1, GATE_LANES)),
        _whole((CONV_WIDTH * BLOCK, 2 * BLOCK)),
        _whole((CONV_WIDTH, 2 * ML_QK_WIDTH)),
        _whole((1, 2 * ML_QK_WIDTH)),
        _whole((1, ML_WIDTH)),
    ] + [_whole(s.shape) for s in _state_shapes()]


def _mixer_args(consts, proj_b, proj_lead, state):
    sinks, bias_band_t, bias_meta_t, gate_bias, shift, conv_w, conv_b, mh_norm = consts
    nat_b, kn_b, projt_b, gt_b = proj_b
    _, kn_lead, projt_lead, _ = proj_lead
    return [sinks, projt_b, kn_b, kn_b, kn_lead, projt_b, projt_b, projt_lead, bias_band_t,
            bias_meta_t, nat_b, gt_b, gate_bias, shift, conv_w, conv_b, mh_norm] + list(state)


def _mixer_out(step, n_rows):
    specs = [
        pl.BlockSpec((ATT_WIDTH, BLOCK), lambda *ids: (0, step(*ids))),
        pl.BlockSpec((BLOCK, ML_WIDTH), lambda *ids: (step(*ids), 0)),
    ]
    shapes = [
        jax.ShapeDtypeStruct((ATT_WIDTH, n_rows), BF16),
        jax.ShapeDtypeStruct((n_rows, ML_WIDTH), BF16),
    ]
    return specs, shapes


def _lead_state_kernel(p_ref, gate_ref, gb_ref, shift_ref, cw_ref, cb_ref, mhn_ref,
                       c_ref, n_ref, m_ref, xprev_ref, hm_ref):
    for ref in (c_ref, n_ref, m_ref, xprev_ref):
        ref[...] = jnp.zeros_like(ref)
    gates = _mlstm_gates(0, gate_ref, gb_ref)
    q_all, k_all = _mlstm_conv(p_ref, shift_ref, cw_ref, cb_ref, xprev_ref)
    for h in range(ML_HEADS):
        _drain(_mlstm_head(h, gates, q_all, k_all, p_ref, mhn_ref, hm_ref, c_ref, n_ref, m_ref))


def _lead_state(proj_lead, consts):
    nat_lead, _, _, gates_lead = proj_lead
    _, _, _, gate_bias, shift, conv_w, conv_b, mh_norm = consts
    shapes = _state_shapes()
    return pl.pallas_call(
        _lead_state_kernel,
        grid=(1,),
        in_specs=[
            _whole((BLOCK, NAT_COLS)),
            _whole((BLOCK, GATE_LANES)),
            _whole((1, GATE_LANES)),
            _whole((CONV_WIDTH * BLOCK, 2 * BLOCK)),
            _whole((CONV_WIDTH, 2 * ML_QK_WIDTH)),
            _whole((1, 2 * ML_QK_WIDTH)),
            _whole((1, ML_WIDTH)),
        ],
        out_specs=[_whole(s.shape) for s in shapes],
        out_shape=shapes,
        scratch_shapes=[pltpu.VMEM((BLOCK, ML_WIDTH), BF16)],
        compiler_params=_params(1),
        name="lead_state",
    )(nat_lead, gates_lead, gate_bias, shift, conv_w, conv_b, mh_norm)


FUSED_TM = 512


def _inproj_mixer_kernel(*refs):
    x_ref, g_ref, wn_ref, w3_ref = refs[:4]
    mix_in = refs[4:4 + N_MIX_IN]
    state_in = refs[4 + N_MIX_IN:4 + N_MIX_IN + N_STATE]
    wu_ref = refs[4 + N_MIX_IN + N_STATE]
    (nat_ref, kn_ref, t_ref, gt_ref, att_ref, hm_ref,
     wu_bf_ref) = refs[5 + N_MIX_IN + N_STATE:-(1 + N_STATE)]
    u_ref = refs[-(1 + N_STATE)]
    state = refs[-N_STATE:]
    i = pl.program_id(0)
    j = pl.program_id(1)
    step = i * (IN_NAT_STEPS + 1) + j
    wu_bf_ref[...] = wu_ref[...].astype(BF16)

    @pl.when(j == 0)
    def _():
        u_ref[...] = _rms(x_ref[...], g_ref[...]).astype(BF16)

    @pl.when(step == 0)
    def _():
        _load_state(state, state_in)

    @pl.when(j < IN_NAT_STEPS)
    def _():
        _drain(_interleave([_inproj_nat_phases(FUSED_TM, u_ref, wn_ref, nat_ref),
                            _mixer_phases(step + 1, mix_in, att_ref, hm_ref, state)],
                           [1, MIXER_PHASES_PER_NAT_PHASE]))

    @pl.when(j == IN_NAT_STEPS)
    def _():
        _drain(_interleave([_inproj_last_phases(FUSED_TM, u_ref, w3_ref, kn_ref, t_ref, gt_ref),
                            _mixer_phases(step + 1, mix_in, att_ref, hm_ref, state)],
                           [1, MIXER_PHASES_PER_LAST_PHASE]))


def _inproj_mixer(x2, row_tile0, g, w_nat, w3, mixer_args, w_up, nblk):
    n_tiles = nblk * BLOCK // FUSED_TM
    assert n_tiles * (IN_NAT_STEPS + 1) == nblk
    step = lambda i, j: i * (IN_NAT_STEPS + 1) + j
    proj_specs, proj_shapes = _inproj_out(FUSED_TM, n_tiles * FUSED_TM)
    mix_specs, mix_shapes = _mixer_out(step, nblk * BLOCK)
    wu_spec = pl.BlockSpec((D_MODEL, D_FF // nblk), lambda i, j: (0, step(i, j)))
    return pl.pallas_call(
        _inproj_mixer_kernel,
        grid=(n_tiles, IN_NAT_STEPS + 1),
        in_specs=[
            pl.BlockSpec((FUSED_TM, D_MODEL), lambda i, j: (row_tile0 + i, 0)),
            pl.BlockSpec((1, D_MODEL), lambda i, j: (0, 0)),
        ] + _inproj_weight_specs() + _mixer_in_specs(step) + [wu_spec],
        out_specs=proj_specs + mix_specs + [wu_spec],
        out_shape=proj_shapes + mix_shapes + [jax.ShapeDtypeStruct(w_up.shape, BF16)],
        scratch_shapes=[pltpu.VMEM((FUSED_TM, D_MODEL), BF16)] + _state_scratch(),
        compiler_params=_params(2),
        name="inproj_mixer",
    )(x2, g, w_nat, w3, *mixer_args, w_up)


def _mixer_kernel(*refs):
    mix_in = refs[:N_MIX_IN]
    state_in = refs[N_MIX_IN:N_MIX_IN + N_STATE]
    wo_ref, wd_ref = refs[N_MIX_IN + N_STATE:N_MIX_IN + N_STATE + 2]
    att_ref, hm_ref, wo_bf_ref, wd_bf_ref = refs[N_MIX_IN + N_STATE + 2:-N_STATE]
    state = refs[-N_STATE:]
    step = pl.program_id(0)

    @pl.when(step == 0)
    def _():
        _load_state(state, state_in)

    wo_bf_ref[...] = wo_ref[...].astype(BF16)
    wd_bf_ref[...] = wd_ref[...].astype(BF16)
    _drain(_mixer_phases(step + 1, mix_in, att_ref, hm_ref, state))


def _mixer(mixer_args, w_out, w_down, nblk):
    step = lambda s: s
    mix_specs, mix_shapes = _mixer_out(step, nblk * BLOCK)
    w_specs = [
        pl.BlockSpec((w_out.shape[0] // nblk, D_MODEL), lambda s: (s, 0)),
        pl.BlockSpec((D_FF // nblk, D_MODEL), lambda s: (s, 0)),
    ]
    return pl.pallas_call(
        _mixer_kernel,
        grid=(nblk,),
        in_specs=_mixer_in_specs(step) + w_specs,
        out_specs=mix_specs + w_specs,
        out_shape=mix_shapes + [jax.ShapeDtypeStruct(w.shape, BF16) for w in (w_out, w_down)],
        scratch_shapes=_state_scratch(),
        compiler_params=_params(1),
        name="mixer",
    )(*mixer_args, w_out, w_down)


OUT_TM = 512


def _outproj_kernel(att0_ref, att1_ref, hm0_ref, hm1_ref, x_ref, wa_ref, wm_ref, g_ref,
                    h_ref, u_ref):
    i = pl.program_id(0)
    half = pl.num_programs(0) // 2

    def body(att_ref, hm_ref):
        h = (x_ref[...]
             + lax.dot_general(att_ref[...], wa_ref[...], CONTRACT_FIRST, preferred_element_type=F32)
             + jnp.dot(hm_ref[...], wm_ref[...], preferred_element_type=F32))
        h_ref[...] = h
        u_ref[...] = _rms(h, g_ref[...]).astype(BF16)

    @pl.when(i < half)
    def _():
        body(att0_ref, hm0_ref)

    @pl.when(i >= half)
    def _():
        body(att1_ref, hm1_ref)


def _outproj(att0, att1, hm0, hm1, x2, w_out, g):
    rows = x2.shape[0]
    n = rows // OUT_TM
    half = n // 2
    first = lambda i: jnp.minimum(i, half - 1)
    second = lambda i: jnp.maximum(i - half, 0)
    return pl.pallas_call(
        _outproj_kernel,
        grid=(n,),
        in_specs=[
            pl.BlockSpec((ATT_WIDTH, OUT_TM), lambda i: (0, first(i))),
            pl.BlockSpec((ATT_WIDTH, OUT_TM), lambda i: (0, second(i))),
            pl.BlockSpec((OUT_TM, ML_WIDTH), lambda i: (first(i), 0)),
            pl.BlockSpec((OUT_TM, ML_WIDTH), lambda i: (second(i), 0)),
            pl.BlockSpec((OUT_TM, D_MODEL), lambda i: (i, 0)),
            pl.BlockSpec((ATT_WIDTH, D_MODEL), lambda i: (0, 0)),
            pl.BlockSpec((ML_WIDTH, D_MODEL), lambda i: (ATT_WIDTH // ML_WIDTH, 0)),
            pl.BlockSpec((1, D_MODEL), lambda i: (0, 0)),
        ],
        out_specs=[
            pl.BlockSpec((OUT_TM, D_MODEL), lambda i: (i, 0)),
            pl.BlockSpec((OUT_TM, D_MODEL), lambda i: (i, 0)),
        ],
        out_shape=[
            jax.ShapeDtypeStruct((rows, D_MODEL), F32),
            jax.ShapeDtypeStruct((rows, D_MODEL), BF16),
        ],
        compiler_params=_params(1),
        name="outproj",
    )(att0, att1, hm0, hm1, x2, w_out, w_out, g)


MLP_TM = 1024
MLP_TF = 1024


def _mlp_step(mode, u_ref, wu_ref, wd_ref, o_ref, hbuf_ref=None, ssq_ref=None):
    acts = []
    for c in range(MLP_TF // MXU_COLS):
        cs = slice(c * MXU_COLS, (c + 1) * MXU_COLS)
        a = jnp.dot(u_ref[...], wu_ref[:, cs], preferred_element_type=F32)
        acts.append(jnp.square(jnp.maximum(a, 0.0)).astype(BF16))
        yield
    act = jnp.concatenate(acts, axis=1)
    for c in range(D_MODEL // MXU_COLS):
        cs = slice(c * MXU_COLS, (c + 1) * MXU_COLS)
        d = jnp.dot(act, wd_ref[:, cs], preferred_element_type=F32)
        if mode == "first":
            o_ref[:, cs] = d
        elif mode == "mid":
            o_ref[:, cs] += d
        else:
            v = o_ref[:, cs] + d + hbuf_ref[:, cs]
            o_ref[:, cs] = v
            sq = v * v
            part = sum(sq[:, k * LANE:(k + 1) * LANE] for k in range(MXU_COLS // LANE))
            if c == 0:
                ssq_ref[...] = part
            else:
                ssq_ref[...] += part
        yield


def _mlp_kernel(u_ref, h_hbm, wu_ref, wd_ref, g_ref, o_ref, hbuf_ref, ssq_ref, hsem):
    i = pl.program_id(0)
    j = pl.program_id(1)
    last = pl.num_programs(1) - 1
    r0 = pl.multiple_of(i * MLP_TM, MLP_TM)
    h_copy = pltpu.make_async_copy(h_hbm.at[pl.ds(r0, MLP_TM), :], hbuf_ref, hsem)

    @pl.when(j == 0)
    def _():
        h_copy.start()
        _drain(_mlp_step("first", u_ref, wu_ref, wd_ref, o_ref))

    @pl.when((j > 0) & (j < last))
    def _():
        _drain(_mlp_step("mid", u_ref, wu_ref, wd_ref, o_ref))

    @pl.when(j == last)
    def _():
        h_copy.wait()
        _drain(_mlp_step("last", u_ref, wu_ref, wd_ref, o_ref, hbuf_ref, ssq_ref))
        ms = jnp.sum(ssq_ref[...], axis=-1, keepdims=True) / D_MODEL
        o_ref[...] = o_ref[...] * lax.rsqrt(ms + EPS) * g_ref[...]


def _mlp(u, h, w_up, w_down, g):
    rows = u.shape[0]
    return pl.pallas_call(
        _mlp_kernel,
        grid=(rows // MLP_TM, D_FF // MLP_TF),
        in_specs=[
            pl.BlockSpec((MLP_TM, D_MODEL), lambda i, j: (i, 0)),
            pl.BlockSpec(memory_space=pl.ANY),
            pl.BlockSpec((D_MODEL, MLP_TF), lambda i, j: (0, j)),
            pl.BlockSpec((MLP_TF, D_MODEL), lambda i, j: (j, 0)),
            pl.BlockSpec((1, D_MODEL), lambda i, j: (0, 0)),
        ],
        out_specs=pl.BlockSpec((MLP_TM, D_MODEL), lambda i, j: (i, 0)),
        out_shape=jax.ShapeDtypeStruct((rows, D_MODEL), F32),
        scratch_shapes=[pltpu.VMEM((MLP_TM, D_MODEL), F32), pltpu.VMEM((MLP_TM, LANE), F32),
                        pltpu.SemaphoreType.DMA(())],
        compiler_params=_params(2),
        name="mlp",
    )(u, h, w_up, w_down, g)


def _t5_bucket(dist):
    max_exact = N_BUCKETS // 2
    d = jnp.maximum(dist, 0)
    ratio = jnp.maximum(d, max_exact).astype(F32) / max_exact
    large = max_exact + (jnp.log(ratio) / math.log(MAX_DISTANCE / max_exact)
                         * (N_BUCKETS - max_exact)).astype(jnp.int32)
    large = jnp.minimum(large, N_BUCKETS - 1)
    return jnp.where(d < max_exact, d, large)


BIAS_ROWS = 16


BIAS_BUCKETS_PER_PHASE = 8


def _bias_fill_splats(rb_ref, splat_ref):
    def fill(b, carry):
        for hd in range(ATT_HEADS):
            splat_ref[b * ATT_HEADS + hd] = jnp.full((SUBLANES, BLOCK), rb_ref[b, hd], F32) * LOG2E
        return carry

    lax.fori_loop(0, N_BUCKETS, fill, 0)


def _bias_lookup_phases(bkt, splat_ref, write):
    accs = [jnp.zeros(bkt.shape, F32)] * ATT_HEADS
    for b in range(N_BUCKETS):
        hit = (bkt >= b) & (bkt < b + 1)
        accs = [jnp.where(hit, jnp.concatenate([splat_ref[b * ATT_HEADS + hd]]
                                               * (BIAS_ROWS // SUBLANES), axis=0), a)
                for hd, a in enumerate(accs)]
        if b % BIAS_BUCKETS_PER_PHASE == BIAS_BUCKETS_PER_PHASE - 1:
            yield
    write(accs)


def _bias_step_phases(bkt_band_ref, bkt_meta_ref, band_ref, meta_ref, splat_ref):
    def write_band(tables):
        for hd, t in enumerate(tables):
            h, g = divmod(hd, ATT_GROUP)
            band_ref[h, :, g * BLOCK:(g + 1) * BLOCK] = t

    yield from _bias_lookup_phases(bkt_band_ref[...], splat_ref, write_band)
    for k in range(bkt_meta_ref.shape[0]):
        def write_meta(tables, k=k):
            for hd, t in enumerate(tables):
                h, g = divmod(hd, ATT_GROUP)
                meta_ref[k, h, :, g * BLOCK:(g + 1) * BLOCK] = t

        yield from _bias_lookup_phases(bkt_meta_ref[k], splat_ref, write_meta)


def _bias_buckets(nblk):
    c = jnp.arange(2 * BLOCK)[:, None]
    r = jnp.arange(BLOCK)[None, :]
    bkt_band = _t5_bucket(r + BLOCK - c).astype(jnp.int32)
    q_pos = BLOCK + jnp.arange(nblk * BLOCK)[None, :]
    m_pos = N_PAD + jnp.arange(N_META)[:, None]
    bkt_meta = _t5_bucket(q_pos - m_pos).astype(jnp.int32)
    return bkt_band, bkt_meta.reshape(N_META, nblk, BLOCK).transpose(1, 0, 2)


W_IN_OFFS = dict(q=0, k=1024, v=1280, mq=1536, mk=2048, mv=2560, mo=3584, gates=4608)
PREP_ROWS = 256
NAT_SRC = [W_IN_OFFS["mq"] // PREP_ROWS + t for t in range(NAT_COLS // PREP_ROWS)]
W3_SRC = ([W_IN_OFFS["q"] // PREP_ROWS + t for t in range(ATT_WIDTH // PREP_ROWS)]
          + [W_IN_OFFS["v"] // PREP_ROWS + t for t in range(ATT_KV_WIDTH // PREP_ROWS)]
          + [W_IN_OFFS["k"] // PREP_ROWS + t for t in range(ATT_KV_WIDTH // PREP_ROWS)]
          + [W_IN_OFFS["gates"] // PREP_ROWS])
N_NAT_BLK = len(NAT_SRC)
N_Q_BLK = ATT_WIDTH // PREP_ROWS
N_PREP = N_NAT_BLK + len(W3_SRC)
assert W_IN_OFFS["mq"] + NAT_COLS == W_IN_OFFS["gates"] and len(W3_SRC) * PREP_ROWS == W3_ROWS


def _win_prep_kernel(src_ref, w_ref, nat_ref, w3_ref):
    j = pl.program_id(0)
    wv = w_ref[0]

    @pl.when(j < N_NAT_BLK)
    def _():
        nat_ref[...] = wv.astype(BF16)

    @pl.when((j >= N_NAT_BLK) & (j < N_NAT_BLK + N_Q_BLK))
    def _():
        w3_ref[...] = (wv * (ATT_HEAD_DIM ** -0.5 * LOG2E)).astype(BF16)

    @pl.when((j >= N_NAT_BLK + N_Q_BLK) & (j < N_PREP - 1))
    def _():
        w3_ref[...] = wv.astype(BF16)

    @pl.when(j == N_PREP - 1)
    def _():
        row = lax.broadcasted_iota(jnp.int32, wv.shape, 0)
        w3_ref[...] = jnp.where(row < 2 * ML_HEADS, wv, 0.0).astype(BF16)


def _win_prep(w_in_t):
    src = jnp.asarray(NAT_SRC + W3_SRC, jnp.int32)
    d = w_in_t.shape[2]
    return pl.pallas_call(
        _win_prep_kernel,
        grid_spec=pltpu.PrefetchScalarGridSpec(
            num_scalar_prefetch=1,
            grid=(N_PREP,),
            in_specs=[pl.BlockSpec((1, PREP_ROWS, d), lambda j, src: (0, src[j], 0))],
            out_specs=[
                pl.BlockSpec((PREP_ROWS, d), lambda j, src: (jnp.minimum(j, N_NAT_BLK - 1), 0)),
                pl.BlockSpec((PREP_ROWS, d), lambda j, src: (jnp.maximum(j - N_NAT_BLK, 0), 0)),
            ],
        ),
        out_shape=[
            jax.ShapeDtypeStruct((NAT_COLS, d), BF16),
            jax.ShapeDtypeStruct((W3_ROWS, d), BF16),
        ],
        compiler_params=_params(1),
        name="w_in_prep",
    )(src, w_in_t)


def kernel(x, meta_tokens, w_in, conv_w, conv_b, b_igate, b_fgate, attn_sinks, rel_bias,
           mh_norm, w_out, norm_mix, norm_mlp, w_up, w_down, norm_final):
    batch, seq, d = x.shape
    assert batch == 2, "the two batches are processed one projection / mixer stage apart"
    nblk = seq // BLOCK
    x2 = x.reshape(batch * seq, d)
    lead = jnp.concatenate([jnp.zeros((N_PAD, d), x.dtype), meta_tokens.astype(x.dtype)], axis=0)
    g_mix = norm_mix[0][None].astype(F32)

    w_nat, w3 = _win_prep(jnp.swapaxes(w_in, 1, 2))
    gate_bias = jnp.pad(jnp.concatenate([b_igate[0], b_fgate[0]]).astype(F32),
                        (0, GATE_LANES - 2 * ML_HEADS))[None]

    proj0, proj_lead, (bias_band_t, bias_meta_t) = _inproj(
        x2, lead, g_mix, w_nat, w3, rel_bias, seq // IN_TM, nblk)
    consts = (attn_sinks[0].astype(F32), bias_band_t, bias_meta_t, gate_bias, _conv_shift_matrix(),
              conv_w[0].astype(F32), conv_b[0][None].astype(F32), mh_norm[0][None].astype(F32))
    state = _lead_state(proj_lead, consts)
    outs = _inproj_mixer(x2, seq // FUSED_TM, g_mix, w_nat, w3,
                         _mixer_args(consts, proj0, proj_lead, state), w_up[0], nblk)
    proj1, (att0, hm0, w_up_bf) = outs[:N_PROJ_OUT], outs[N_PROJ_OUT:]
    att1, hm1, w_out_bf, w_down_bf = _mixer(_mixer_args(consts, proj1, proj_lead, state),
                                            w_out[0], w_down[0], nblk)

    h2, u2 = _outproj(att0, att1, hm0, hm1, x2, w_out_bf, norm_mlp[0][None].astype(F32))
    out = _mlp(u2, h2, w_up_bf, w_down_bf, norm_final[None].astype(F32))
    return out.reshape(batch, seq, d)
```

```python
import math

import numpy as np
import jax
import jax.numpy as jnp
from jax import lax
from jax.experimental import pallas as pl
from jax.experimental.pallas import tpu as pltpu

F32 = jnp.float32
BF16 = jnp.bfloat16

D_MODEL = 2048
N_META = 16
BLOCK = 128
N_PAD = BLOCK - N_META
WINDOW = 128
ATT_HEADS = 16
ATT_KV_HEADS = 4
ATT_GROUP = ATT_HEADS // ATT_KV_HEADS
ATT_HEAD_DIM = 64
ATT_WIDTH = ATT_HEADS * ATT_HEAD_DIM
ATT_KV_WIDTH = ATT_KV_HEADS * ATT_HEAD_DIM
ML_HEADS = 4
ML_V_DIM = 256
ML_QK_DIM = 128
ML_WIDTH = ML_HEADS * ML_V_DIM
ML_QK_WIDTH = ML_HEADS * ML_QK_DIM
CONV_WIDTH = 4
GATE_SOFTCAP = 15.0
D_FF = 4 * D_MODEL
N_BUCKETS = 32
MAX_DISTANCE = 128
EPS = 1e-6
NEG = -1e30
LOG2E = math.log2(math.e)

NAT_MQK = 0
NAT_MV = 2 * ML_QK_WIDTH
NAT_MO = NAT_MV + ML_WIDTH
NAT_COLS = NAT_MO + ML_WIDTH
T_ROWS = ATT_WIDTH + ATT_KV_WIDTH
GATE_LANES = 128
W3_K = T_ROWS
W3_G = W3_K + ATT_KV_WIDTH
W3_ROWS = W3_G + 256

V7X_VMEM_LIMIT = 60 * 1024 * 1024
LANE = 128
SUBLANES = 8
CONTRACT_LAST = (((1,), (1,)), ((), ()))
CONTRACT_FIRST = (((0,), (0,)), ((), ()))


def _params(n_axes, vmem=V7X_VMEM_LIMIT):
    return pltpu.CompilerParams(dimension_semantics=("arbitrary",) * n_axes,
                                vmem_limit_bytes=vmem)


def _rms(x, g):
    return x * lax.rsqrt(jnp.mean(x * x, axis=-1, keepdims=True) + EPS) * g


_DONE = object()


def _drain(phases):
    for _ in phases:
        pass


def _interleave(streams, weights):
    live = list(zip(streams, weights))
    while live:
        for item in list(live):
            stream, weight = item
            for _ in range(weight):
                if next(stream, _DONE) is _DONE:
                    live.remove(item)
                    break
        yield


IN_TM = 1024
IN_NAT_STEPS = 3
IN_TN = NAT_COLS // IN_NAT_STEPS
MXU_COLS = 256
N_PROJ_OUT = 4


def _inproj_nat_phases(rows, u_ref, wn_ref, nat_ref):
    u = u_ref[0:rows, :]
    for c0 in range(0, IN_TN, MXU_COLS):
        cs = slice(c0, c0 + MXU_COLS)
        nat_ref[0:rows, cs] = lax.dot_general(u, wn_ref[cs, :], CONTRACT_LAST,
                                              preferred_element_type=F32).astype(BF16)
        yield


def _inproj_last_phases(rows, u_ref, w3_ref, kn_ref, t_ref, gt_ref):
    u = u_ref[0:rows, :]
    for c0 in range(0, T_ROWS, MXU_COLS):
        cs = slice(c0, c0 + MXU_COLS)
        t_ref[cs, 0:rows] = lax.dot_general(w3_ref[cs, :], u, CONTRACT_LAST,
                                            preferred_element_type=F32).astype(BF16)
        yield
    kn_ref[0:rows, :] = lax.dot_general(u, w3_ref[W3_K:W3_G, :], CONTRACT_LAST,
                                        preferred_element_type=F32).astype(BF16)
    yield
    gt_ref[0:rows, :] = lax.dot_general(u, w3_ref[W3_G:W3_G + GATE_LANES, :], CONTRACT_LAST,
                                        preferred_element_type=F32)
    yield


BIAS_PHASES_PER_ROUND = 3


def _inproj_kernel(x_ref, lead_ref, g_ref, wn_ref, w3_ref, rb_ref, bkt_band_ref, bkt_meta_ref,
                   nat_ref, kn_ref, t_ref, gt_ref, nat_lead_ref, kn_lead_ref, t_lead_ref, gt_lead_ref,
                   band_ref, meta_ref, u_ref, u_lead_ref, splat_ref):
    i = pl.program_id(0)
    j = pl.program_id(1)

    @pl.when(j == 0)
    def _():
        u_ref[...] = _rms(x_ref[...], g_ref[...]).astype(BF16)

    @pl.when((j == 0) & (i == 0))
    def _():
        u_lead_ref[...] = _rms(lead_ref[...], g_ref[...]).astype(BF16)
        _bias_fill_splats(rb_ref, splat_ref)

    def with_bias(phases):
        bias = _bias_step_phases(bkt_band_ref, bkt_meta_ref, band_ref, meta_ref, splat_ref)
        _drain(_interleave([phases, bias], [1, BIAS_PHASES_PER_ROUND]))

    @pl.when(j < IN_NAT_STEPS)
    def _():
        with_bias(_inproj_nat_phases(IN_TM, u_ref, wn_ref, nat_ref))

    @pl.when(j == IN_NAT_STEPS)
    def _():
        with_bias(_inproj_last_phases(IN_TM, u_ref, w3_ref, kn_ref, t_ref, gt_ref))

    @pl.when((i == 0) & (j < IN_NAT_STEPS))
    def _():
        _drain(_inproj_nat_phases(BLOCK, u_lead_ref, wn_ref, nat_lead_ref))

    @pl.when((i == 0) & (j == IN_NAT_STEPS))
    def _():
        _drain(_inproj_last_phases(BLOCK, u_lead_ref, w3_ref, kn_lead_ref, t_lead_ref, gt_lead_ref))


def _inproj_weight_specs():
    last_nat = IN_NAT_STEPS - 1
    return [
        pl.BlockSpec((IN_TN, D_MODEL), lambda i, j: (jnp.minimum(j, last_nat), 0)),
        pl.BlockSpec((W3_ROWS, D_MODEL), lambda i, j: (0, 0)),
    ]


def _inproj_out(tm, rows):
    last_nat = IN_NAT_STEPS - 1
    specs = [
        pl.BlockSpec((tm, IN_TN), lambda i, j: (i, jnp.minimum(j, last_nat))),
        pl.BlockSpec((tm, ATT_KV_WIDTH), lambda i, j: (i, 0)),
        pl.BlockSpec((T_ROWS, tm), lambda i, j: (0, i)),
        pl.BlockSpec((tm, GATE_LANES), lambda i, j: (i, 0)),
    ]
    shapes = [
        jax.ShapeDtypeStruct((rows, NAT_COLS), BF16),
        jax.ShapeDtypeStruct((rows, ATT_KV_WIDTH), BF16),
        jax.ShapeDtypeStruct((T_ROWS, rows), BF16),
        jax.ShapeDtypeStruct((rows, GATE_LANES), F32),
    ]
    return specs, shapes


def _inproj(x2, lead, g, w_nat, w3, rel_bias, n_tiles, nblk):
    out_specs, out_shape = _inproj_out(IN_TM, n_tiles * IN_TM)
    last_nat = IN_NAT_STEPS - 1
    n_steps = n_tiles * (IN_NAT_STEPS + 1)
    assert n_steps * BIAS_ROWS == 2 * BLOCK and nblk % n_steps == 0
    meta_per_step = nblk // n_steps
    bkt_band, bkt_meta = _bias_buckets(nblk)
    step = lambda i, j: i * (IN_NAT_STEPS + 1) + j
    bias_in_specs = [
        pl.BlockSpec(memory_space=pltpu.SMEM),
        pl.BlockSpec((BIAS_ROWS, BLOCK), lambda i, j: (step(i, j), 0)),
        pl.BlockSpec((meta_per_step, N_META, BLOCK), lambda i, j: (step(i, j), 0, 0)),
    ]
    bias_out_specs = [
        pl.BlockSpec((ATT_KV_HEADS, BIAS_ROWS, GQ), lambda i, j: (0, step(i, j), 0)),
        pl.BlockSpec((meta_per_step, ATT_KV_HEADS, N_META, GQ), lambda i, j: (step(i, j), 0, 0, 0)),
    ]
    bias_shapes = [
        jax.ShapeDtypeStruct((ATT_KV_HEADS, 2 * BLOCK, GQ), F32),
        jax.ShapeDtypeStruct((nblk, ATT_KV_HEADS, N_META, GQ), F32),
    ]
    lead_col = lambda i, j: jnp.where(i == 0, jnp.minimum(j, last_nat), last_nat)
    lead_specs = [
        pl.BlockSpec((BLOCK, IN_TN), lambda i, j: (0, lead_col(i, j))),
        pl.BlockSpec((BLOCK, ATT_KV_WIDTH), lambda i, j: (0, 0)),
        pl.BlockSpec((T_ROWS, BLOCK), lambda i, j: (0, 0)),
        pl.BlockSpec((BLOCK, GATE_LANES), lambda i, j: (0, 0)),
    ]
    lead_shapes = [
        jax.ShapeDtypeStruct((BLOCK, NAT_COLS), BF16),
        jax.ShapeDtypeStruct((BLOCK, ATT_KV_WIDTH), BF16),
        jax.ShapeDtypeStruct((T_ROWS, BLOCK), BF16),
        jax.ShapeDtypeStruct((BLOCK, GATE_LANES), F32),
    ]
    outs = pl.pallas_call(
        _inproj_kernel,
        grid=(n_tiles, IN_NAT_STEPS + 1),
        in_specs=[
            pl.BlockSpec((IN_TM, D_MODEL), lambda i, j: (i, 0)),
            pl.BlockSpec((BLOCK, D_MODEL), lambda i, j: (0, 0)),
            pl.BlockSpec((1, D_MODEL), lambda i, j: (0, 0)),
        ] + _inproj_weight_specs() + bias_in_specs,
        out_specs=out_specs + lead_specs + bias_out_specs,
        out_shape=out_shape + lead_shapes + bias_shapes,
        scratch_shapes=[pltpu.VMEM((IN_TM, D_MODEL), BF16), pltpu.VMEM((BLOCK, D_MODEL), BF16),
                        pltpu.VMEM((N_BUCKETS * ATT_HEADS, SUBLANES, BLOCK), F32)],
        compiler_params=_params(2),
        name="inproj",
    )(x2, lead, g, w_nat, w3, rel_bias.astype(F32), bkt_band, bkt_meta)
    return outs[:N_PROJ_OUT], outs[N_PROJ_OUT:2 * N_PROJ_OUT], outs[2 * N_PROJ_OUT:]


GQ = ATT_GROUP * BLOCK
ATT_SLABS_PER_PHASE = 2


def _attn_prologue(n, kc_ref, kp_ref, km_ref):
    c = lax.broadcasted_iota(jnp.int32, (2 * BLOCK, BLOCK), 0)
    r = lax.broadcasted_iota(jnp.int32, (2 * BLOCK, BLOCK), 1)
    dist = r + BLOCK - c
    band_ok = (dist >= 0) & (dist < WINDOW) & ((c >= BLOCK) | (n >= 1))
    mm = lax.broadcasted_iota(jnp.int32, (N_META, BLOCK), 0)
    rr = lax.broadcasted_iota(jnp.int32, (N_META, BLOCK), 1)
    meta_ok = (n + 1) * BLOCK + rr >= N_PAD + mm
    kall = jnp.concatenate([kp_ref[...], kc_ref[...], km_ref[...]], axis=0)
    return band_ok, meta_ok, kall


BF16_ROWS = 16


def _ones_row(width):
    row = lax.broadcasted_iota(jnp.int32, (BF16_ROWS, width), 0)
    return jnp.where(row == 0, 1.0, 0.0).astype(BF16)


def _attn_head(h, pro, sink_ref, qt_ref, vc_ref, vp_ref, vl_ref, bb_ref, bm_ref, o_ref):
    band_ok, meta_ok, kall = pro
    lo = (h // 2) * LANE
    kpair = kall[:, lo:lo + LANE]
    qh = jnp.concatenate(
        [qt_ref[(ATT_GROUP * h + g) * ATT_HEAD_DIM:(ATT_GROUP * h + g + 1) * ATT_HEAD_DIM, :]
         for g in range(ATT_GROUP)], axis=1)
    zq = jnp.zeros_like(qh)
    qz = jnp.concatenate([qh, zq] if h % 2 == 0 else [zq, qh], axis=0)
    st = jnp.dot(kpair, qz, preferred_element_type=F32)
    yield
    zpad = jnp.zeros((N_PAD, BLOCK), BF16)
    p_band, p_lead, sink_p = [], [], []
    for g in range(ATT_GROUP):
        gs = slice(g * BLOCK, (g + 1) * BLOCK)
        sl = st[:, gs]
        sink = jnp.full((1, BLOCK), sink_ref[ATT_GROUP * h + g], F32) * LOG2E
        lb = jnp.where(band_ok, sl[:2 * BLOCK] + bb_ref[h, :, gs], NEG * LOG2E)
        lm = jnp.where(meta_ok, sl[2 * BLOCK:] + bm_ref[0, h, :, gs], NEG * LOG2E)
        m = jnp.maximum(jnp.max(lb, axis=0, keepdims=True), jnp.max(lm, axis=0, keepdims=True))
        m = jnp.maximum(m, sink)
        pb = jnp.exp2(lb - m)
        pm = jnp.exp2(lm - m)
        sink_p.append(jnp.exp2(sink - m))
        p_band.append(pb.astype(BF16))
        p_lead.append(jnp.concatenate([zpad, pm.astype(BF16)], axis=0))
        if g % ATT_SLABS_PER_PHASE == ATT_SLABS_PER_PHASE - 1:
            yield
    pbt = jnp.concatenate(p_band, axis=1)
    plt = jnp.concatenate(p_lead, axis=1)
    hs = slice(h * ATT_HEAD_DIM, (h + 1) * ATT_HEAD_DIM)
    vband = jnp.concatenate([vp_ref[hs, :], vc_ref[hs, :]], axis=1)
    ones_b = _ones_row(2 * BLOCK)
    ones_l = _ones_row(BLOCK)
    ot = (jnp.dot(jnp.concatenate([vband, ones_b], axis=0), pbt, preferred_element_type=F32)
          + jnp.dot(jnp.concatenate([vl_ref[hs, :], ones_l], axis=0), plt,
                    preferred_element_type=F32))
    for g in range(ATT_GROUP):
        gs = slice(g * BLOCK, (g + 1) * BLOCK)
        den = ot[ATT_HEAD_DIM:ATT_HEAD_DIM + 1, gs] + sink_p[g]
        row = (ATT_GROUP * h + g) * ATT_HEAD_DIM
        o_ref[row:row + ATT_HEAD_DIM, :] = (ot[:ATT_HEAD_DIM, gs] * (1.0 / den)).astype(BF16)


def _conv_shift_matrix():
    s = np.zeros((CONV_WIDTH * BLOCK, 2 * BLOCK), np.float32)
    for k in range(CONV_WIDTH):
        t = np.arange(BLOCK)
        s[k * BLOCK + t, BLOCK + t - k] = 1.0
    return jnp.asarray(s, BF16)


def _mlstm_gates(n, gate_ref, gb_ref):
    t_idx = lax.broadcasted_iota(jnp.int32, (BLOCK, GATE_LANES), 0)
    valid = (n > 0) | (t_idx >= N_PAD)
    pre = GATE_SOFTCAP * jnp.tanh((gate_ref[...] + gb_ref[...]) / GATE_SOFTCAP)
    log_i = jnp.where(valid, pre, NEG)
    log_sig = jnp.minimum(pre, 0.0) - jnp.log1p(jnp.exp(-jnp.abs(pre)))
    log_f = jnp.where(valid, log_sig, 0.0)
    row = lax.broadcasted_iota(jnp.int32, (BLOCK, BLOCK), 0)
    col = lax.broadcasted_iota(jnp.int32, (BLOCK, BLOCK), 1)
    causal = col <= row
    tril = jnp.where(causal, 1.0, 0.0).astype(F32)
    b_all = jnp.dot(tril, log_f, preferred_element_type=F32,
                    precision=lax.Precision.HIGHEST)
    return log_i, b_all, log_i.T, b_all.T, causal


def _mlstm_conv(p_ref, shift_ref, cw_ref, cb_ref, xprev_ref):
    xcur = p_ref[:, NAT_MQK:NAT_MV]
    xcat = jnp.concatenate([xprev_ref[...], xcur], axis=0)
    sh = jnp.dot(shift_ref[...], xcat, preferred_element_type=F32)
    xprev_ref[...] = xcur
    y = cb_ref[...]
    for k in range(CONV_WIDTH):
        y = y + cw_ref[CONV_WIDTH - 1 - k:CONV_WIDTH - k, :] * sh[k * BLOCK:(k + 1) * BLOCK]
    act = y * jax.nn.sigmoid(y)
    return act[:, :ML_QK_WIDTH] * (ML_QK_DIM ** -0.5), act[:, ML_QK_WIDTH:]


def _mlstm_head(h, gates, q_all, k_all, p_ref, mhn_ref, o_ref, c_ref, n_ref, m_ref):
    log_i, b_all, log_i_t, b_all_t, causal = gates
    q_f = q_all[:, h * ML_QK_DIM:(h + 1) * ML_QK_DIM]
    k_f = k_all[:, h * ML_QK_DIM:(h + 1) * ML_QK_DIM]
    q_h = q_f.astype(BF16)
    k_h = k_f.astype(BF16)
    v_h = p_ref[:, NAT_MV + h * ML_V_DIM:NAT_MV + (h + 1) * ML_V_DIM]
    b_col = b_all[:, ML_HEADS + h:ML_HEADS + h + 1]
    li_col = log_i[:, h:h + 1]
    b_row = b_all_t[ML_HEADS + h:ML_HEADS + h + 1, :]
    li_row = log_i_t[h:h + 1, :]
    b_last = b_all[BLOCK - 1:BLOCK, ML_HEADS + h:ML_HEADS + h + 1]
    c_prev = c_ref[h]
    n_prev = n_ref[h][0:1, :]
    m_prev = m_ref[h][0:1, 0:1]

    dmat = jnp.where(causal, b_col - b_row + li_row, -jnp.inf)
    inter = b_col + m_prev
    m_t = jnp.maximum(inter, jnp.max(dmat, axis=1, keepdims=True))
    s = lax.dot_general(q_h, k_h, CONTRACT_LAST, preferred_element_type=F32) * jnp.exp(dmat - m_t)
    yield
    a_t = jnp.exp(inter - m_t)
    num = (jnp.dot(s.astype(BF16), v_h, preferred_element_type=F32)
           + a_t * jnp.dot(q_h, c_prev.astype(BF16), preferred_element_type=F32))
    den = (jnp.sum(s, axis=1, keepdims=True)
           + a_t * jnp.sum(q_f * n_prev, axis=1, keepdims=True))
    den = jnp.maximum(jnp.abs(den), jnp.exp(-m_t))
    hh = num / den
    hh = hh * lax.rsqrt(jnp.mean(hh * hh, axis=-1, keepdims=True) + EPS)
    vs = slice(h * ML_V_DIM, (h + 1) * ML_V_DIM)
    hh = hh * mhn_ref[:, vs]
    m_o = p_ref[:, NAT_MO + h * ML_V_DIM:NAT_MO + (h + 1) * ML_V_DIM].astype(F32)
    o_ref[:, vs] = (hh * jax.nn.sigmoid(m_o)).astype(BF16)
    yield

    g_col = b_last - b_col + li_col
    m_loc = jnp.max(g_col, axis=0, keepdims=True)
    wk = jnp.exp(g_col - m_loc) * k_f
    c_loc = lax.dot_general(wk.astype(BF16), v_h, CONTRACT_FIRST,
                            preferred_element_type=F32)
    n_loc = jnp.sum(wk, axis=0, keepdims=True)
    m_new = jnp.maximum(b_last + m_prev, m_loc)
    a = jnp.exp(b_last + m_prev - m_new)
    cc = jnp.exp(m_loc - m_new)
    c_ref[h] = a * c_prev + cc * c_loc
    n_ref[h] = jnp.broadcast_to(a * n_prev + cc * n_loc, (8, ML_QK_DIM))
    m_ref[h] = jnp.broadcast_to(m_new, (8, LANE))


N_MIX_IN = 17
N_STATE = 4
MIXER_PHASES_PER_NAT_PHASE = 6
MIXER_PHASES_PER_LAST_PHASE = 3


N_ATT_IN = 10


def _attention_phases(n, att_in, att_ref):
    sink_ref, qt_ref, kc_ref, kp_ref, km_ref, vc_ref, vp_ref, vl_ref, bb_ref, bm_ref = att_in
    pro = _attn_prologue(n - 1, kc_ref, kp_ref, km_ref)
    for h in range(ATT_KV_HEADS):
        yield from _attn_head(h, pro, sink_ref, qt_ref, vc_ref, vp_ref, vl_ref, bb_ref, bm_ref, att_ref)


def _mlstm_phases(n, ml_in, hm_ref, state):
    p_ref, gate_ref, gb_ref, shift_ref, cw_ref, cb_ref, mhn_ref = ml_in
    c_ref, n_ref, m_ref, xprev_ref = state
    gates = _mlstm_gates(n, gate_ref, gb_ref)
    yield
    q_all, k_all = _mlstm_conv(p_ref, shift_ref, cw_ref, cb_ref, xprev_ref)
    yield
    for h in range(ML_HEADS):
        yield from _mlstm_head(h, gates, q_all, k_all, p_ref, mhn_ref, hm_ref, c_ref, n_ref, m_ref)


def _mixer_phases(n, mix_in, att_ref, hm_ref, state):
    return _interleave([_mlstm_phases(n, mix_in[N_ATT_IN:], hm_ref, state),
                        _attention_phases(n, mix_in[:N_ATT_IN], att_ref)], [1, 1])


def _load_state(state, state_in):
    for dst, src in zip(state, state_in):
        dst[...] = src[...]


def _state_shapes():
    return [
        jax.ShapeDtypeStruct((ML_HEADS, ML_QK_DIM, ML_V_DIM), F32),
        jax.ShapeDtypeStruct((ML_HEADS, SUBLANES, ML_QK_DIM), F32),
        jax.ShapeDtypeStruct((ML_HEADS, SUBLANES, LANE), F32),
        jax.ShapeDtypeStruct((BLOCK, 2 * ML_QK_WIDTH), BF16),
    ]


def _state_scratch():
    return [pltpu.VMEM(s.shape, s.dtype) for s in _state_shapes()]


def _whole(shape):
    return pl.BlockSpec(shape, lambda *ids: (0,) * len(shape))


def _mixer_in_specs(step):
    cur = lambda *ids: step(*ids)
    prev = lambda *ids: jnp.maximum(step(*ids) - 1, 0)
    v_row = ATT_WIDTH // ATT_KV_WIDTH
    meta_rows = N_PAD // N_META
    return [
        pl.BlockSpec(memory_space=pltpu.SMEM),
        pl.BlockSpec((ATT_WIDTH, BLOCK), lambda *ids: (0, cur(*ids))),
        pl.BlockSpec((BLOCK, ATT_KV_WIDTH), lambda *ids: (cur(*ids), 0)),
        pl.BlockSpec((BLOCK, ATT_KV_WIDTH), lambda *ids: (prev(*ids), 0)),
        pl.BlockSpec((N_META, ATT_KV_WIDTH), lambda *ids: (meta_rows, 0)),
        pl.BlockSpec((ATT_KV_WIDTH, BLOCK), lambda *ids: (v_row, cur(*ids))),
        pl.BlockSpec((ATT_KV_WIDTH, BLOCK), lambda *ids: (v_row, prev(*ids))),
        pl.BlockSpec((ATT_KV_WIDTH, BLOCK), lambda *ids: (v_row, 0)),
        _whole((ATT_KV_HEADS, 2 * BLOCK, GQ)),
        pl.BlockSpec((1, ATT_KV_HEADS, N_META, GQ), lambda *ids: (cur(*ids), 0, 0, 0)),
        pl.BlockSpec((BLOCK, NAT_COLS), lambda *ids: (cur(*ids), 0)),
        pl.BlockSpec((BLOCK, GATE_LANES), lambda *ids: (cur(*ids), 0)),
        _whole((1, GATE_LANES)),
        _whole((CONV_WIDTH * BLOCK, 2 * BLOCK)),
        _whole((CONV_WIDTH, 2 * ML_QK_WIDTH)),
        _whole((1, 2 * ML_QK_WIDTH)),
        _whole((1, ML_WIDTH)),
    ] + [_whole(s.shape) for s in _state_shapes()]


def _mixer_args(consts, proj_b, proj_lead, state):
    sinks, bias_band_t, bias_meta_t, gate_bias, shift, conv_w, conv_b, mh_norm = consts
    nat_b, kn_b, projt_b, gt_b = proj_b
    _, kn_lead, projt_lead, _ = proj_lead
    return [sinks, projt_b, kn_b, kn_b, kn_lead, projt_b, projt_b, projt_lead, bias_band_t,
            bias_meta_t, nat_b, gt_b, gate_bias, shift, conv_w, conv_b, mh_norm] + list(state)


def _mixer_out(step, n_rows):
    specs = [
        pl.BlockSpec((ATT_WIDTH, BLOCK), lambda *ids: (0, step(*ids))),
        pl.BlockSpec((BLOCK, ML_WIDTH), lambda *ids: (step(*ids), 0)),
    ]
    shapes = [
        jax.ShapeDtypeStruct((ATT_WIDTH, n_rows), BF16),
        jax.ShapeDtypeStruct((n_rows, ML_WIDTH), BF16),
    ]
    return specs, shapes


def _lead_state_kernel(p_ref, gate_ref, gb_ref, shift_ref, cw_ref, cb_ref, mhn_ref,
                       c_ref, n_ref, m_ref, xprev_ref, hm_ref):
    for ref in (c_ref, n_ref, m_ref, xprev_ref):
        ref[...] = jnp.zeros_like(ref)
    gates = _mlstm_gates(0, gate_ref, gb_ref)
    q_all, k_all = _mlstm_conv(p_ref, shift_ref, cw_ref, cb_ref, xprev_ref)
    for h in range(ML_HEADS):
        _drain(_mlstm_head(h, gates, q_all, k_all, p_ref, mhn_ref, hm_ref, c_ref, n_ref, m_ref))


def _lead_state(proj_lead, consts):
    nat_lead, _, _, gates_lead = proj_lead
    _, _, _, gate_bias, shift, conv_w, conv_b, mh_norm = consts
    shapes = _state_shapes()
    return pl.pallas_call(
        _lead_state_kernel,
        grid=(1,),
        in_specs=[
            _whole((BLOCK, NAT_COLS)),
            _whole((BLOCK, GATE_LANES)),
            _whole((1, GATE_LANES)),
            _whole((CONV_WIDTH * BLOCK, 2 * BLOCK)),
            _whole((CONV_WIDTH, 2 * ML_QK_WIDTH)),
            _whole((1, 2 * ML_QK_WIDTH)),
            _whole((1, ML_WIDTH)),
        ],
        out_specs=[_whole(s.shape) for s in shapes],
        out_shape=shapes,
        scratch_shapes=[pltpu.VMEM((BLOCK, ML_WIDTH), BF16)],
        compiler_params=_params(1),
        name="lead_state",
    )(nat_lead, gates_lead, gate_bias, shift, conv_w, conv_b, mh_norm)


FUSED_TM = 512


def _inproj_mixer_kernel(*refs):
    x_ref, g_ref, wn_ref, w3_ref = refs[:4]
    mix_in = refs[4:4 + N_MIX_IN]
    state_in = refs[4 + N_MIX_IN:4 + N_MIX_IN + N_STATE]
    wu_ref = refs[4 + N_MIX_IN + N_STATE]
    (nat_ref, kn_ref, t_ref, gt_ref, att_ref, hm_ref,
     wu_bf_ref) = refs[5 + N_MIX_IN + N_STATE:-(1 + N_STATE)]
    u_ref = refs[-(1 + N_STATE)]
    state = refs[-N_STATE:]
    i = pl.program_id(0)
    j = pl.program_id(1)
    step = i * (IN_NAT_STEPS + 1) + j
    wu_bf_ref[...] = wu_ref[...].astype(BF16)

    @pl.when(j == 0)
    def _():
        u_ref[...] = _rms(x_ref[...], g_ref[...]).astype(BF16)

    @pl.when(step == 0)
    def _():
        _load_state(state, state_in)

    @pl.when(j < IN_NAT_STEPS)
    def _():
        _drain(_interleave([_inproj_nat_phases(FUSED_TM, u_ref, wn_ref, nat_ref),
                            _mixer_phases(step + 1, mix_in, att_ref, hm_ref, state)],
                           [1, MIXER_PHASES_PER_NAT_PHASE]))

    @pl.when(j == IN_NAT_STEPS)
    def _():
        _drain(_interleave([_inproj_last_phases(FUSED_TM, u_ref, w3_ref, kn_ref, t_ref, gt_ref),
                            _mixer_phases(step + 1, mix_in, att_ref, hm_ref, state)],
                           [1, MIXER_PHASES_PER_LAST_PHASE]))


def _inproj_mixer(x2, row_tile0, g, w_nat, w3, mixer_args, w_up, nblk):
    n_tiles = nblk * BLOCK // FUSED_TM
    assert n_tiles * (IN_NAT_STEPS + 1) == nblk
    step = lambda i, j: i * (IN_NAT_STEPS + 1) + j
    proj_specs, proj_shapes = _inproj_out(FUSED_TM, n_tiles * FUSED_TM)
    mix_specs, mix_shapes = _mixer_out(step, nblk * BLOCK)
    wu_spec = pl.BlockSpec((D_MODEL, D_FF // nblk), lambda i, j: (0, step(i, j)))
    return pl.pallas_call(
        _inproj_mixer_kernel,
        grid=(n_tiles, IN_NAT_STEPS + 1),
        in_specs=[
            pl.BlockSpec((FUSED_TM, D_MODEL), lambda i, j: (row_tile0 + i, 0)),
            pl.BlockSpec((1, D_MODEL), lambda i, j: (0, 0)),
        ] + _inproj_weight_specs() + _mixer_in_specs(step) + [wu_spec],
        out_specs=proj_specs + mix_specs + [wu_spec],
        out_shape=proj_shapes + mix_shapes + [jax.ShapeDtypeStruct(w_up.shape, BF16)],
        scratch_shapes=[pltpu.VMEM((FUSED_TM, D_MODEL), BF16)] + _state_scratch(),
        compiler_params=_params(2),
        name="inproj_mixer",
    )(x2, g, w_nat, w3, *mixer_args, w_up)


MIXER_PAIR = 2
PAIR_ROWS = MIXER_PAIR * BLOCK


def _mixer_kernel(*refs):
    (sink_ref, qt_ref, kc_ref, kp_ref, km_ref, vc_ref, vp_ref, vl_ref, bb_ref, bm_ref,
     p_ref, gate_ref, gb_ref, shift_ref, cw_ref, cb_ref, mhn_ref) = refs[:N_MIX_IN]
    state_in = refs[N_MIX_IN:N_MIX_IN + N_STATE]
    wo_ref, wd_ref = refs[N_MIX_IN + N_STATE:N_MIX_IN + N_STATE + 2]
    att_ref, hm_ref, wo_bf_ref, wd_bf_ref = refs[N_MIX_IN + N_STATE + 2:-N_STATE]
    state = refs[-N_STATE:]
    step = pl.program_id(0)

    @pl.when(step == 0)
    def _():
        _load_state(state, state_in)

    wo_bf_ref[...] = wo_ref[...].astype(BF16)
    wd_bf_ref[...] = wd_ref[...].astype(BF16)

    rows = lambda ref, k: ref.at[k * BLOCK:(k + 1) * BLOCK]
    cols = lambda ref, k: ref.at[:, k * BLOCK:(k + 1) * BLOCK]
    attention, chunks = [], []
    for k in range(MIXER_PAIR):
        n = MIXER_PAIR * step + 1 + k
        kp_k = kp_ref if k == 0 else rows(kc_ref, k - 1)
        vp_k = vp_ref if k == 0 else cols(vc_ref, k - 1)
        att_in = (sink_ref, cols(qt_ref, k), rows(kc_ref, k), kp_k, km_ref, cols(vc_ref, k), vp_k,
                  vl_ref, bb_ref, bm_ref.at[k:k + 1])
        attention.append(_attention_phases(n, att_in, cols(att_ref, k)))
        chunks.append((n, (rows(p_ref, k), rows(gate_ref, k), gb_ref, shift_ref, cw_ref, cb_ref,
                           mhn_ref), rows(hm_ref, k)))

    def mlstm_chain():
        for n, ml_in, hm_k in chunks:
            yield from _mlstm_phases(n, ml_in, hm_k, state)

    _drain(_interleave([mlstm_chain()] + attention, [MIXER_PAIR] + [1] * MIXER_PAIR))


def _mixer(mixer_args, w_out, w_down, nblk):
    n_steps = nblk // MIXER_PAIR
    before = lambda s: jnp.maximum(MIXER_PAIR * s - 1, 0)
    v_row = ATT_WIDTH // ATT_KV_WIDTH
    in_specs = [
        pl.BlockSpec(memory_space=pltpu.SMEM),
        pl.BlockSpec((ATT_WIDTH, PAIR_ROWS), lambda s: (0, s)),
        pl.BlockSpec((PAIR_ROWS, ATT_KV_WIDTH), lambda s: (s, 0)),
        pl.BlockSpec((BLOCK, ATT_KV_WIDTH), lambda s: (before(s), 0)),
        pl.BlockSpec((N_META, ATT_KV_WIDTH), lambda s: (N_PAD // N_META, 0)),
        pl.BlockSpec((ATT_KV_WIDTH, PAIR_ROWS), lambda s: (v_row, s)),
        pl.BlockSpec((ATT_KV_WIDTH, BLOCK), lambda s: (v_row, before(s))),
        pl.BlockSpec((ATT_KV_WIDTH, BLOCK), lambda s: (v_row, 0)),
        _whole((ATT_KV_HEADS, 2 * BLOCK, GQ)),
        pl.BlockSpec((MIXER_PAIR, ATT_KV_HEADS, N_META, GQ), lambda s: (s, 0, 0, 0)),
        pl.BlockSpec((PAIR_ROWS, NAT_COLS), lambda s: (s, 0)),
        pl.BlockSpec((PAIR_ROWS, GATE_LANES), lambda s: (s, 0)),
        _whole((1, GATE_LANES)),
        _whole((CONV_WIDTH * BLOCK, 2 * BLOCK)),
        _whole((CONV_WIDTH, 2 * ML_QK_WIDTH)),
        _whole((1, 2 * ML_QK_WIDTH)),
        _whole((1, ML_WIDTH)),
    ] + [_whole(s.shape) for s in _state_shapes()]
    assert len(in_specs) == N_MIX_IN + N_STATE
    w_specs = [
        pl.BlockSpec((w_out.shape[0] // n_steps, D_MODEL), lambda s: (s, 0)),
        pl.BlockSpec((D_FF // n_steps, D_MODEL), lambda s: (s, 0)),
    ]
    return pl.pallas_call(
        _mixer_kernel,
        grid=(n_steps,),
        in_specs=in_specs + w_specs,
        out_specs=[
            pl.BlockSpec((ATT_WIDTH, PAIR_ROWS), lambda s: (0, s)),
            pl.BlockSpec((PAIR_ROWS, ML_WIDTH), lambda s: (s, 0)),
        ] + w_specs,
        out_shape=[
            jax.ShapeDtypeStruct((ATT_WIDTH, nblk * BLOCK), BF16),
            jax.ShapeDtypeStruct((nblk * BLOCK, ML_WIDTH), BF16),
        ] + [jax.ShapeDtypeStruct(w.shape, BF16) for w in (w_out, w_down)],
        scratch_shapes=_state_scratch(),
        compiler_params=_params(1),
        name="mixer",
    )(*mixer_args, w_out, w_down)


OUT_TM = 512


def _outproj_kernel(att0_ref, att1_ref, hm0_ref, hm1_ref, x_ref, wa_ref, wm_ref, g_ref,
                    h_ref, u_ref):
    i = pl.program_id(0)
    half = pl.num_programs(0) // 2

    def body(att_ref, hm_ref):
        h = (x_ref[...]
             + lax.dot_general(att_ref[...], wa_ref[...], CONTRACT_FIRST, preferred_element_type=F32)
             + jnp.dot(hm_ref[...], wm_ref[...], preferred_element_type=F32))
        h_ref[...] = h
        u_ref[...] = _rms(h, g_ref[...]).astype(BF16)

    @pl.when(i < half)
    def _():
        body(att0_ref, hm0_ref)

    @pl.when(i >= half)
    def _():
        body(att1_ref, hm1_ref)


def _outproj(att0, att1, hm0, hm1, x2, w_out, g):
    rows = x2.shape[0]
    n = rows // OUT_TM
    half = n // 2
    first = lambda i: jnp.minimum(i, half - 1)
    second = lambda i: jnp.maximum(i - half, 0)
    return pl.pallas_call(
        _outproj_kernel,
        grid=(n,),
        in_specs=[
            pl.BlockSpec((ATT_WIDTH, OUT_TM), lambda i: (0, first(i))),
            pl.BlockSpec((ATT_WIDTH, OUT_TM), lambda i: (0, second(i))),
            pl.BlockSpec((OUT_TM, ML_WIDTH), lambda i: (first(i), 0)),
            pl.BlockSpec((OUT_TM, ML_WIDTH), lambda i: (second(i), 0)),
            pl.BlockSpec((OUT_TM, D_MODEL), lambda i: (i, 0)),
            pl.BlockSpec((ATT_WIDTH, D_MODEL), lambda i: (0, 0)),
            pl.BlockSpec((ML_WIDTH, D_MODEL), lambda i: (ATT_WIDTH // ML_WIDTH, 0)),
            pl.BlockSpec((1, D_MODEL), lambda i: (0, 0)),
        ],
        out_specs=[
            pl.BlockSpec((OUT_TM, D_MODEL), lambda i: (i, 0)),
            pl.BlockSpec((OUT_TM, D_MODEL), lambda i: (i, 0)),
        ],
        out_shape=[
            jax.ShapeDtypeStruct((rows, D_MODEL), F32),
            jax.ShapeDtypeStruct((rows, D_MODEL), BF16),
        ],
        compiler_params=_params(1),
        name="outproj",
    )(att0, att1, hm0, hm1, x2, w_out, w_out, g)


MLP_TM = 1024
MLP_TF = 1024


def _mlp_step(mode, u_ref, wu_ref, wd_ref, o_ref, hbuf_ref=None, ssq_ref=None):
    acts = []
    for c in range(MLP_TF // MXU_COLS):
        cs = slice(c * MXU_COLS, (c + 1) * MXU_COLS)
        a = jnp.dot(u_ref[...], wu_ref[:, cs], preferred_element_type=F32)
        acts.append(jnp.square(jnp.maximum(a, 0.0)).astype(BF16))
        yield
    act = jnp.concatenate(acts, axis=1)
    for c in range(D_MODEL // MXU_COLS):
        cs = slice(c * MXU_COLS, (c + 1) * MXU_COLS)
        d = jnp.dot(act, wd_ref[:, cs], preferred_element_type=F32)
        if mode == "first":
            o_ref[:, cs] = d
        elif mode == "mid":
            o_ref[:, cs] += d
        else:
            v = o_ref[:, cs] + d + hbuf_ref[:, cs]
            o_ref[:, cs] = v
            sq = v * v
            part = sum(sq[:, k * LANE:(k + 1) * LANE] for k in range(MXU_COLS // LANE))
            if c == 0:
                ssq_ref[...] = part
            else:
                ssq_ref[...] += part
        yield


def _mlp_kernel(u_ref, h_hbm, wu_ref, wd_ref, g_ref, o_ref, hbuf_ref, ssq_ref, hsem):
    i = pl.program_id(0)
    j = pl.program_id(1)
    last = pl.num_programs(1) - 1
    r0 = pl.multiple_of(i * MLP_TM, MLP_TM)
    h_copy = pltpu.make_async_copy(h_hbm.at[pl.ds(r0, MLP_TM), :], hbuf_ref, hsem)

    @pl.when(j == 0)
    def _():
        h_copy.start()
        _drain(_mlp_step("first", u_ref, wu_ref, wd_ref, o_ref))

    @pl.when((j > 0) & (j < last))
    def _():
        _drain(_mlp_step("mid", u_ref, wu_ref, wd_ref, o_ref))

    @pl.when(j == last)
    def _():
        h_copy.wait()
        _drain(_mlp_step("last", u_ref, wu_ref, wd_ref, o_ref, hbuf_ref, ssq_ref))
        ms = jnp.sum(ssq_ref[...], axis=-1, keepdims=True) / D_MODEL
        o_ref[...] = o_ref[...] * lax.rsqrt(ms + EPS) * g_ref[...]


def _mlp(u, h, w_up, w_down, g):
    rows = u.shape[0]
    return pl.pallas_call(
        _mlp_kernel,
        grid=(rows // MLP_TM, D_FF // MLP_TF),
        in_specs=[
            pl.BlockSpec((MLP_TM, D_MODEL), lambda i, j: (i, 0)),
            pl.BlockSpec(memory_space=pl.ANY),
            pl.BlockSpec((D_MODEL, MLP_TF), lambda i, j: (0, j)),
            pl.BlockSpec((MLP_TF, D_MODEL), lambda i, j: (j, 0)),
            pl.BlockSpec((1, D_MODEL), lambda i, j: (0, 0)),
        ],
        out_specs=pl.BlockSpec((MLP_TM, D_MODEL), lambda i, j: (i, 0)),
        out_shape=jax.ShapeDtypeStruct((rows, D_MODEL), F32),
        scratch_shapes=[pltpu.VMEM((MLP_TM, D_MODEL), F32), pltpu.VMEM((MLP_TM, LANE), F32),
                        pltpu.SemaphoreType.DMA(())],
        compiler_params=_params(2),
        name="mlp",
    )(u, h, w_up, w_down, g)


def _t5_bucket(dist):
    max_exact = N_BUCKETS // 2
    d = jnp.maximum(dist, 0)
    ratio = jnp.maximum(d, max_exact).astype(F32) / max_exact
    large = max_exact + (jnp.log(ratio) / math.log(MAX_DISTANCE / max_exact)
                         * (N_BUCKETS - max_exact)).astype(jnp.int32)
    large = jnp.minimum(large, N_BUCKETS - 1)
    return jnp.where(d < max_exact, d, large)


BIAS_ROWS = 16


BIAS_BUCKETS_PER_PHASE = 8


def _bias_fill_splats(rb_ref, splat_ref):
    def fill(b, carry):
        for hd in range(ATT_HEADS):
            splat_ref[b * ATT_HEADS + hd] = jnp.full((SUBLANES, BLOCK), rb_ref[b, hd], F32) * LOG2E
        return carry

    lax.fori_loop(0, N_BUCKETS, fill, 0)


def _bias_lookup_phases(bkt, splat_ref, write):
    accs = [jnp.zeros(bkt.shape, F32)] * ATT_HEADS
    for b in range(N_BUCKETS):
        hit = (bkt >= b) & (bkt < b + 1)
        accs = [jnp.where(hit, jnp.concatenate([splat_ref[b * ATT_HEADS + hd]]
                                               * (BIAS_ROWS // SUBLANES), axis=0), a)
                for hd, a in enumerate(accs)]
        if b % BIAS_BUCKETS_PER_PHASE == BIAS_BUCKETS_PER_PHASE - 1:
            yield
    write(accs)


def _bias_step_phases(bkt_band_ref, bkt_meta_ref, band_ref, meta_ref, splat_ref):
    def write_band(tables):
        for hd, t in enumerate(tables):
            h, g = divmod(hd, ATT_GROUP)
            band_ref[h, :, g * BLOCK:(g + 1) * BLOCK] = t

    yield from _bias_lookup_phases(bkt_band_ref[...], splat_ref, write_band)
    for k in range(bkt_meta_ref.shape[0]):
        def write_meta(tables, k=k):
            for hd, t in enumerate(tables):
                h, g = divmod(hd, ATT_GROUP)
                meta_ref[k, h, :, g * BLOCK:(g + 1) * BLOCK] = t

        yield from _bias_lookup_phases(bkt_meta_ref[k], splat_ref, write_meta)


def _bias_buckets(nblk):
    c = jnp.arange(2 * BLOCK)[:, None]
    r = jnp.arange(BLOCK)[None, :]
    bkt_band = _t5_bucket(r + BLOCK - c).astype(jnp.int32)
    q_pos = BLOCK + jnp.arange(nblk * BLOCK)[None, :]
    m_pos = N_PAD + jnp.arange(N_META)[:, None]
    bkt_meta = _t5_bucket(q_pos - m_pos).astype(jnp.int32)
    return bkt_band, bkt_meta.reshape(N_META, nblk, BLOCK).transpose(1, 0, 2)


W_IN_OFFS = dict(q=0, k=1024, v=1280, mq=1536, mk=2048, mv=2560, mo=3584, gates=4608)
PREP_ROWS = 256
NAT_SRC = [W_IN_OFFS["mq"] // PREP_ROWS + t for t in range(NAT_COLS // PREP_ROWS)]
W3_SRC = ([W_IN_OFFS["q"] // PREP_ROWS + t for t in range(ATT_WIDTH // PREP_ROWS)]
          + [W_IN_OFFS["v"] // PREP_ROWS + t for t in range(ATT_KV_WIDTH // PREP_ROWS)]
          + [W_IN_OFFS["k"] // PREP_ROWS + t for t in range(ATT_KV_WIDTH // PREP_ROWS)]
          + [W_IN_OFFS["gates"] // PREP_ROWS])
N_NAT_BLK = len(NAT_SRC)
N_Q_BLK = ATT_WIDTH // PREP_ROWS
N_PREP = N_NAT_BLK + len(W3_SRC)
assert W_IN_OFFS["mq"] + NAT_COLS == W_IN_OFFS["gates"] and len(W3_SRC) * PREP_ROWS == W3_ROWS


def _win_prep_kernel(src_ref, w_ref, nat_ref, w3_ref):
    j = pl.program_id(0)
    wv = w_ref[0]

    @pl.when(j < N_NAT_BLK)
    def _():
        nat_ref[...] = wv.astype(BF16)

    @pl.when((j >= N_NAT_BLK) & (j < N_NAT_BLK + N_Q_BLK))
    def _():
        w3_ref[...] = (wv * (ATT_HEAD_DIM ** -0.5 * LOG2E)).astype(BF16)

    @pl.when((j >= N_NAT_BLK + N_Q_BLK) & (j < N_PREP - 1))
    def _():
        w3_ref[...] = wv.astype(BF16)

    @pl.when(j == N_PREP - 1)
    def _():
        row = lax.broadcasted_iota(jnp.int32, wv.shape, 0)
        w3_ref[...] = jnp.where(row < 2 * ML_HEADS, wv, 0.0).astype(BF16)


def _win_prep(w_in_t):
    src = jnp.asarray(NAT_SRC + W3_SRC, jnp.int32)
    d = w_in_t.shape[2]
    return pl.pallas_call(
        _win_prep_kernel,
        grid_spec=pltpu.PrefetchScalarGridSpec(
            num_scalar_prefetch=1,
            grid=(N_PREP,),
            in_specs=[pl.BlockSpec((1, PREP_ROWS, d), lambda j, src: (0, src[j], 0))],
            out_specs=[
                pl.BlockSpec((PREP_ROWS, d), lambda j, src: (jnp.minimum(j, N_NAT_BLK - 1), 0)),
                pl.BlockSpec((PREP_ROWS, d), lambda j, src: (jnp.maximum(j - N_NAT_BLK, 0), 0)),
            ],
        ),
        out_shape=[
            jax.ShapeDtypeStruct((NAT_COLS, d), BF16),
            jax.ShapeDtypeStruct((W3_ROWS, d), BF16),
        ],
        compiler_params=_params(1),
        name="w_in_prep",
    )(src, w_in_t)


def kernel(x, meta_tokens, w_in, conv_w, conv_b, b_igate, b_fgate, attn_sinks, rel_bias,
           mh_norm, w_out, norm_mix, norm_mlp, w_up, w_down, norm_final):
    batch, seq, d = x.shape
    assert batch == 2, "the two batches are processed one projection / mixer stage apart"
    nblk = seq // BLOCK
    x2 = x.reshape(batch * seq, d)
    lead = jnp.concatenate([jnp.zeros((N_PAD, d), x.dtype), meta_tokens.astype(x.dtype)], axis=0)
    g_mix = norm_mix[0][None].astype(F32)

    w_nat, w3 = _win_prep(jnp.swapaxes(w_in, 1, 2))
    gate_bias = jnp.pad(jnp.concatenate([b_igate[0], b_fgate[0]]).astype(F32),
                        (0, GATE_LANES - 2 * ML_HEADS))[None]

    proj0, proj_lead, (bias_band_t, bias_meta_t) = _inproj(
        x2, lead, g_mix, w_nat, w3, rel_bias, seq // IN_TM, nblk)
    consts = (attn_sinks[0].astype(F32), bias_band_t, bias_meta_t, gate_bias, _conv_shift_matrix(),
              conv_w[0].astype(F32), conv_b[0][None].astype(F32), mh_norm[0][None].astype(F32))
    state = _lead_state(proj_lead, consts)
    outs = _inproj_mixer(x2, seq // FUSED_TM, g_mix, w_nat, w3,
                         _mixer_args(consts, proj0, proj_lead, state), w_up[0], nblk)
    proj1, (att0, hm0, w_up_bf) = outs[:N_PROJ_OUT], outs[N_PROJ_OUT:]
    att1, hm1, w_out_bf, w_down_bf = _mixer(_mixer_args(consts, proj1, proj_lead, state),
                                            w_out[0], w_down[0], nblk)

    h2, u2 = _outproj(att0, att1, hm0, hm1, x2, w_out_bf, norm_mlp[0][None].astype(F32))
    out = _mlp(u2, h2, w_up_bf, w_down_bf, norm_final[None].astype(F32))
    return out.reshape(batch, seq, d)
```

```python
import math

import numpy as np
import jax
import jax.numpy as jnp
from jax import lax
from jax.experimental import pallas as pl
from jax.experimental.pallas import tpu as pltpu

F32 = jnp.float32
BF16 = jnp.bfloat16

D_MODEL = 2048
N_META = 16
BLOCK = 128
N_PAD = BLOCK - N_META
WINDOW = 128
ATT_HEADS = 16
ATT_KV_HEADS = 4
ATT_GROUP = ATT_HEADS // ATT_KV_HEADS
ATT_HEAD_DIM = 64
ATT_WIDTH = ATT_HEADS * ATT_HEAD_DIM
ATT_KV_WIDTH = ATT_KV_HEADS * ATT_HEAD_DIM
ML_HEADS = 4
ML_V_DIM = 256
ML_QK_DIM = 128
ML_WIDTH = ML_HEADS * ML_V_DIM
ML_QK_WIDTH = ML_HEADS * ML_QK_DIM
CONV_WIDTH = 4
GATE_SOFTCAP = 15.0
D_FF = 4 * D_MODEL
N_BUCKETS = 32
MAX_DISTANCE = 128
EPS = 1e-6
NEG = -1e30
LOG2E = math.log2(math.e)

NAT_MQK = 0
NAT_MV = 2 * ML_QK_WIDTH
NAT_MO = NAT_MV + ML_WIDTH
NAT_COLS = NAT_MO + ML_WIDTH
T_ROWS = ATT_WIDTH + ATT_KV_WIDTH
GATE_LANES = 128
W3_K = T_ROWS
W3_G = W3_K + ATT_KV_WIDTH
W3_ROWS = W3_G + 256

V7X_VMEM_LIMIT = 60 * 1024 * 1024
LANE = 128
SUBLANES = 8
CONTRACT_LAST = (((1,), (1,)), ((), ()))
CONTRACT_FIRST = (((0,), (0,)), ((), ()))


def _params(n_axes, vmem=V7X_VMEM_LIMIT):
    return pltpu.CompilerParams(dimension_semantics=("arbitrary",) * n_axes,
                                vmem_limit_bytes=vmem)


def _rms(x, g):
    return x * lax.rsqrt(jnp.mean(x * x, axis=-1, keepdims=True) + EPS) * g


_DONE = object()


def _drain(phases):
    for _ in phases:
        pass


def _interleave(streams, weights):
    live = list(zip(streams, weights))
    while live:
        for item in list(live):
            stream, weight = item
            for _ in range(weight):
                if next(stream, _DONE) is _DONE:
                    live.remove(item)
                    break
        yield


IN_TM = 1024
IN_NAT_STEPS = 3
IN_TN = NAT_COLS // IN_NAT_STEPS
MXU_COLS = 256
N_PROJ_OUT = 4


def _inproj_nat_phases(rows, u_ref, wn_ref, nat_ref):
    u = u_ref[0:rows, :]
    for c0 in range(0, IN_TN, MXU_COLS):
        cs = slice(c0, c0 + MXU_COLS)
        nat_ref[0:rows, cs] = lax.dot_general(u, wn_ref[cs, :], CONTRACT_LAST,
                                              preferred_element_type=F32).astype(BF16)
        yield


def _inproj_last_phases(rows, u_ref, w3_ref, kn_ref, t_ref, gt_ref):
    u = u_ref[0:rows, :]
    for c0 in range(0, T_ROWS, MXU_COLS):
        cs = slice(c0, c0 + MXU_COLS)
        t_ref[cs, 0:rows] = lax.dot_general(w3_ref[cs, :], u, CONTRACT_LAST,
                                            preferred_element_type=F32).astype(BF16)
        yield
    kn_ref[0:rows, :] = lax.dot_general(u, w3_ref[W3_K:W3_G, :], CONTRACT_LAST,
                                        preferred_element_type=F32).astype(BF16)
    yield
    gt_ref[0:rows, :] = lax.dot_general(u, w3_ref[W3_G:W3_G + GATE_LANES, :], CONTRACT_LAST,
                                        preferred_element_type=F32)
    yield


BIAS_PHASES_PER_ROUND = 3


def _inproj_kernel(x_ref, lead_ref, g_ref, wn_ref, w3_ref, rb_ref, bkt_band_ref, bkt_meta_ref,
                   nat_ref, kn_ref, t_ref, gt_ref, nat_lead_ref, kn_lead_ref, t_lead_ref, gt_lead_ref,
                   band_ref, meta_ref, u_ref, u_lead_ref, splat_ref):
    i = pl.program_id(0)
    j = pl.program_id(1)

    @pl.when(j == 0)
    def _():
        u_ref[...] = _rms(x_ref[...], g_ref[...]).astype(BF16)

    @pl.when((j == 0) & (i == 0))
    def _():
        u_lead_ref[...] = _rms(lead_ref[...], g_ref[...]).astype(BF16)
        _bias_fill_splats(rb_ref, splat_ref)

    def with_bias(phases):
        bias = _bias_step_phases(bkt_band_ref, bkt_meta_ref, band_ref, meta_ref, splat_ref)
        _drain(_interleave([phases, bias], [1, BIAS_PHASES_PER_ROUND]))

    @pl.when(j < IN_NAT_STEPS)
    def _():
        with_bias(_inproj_nat_phases(IN_TM, u_ref, wn_ref, nat_ref))

    @pl.when(j == IN_NAT_STEPS)
    def _():
        with_bias(_inproj_last_phases(IN_TM, u_ref, w3_ref, kn_ref, t_ref, gt_ref))

    @pl.when((i == 0) & (j < IN_NAT_STEPS))
    def _():
        _drain(_inproj_nat_phases(BLOCK, u_lead_ref, wn_ref, nat_lead_ref))

    @pl.when((i == 0) & (j == IN_NAT_STEPS))
    def _():
        _drain(_inproj_last_phases(BLOCK, u_lead_ref, w3_ref, kn_lead_ref, t_lead_ref, gt_lead_ref))


def _inproj_weight_specs():
    last_nat = IN_NAT_STEPS - 1
    return [
        pl.BlockSpec((IN_TN, D_MODEL), lambda i, j: (jnp.minimum(j, last_nat), 0)),
        pl.BlockSpec((W3_ROWS, D_MODEL), lambda i, j: (0, 0)),
    ]


def _inproj_out(tm, rows):
    last_nat = IN_NAT_STEPS - 1
    specs = [
        pl.BlockSpec((tm, IN_TN), lambda i, j: (i, jnp.minimum(j, last_nat))),
        pl.BlockSpec((tm, ATT_KV_WIDTH), lambda i, j: (i, 0)),
        pl.BlockSpec((T_ROWS, tm), lambda i, j: (0, i)),
        pl.BlockSpec((tm, GATE_LANES), lambda i, j: (i, 0)),
    ]
    shapes = [
        jax.ShapeDtypeStruct((rows, NAT_COLS), BF16),
        jax.ShapeDtypeStruct((rows, ATT_KV_WIDTH), BF16),
        jax.ShapeDtypeStruct((T_ROWS, rows), BF16),
        jax.ShapeDtypeStruct((rows, GATE_LANES), F32),
    ]
    return specs, shapes


def _inproj(x2, lead, g, w_nat, w3, rel_bias, n_tiles, nblk):
    out_specs, out_shape = _inproj_out(IN_TM, n_tiles * IN_TM)
    last_nat = IN_NAT_STEPS - 1
    n_steps = n_tiles * (IN_NAT_STEPS + 1)
    assert n_steps * BIAS_ROWS == 2 * BLOCK and nblk % n_steps == 0
    meta_per_step = nblk // n_steps
    bkt_band, bkt_meta = _bias_buckets(nblk)
    step = lambda i, j: i * (IN_NAT_STEPS + 1) + j
    bias_in_specs = [
        pl.BlockSpec(memory_space=pltpu.SMEM),
        pl.BlockSpec((BIAS_ROWS, BLOCK), lambda i, j: (step(i, j), 0)),
        pl.BlockSpec((meta_per_step, N_META, BLOCK), lambda i, j: (step(i, j), 0, 0)),
    ]
    bias_out_specs = [
        pl.BlockSpec((ATT_KV_HEADS, BIAS_ROWS, GQ), lambda i, j: (0, step(i, j), 0)),
        pl.BlockSpec((meta_per_step, ATT_KV_HEADS, N_META, GQ), lambda i, j: (step(i, j), 0, 0, 0)),
    ]
    bias_shapes = [
        jax.ShapeDtypeStruct((ATT_KV_HEADS, 2 * BLOCK, GQ), F32),
        jax.ShapeDtypeStruct((nblk, ATT_KV_HEADS, N_META, GQ), F32),
    ]
    lead_col = lambda i, j: jnp.where(i == 0, jnp.minimum(j, last_nat), last_nat)
    lead_specs = [
        pl.BlockSpec((BLOCK, IN_TN), lambda i, j: (0, lead_col(i, j))),
        pl.BlockSpec((BLOCK, ATT_KV_WIDTH), lambda i, j: (0, 0)),
        pl.BlockSpec((T_ROWS, BLOCK), lambda i, j: (0, 0)),
        pl.BlockSpec((BLOCK, GATE_LANES), lambda i, j: (0, 0)),
    ]
    lead_shapes = [
        jax.ShapeDtypeStruct((BLOCK, NAT_COLS), BF16),
        jax.ShapeDtypeStruct((BLOCK, ATT_KV_WIDTH), BF16),
        jax.ShapeDtypeStruct((T_ROWS, BLOCK), BF16),
        jax.ShapeDtypeStruct((BLOCK, GATE_LANES), F32),
    ]
    outs = pl.pallas_call(
        _inproj_kernel,
        grid=(n_tiles, IN_NAT_STEPS + 1),
        in_specs=[
            pl.BlockSpec((IN_TM, D_MODEL), lambda i, j: (i, 0)),
            pl.BlockSpec((BLOCK, D_MODEL), lambda i, j: (0, 0)),
            pl.BlockSpec((1, D_MODEL), lambda i, j: (0, 0)),
        ] + _inproj_weight_specs() + bias_in_specs,
        out_specs=out_specs + lead_specs + bias_out_specs,
        out_shape=out_shape + lead_shapes + bias_shapes,
        scratch_shapes=[pltpu.VMEM((IN_TM, D_MODEL), BF16), pltpu.VMEM((BLOCK, D_MODEL), BF16),
                        pltpu.VMEM((N_BUCKETS * ATT_HEADS, SUBLANES, BLOCK), F32)],
        compiler_params=_params(2),
        name="inproj",
    )(x2, lead, g, w_nat, w3, rel_bias.astype(F32), bkt_band, bkt_meta)
    return outs[:N_PROJ_OUT], outs[N_PROJ_OUT:2 * N_PROJ_OUT], outs[2 * N_PROJ_OUT:]


GQ = ATT_GROUP * BLOCK
ATT_SLABS_PER_PHASE = 2


def _attn_prologue(n, kc_ref, kp_ref, km_ref):
    c = lax.broadcasted_iota(jnp.int32, (2 * BLOCK, BLOCK), 0)
    r = lax.broadcasted_iota(jnp.int32, (2 * BLOCK, BLOCK), 1)
    dist = r + BLOCK - c
    band_ok = (dist >= 0) & (dist < WINDOW) & ((c >= BLOCK) | (n >= 1))
    mm = lax.broadcasted_iota(jnp.int32, (N_META, BLOCK), 0)
    rr = lax.broadcasted_iota(jnp.int32, (N_META, BLOCK), 1)
    meta_ok = (n + 1) * BLOCK + rr >= N_PAD + mm
    kall = jnp.concatenate([kp_ref[...], kc_ref[...], km_ref[...]], axis=0)
    return band_ok, meta_ok, kall


BF16_ROWS = 16


def _ones_row(width):
    row = lax.broadcasted_iota(jnp.int32, (BF16_ROWS, width), 0)
    return jnp.where(row == 0, 1.0, 0.0).astype(BF16)


def _attn_head(h, pro, sink_ref, qt_ref, vc_ref, vp_ref, vl_ref, bb_ref, bm_ref, o_ref):
    band_ok, meta_ok, kall = pro
    lo = (h // 2) * LANE
    kpair = kall[:, lo:lo + LANE]
    qh = jnp.concatenate(
        [qt_ref[(ATT_GROUP * h + g) * ATT_HEAD_DIM:(ATT_GROUP * h + g + 1) * ATT_HEAD_DIM, :]
         for g in range(ATT_GROUP)], axis=1)
    zq = jnp.zeros_like(qh)
    qz = jnp.concatenate([qh, zq] if h % 2 == 0 else [zq, qh], axis=0)
    st = jnp.dot(kpair, qz, preferred_element_type=F32)
    yield
    zpad = jnp.zeros((N_PAD, BLOCK), BF16)
    p_band, p_lead, sink_p = [], [], []
    for g in range(ATT_GROUP):
        gs = slice(g * BLOCK, (g + 1) * BLOCK)
        sl = st[:, gs]
        sink = jnp.full((1, BLOCK), sink_ref[ATT_GROUP * h + g], F32) * LOG2E
        lb = jnp.where(band_ok, sl[:2 * BLOCK] + bb_ref[h, :, gs], NEG * LOG2E)
        lm = jnp.where(meta_ok, sl[2 * BLOCK:] + bm_ref[0, h, :, gs], NEG * LOG2E)
        m = jnp.maximum(jnp.max(lb, axis=0, keepdims=True), jnp.max(lm, axis=0, keepdims=True))
        m = jnp.maximum(m, sink)
        pb = jnp.exp2(lb - m)
        pm = jnp.exp2(lm - m)
        sink_p.append(jnp.exp2(sink - m))
        p_band.append(pb.astype(BF16))
        p_lead.append(jnp.concatenate([zpad, pm.astype(BF16)], axis=0))
        if g % ATT_SLABS_PER_PHASE == ATT_SLABS_PER_PHASE - 1:
            yield
    pbt = jnp.concatenate(p_band, axis=1)
    plt = jnp.concatenate(p_lead, axis=1)
    hs = slice(h * ATT_HEAD_DIM, (h + 1) * ATT_HEAD_DIM)
    vband = jnp.concatenate([vp_ref[hs, :], vc_ref[hs, :]], axis=1)
    ones_b = _ones_row(2 * BLOCK)
    ones_l = _ones_row(BLOCK)
    ot = (jnp.dot(jnp.concatenate([vband, ones_b], axis=0), pbt, preferred_element_type=F32)
          + jnp.dot(jnp.concatenate([vl_ref[hs, :], ones_l], axis=0), plt,
                    preferred_element_type=F32))
    for g in range(ATT_GROUP):
        gs = slice(g * BLOCK, (g + 1) * BLOCK)
        den = ot[ATT_HEAD_DIM:ATT_HEAD_DIM + 1, gs] + sink_p[g]
        row = (ATT_GROUP * h + g) * ATT_HEAD_DIM
        o_ref[row:row + ATT_HEAD_DIM, :] = (ot[:ATT_HEAD_DIM, gs] * (1.0 / den)).astype(BF16)


def _conv_shift_matrix():
    s = np.zeros((CONV_WIDTH * BLOCK, 2 * BLOCK), np.float32)
    for k in range(CONV_WIDTH):
        t = np.arange(BLOCK)
        s[k * BLOCK + t, BLOCK + t - k] = 1.0
    return jnp.asarray(s, BF16)


def _mlstm_gates(n, gate_ref, gb_ref):
    t_idx = lax.broadcasted_iota(jnp.int32, (BLOCK, GATE_LANES), 0)
    valid = (n > 0) | (t_idx >= N_PAD)
    pre = GATE_SOFTCAP * jnp.tanh((gate_ref[...] + gb_ref[...]) / GATE_SOFTCAP)
    log_i = jnp.where(valid, pre, NEG)
    log_sig = jnp.minimum(pre, 0.0) - jnp.log1p(jnp.exp(-jnp.abs(pre)))
    log_f = jnp.where(valid, log_sig, 0.0)
    row = lax.broadcasted_iota(jnp.int32, (BLOCK, BLOCK), 0)
    col = lax.broadcasted_iota(jnp.int32, (BLOCK, BLOCK), 1)
    causal = col <= row
    tril = jnp.where(causal, 1.0, 0.0).astype(F32)
    b_all = jnp.dot(tril, log_f, preferred_element_type=F32,
                    precision=lax.Precision.HIGHEST)
    return log_i, b_all, log_i.T, b_all.T, causal


def _mlstm_conv(p_ref, shift_ref, cw_ref, cb_ref, xprev_ref):
    xcur = p_ref[:, NAT_MQK:NAT_MV]
    xcat = jnp.concatenate([xprev_ref[...], xcur], axis=0)
    sh = jnp.dot(shift_ref[...], xcat, preferred_element_type=F32)
    xprev_ref[...] = xcur
    y = cb_ref[...]
    for k in range(CONV_WIDTH):
        y = y + cw_ref[CONV_WIDTH - 1 - k:CONV_WIDTH - k, :] * sh[k * BLOCK:(k + 1) * BLOCK]
    act = y * jax.nn.sigmoid(y)
    return act[:, :ML_QK_WIDTH] * (ML_QK_DIM ** -0.5), act[:, ML_QK_WIDTH:]


def _mlstm_head(h, gates, q_all, k_all, p_ref, mhn_ref, o_ref, c_ref, n_ref, m_ref):
    log_i, b_all, log_i_t, b_all_t, causal = gates
    q_f = q_all[:, h * ML_QK_DIM:(h + 1) * ML_QK_DIM]
    k_f = k_all[:, h * ML_QK_DIM:(h + 1) * ML_QK_DIM]
    q_h = q_f.astype(BF16)
    k_h = k_f.astype(BF16)
    v_h = p_ref[:, NAT_MV + h * ML_V_DIM:NAT_MV + (h + 1) * ML_V_DIM]
    b_col = b_all[:, ML_HEADS + h:ML_HEADS + h + 1]
    li_col = log_i[:, h:h + 1]
    b_row = b_all_t[ML_HEADS + h:ML_HEADS + h + 1, :]
    li_row = log_i_t[h:h + 1, :]
    b_last = b_all[BLOCK - 1:BLOCK, ML_HEADS + h:ML_HEADS + h + 1]
    c_prev = c_ref[h]
    n_prev = n_ref[h][0:1, :]
    m_prev = m_ref[h][0:1, 0:1]

    dmat = jnp.where(causal, b_col - b_row + li_row, -jnp.inf)
    inter = b_col + m_prev
    m_t = jnp.maximum(inter, jnp.max(dmat, axis=1, keepdims=True))
    s = lax.dot_general(q_h, k_h, CONTRACT_LAST, preferred_element_type=F32) * jnp.exp(dmat - m_t)
    yield
    a_t = jnp.exp(inter - m_t)
    num = (jnp.dot(s.astype(BF16), v_h, preferred_element_type=F32)
           + a_t * jnp.dot(q_h, c_prev.astype(BF16), preferred_element_type=F32))
    den = (jnp.sum(s, axis=1, keepdims=True)
           + a_t * jnp.sum(q_f * n_prev, axis=1, keepdims=True))
    den = jnp.maximum(jnp.abs(den), jnp.exp(-m_t))
    hh = num / den
    hh = hh * lax.rsqrt(jnp.mean(hh * hh, axis=-1, keepdims=True) + EPS)
    vs = slice(h * ML_V_DIM, (h + 1) * ML_V_DIM)
    hh = hh * mhn_ref[:, vs]
    m_o = p_ref[:, NAT_MO + h * ML_V_DIM:NAT_MO + (h + 1) * ML_V_DIM].astype(F32)
    o_ref[:, vs] = (hh * jax.nn.sigmoid(m_o)).astype(BF16)
    yield

    g_col = b_last - b_col + li_col
    m_loc = jnp.max(g_col, axis=0, keepdims=True)
    wk = jnp.exp(g_col - m_loc) * k_f
    c_loc = lax.dot_general(wk.astype(BF16), v_h, CONTRACT_FIRST,
                            preferred_element_type=F32)
    n_loc = jnp.sum(wk, axis=0, keepdims=True)
    m_new = jnp.maximum(b_last + m_prev, m_loc)
    a = jnp.exp(b_last + m_prev - m_new)
    cc = jnp.exp(m_loc - m_new)
    c_ref[h] = a * c_prev + cc * c_loc
    n_ref[h] = jnp.broadcast_to(a * n_prev + cc * n_loc, (8, ML_QK_DIM))
    m_ref[h] = jnp.broadcast_to(m_new, (8, LANE))


N_MIX_IN = 17
N_STATE = 4
MIXER_PHASES_PER_NAT_PHASE = 6
MIXER_PHASES_PER_LAST_PHASE = 3


N_ATT_IN = 10


def _attention_phases(n, att_in, att_ref):
    sink_ref, qt_ref, kc_ref, kp_ref, km_ref, vc_ref, vp_ref, vl_ref, bb_ref, bm_ref = att_in
    pro = _attn_prologue(n - 1, kc_ref, kp_ref, km_ref)
    for h in range(ATT_KV_HEADS):
        yield from _attn_head(h, pro, sink_ref, qt_ref, vc_ref, vp_ref, vl_ref, bb_ref, bm_ref, att_ref)


def _mlstm_phases(n, ml_in, hm_ref, state):
    p_ref, gate_ref, gb_ref, shift_ref, cw_ref, cb_ref, mhn_ref = ml_in
    c_ref, n_ref, m_ref, xprev_ref = state
    gates = _mlstm_gates(n, gate_ref, gb_ref)
    yield
    q_all, k_all = _mlstm_conv(p_ref, shift_ref, cw_ref, cb_ref, xprev_ref)
    yield
    for h in range(ML_HEADS):
        yield from _mlstm_head(h, gates, q_all, k_all, p_ref, mhn_ref, hm_ref, c_ref, n_ref, m_ref)


def _mixer_phases(n, mix_in, att_ref, hm_ref, state):
    return _interleave([_mlstm_phases(n, mix_in[N_ATT_IN:], hm_ref, state),
                        _attention_phases(n, mix_in[:N_ATT_IN], att_ref)], [1, 1])


def _load_state(state, state_in):
    for dst, src in zip(state, state_in):
        dst[...] = src[...]


def _state_shapes():
    return [
        jax.ShapeDtypeStruct((ML_HEADS, ML_QK_DIM, ML_V_DIM), F32),
        jax.ShapeDtypeStruct((ML_HEADS, SUBLANES, ML_QK_DIM), F32),
        jax.ShapeDtypeStruct((ML_HEADS, SUBLANES, LANE), F32),
        jax.ShapeDtypeStruct((BLOCK, 2 * ML_QK_WIDTH), BF16),
    ]


def _state_scratch():
    return [pltpu.VMEM(s.shape, s.dtype) for s in _state_shapes()]


def _whole(shape):
    return pl.BlockSpec(shape, lambda *ids: (0,) * len(shape))


def _mixer_in_specs(step):
    cur = lambda *ids: step(*ids)
    prev = lambda *ids: jnp.maximum(step(*ids) - 1, 0)
    v_row = ATT_WIDTH // ATT_KV_WIDTH
    meta_rows = N_PAD // N_META
    return [
        pl.BlockSpec(memory_space=pltpu.SMEM),
        pl.BlockSpec((ATT_WIDTH, BLOCK), lambda *ids: (0, cur(*ids))),
        pl.BlockSpec((BLOCK, ATT_KV_WIDTH), lambda *ids: (cur(*ids), 0)),
        pl.BlockSpec((BLOCK, ATT_KV_WIDTH), lambda *ids: (prev(*ids), 0)),
        pl.BlockSpec((N_META, ATT_KV_WIDTH), lambda *ids: (meta_rows, 0)),
        pl.BlockSpec((ATT_KV_WIDTH, BLOCK), lambda *ids: (v_row, cur(*ids))),
        pl.BlockSpec((ATT_KV_WIDTH, BLOCK), lambda *ids: (v_row, prev(*ids))),
        pl.BlockSpec((ATT_KV_WIDTH, BLOCK), lambda *ids: (v_row, 0)),
        _whole((ATT_KV_HEADS, 2 * BLOCK, GQ)),
        pl.BlockSpec((1, ATT_KV_HEADS, N_META, GQ), lambda *ids: (cur(*ids), 0, 0, 0)),
        pl.BlockSpec((BLOCK, NAT_COLS), lambda *ids: (cur(*ids), 0)),
        pl.BlockSpec((BLOCK, GATE_LANES), lambda *ids: (cur(*ids), 0)),
        _whole((1, GATE_LANES)),
        _whole((CONV_WIDTH * BLOCK, 2 * BLOCK)),
        _whole((CONV_WIDTH, 2 * ML_QK_WIDTH)),
        _whole((1, 2 * ML_QK_WIDTH)),
        _whole((1, ML_WIDTH)),
    ] + [_whole(s.shape) for s in _state_shapes()]


def _mixer_args(consts, proj_b, proj_lead, state):
    sinks, bias_band_t, bias_meta_t, gate_bias, shift, conv_w, conv_b, mh_norm = consts
    nat_b, kn_b, projt_b, gt_b = proj_b
    _, kn_lead, projt_lead, _ = proj_lead
    return [sinks, projt_b, kn_b, kn_b, kn_lead, projt_b, projt_b, projt_lead, bias_band_t,
            bias_meta_t, nat_b, gt_b, gate_bias, shift, conv_w, conv_b, mh_norm] + list(state)


def _mixer_out(step, n_rows):
    specs = [
        pl.BlockSpec((ATT_WIDTH, BLOCK), lambda *ids: (0, step(*ids))),
        pl.BlockSpec((BLOCK, ML_WIDTH), lambda *ids: (step(*ids), 0)),
    ]
    shapes = [
        jax.ShapeDtypeStruct((ATT_WIDTH, n_rows), BF16),
        jax.ShapeDtypeStruct((n_rows, ML_WIDTH), BF16),
    ]
    return specs, shapes


def _lead_state_kernel(p_ref, gate_ref, gb_ref, shift_ref, cw_ref, cb_ref, mhn_ref,
                       c_ref, n_ref, m_ref, xprev_ref, hm_ref):
    for ref in (c_ref, n_ref, m_ref, xprev_ref):
        ref[...] = jnp.zeros_like(ref)
    gates = _mlstm_gates(0, gate_ref, gb_ref)
    q_all, k_all = _mlstm_conv(p_ref, shift_ref, cw_ref, cb_ref, xprev_ref)
    for h in range(ML_HEADS):
        _drain(_mlstm_head(h, gates, q_all, k_all, p_ref, mhn_ref, hm_ref, c_ref, n_ref, m_ref))


def _lead_state(proj_lead, consts):
    nat_lead, _, _, gates_lead = proj_lead
    _, _, _, gate_bias, shift, conv_w, conv_b, mh_norm = consts
    shapes = _state_shapes()
    return pl.pallas_call(
        _lead_state_kernel,
        grid=(1,),
        in_specs=[
            _whole((BLOCK, NAT_COLS)),
            _whole((BLOCK, GATE_LANES)),
            _whole((1, GATE_LANES)),
            _whole((CONV_WIDTH * BLOCK, 2 * BLOCK)),
            _whole((CONV_WIDTH, 2 * ML_QK_WIDTH)),
            _whole((1, 2 * ML_QK_WIDTH)),
            _whole((1, ML_WIDTH)),
        ],
        out_specs=[_whole(s.shape) for s in shapes],
        out_shape=shapes,
        scratch_shapes=[pltpu.VMEM((BLOCK, ML_WIDTH), BF16)],
        compiler_params=_params(1),
        name="lead_state",
    )(nat_lead, gates_lead, gate_bias, shift, conv_w, conv_b, mh_norm)


FUSED_TM = 512


def _inproj_mixer_kernel(*refs):
    x_ref, g_ref, wn_ref, w3_ref = refs[:4]
    mix_in = refs[4:4 + N_MIX_IN]
    state_in = refs[4 + N_MIX_IN:4 + N_MIX_IN + N_STATE]
    wu_ref = refs[4 + N_MIX_IN + N_STATE]
    (nat_ref, kn_ref, t_ref, gt_ref, att_ref, hm_ref,
     wu_bf_ref) = refs[5 + N_MIX_IN + N_STATE:-(1 + N_STATE)]
    u_ref = refs[-(1 + N_STATE)]
    state = refs[-N_STATE:]
    i = pl.program_id(0)
    j = pl.program_id(1)
    step = i * (IN_NAT_STEPS + 1) + j
    wu_bf_ref[...] = wu_ref[...].astype(BF16)

    @pl.when(j == 0)
    def _():
        u_ref[...] = _rms(x_ref[...], g_ref[...]).astype(BF16)

    @pl.when(step == 0)
    def _():
        _load_state(state, state_in)

    @pl.when(j < IN_NAT_STEPS)
    def _():
        _drain(_interleave([_inproj_nat_phases(FUSED_TM, u_ref, wn_ref, nat_ref),
                            _mixer_phases(step + 1, mix_in, att_ref, hm_ref, state)],
                           [1, MIXER_PHASES_PER_NAT_PHASE]))

    @pl.when(j == IN_NAT_STEPS)
    def _():
        _drain(_interleave([_inproj_last_phases(FUSED_TM, u_ref, w3_ref, kn_ref, t_ref, gt_ref),
                            _mixer_phases(step + 1, mix_in, att_ref, hm_ref, state)],
                           [1, MIXER_PHASES_PER_LAST_PHASE]))


def _inproj_mixer(x2, row_tile0, g, w_nat, w3, mixer_args, w_up, nblk):
    n_tiles = nblk * BLOCK // FUSED_TM
    assert n_tiles * (IN_NAT_STEPS + 1) == nblk
    step = lambda i, j: i * (IN_NAT_STEPS + 1) + j
    proj_specs, proj_shapes = _inproj_out(FUSED_TM, n_tiles * FUSED_TM)
    mix_specs, mix_shapes = _mixer_out(step, nblk * BLOCK)
    wu_spec = pl.BlockSpec((D_MODEL, D_FF // nblk), lambda i, j: (0, step(i, j)))
    return pl.pallas_call(
        _inproj_mixer_kernel,
        grid=(n_tiles, IN_NAT_STEPS + 1),
        in_specs=[
            pl.BlockSpec((FUSED_TM, D_MODEL), lambda i, j: (row_tile0 + i, 0)),
            pl.BlockSpec((1, D_MODEL), lambda i, j: (0, 0)),
        ] + _inproj_weight_specs() + _mixer_in_specs(step) + [wu_spec],
        out_specs=proj_specs + mix_specs + [wu_spec],
        out_shape=proj_shapes + mix_shapes + [jax.ShapeDtypeStruct(w_up.shape, BF16)],
        scratch_shapes=[pltpu.VMEM((FUSED_TM, D_MODEL), BF16)] + _state_scratch(),
        compiler_params=_params(2),
        name="inproj_mixer",
    )(x2, g, w_nat, w3, *mixer_args, w_up)


MIXER_PAIR = 2
PAIR_ROWS = MIXER_PAIR * BLOCK
ATT_PHASES_PER_MLSTM_PHASE = 3


def _mixer_kernel(*refs):
    (sink_ref, qt_ref, kc_ref, kp_ref, km_ref, vc_ref, vp_ref, vl_ref, bb_ref, bm_ref,
     p_ref, gate_ref, gb_ref, shift_ref, cw_ref, cb_ref, mhn_ref) = refs[:N_MIX_IN]
    state_in = refs[N_MIX_IN:N_MIX_IN + N_STATE]
    wo_ref, wd_ref = refs[N_MIX_IN + N_STATE:N_MIX_IN + N_STATE + 2]
    att_ref, hm_ref, wo_bf_ref, wd_bf_ref = refs[N_MIX_IN + N_STATE + 2:-N_STATE]
    state = refs[-N_STATE:]
    step = pl.program_id(0)

    @pl.when(step == 0)
    def _():
        _load_state(state, state_in)

    wo_bf_ref[...] = wo_ref[...].astype(BF16)
    wd_bf_ref[...] = wd_ref[...].astype(BF16)

    rows = lambda ref, k: ref.at[k * BLOCK:(k + 1) * BLOCK]
    cols = lambda ref, k: ref.at[:, k * BLOCK:(k + 1) * BLOCK]
    attention, chunks = [], []
    for k in range(MIXER_PAIR):
        n = MIXER_PAIR * step + 1 + k
        kp_k = kp_ref if k == 0 else rows(kc_ref, k - 1)
        vp_k = vp_ref if k == 0 else cols(vc_ref, k - 1)
        att_in = (sink_ref, cols(qt_ref, k), rows(kc_ref, k), kp_k, km_ref, cols(vc_ref, k), vp_k,
                  vl_ref, bb_ref, bm_ref.at[k:k + 1])
        attention.append(_attention_phases(n, att_in, cols(att_ref, k)))
        chunks.append((n, (rows(p_ref, k), rows(gate_ref, k), gb_ref, shift_ref, cw_ref, cb_ref,
                           mhn_ref), rows(hm_ref, k)))

    def mlstm_chain():
        for n, ml_in, hm_k in chunks:
            yield from _mlstm_phases(n, ml_in, hm_k, state)

    _drain(_interleave([mlstm_chain()] + attention, [1] + [ATT_PHASES_PER_MLSTM_PHASE] * MIXER_PAIR))


def _mixer(mixer_args, w_out, w_down, nblk):
    n_steps = nblk // MIXER_PAIR
    before = lambda s: jnp.maximum(MIXER_PAIR * s - 1, 0)
    v_row = ATT_WIDTH // ATT_KV_WIDTH
    in_specs = [
        pl.BlockSpec(memory_space=pltpu.SMEM),
        pl.BlockSpec((ATT_WIDTH, PAIR_ROWS), lambda s: (0, s)),
        pl.BlockSpec((PAIR_ROWS, ATT_KV_WIDTH), lambda s: (s, 0)),
        pl.BlockSpec((BLOCK, ATT_KV_WIDTH), lambda s: (before(s), 0)),
        pl.BlockSpec((N_META, ATT_KV_WIDTH), lambda s: (N_PAD // N_META, 0)),
        pl.BlockSpec((ATT_KV_WIDTH, PAIR_ROWS), lambda s: (v_row, s)),
        pl.BlockSpec((ATT_KV_WIDTH, BLOCK), lambda s: (v_row, before(s))),
        pl.BlockSpec((ATT_KV_WIDTH, BLOCK), lambda s: (v_row, 0)),
        _whole((ATT_KV_HEADS, 2 * BLOCK, GQ)),
        pl.BlockSpec((MIXER_PAIR, ATT_KV_HEADS, N_META, GQ), lambda s: (s, 0, 0, 0)),
        pl.BlockSpec((PAIR_ROWS, NAT_COLS), lambda s: (s, 0)),
        pl.BlockSpec((PAIR_ROWS, GATE_LANES), lambda s: (s, 0)),
        _whole((1, GATE_LANES)),
        _whole((CONV_WIDTH * BLOCK, 2 * BLOCK)),
        _whole((CONV_WIDTH, 2 * ML_QK_WIDTH)),
        _whole((1, 2 * ML_QK_WIDTH)),
        _whole((1, ML_WIDTH)),
    ] + [_whole(s.shape) for s in _state_shapes()]
    assert len(in_specs) == N_MIX_IN + N_STATE
    w_specs = [
        pl.BlockSpec((w_out.shape[0] // n_steps, D_MODEL), lambda s: (s, 0)),
        pl.BlockSpec((D_FF // n_steps, D_MODEL), lambda s: (s, 0)),
    ]
    return pl.pallas_call(
        _mixer_kernel,
        grid=(n_steps,),
        in_specs=in_specs + w_specs,
        out_specs=[
            pl.BlockSpec((ATT_WIDTH, PAIR_ROWS), lambda s: (0, s)),
            pl.BlockSpec((PAIR_ROWS, ML_WIDTH), lambda s: (s, 0)),
        ] + w_specs,
        out_shape=[
            jax.ShapeDtypeStruct((ATT_WIDTH, nblk * BLOCK), BF16),
            jax.ShapeDtypeStruct((nblk * BLOCK, ML_WIDTH), BF16),
        ] + [jax.ShapeDtypeStruct(w.shape, BF16) for w in (w_out, w_down)],
        scratch_shapes=_state_scratch(),
        compiler_params=_params(1),
        name="mixer",
    )(*mixer_args, w_out, w_down)


OUT_TM = 512


def _outproj_kernel(att0_ref, att1_ref, hm0_ref, hm1_ref, x_ref, wa_ref, wm_ref, g_ref,
                    h_ref, u_ref):
    i = pl.program_id(0)
    half = pl.num_programs(0) // 2

    def body(att_ref, hm_ref):
        h = (x_ref[...]
             + lax.dot_general(att_ref[...], wa_ref[...], CONTRACT_FIRST, preferred_element_type=F32)
             + jnp.dot(hm_ref[...], wm_ref[...], preferred_element_type=F32))
        h_ref[...] = h
        u_ref[...] = _rms(h, g_ref[...]).astype(BF16)

    @pl.when(i < half)
    def _():
        body(att0_ref, hm0_ref)

    @pl.when(i >= half)
    def _():
        body(att1_ref, hm1_ref)


def _outproj(att0, att1, hm0, hm1, x2, w_out, g):
    rows = x2.shape[0]
    n = rows // OUT_TM
    half = n // 2
    first = lambda i: jnp.minimum(i, half - 1)
    second = lambda i: jnp.maximum(i - half, 0)
    return pl.pallas_call(
        _outproj_kernel,
        grid=(n,),
        in_specs=[
            pl.BlockSpec((ATT_WIDTH, OUT_TM), lambda i: (0, first(i))),
            pl.BlockSpec((ATT_WIDTH, OUT_TM), lambda i: (0, second(i))),
            pl.BlockSpec((OUT_TM, ML_WIDTH), lambda i: (first(i), 0)),
            pl.BlockSpec((OUT_TM, ML_WIDTH), lambda i: (second(i), 0)),
            pl.BlockSpec((OUT_TM, D_MODEL), lambda i: (i, 0)),
            pl.BlockSpec((ATT_WIDTH, D_MODEL), lambda i: (0, 0)),
            pl.BlockSpec((ML_WIDTH, D_MODEL), lambda i: (ATT_WIDTH // ML_WIDTH, 0)),
            pl.BlockSpec((1, D_MODEL), lambda i: (0, 0)),
        ],
        out_specs=[
            pl.BlockSpec((OUT_TM, D_MODEL), lambda i: (i, 0)),
            pl.BlockSpec((OUT_TM, D_MODEL), lambda i: (i, 0)),
        ],
        out_shape=[
            jax.ShapeDtypeStruct((rows, D_MODEL), F32),
            jax.ShapeDtypeStruct((rows, D_MODEL), BF16),
        ],
        compiler_params=_params(1),
        name="outproj",
    )(att0, att1, hm0, hm1, x2, w_out, w_out, g)


MLP_TM = 1024
MLP_TF = 1024


def _mlp_step(mode, u_ref, wu_ref, wd_ref, o_ref, hbuf_ref=None, ssq_ref=None):
    acts = []
    for c in range(MLP_TF // MXU_COLS):
        cs = slice(c * MXU_COLS, (c + 1) * MXU_COLS)
        a = jnp.dot(u_ref[...], wu_ref[:, cs], preferred_element_type=F32)
        acts.append(jnp.square(jnp.maximum(a, 0.0)).astype(BF16))
        yield
    act = jnp.concatenate(acts, axis=1)
    for c in range(D_MODEL // MXU_COLS):
        cs = slice(c * MXU_COLS, (c + 1) * MXU_COLS)
        d = jnp.dot(act, wd_ref[:, cs], preferred_element_type=F32)
        if mode == "first":
            o_ref[:, cs] = d
        elif mode == "mid":
            o_ref[:, cs] += d
        else:
            v = o_ref[:, cs] + d + hbuf_ref[:, cs]
            o_ref[:, cs] = v
            sq = v * v
            part = sum(sq[:, k * LANE:(k + 1) * LANE] for k in range(MXU_COLS // LANE))
            if c == 0:
                ssq_ref[...] = part
            else:
                ssq_ref[...] += part
        yield


def _mlp_kernel(u_ref, h_hbm, wu_ref, wd_ref, g_ref, o_ref, hbuf_ref, ssq_ref, hsem):
    i = pl.program_id(0)
    j = pl.program_id(1)
    last = pl.num_programs(1) - 1
    r0 = pl.multiple_of(i * MLP_TM, MLP_TM)
    h_copy = pltpu.make_async_copy(h_hbm.at[pl.ds(r0, MLP_TM), :], hbuf_ref, hsem)

    @pl.when(j == 0)
    def _():
        h_copy.start()
        _drain(_mlp_step("first", u_ref, wu_ref, wd_ref, o_ref))

    @pl.when((j > 0) & (j < last))
    def _():
        _drain(_mlp_step("mid", u_ref, wu_ref, wd_ref, o_ref))

    @pl.when(j == last)
    def _():
        h_copy.wait()
        _drain(_mlp_step("last", u_ref, wu_ref, wd_ref, o_ref, hbuf_ref, ssq_ref))
        ms = jnp.sum(ssq_ref[...], axis=-1, keepdims=True) / D_MODEL
        o_ref[...] = o_ref[...] * lax.rsqrt(ms + EPS) * g_ref[...]


def _mlp(u, h, w_up, w_down, g):
    rows = u.shape[0]
    return pl.pallas_call(
        _mlp_kernel,
        grid=(rows // MLP_TM, D_FF // MLP_TF),
        in_specs=[
            pl.BlockSpec((MLP_TM, D_MODEL), lambda i, j: (i, 0)),
            pl.BlockSpec(memory_space=pl.ANY),
            pl.BlockSpec((D_MODEL, MLP_TF), lambda i, j: (0, j)),
            pl.BlockSpec((MLP_TF, D_MODEL), lambda i, j: (j, 0)),
            pl.BlockSpec((1, D_MODEL), lambda i, j: (0, 0)),
        ],
        out_specs=pl.BlockSpec((MLP_TM, D_MODEL), lambda i, j: (i, 0)),
        out_shape=jax.ShapeDtypeStruct((rows, D_MODEL), F32),
        scratch_shapes=[pltpu.VMEM((MLP_TM, D_MODEL), F32), pltpu.VMEM((MLP_TM, LANE), F32),
                        pltpu.SemaphoreType.DMA(())],
        compiler_params=_params(2),
        name="mlp",
    )(u, h, w_up, w_down, g)


def _t5_bucket(dist):
    max_exact = N_BUCKETS // 2
    d = jnp.maximum(dist, 0)
    ratio = jnp.maximum(d, max_exact).astype(F32) / max_exact
    large = max_exact + (jnp.log(ratio) / math.log(MAX_DISTANCE / max_exact)
                         * (N_BUCKETS - max_exact)).astype(jnp.int32)
    large = jnp.minimum(large, N_BUCKETS - 1)
    return jnp.where(d < max_exact, d, large)


BIAS_ROWS = 16


BIAS_BUCKETS_PER_PHASE = 8


def _bias_fill_splats(rb_ref, splat_ref):
    def fill(b, carry):
        for hd in range(ATT_HEADS):
            splat_ref[b * ATT_HEADS + hd] = jnp.full((SUBLANES, BLOCK), rb_ref[b, hd], F32) * LOG2E
        return carry

    lax.fori_loop(0, N_BUCKETS, fill, 0)


def _bias_lookup_phases(bkt, splat_ref, write):
    accs = [jnp.zeros(bkt.shape, F32)] * ATT_HEADS
    for b in range(N_BUCKETS):
        hit = (bkt >= b) & (bkt < b + 1)
        accs = [jnp.where(hit, jnp.concatenate([splat_ref[b * ATT_HEADS + hd]]
                                               * (BIAS_ROWS // SUBLANES), axis=0), a)
                for hd, a in enumerate(accs)]
        if b % BIAS_BUCKETS_PER_PHASE == BIAS_BUCKETS_PER_PHASE - 1:
            yield
    write(accs)


def _bias_step_phases(bkt_band_ref, bkt_meta_ref, band_ref, meta_ref, splat_ref):
    def write_band(tables):
        for hd, t in enumerate(tables):
            h, g = divmod(hd, ATT_GROUP)
            band_ref[h, :, g * BLOCK:(g + 1) * BLOCK] = t

    yield from _bias_lookup_phases(bkt_band_ref[...], splat_ref, write_band)
    for k in range(bkt_meta_ref.shape[0]):
        def write_meta(tables, k=k):
            for hd, t in enumerate(tables):
                h, g = divmod(hd, ATT_GROUP)
                meta_ref[k, h, :, g * BLOCK:(g + 1) * BLOCK] = t

        yield from _bias_lookup_phases(bkt_meta_ref[k], splat_ref, write_meta)


def _bias_buckets(nblk):
    c = jnp.arange(2 * BLOCK)[:, None]
    r = jnp.arange(BLOCK)[None, :]
    bkt_band = _t5_bucket(r + BLOCK - c).astype(jnp.int32)
    q_pos = BLOCK + jnp.arange(nblk * BLOCK)[None, :]
    m_pos = N_PAD + jnp.arange(N_META)[:, None]
    bkt_meta = _t5_bucket(q_pos - m_pos).astype(jnp.int32)
    return bkt_band, bkt_meta.reshape(N_META, nblk, BLOCK).transpose(1, 0, 2)


W_IN_OFFS = dict(q=0, k=1024, v=1280, mq=1536, mk=2048, mv=2560, mo=3584, gates=4608)
PREP_ROWS = 256
NAT_SRC = [W_IN_OFFS["mq"] // PREP_ROWS + t for t in range(NAT_COLS // PREP_ROWS)]
W3_SRC = ([W_IN_OFFS["q"] // PREP_ROWS + t for t in range(ATT_WIDTH // PREP_ROWS)]
          + [W_IN_OFFS["v"] // PREP_ROWS + t for t in range(ATT_KV_WIDTH // PREP_ROWS)]
          + [W_IN_OFFS["k"] // PREP_ROWS + t for t in range(ATT_KV_WIDTH // PREP_ROWS)]
          + [W_IN_OFFS["gates"] // PREP_ROWS])
N_NAT_BLK = len(NAT_SRC)
N_Q_BLK = ATT_WIDTH // PREP_ROWS
N_PREP = N_NAT_BLK + len(W3_SRC)
assert W_IN_OFFS["mq"] + NAT_COLS == W_IN_OFFS["gates"] and len(W3_SRC) * PREP_ROWS == W3_ROWS


def _win_prep_kernel(src_ref, w_ref, nat_ref, w3_ref):
    j = pl.program_id(0)
    wv = w_ref[0]

    @pl.when(j < N_NAT_BLK)
    def _():
        nat_ref[...] = wv.astype(BF16)

    @pl.when((j >= N_NAT_BLK) & (j < N_NAT_BLK + N_Q_BLK))
    def _():
        w3_ref[...] = (wv * (ATT_HEAD_DIM ** -0.5 * LOG2E)).astype(BF16)

    @pl.when((j >= N_NAT_BLK + N_Q_BLK) & (j < N_PREP - 1))
    def _():
        w3_ref[...] = wv.astype(BF16)

    @pl.when(j == N_PREP - 1)
    def _():
        row = lax.broadcasted_iota(jnp.int32, wv.shape, 0)
        w3_ref[...] = jnp.where(row < 2 * ML_HEADS, wv, 0.0).astype(BF16)


def _win_prep(w_in_t):
    src = jnp.asarray(NAT_SRC + W3_SRC, jnp.int32)
    d = w_in_t.shape[2]
    return pl.pallas_call(
        _win_prep_kernel,
        grid_spec=pltpu.PrefetchScalarGridSpec(
            num_scalar_prefetch=1,
            grid=(N_PREP,),
            in_specs=[pl.BlockSpec((1, PREP_ROWS, d), lambda j, src: (0, src[j], 0))],
            out_specs=[
                pl.BlockSpec((PREP_ROWS, d), lambda j, src: (jnp.minimum(j, N_NAT_BLK - 1), 0)),
                pl.BlockSpec((PREP_ROWS, d), lambda j, src: (jnp.maximum(j - N_NAT_BLK, 0), 0)),
            ],
        ),
        out_shape=[
            jax.ShapeDtypeStruct((NAT_COLS, d), BF16),
            jax.ShapeDtypeStruct((W3_ROWS, d), BF16),
        ],
        compiler_params=_params(1),
        name="w_in_prep",
    )(src, w_in_t)


def kernel(x, meta_tokens, w_in, conv_w, conv_b, b_igate, b_fgate, attn_sinks, rel_bias,
           mh_norm, w_out, norm_mix, norm_mlp, w_up, w_down, norm_final):
    batch, seq, d = x.shape
    assert batch == 2, "the two batches are processed one projection / mixer stage apart"
    nblk = seq // BLOCK
    x2 = x.reshape(batch * seq, d)
    lead = jnp.concatenate([jnp.zeros((N_PAD, d), x.dtype), meta_tokens.astype(x.dtype)], axis=0)
    g_mix = norm_mix[0][None].astype(F32)

    w_nat, w3 = _win_prep(jnp.swapaxes(w_in, 1, 2))
    gate_bias = jnp.pad(jnp.concatenate([b_igate[0], b_fgate[0]]).astype(F32),
                        (0, GATE_LANES - 2 * ML_HEADS))[None]

    proj0, proj_lead, (bias_band_t, bias_meta_t) = _inproj(
        x2, lead, g_mix, w_nat, w3, rel_bias, seq // IN_TM, nblk)
    consts = (attn_sinks[0].astype(F32), bias_band_t, bias_meta_t, gate_bias, _conv_shift_matrix(),
              conv_w[0].astype(F32), conv_b[0][None].astype(F32), mh_norm[0][None].astype(F32))
    state = _lead_state(proj_lead, consts)
    outs = _inproj_mixer(x2, seq // FUSED_TM, g_mix, w_nat, w3,
                         _mixer_args(consts, proj0, proj_lead, state), w_up[0], nblk)
    proj1, (att0, hm0, w_up_bf) = outs[:N_PROJ_OUT], outs[N_PROJ_OUT:]
    att1, hm1, w_out_bf, w_down_bf = _mixer(_mixer_args(consts, proj1, proj_lead, state),
                                            w_out[0], w_down[0], nblk)

    h2, u2 = _outproj(att0, att1, hm0, hm1, x2, w_out_bf, norm_mlp[0][None].astype(F32))
    out = _mlp(u2, h2, w_up_bf, w_down_bf, norm_final[None].astype(F32))
    return out.reshape(batch, seq, d)
```

```python
import math

import numpy as np
import jax
import jax.numpy as jnp
from jax import lax
from jax.experimental import pallas as pl
from jax.experimental.pallas import tpu as pltpu

F32 = jnp.float32
BF16 = jnp.bfloat16

D_MODEL = 2048
N_META = 16
BLOCK = 128
N_PAD = BLOCK - N_META
WINDOW = 128
ATT_HEADS = 16
ATT_KV_HEADS = 4
ATT_GROUP = ATT_HEADS // ATT_KV_HEADS
ATT_HEAD_DIM = 64
ATT_WIDTH = ATT_HEADS * ATT_HEAD_DIM
ATT_KV_WIDTH = ATT_KV_HEADS * ATT_HEAD_DIM
ML_HEADS = 4
ML_V_DIM = 256
ML_QK_DIM = 128
ML_WIDTH = ML_HEADS * ML_V_DIM
ML_QK_WIDTH = ML_HEADS * ML_QK_DIM
CONV_WIDTH = 4
GATE_SOFTCAP = 15.0
D_FF = 4 * D_MODEL
N_BUCKETS = 32
MAX_DISTANCE = 128
EPS = 1e-6
NEG = -1e30
LOG2E = math.log2(math.e)

NAT_MQK = 0
NAT_MV = 2 * ML_QK_WIDTH
NAT_MO = NAT_MV + ML_WIDTH
NAT_COLS = NAT_MO + ML_WIDTH
T_ROWS = ATT_WIDTH + ATT_KV_WIDTH
GATE_LANES = 128
W3_K = T_ROWS
W3_G = W3_K + ATT_KV_WIDTH
W3_ROWS = W3_G + 256

V7X_VMEM_LIMIT = 60 * 1024 * 1024
LANE = 128
SUBLANES = 8
CONTRACT_LAST = (((1,), (1,)), ((), ()))
CONTRACT_FIRST = (((0,), (0,)), ((), ()))


def _params(n_axes, vmem=V7X_VMEM_LIMIT):
    return pltpu.CompilerParams(dimension_semantics=("arbitrary",) * n_axes,
                                vmem_limit_bytes=vmem)


def _rms(x, g):
    return x * lax.rsqrt(jnp.mean(x * x, axis=-1, keepdims=True) + EPS) * g


_DONE = object()


def _drain(phases):
    for _ in phases:
        pass


def _interleave(streams, weights):
    live = list(zip(streams, weights))
    while live:
        for item in list(live):
            stream, weight = item
            for _ in range(weight):
                if next(stream, _DONE) is _DONE:
                    live.remove(item)
                    break
        yield


IN_TM = 1024
IN_NAT_STEPS = 3
IN_TN = NAT_COLS // IN_NAT_STEPS
MXU_COLS = 256
N_PROJ_OUT = 4


def _inproj_nat_phases(rows, u_ref, wn_ref, nat_ref):
    u = u_ref[0:rows, :]
    for c0 in range(0, IN_TN, MXU_COLS):
        cs = slice(c0, c0 + MXU_COLS)
        nat_ref[0:rows, cs] = lax.dot_general(u, wn_ref[cs, :], CONTRACT_LAST,
                                              preferred_element_type=F32).astype(BF16)
        yield


def _inproj_last_phases(rows, u_ref, w3_ref, kn_ref, t_ref, gt_ref):
    u = u_ref[0:rows, :]
    for c0 in range(0, T_ROWS, MXU_COLS):
        cs = slice(c0, c0 + MXU_COLS)
        t_ref[cs, 0:rows] = lax.dot_general(w3_ref[cs, :], u, CONTRACT_LAST,
                                            preferred_element_type=F32).astype(BF16)
        yield
    kn_ref[0:rows, :] = lax.dot_general(u, w3_ref[W3_K:W3_G, :], CONTRACT_LAST,
                                        preferred_element_type=F32).astype(BF16)
    yield
    gt_ref[0:rows, :] = lax.dot_general(u, w3_ref[W3_G:W3_G + GATE_LANES, :], CONTRACT_LAST,
                                        preferred_element_type=F32)
    yield


BIAS_PHASES_PER_ROUND = 3


def _inproj_kernel(x_ref, lead_ref, g_ref, wn_ref, w3_ref, rb_ref, bkt_band_ref, bkt_meta_ref,
                   nat_ref, kn_ref, t_ref, gt_ref, nat_lead_ref, kn_lead_ref, t_lead_ref, gt_lead_ref,
                   band_ref, meta_ref, u_ref, u_lead_ref, splat_ref):
    i = pl.program_id(0)
    j = pl.program_id(1)

    @pl.when(j == 0)
    def _():
        u_ref[...] = _rms(x_ref[...], g_ref[...]).astype(BF16)

    @pl.when((j == 0) & (i == 0))
    def _():
        u_lead_ref[...] = _rms(lead_ref[...], g_ref[...]).astype(BF16)
        _bias_fill_splats(rb_ref, splat_ref)

    def with_bias(phases):
        bias = _bias_step_phases(bkt_band_ref, bkt_meta_ref, band_ref, meta_ref, splat_ref)
        _drain(_interleave([phases, bias], [1, BIAS_PHASES_PER_ROUND]))

    @pl.when(j < IN_NAT_STEPS)
    def _():
        with_bias(_inproj_nat_phases(IN_TM, u_ref, wn_ref, nat_ref))

    @pl.when(j == IN_NAT_STEPS)
    def _():
        with_bias(_inproj_last_phases(IN_TM, u_ref, w3_ref, kn_ref, t_ref, gt_ref))

    @pl.when((i == 0) & (j < IN_NAT_STEPS))
    def _():
        _drain(_inproj_nat_phases(BLOCK, u_lead_ref, wn_ref, nat_lead_ref))

    @pl.when((i == 0) & (j == IN_NAT_STEPS))
    def _():
        _drain(_inproj_last_phases(BLOCK, u_lead_ref, w3_ref, kn_lead_ref, t_lead_ref, gt_lead_ref))


def _inproj_weight_specs():
    last_nat = IN_NAT_STEPS - 1
    return [
        pl.BlockSpec((IN_TN, D_MODEL), lambda i, j: (jnp.minimum(j, last_nat), 0)),
        pl.BlockSpec((W3_ROWS, D_MODEL), lambda i, j: (0, 0)),
    ]


def _inproj_out(tm, rows):
    last_nat = IN_NAT_STEPS - 1
    specs = [
        pl.BlockSpec((tm, IN_TN), lambda i, j: (i, jnp.minimum(j, last_nat))),
        pl.BlockSpec((tm, ATT_KV_WIDTH), lambda i, j: (i, 0)),
        pl.BlockSpec((T_ROWS, tm), lambda i, j: (0, i)),
        pl.BlockSpec((tm, GATE_LANES), lambda i, j: (i, 0)),
    ]
    shapes = [
        jax.ShapeDtypeStruct((rows, NAT_COLS), BF16),
        jax.ShapeDtypeStruct((rows, ATT_KV_WIDTH), BF16),
        jax.ShapeDtypeStruct((T_ROWS, rows), BF16),
        jax.ShapeDtypeStruct((rows, GATE_LANES), F32),
    ]
    return specs, shapes


def _inproj(x2, lead, g, w_nat, w3, rel_bias, n_tiles, nblk):
    out_specs, out_shape = _inproj_out(IN_TM, n_tiles * IN_TM)
    last_nat = IN_NAT_STEPS - 1
    n_steps = n_tiles * (IN_NAT_STEPS + 1)
    assert n_steps * BIAS_ROWS == 2 * BLOCK and nblk % n_steps == 0
    meta_per_step = nblk // n_steps
    bkt_band, bkt_meta = _bias_buckets(nblk)
    step = lambda i, j: i * (IN_NAT_STEPS + 1) + j
    bias_in_specs = [
        pl.BlockSpec(memory_space=pltpu.SMEM),
        pl.BlockSpec((BIAS_ROWS, BLOCK), lambda i, j: (step(i, j), 0)),
        pl.BlockSpec((meta_per_step, N_META, BLOCK), lambda i, j: (step(i, j), 0, 0)),
    ]
    bias_out_specs = [
        pl.BlockSpec((ATT_KV_HEADS, BIAS_ROWS, GQ), lambda i, j: (0, step(i, j), 0)),
        pl.BlockSpec((meta_per_step, ATT_KV_HEADS, N_META, GQ), lambda i, j: (step(i, j), 0, 0, 0)),
    ]
    bias_shapes = [
        jax.ShapeDtypeStruct((ATT_KV_HEADS, 2 * BLOCK, GQ), F32),
        jax.ShapeDtypeStruct((nblk, ATT_KV_HEADS, N_META, GQ), F32),
    ]
    lead_col = lambda i, j: jnp.where(i == 0, jnp.minimum(j, last_nat), last_nat)
    lead_specs = [
        pl.BlockSpec((BLOCK, IN_TN), lambda i, j: (0, lead_col(i, j))),
        pl.BlockSpec((BLOCK, ATT_KV_WIDTH), lambda i, j: (0, 0)),
        pl.BlockSpec((T_ROWS, BLOCK), lambda i, j: (0, 0)),
        pl.BlockSpec((BLOCK, GATE_LANES), lambda i, j: (0, 0)),
    ]
    lead_shapes = [
        jax.ShapeDtypeStruct((BLOCK, NAT_COLS), BF16),
        jax.ShapeDtypeStruct((BLOCK, ATT_KV_WIDTH), BF16),
        jax.ShapeDtypeStruct((T_ROWS, BLOCK), BF16),
        jax.ShapeDtypeStruct((BLOCK, GATE_LANES), F32),
    ]
    outs = pl.pallas_call(
        _inproj_kernel,
        grid=(n_tiles, IN_NAT_STEPS + 1),
        in_specs=[
            pl.BlockSpec((IN_TM, D_MODEL), lambda i, j: (i, 0)),
            pl.BlockSpec((BLOCK, D_MODEL), lambda i, j: (0, 0)),
            pl.BlockSpec((1, D_MODEL), lambda i, j: (0, 0)),
        ] + _inproj_weight_specs() + bias_in_specs,
        out_specs=out_specs + lead_specs + bias_out_specs,
        out_shape=out_shape + lead_shapes + bias_shapes,
        scratch_shapes=[pltpu.VMEM((IN_TM, D_MODEL), BF16), pltpu.VMEM((BLOCK, D_MODEL), BF16),
                        pltpu.VMEM((N_BUCKETS * ATT_HEADS, SUBLANES, BLOCK), F32)],
        compiler_params=_params(2),
        name="inproj",
    )(x2, lead, g, w_nat, w3, rel_bias.astype(F32), bkt_band, bkt_meta)
    return outs[:N_PROJ_OUT], outs[N_PROJ_OUT:2 * N_PROJ_OUT], outs[2 * N_PROJ_OUT:]


GQ = ATT_GROUP * BLOCK
ATT_SLABS_PER_PHASE = 2


def _attn_prologue(n, kc_ref, kp_ref, km_ref):
    c = lax.broadcasted_iota(jnp.int32, (2 * BLOCK, BLOCK), 0)
    r = lax.broadcasted_iota(jnp.int32, (2 * BLOCK, BLOCK), 1)
    dist = r + BLOCK - c
    band_ok = (dist >= 0) & (dist < WINDOW) & ((c >= BLOCK) | (n >= 1))
    mm = lax.broadcasted_iota(jnp.int32, (N_META, BLOCK), 0)
    rr = lax.broadcasted_iota(jnp.int32, (N_META, BLOCK), 1)
    meta_ok = (n + 1) * BLOCK + rr >= N_PAD + mm
    kall = jnp.concatenate([kp_ref[...], kc_ref[...], km_ref[...]], axis=0)
    return band_ok, meta_ok, kall


BF16_ROWS = 16


def _ones_row(width):
    row = lax.broadcasted_iota(jnp.int32, (BF16_ROWS, width), 0)
    return jnp.where(row == 0, 1.0, 0.0).astype(BF16)


def _attn_head(h, pro, sink_ref, qt_ref, vc_ref, vp_ref, vl_ref, bb_ref, bm_ref, o_ref):
    band_ok, meta_ok, kall = pro
    lo = (h // 2) * LANE
    kpair = kall[:, lo:lo + LANE]
    qh = jnp.concatenate(
        [qt_ref[(ATT_GROUP * h + g) * ATT_HEAD_DIM:(ATT_GROUP * h + g + 1) * ATT_HEAD_DIM, :]
         for g in range(ATT_GROUP)], axis=1)
    zq = jnp.zeros_like(qh)
    qz = jnp.concatenate([qh, zq] if h % 2 == 0 else [zq, qh], axis=0)
    st = jnp.dot(kpair, qz, preferred_element_type=F32)
    yield
    zpad = jnp.zeros((N_PAD, BLOCK), BF16)
    p_band, p_lead, sink_p = [], [], []
    for g in range(ATT_GROUP):
        gs = slice(g * BLOCK, (g + 1) * BLOCK)
        sl = st[:, gs]
        sink = jnp.full((1, BLOCK), sink_ref[ATT_GROUP * h + g], F32) * LOG2E
        lb = jnp.where(band_ok, sl[:2 * BLOCK] + bb_ref[h, :, gs], NEG * LOG2E)
        lm = jnp.where(meta_ok, sl[2 * BLOCK:] + bm_ref[0, h, :, gs], NEG * LOG2E)
        m = jnp.maximum(jnp.max(lb, axis=0, keepdims=True), jnp.max(lm, axis=0, keepdims=True))
        m = jnp.maximum(m, sink)
        pb = jnp.exp2(lb - m)
        pm = jnp.exp2(lm - m)
        sink_p.append(jnp.exp2(sink - m))
        p_band.append(pb.astype(BF16))
        p_lead.append(jnp.concatenate([zpad, pm.astype(BF16)], axis=0))
        if g % ATT_SLABS_PER_PHASE == ATT_SLABS_PER_PHASE - 1:
            yield
    pbt = jnp.concatenate(p_band, axis=1)
    plt = jnp.concatenate(p_lead, axis=1)
    hs = slice(h * ATT_HEAD_DIM, (h + 1) * ATT_HEAD_DIM)
    vband = jnp.concatenate([vp_ref[hs, :], vc_ref[hs, :]], axis=1)
    ones_b = _ones_row(2 * BLOCK)
    ones_l = _ones_row(BLOCK)
    ot = (jnp.dot(jnp.concatenate([vband, ones_b], axis=0), pbt, preferred_element_type=F32)
          + jnp.dot(jnp.concatenate([vl_ref[hs, :], ones_l], axis=0), plt,
                    preferred_element_type=F32))
    for g in range(ATT_GROUP):
        gs = slice(g * BLOCK, (g + 1) * BLOCK)
        den = ot[ATT_HEAD_DIM:ATT_HEAD_DIM + 1, gs] + sink_p[g]
        row = (ATT_GROUP * h + g) * ATT_HEAD_DIM
        o_ref[row:row + ATT_HEAD_DIM, :] = (ot[:ATT_HEAD_DIM, gs] * (1.0 / den)).astype(BF16)


def _conv_shift_matrix():
    s = np.zeros((CONV_WIDTH * BLOCK, 2 * BLOCK), np.float32)
    for k in range(CONV_WIDTH):
        t = np.arange(BLOCK)
        s[k * BLOCK + t, BLOCK + t - k] = 1.0
    return jnp.asarray(s, BF16)


def _mlstm_gates(n, gate_ref, gb_ref):
    t_idx = lax.broadcasted_iota(jnp.int32, (BLOCK, GATE_LANES), 0)
    valid = (n > 0) | (t_idx >= N_PAD)
    pre = GATE_SOFTCAP * jnp.tanh((gate_ref[...] + gb_ref[...]) / GATE_SOFTCAP)
    log_i = jnp.where(valid, pre, NEG)
    log_sig = jnp.minimum(pre, 0.0) - jnp.log1p(jnp.exp(-jnp.abs(pre)))
    log_f = jnp.where(valid, log_sig, 0.0)
    row = lax.broadcasted_iota(jnp.int32, (BLOCK, BLOCK), 0)
    col = lax.broadcasted_iota(jnp.int32, (BLOCK, BLOCK), 1)
    causal = col <= row
    tril = jnp.where(causal, 1.0, 0.0).astype(F32)
    b_all = jnp.dot(tril, log_f, preferred_element_type=F32,
                    precision=lax.Precision.HIGHEST)
    return log_i, b_all, log_i.T, b_all.T, causal


def _mlstm_conv(p_ref, shift_ref, cw_ref, cb_ref, xprev_ref):
    xcur = p_ref[:, NAT_MQK:NAT_MV]
    xcat = jnp.concatenate([xprev_ref[...], xcur], axis=0)
    sh = jnp.dot(shift_ref[...], xcat, preferred_element_type=F32)
    xprev_ref[...] = xcur
    y = cb_ref[...]
    for k in range(CONV_WIDTH):
        y = y + cw_ref[CONV_WIDTH - 1 - k:CONV_WIDTH - k, :] * sh[k * BLOCK:(k + 1) * BLOCK]
    act = y * jax.nn.sigmoid(y)
    return act[:, :ML_QK_WIDTH] * (ML_QK_DIM ** -0.5), act[:, ML_QK_WIDTH:]


def _mlstm_head(h, gates, q_all, k_all, p_ref, mhn_ref, o_ref, c_ref, n_ref, m_ref):
    log_i, b_all, log_i_t, b_all_t, causal = gates
    q_f = q_all[:, h * ML_QK_DIM:(h + 1) * ML_QK_DIM]
    k_f = k_all[:, h * ML_QK_DIM:(h + 1) * ML_QK_DIM]
    q_h = q_f.astype(BF16)
    k_h = k_f.astype(BF16)
    v_h = p_ref[:, NAT_MV + h * ML_V_DIM:NAT_MV + (h + 1) * ML_V_DIM]
    b_col = b_all[:, ML_HEADS + h:ML_HEADS + h + 1]
    li_col = log_i[:, h:h + 1]
    b_row = b_all_t[ML_HEADS + h:ML_HEADS + h + 1, :]
    li_row = log_i_t[h:h + 1, :]
    b_last = b_all[BLOCK - 1:BLOCK, ML_HEADS + h:ML_HEADS + h + 1]
    c_prev = c_ref[h]
    n_prev = n_ref[h][0:1, :]
    m_prev = m_ref[h][0:1, 0:1]

    dmat = jnp.where(causal, b_col - b_row + li_row, -jnp.inf)
    inter = b_col + m_prev
    m_t = jnp.maximum(inter, jnp.max(dmat, axis=1, keepdims=True))
    s = lax.dot_general(q_h, k_h, CONTRACT_LAST, preferred_element_type=F32) * jnp.exp(dmat - m_t)
    yield
    a_t = jnp.exp(inter - m_t)
    num = (jnp.dot(s.astype(BF16), v_h, preferred_element_type=F32)
           + a_t * jnp.dot(q_h, c_prev.astype(BF16), preferred_element_type=F32))
    den = (jnp.sum(s, axis=1, keepdims=True)
           + a_t * jnp.sum(q_f * n_prev, axis=1, keepdims=True))
    den = jnp.maximum(jnp.abs(den), jnp.exp(-m_t))
    hh = num / den
    hh = hh * lax.rsqrt(jnp.mean(hh * hh, axis=-1, keepdims=True) + EPS)
    vs = slice(h * ML_V_DIM, (h + 1) * ML_V_DIM)
    hh = hh * mhn_ref[:, vs]
    m_o = p_ref[:, NAT_MO + h * ML_V_DIM:NAT_MO + (h + 1) * ML_V_DIM].astype(F32)
    o_ref[:, vs] = (hh * jax.nn.sigmoid(m_o)).astype(BF16)
    yield

    g_col = b_last - b_col + li_col
    m_loc = jnp.max(g_col, axis=0, keepdims=True)
    wk = jnp.exp(g_col - m_loc) * k_f
    c_loc = lax.dot_general(wk.astype(BF16), v_h, CONTRACT_FIRST,
                            preferred_element_type=F32)
    n_loc = jnp.sum(wk, axis=0, keepdims=True)
    m_new = jnp.maximum(b_last + m_prev, m_loc)
    a = jnp.exp(b_last + m_prev - m_new)
    cc = jnp.exp(m_loc - m_new)
    c_ref[h] = a * c_prev + cc * c_loc
    n_ref[h] = jnp.broadcast_to(a * n_prev + cc * n_loc, (8, ML_QK_DIM))
    m_ref[h] = jnp.broadcast_to(m_new, (8, LANE))


N_MIX_IN = 17
N_STATE = 4
MIXER_PHASES_PER_NAT_PHASE = 6
MIXER_PHASES_PER_LAST_PHASE = 3


N_ATT_IN = 10


def _attention_phases(n, att_in, att_ref):
    sink_ref, qt_ref, kc_ref, kp_ref, km_ref, vc_ref, vp_ref, vl_ref, bb_ref, bm_ref = att_in
    pro = _attn_prologue(n - 1, kc_ref, kp_ref, km_ref)
    for h in range(ATT_KV_HEADS):
        yield from _attn_head(h, pro, sink_ref, qt_ref, vc_ref, vp_ref, vl_ref, bb_ref, bm_ref, att_ref)


def _mlstm_phases(n, ml_in, hm_ref, state):
    p_ref, gate_ref, gb_ref, shift_ref, cw_ref, cb_ref, mhn_ref = ml_in
    c_ref, n_ref, m_ref, xprev_ref = state
    gates = _mlstm_gates(n, gate_ref, gb_ref)
    yield
    q_all, k_all = _mlstm_conv(p_ref, shift_ref, cw_ref, cb_ref, xprev_ref)
    yield
    for h in range(ML_HEADS):
        yield from _mlstm_head(h, gates, q_all, k_all, p_ref, mhn_ref, hm_ref, c_ref, n_ref, m_ref)


def _mixer_phases(n, mix_in, att_ref, hm_ref, state):
    return _interleave([_mlstm_phases(n, mix_in[N_ATT_IN:], hm_ref, state),
                        _attention_phases(n, mix_in[:N_ATT_IN], att_ref)], [1, 1])


def _load_state(state, state_in):
    for dst, src in zip(state, state_in):
        dst[...] = src[...]


def _state_shapes():
    return [
        jax.ShapeDtypeStruct((ML_HEADS, ML_QK_DIM, ML_V_DIM), F32),
        jax.ShapeDtypeStruct((ML_HEADS, SUBLANES, ML_QK_DIM), F32),
        jax.ShapeDtypeStruct((ML_HEADS, SUBLANES, LANE), F32),
        jax.ShapeDtypeStruct((BLOCK, 2 * ML_QK_WIDTH), BF16),
    ]


def _state_scratch():
    return [pltpu.VMEM(s.shape, s.dtype) for s in _state_shapes()]


def _whole(shape):
    return pl.BlockSpec(shape, lambda *ids: (0,) * len(shape))


def _mixer_in_specs(step):
    cur = lambda *ids: step(*ids)
    prev = lambda *ids: jnp.maximum(step(*ids) - 1, 0)
    v_row = ATT_WIDTH // ATT_KV_WIDTH
    meta_rows = N_PAD // N_META
    return [
        pl.BlockSpec(memory_space=pltpu.SMEM),
        pl.BlockSpec((ATT_WIDTH, BLOCK), lambda *ids: (0, cur(*ids))),
        pl.BlockSpec((BLOCK, ATT_KV_WIDTH), lambda *ids: (cur(*ids), 0)),
        pl.BlockSpec((BLOCK, ATT_KV_WIDTH), lambda *ids: (prev(*ids), 0)),
        pl.BlockSpec((N_META, ATT_KV_WIDTH), lambda *ids: (meta_rows, 0)),
        pl.BlockSpec((ATT_KV_WIDTH, BLOCK), lambda *ids: (v_row, cur(*ids))),
        pl.BlockSpec((ATT_KV_WIDTH, BLOCK), lambda *ids: (v_row, prev(*ids))),
        pl.BlockSpec((ATT_KV_WIDTH, BLOCK), lambda *ids: (v_row, 0)),
        _whole((ATT_KV_HEADS, 2 * BLOCK, GQ)),
        pl.BlockSpec((1, ATT_KV_HEADS, N_META, GQ), lambda *ids: (cur(*ids), 0, 0, 0)),
        pl.BlockSpec((BLOCK, NAT_COLS), lambda *ids: (cur(*ids), 0)),
        pl.BlockSpec((BLOCK, GATE_LANES), lambda *ids: (cur(*ids), 0)),
        _whole((1, GATE_LANES)),
        _whole((CONV_WIDTH * BLOCK, 2 * BLOCK)),
        _whole((CONV_WIDTH, 2 * ML_QK_WIDTH)),
        _whole((1, 2 * ML_QK_WIDTH)),
        _whole((1, ML_WIDTH)),
    ] + [_whole(s.shape) for s in _state_shapes()]


def _mixer_args(consts, proj_b, proj_lead, state):
    sinks, bias_band_t, bias_meta_t, gate_bias, shift, conv_w, conv_b, mh_norm = consts
    nat_b, kn_b, projt_b, gt_b = proj_b
    _, kn_lead, projt_lead, _ = proj_lead
    return [sinks, projt_b, kn_b, kn_b, kn_lead, projt_b, projt_b, projt_lead, bias_band_t,
            bias_meta_t, nat_b, gt_b, gate_bias, shift, conv_w, conv_b, mh_norm] + list(state)


def _mixer_out(step, n_rows):
    specs = [
        pl.BlockSpec((ATT_WIDTH, BLOCK), lambda *ids: (0, step(*ids))),
        pl.BlockSpec((BLOCK, ML_WIDTH), lambda *ids: (step(*ids), 0)),
    ]
    shapes = [
        jax.ShapeDtypeStruct((ATT_WIDTH, n_rows), BF16),
        jax.ShapeDtypeStruct((n_rows, ML_WIDTH), BF16),
    ]
    return specs, shapes


def _lead_state_kernel(p_ref, gate_ref, gb_ref, shift_ref, cw_ref, cb_ref, mhn_ref,
                       c_ref, n_ref, m_ref, xprev_ref, hm_ref):
    for ref in (c_ref, n_ref, m_ref, xprev_ref):
        ref[...] = jnp.zeros_like(ref)
    gates = _mlstm_gates(0, gate_ref, gb_ref)
    q_all, k_all = _mlstm_conv(p_ref, shift_ref, cw_ref, cb_ref, xprev_ref)
    for h in range(ML_HEADS):
        _drain(_mlstm_head(h, gates, q_all, k_all, p_ref, mhn_ref, hm_ref, c_ref, n_ref, m_ref))


def _lead_state(proj_lead, consts):
    nat_lead, _, _, gates_lead = proj_lead
    _, _, _, gate_bias, shift, conv_w, conv_b, mh_norm = consts
    shapes = _state_shapes()
    return pl.pallas_call(
        _lead_state_kernel,
        grid=(1,),
        in_specs=[
            _whole((BLOCK, NAT_COLS)),
            _whole((BLOCK, GATE_LANES)),
            _whole((1, GATE_LANES)),
            _whole((CONV_WIDTH * BLOCK, 2 * BLOCK)),
            _whole((CONV_WIDTH, 2 * ML_QK_WIDTH)),
            _whole((1, 2 * ML_QK_WIDTH)),
            _whole((1, ML_WIDTH)),
        ],
        out_specs=[_whole(s.shape) for s in shapes],
        out_shape=shapes,
        scratch_shapes=[pltpu.VMEM((BLOCK, ML_WIDTH), BF16)],
        compiler_params=_params(1),
        name="lead_state",
    )(nat_lead, gates_lead, gate_bias, shift, conv_w, conv_b, mh_norm)


FUSED_TM = 512


def _inproj_mixer_kernel(*refs):
    x_ref, g_ref, wn_ref, w3_ref = refs[:4]
    mix_in = refs[4:4 + N_MIX_IN]
    state_in = refs[4 + N_MIX_IN:4 + N_MIX_IN + N_STATE]
    wu_ref = refs[4 + N_MIX_IN + N_STATE]
    (nat_ref, kn_ref, t_ref, gt_ref, att_ref, hm_ref,
     wu_bf_ref) = refs[5 + N_MIX_IN + N_STATE:-(1 + N_STATE)]
    u_ref = refs[-(1 + N_STATE)]
    state = refs[-N_STATE:]
    i = pl.program_id(0)
    j = pl.program_id(1)
    step = i * (IN_NAT_STEPS + 1) + j
    wu_bf_ref[...] = wu_ref[...].astype(BF16)

    @pl.when(j == 0)
    def _():
        u_ref[...] = _rms(x_ref[...], g_ref[...]).astype(BF16)

    @pl.when(step == 0)
    def _():
        _load_state(state, state_in)

    @pl.when(j < IN_NAT_STEPS)
    def _():
        _drain(_interleave([_inproj_nat_phases(FUSED_TM, u_ref, wn_ref, nat_ref),
                            _mixer_phases(step + 1, mix_in, att_ref, hm_ref, state)],
                           [1, MIXER_PHASES_PER_NAT_PHASE]))

    @pl.when(j == IN_NAT_STEPS)
    def _():
        _drain(_interleave([_inproj_last_phases(FUSED_TM, u_ref, w3_ref, kn_ref, t_ref, gt_ref),
                            _mixer_phases(step + 1, mix_in, att_ref, hm_ref, state)],
                           [1, MIXER_PHASES_PER_LAST_PHASE]))


def _inproj_mixer(x2, row_tile0, g, w_nat, w3, mixer_args, w_up, nblk):
    n_tiles = nblk * BLOCK // FUSED_TM
    assert n_tiles * (IN_NAT_STEPS + 1) == nblk
    step = lambda i, j: i * (IN_NAT_STEPS + 1) + j
    proj_specs, proj_shapes = _inproj_out(FUSED_TM, n_tiles * FUSED_TM)
    mix_specs, mix_shapes = _mixer_out(step, nblk * BLOCK)
    wu_spec = pl.BlockSpec((D_MODEL, D_FF // nblk), lambda i, j: (0, step(i, j)))
    return pl.pallas_call(
        _inproj_mixer_kernel,
        grid=(n_tiles, IN_NAT_STEPS + 1),
        in_specs=[
            pl.BlockSpec((FUSED_TM, D_MODEL), lambda i, j: (row_tile0 + i, 0)),
            pl.BlockSpec((1, D_MODEL), lambda i, j: (0, 0)),
        ] + _inproj_weight_specs() + _mixer_in_specs(step) + [wu_spec],
        out_specs=proj_specs + mix_specs + [wu_spec],
        out_shape=proj_shapes + mix_shapes + [jax.ShapeDtypeStruct(w_up.shape, BF16)],
        scratch_shapes=[pltpu.VMEM((FUSED_TM, D_MODEL), BF16)] + _state_scratch(),
        compiler_params=_params(2),
        name="inproj_mixer",
    )(x2, g, w_nat, w3, *mixer_args, w_up)


MIXER_PAIR = 2
PAIR_ROWS = MIXER_PAIR * BLOCK
ATT_PHASES_PER_MLSTM_PHASE = 4
MLSTM_CHUNK_LAG = 3


def _mixer_kernel(*refs):
    (sink_ref, qt_ref, kc_ref, kp_ref, km_ref, vc_ref, vp_ref, vl_ref, bb_ref, bm_ref,
     p_ref, gate_ref, gb_ref, shift_ref, cw_ref, cb_ref, mhn_ref) = refs[:N_MIX_IN]
    state_in = refs[N_MIX_IN:N_MIX_IN + N_STATE]
    wo_ref, wd_ref = refs[N_MIX_IN + N_STATE:N_MIX_IN + N_STATE + 2]
    att_ref, hm_ref, wo_bf_ref, wd_bf_ref = refs[N_MIX_IN + N_STATE + 2:-N_STATE]
    state = refs[-N_STATE:]
    step = pl.program_id(0)

    @pl.when(step == 0)
    def _():
        _load_state(state, state_in)

    wo_bf_ref[...] = wo_ref[...].astype(BF16)
    wd_bf_ref[...] = wd_ref[...].astype(BF16)

    rows = lambda ref, k: ref.at[k * BLOCK:(k + 1) * BLOCK]
    cols = lambda ref, k: ref.at[:, k * BLOCK:(k + 1) * BLOCK]
    attention, chunks = [], []
    for k in range(MIXER_PAIR):
        n = MIXER_PAIR * step + 1 + k
        kp_k = kp_ref if k == 0 else rows(kc_ref, k - 1)
        vp_k = vp_ref if k == 0 else cols(vc_ref, k - 1)
        att_in = (sink_ref, cols(qt_ref, k), rows(kc_ref, k), kp_k, km_ref, cols(vc_ref, k), vp_k,
                  vl_ref, bb_ref, bm_ref.at[k:k + 1])
        attention.append(_attention_phases(n, att_in, cols(att_ref, k)))
        chunks.append((n, (rows(p_ref, k), rows(gate_ref, k), gb_ref, shift_ref, cw_ref, cb_ref,
                           mhn_ref), rows(hm_ref, k)))

    def mlstm_chunk(k):
        for _ in range(k * MLSTM_CHUNK_LAG):
            yield
        n, ml_in, hm_k = chunks[k]
        yield from _mlstm_phases(n, ml_in, hm_k, state)

    _drain(_interleave([mlstm_chunk(k) for k in range(MIXER_PAIR)] + attention,
                       [1] * MIXER_PAIR + [ATT_PHASES_PER_MLSTM_PHASE] * MIXER_PAIR))


def _mixer(mixer_args, w_out, w_down, nblk):
    n_steps = nblk // MIXER_PAIR
    before = lambda s: jnp.maximum(MIXER_PAIR * s - 1, 0)
    v_row = ATT_WIDTH // ATT_KV_WIDTH
    in_specs = [
        pl.BlockSpec(memory_space=pltpu.SMEM),
        pl.BlockSpec((ATT_WIDTH, PAIR_ROWS), lambda s: (0, s)),
        pl.BlockSpec((PAIR_ROWS, ATT_KV_WIDTH), lambda s: (s, 0)),
        pl.BlockSpec((BLOCK, ATT_KV_WIDTH), lambda s: (before(s), 0)),
        pl.BlockSpec((N_META, ATT_KV_WIDTH), lambda s: (N_PAD // N_META, 0)),
        pl.BlockSpec((ATT_KV_WIDTH, PAIR_ROWS), lambda s: (v_row, s)),
        pl.BlockSpec((ATT_KV_WIDTH, BLOCK), lambda s: (v_row, before(s))),
        pl.BlockSpec((ATT_KV_WIDTH, BLOCK), lambda s: (v_row, 0)),
        _whole((ATT_KV_HEADS, 2 * BLOCK, GQ)),
        pl.BlockSpec((MIXER_PAIR, ATT_KV_HEADS, N_META, GQ), lambda s: (s, 0, 0, 0)),
        pl.BlockSpec((PAIR_ROWS, NAT_COLS), lambda s: (s, 0)),
        pl.BlockSpec((PAIR_ROWS, GATE_LANES), lambda s: (s, 0)),
        _whole((1, GATE_LANES)),
        _whole((CONV_WIDTH * BLOCK, 2 * BLOCK)),
        _whole((CONV_WIDTH, 2 * ML_QK_WIDTH)),
        _whole((1, 2 * ML_QK_WIDTH)),
        _whole((1, ML_WIDTH)),
    ] + [_whole(s.shape) for s in _state_shapes()]
    assert len(in_specs) == N_MIX_IN + N_STATE
    w_specs = [
        pl.BlockSpec((w_out.shape[0] // n_steps, D_MODEL), lambda s: (s, 0)),
        pl.BlockSpec((D_FF // n_steps, D_MODEL), lambda s: (s, 0)),
    ]
    return pl.pallas_call(
        _mixer_kernel,
        grid=(n_steps,),
        in_specs=in_specs + w_specs,
        out_specs=[
            pl.BlockSpec((ATT_WIDTH, PAIR_ROWS), lambda s: (0, s)),
            pl.BlockSpec((PAIR_ROWS, ML_WIDTH), lambda s: (s, 0)),
        ] + w_specs,
        out_shape=[
            jax.ShapeDtypeStruct((ATT_WIDTH, nblk * BLOCK), BF16),
            jax.ShapeDtypeStruct((nblk * BLOCK, ML_WIDTH), BF16),
        ] + [jax.ShapeDtypeStruct(w.shape, BF16) for w in (w_out, w_down)],
        scratch_shapes=_state_scratch(),
        compiler_params=_params(1),
        name="mixer",
    )(*mixer_args, w_out, w_down)


OUT_TM = 512


def _outproj_kernel(att0_ref, att1_ref, hm0_ref, hm1_ref, x_ref, wa_ref, wm_ref, g_ref,
                    h_ref, u_ref):
    i = pl.program_id(0)
    half = pl.num_programs(0) // 2

    def body(att_ref, hm_ref):
        h = (x_ref[...]
             + lax.dot_general(att_ref[...], wa_ref[...], CONTRACT_FIRST, preferred_element_type=F32)
             + jnp.dot(hm_ref[...], wm_ref[...], preferred_element_type=F32))
        h_ref[...] = h
        u_ref[...] = _rms(h, g_ref[...]).astype(BF16)

    @pl.when(i < half)
    def _():
        body(att0_ref, hm0_ref)

    @pl.when(i >= half)
    def _():
        body(att1_ref, hm1_ref)


def _outproj(att0, att1, hm0, hm1, x2, w_out, g):
    rows = x2.shape[0]
    n = rows // OUT_TM
    half = n // 2
    first = lambda i: jnp.minimum(i, half - 1)
    second = lambda i: jnp.maximum(i - half, 0)
    return pl.pallas_call(
        _outproj_kernel,
        grid=(n,),
        in_specs=[
            pl.BlockSpec((ATT_WIDTH, OUT_TM), lambda i: (0, first(i))),
            pl.BlockSpec((ATT_WIDTH, OUT_TM), lambda i: (0, second(i))),
            pl.BlockSpec((OUT_TM, ML_WIDTH), lambda i: (first(i), 0)),
            pl.BlockSpec((OUT_TM, ML_WIDTH), lambda i: (second(i), 0)),
            pl.BlockSpec((OUT_TM, D_MODEL), lambda i: (i, 0)),
            pl.BlockSpec((ATT_WIDTH, D_MODEL), lambda i: (0, 0)),
            pl.BlockSpec((ML_WIDTH, D_MODEL), lambda i: (ATT_WIDTH // ML_WIDTH, 0)),
            pl.BlockSpec((1, D_MODEL), lambda i: (0, 0)),
        ],
        out_specs=[
            pl.BlockSpec((OUT_TM, D_MODEL), lambda i: (i, 0)),
            pl.BlockSpec((OUT_TM, D_MODEL), lambda i: (i, 0)),
        ],
        out_shape=[
            jax.ShapeDtypeStruct((rows, D_MODEL), F32),
            jax.ShapeDtypeStruct((rows, D_MODEL), BF16),
        ],
        compiler_params=_params(1),
        name="outproj",
    )(att0, att1, hm0, hm1, x2, w_out, w_out, g)


MLP_TM = 1024
MLP_TF = 1024


def _mlp_step(mode, u_ref, wu_ref, wd_ref, o_ref, hbuf_ref=None, ssq_ref=None):
    acts = []
    for c in range(MLP_TF // MXU_COLS):
        cs = slice(c * MXU_COLS, (c + 1) * MXU_COLS)
        a = jnp.dot(u_ref[...], wu_ref[:, cs], preferred_element_type=F32)
        acts.append(jnp.square(jnp.maximum(a, 0.0)).astype(BF16))
        yield
    act = jnp.concatenate(acts, axis=1)
    for c in range(D_MODEL // MXU_COLS):
        cs = slice(c * MXU_COLS, (c + 1) * MXU_COLS)
        d = jnp.dot(act, wd_ref[:, cs], preferred_element_type=F32)
        if mode == "first":
            o_ref[:, cs] = d
        elif mode == "mid":
            o_ref[:, cs] += d
        else:
            v = o_ref[:, cs] + d + hbuf_ref[:, cs]
            o_ref[:, cs] = v
            sq = v * v
            part = sum(sq[:, k * LANE:(k + 1) * LANE] for k in range(MXU_COLS // LANE))
            if c == 0:
                ssq_ref[...] = part
            else:
                ssq_ref[...] += part
        yield


def _mlp_kernel(u_ref, h_hbm, wu_ref, wd_ref, g_ref, o_ref, hbuf_ref, ssq_ref, hsem):
    i = pl.program_id(0)
    j = pl.program_id(1)
    last = pl.num_programs(1) - 1
    r0 = pl.multiple_of(i * MLP_TM, MLP_TM)
    h_copy = pltpu.make_async_copy(h_hbm.at[pl.ds(r0, MLP_TM), :], hbuf_ref, hsem)

    @pl.when(j == 0)
    def _():
        h_copy.start()
        _drain(_mlp_step("first", u_ref, wu_ref, wd_ref, o_ref))

    @pl.when((j > 0) & (j < last))
    def _():
        _drain(_mlp_step("mid", u_ref, wu_ref, wd_ref, o_ref))

    @pl.when(j == last)
    def _():
        h_copy.wait()
        _drain(_mlp_step("last", u_ref, wu_ref, wd_ref, o_ref, hbuf_ref, ssq_ref))
        ms = jnp.sum(ssq_ref[...], axis=-1, keepdims=True) / D_MODEL
        o_ref[...] = o_ref[...] * lax.rsqrt(ms + EPS) * g_ref[...]


def _mlp(u, h, w_up, w_down, g):
    rows = u.shape[0]
    return pl.pallas_call(
        _mlp_kernel,
        grid=(rows // MLP_TM, D_FF // MLP_TF),
        in_specs=[
            pl.BlockSpec((MLP_TM, D_MODEL), lambda i, j: (i, 0)),
            pl.BlockSpec(memory_space=pl.ANY),
            pl.BlockSpec((D_MODEL, MLP_TF), lambda i, j: (0, j)),
            pl.BlockSpec((MLP_TF, D_MODEL), lambda i, j: (j, 0)),
            pl.BlockSpec((1, D_MODEL), lambda i, j: (0, 0)),
        ],
        out_specs=pl.BlockSpec((MLP_TM, D_MODEL), lambda i, j: (i, 0)),
        out_shape=jax.ShapeDtypeStruct((rows, D_MODEL), F32),
        scratch_shapes=[pltpu.VMEM((MLP_TM, D_MODEL), F32), pltpu.VMEM((MLP_TM, LANE), F32),
                        pltpu.SemaphoreType.DMA(())],
        compiler_params=_params(2),
        name="mlp",
    )(u, h, w_up, w_down, g)


def _t5_bucket(dist):
    max_exact = N_BUCKETS // 2
    d = jnp.maximum(dist, 0)
    ratio = jnp.maximum(d, max_exact).astype(F32) / max_exact
    large = max_exact + (jnp.log(ratio) / math.log(MAX_DISTANCE / max_exact)
                         * (N_BUCKETS - max_exact)).astype(jnp.int32)
    large = jnp.minimum(large, N_BUCKETS - 1)
    return jnp.where(d < max_exact, d, large)


BIAS_ROWS = 16


BIAS_BUCKETS_PER_PHASE = 8


def _bias_fill_splats(rb_ref, splat_ref):
    def fill(b, carry):
        for hd in range(ATT_HEADS):
            splat_ref[b * ATT_HEADS + hd] = jnp.full((SUBLANES, BLOCK), rb_ref[b, hd], F32) * LOG2E
        return carry

    lax.fori_loop(0, N_BUCKETS, fill, 0)


def _bias_lookup_phases(bkt, splat_ref, write):
    accs = [jnp.zeros(bkt.shape, F32)] * ATT_HEADS
    for b in range(N_BUCKETS):
        hit = (bkt >= b) & (bkt < b + 1)
        accs = [jnp.where(hit, jnp.concatenate([splat_ref[b * ATT_HEADS + hd]]
                                               * (BIAS_ROWS // SUBLANES), axis=0), a)
                for hd, a in enumerate(accs)]
        if b % BIAS_BUCKETS_PER_PHASE == BIAS_BUCKETS_PER_PHASE - 1:
            yield
    write(accs)


def _bias_step_phases(bkt_band_ref, bkt_meta_ref, band_ref, meta_ref, splat_ref):
    def write_band(tables):
        for hd, t in enumerate(tables):
            h, g = divmod(hd, ATT_GROUP)
            band_ref[h, :, g * BLOCK:(g + 1) * BLOCK] = t

    yield from _bias_lookup_phases(bkt_band_ref[...], splat_ref, write_band)
    for k in range(bkt_meta_ref.shape[0]):
        def write_meta(tables, k=k):
            for hd, t in enumerate(tables):
                h, g = divmod(hd, ATT_GROUP)
                meta_ref[k, h, :, g * BLOCK:(g + 1) * BLOCK] = t

        yield from _bias_lookup_phases(bkt_meta_ref[k], splat_ref, write_meta)


def _bias_buckets(nblk):
    c = jnp.arange(2 * BLOCK)[:, None]
    r = jnp.arange(BLOCK)[None, :]
    bkt_band = _t5_bucket(r + BLOCK - c).astype(jnp.int32)
    q_pos = BLOCK + jnp.arange(nblk * BLOCK)[None, :]
    m_pos = N_PAD + jnp.arange(N_META)[:, None]
    bkt_meta = _t5_bucket(q_pos - m_pos).astype(jnp.int32)
    return bkt_band, bkt_meta.reshape(N_META, nblk, BLOCK).transpose(1, 0, 2)


W_IN_OFFS = dict(q=0, k=1024, v=1280, mq=1536, mk=2048, mv=2560, mo=3584, gates=4608)
PREP_ROWS = 256
NAT_SRC = [W_IN_OFFS["mq"] // PREP_ROWS + t for t in range(NAT_COLS // PREP_ROWS)]
W3_SRC = ([W_IN_OFFS["q"] // PREP_ROWS + t for t in range(ATT_WIDTH // PREP_ROWS)]
          + [W_IN_OFFS["v"] // PREP_ROWS + t for t in range(ATT_KV_WIDTH // PREP_ROWS)]
          + [W_IN_OFFS["k"] // PREP_ROWS + t for t in range(ATT_KV_WIDTH // PREP_ROWS)]
          + [W_IN_OFFS["gates"] // PREP_ROWS])
N_NAT_BLK = len(NAT_SRC)
N_Q_BLK = ATT_WIDTH // PREP_ROWS
N_PREP = N_NAT_BLK + len(W3_SRC)
assert W_IN_OFFS["mq"] + NAT_COLS == W_IN_OFFS["gates"] and len(W3_SRC) * PREP_ROWS == W3_ROWS


def _win_prep_kernel(src_ref, w_ref, nat_ref, w3_ref):
    j = pl.program_id(0)
    wv = w_ref[0]

    @pl.when(j < N_NAT_BLK)
    def _():
        nat_ref[...] = wv.astype(BF16)

    @pl.when((j >= N_NAT_BLK) & (j < N_NAT_BLK + N_Q_BLK))
    def _():
        w3_ref[...] = (wv * (ATT_HEAD_DIM ** -0.5 * LOG2E)).astype(BF16)

    @pl.when((j >= N_NAT_BLK + N_Q_BLK) & (j < N_PREP - 1))
    def _():
        w3_ref[...] = wv.astype(BF16)

    @pl.when(j == N_PREP - 1)
    def _():
        row = lax.broadcasted_iota(jnp.int32, wv.shape, 0)
        w3_ref[...] = jnp.where(row < 2 * ML_HEADS, wv, 0.0).astype(BF16)


def _win_prep(w_in_t):
    src = jnp.asarray(NAT_SRC + W3_SRC, jnp.int32)
    d = w_in_t.shape[2]
    return pl.pallas_call(
        _win_prep_kernel,
        grid_spec=pltpu.PrefetchScalarGridSpec(
            num_scalar_prefetch=1,
            grid=(N_PREP,),
            in_specs=[pl.BlockSpec((1, PREP_ROWS, d), lambda j, src: (0, src[j], 0))],
            out_specs=[
                pl.BlockSpec((PREP_ROWS, d), lambda j, src: (jnp.minimum(j, N_NAT_BLK - 1), 0)),
                pl.BlockSpec((PREP_ROWS, d), lambda j, src: (jnp.maximum(j - N_NAT_BLK, 0), 0)),
            ],
        ),
        out_shape=[
            jax.ShapeDtypeStruct((NAT_COLS, d), BF16),
            jax.ShapeDtypeStruct((W3_ROWS, d), BF16),
        ],
        compiler_params=_params(1),
        name="w_in_prep",
    )(src, w_in_t)


def kernel(x, meta_tokens, w_in, conv_w, conv_b, b_igate, b_fgate, attn_sinks, rel_bias,
           mh_norm, w_out, norm_mix, norm_mlp, w_up, w_down, norm_final):
    batch, seq, d = x.shape
    assert batch == 2, "the two batches are processed one projection / mixer stage apart"
    nblk = seq // BLOCK
    x2 = x.reshape(batch * seq, d)
    lead = jnp.concatenate([jnp.zeros((N_PAD, d), x.dtype), meta_tokens.astype(x.dtype)], axis=0)
    g_mix = norm_mix[0][None].astype(F32)

    w_nat, w3 = _win_prep(jnp.swapaxes(w_in, 1, 2))
    gate_bias = jnp.pad(jnp.concatenate([b_igate[0], b_fgate[0]]).astype(F32),
                        (0, GATE_LANES - 2 * ML_HEADS))[None]

    proj0, proj_lead, (bias_band_t, bias_meta_t) = _inproj(
        x2, lead, g_mix, w_nat, w3, rel_bias, seq // IN_TM, nblk)
    consts = (attn_sinks[0].astype(F32), bias_band_t, bias_meta_t, gate_bias, _conv_shift_matrix(),
              conv_w[0].astype(F32), conv_b[0][None].astype(F32), mh_norm[0][None].astype(F32))
    state = _lead_state(proj_lead, consts)
    outs = _inproj_mixer(x2, seq // FUSED_TM, g_mix, w_nat, w3,
                         _mixer_args(consts, proj0, proj_lead, state), w_up[0], nblk)
    proj1, (att0, hm0, w_up_bf) = outs[:N_PROJ_OUT], outs[N_PROJ_OUT:]
    att1, hm1, w_out_bf, w_down_bf = _mixer(_mixer_args(consts, proj1, proj_lead, state),
                                            w_out[0], w_down[0], nblk)

    h2, u2 = _outproj(att0, att1, hm0, hm1, x2, w_out_bf, norm_mlp[0][None].astype(F32))
    out = _mlp(u2, h2, w_up_bf, w_down_bf, norm_final[None].astype(F32))
    return out.reshape(batch, seq, d)
```

```python
import math

import numpy as np
import jax
import jax.numpy as jnp
from jax import lax
from jax.experimental import pallas as pl
from jax.experimental.pallas import tpu as pltpu

F32 = jnp.float32
BF16 = jnp.bfloat16

D_MODEL = 2048
N_META = 16
BLOCK = 128
N_PAD = BLOCK - N_META
WINDOW = 128
ATT_HEADS = 16
ATT_KV_HEADS = 4
ATT_GROUP = ATT_HEADS // ATT_KV_HEADS
ATT_HEAD_DIM = 64
ATT_WIDTH = ATT_HEADS * ATT_HEAD_DIM
ATT_KV_WIDTH = ATT_KV_HEADS * ATT_HEAD_DIM
ML_HEADS = 4
ML_V_DIM = 256
ML_QK_DIM = 128
ML_WIDTH = ML_HEADS * ML_V_DIM
ML_QK_WIDTH = ML_HEADS * ML_QK_DIM
CONV_WIDTH = 4
GATE_SOFTCAP = 15.0
D_FF = 4 * D_MODEL
N_BUCKETS = 32
MAX_DISTANCE = 128
EPS = 1e-6
NEG = -1e30
LOG2E = math.log2(math.e)

NAT_MQK = 0
NAT_MV = 2 * ML_QK_WIDTH
NAT_MO = NAT_MV + ML_WIDTH
NAT_COLS = NAT_MO + ML_WIDTH
T_ROWS = ATT_WIDTH + ATT_KV_WIDTH
GATE_LANES = 128
W3_K = T_ROWS
W3_G = W3_K + ATT_KV_WIDTH
W3_ROWS = W3_G + 256

V7X_VMEM_LIMIT = 60 * 1024 * 1024
LANE = 128
SUBLANES = 8
CONTRACT_LAST = (((1,), (1,)), ((), ()))
CONTRACT_FIRST = (((0,), (0,)), ((), ()))


def _params(n_axes, vmem=V7X_VMEM_LIMIT):
    return pltpu.CompilerParams(dimension_semantics=("arbitrary",) * n_axes,
                                vmem_limit_bytes=vmem)


def _rms(x, g):
    return x * lax.rsqrt(jnp.mean(x * x, axis=-1, keepdims=True) + EPS) * g


_DONE = object()


def _drain(phases):
    for _ in phases:
        pass


def _interleave(streams, weights):
    live = list(zip(streams, weights))
    while live:
        for item in list(live):
            stream, weight = item
            for _ in range(weight):
                if next(stream, _DONE) is _DONE:
                    live.remove(item)
                    break
        yield


IN_TM = 1024
IN_NAT_STEPS = 3
IN_TN = NAT_COLS // IN_NAT_STEPS
MXU_COLS = 256
N_PROJ_OUT = 4


def _inproj_nat_phases(rows, u_ref, wn_ref, nat_ref):
    u = u_ref[0:rows, :]
    for c0 in range(0, IN_TN, MXU_COLS):
        cs = slice(c0, c0 + MXU_COLS)
        nat_ref[0:rows, cs] = lax.dot_general(u, wn_ref[cs, :], CONTRACT_LAST,
                                              preferred_element_type=F32).astype(BF16)
        yield


def _inproj_last_phases(rows, u_ref, w3_ref, kn_ref, t_ref, gt_ref):
    u = u_ref[0:rows, :]
    for c0 in range(0, T_ROWS, MXU_COLS):
        cs = slice(c0, c0 + MXU_COLS)
        t_ref[cs, 0:rows] = lax.dot_general(w3_ref[cs, :], u, CONTRACT_LAST,
                                            preferred_element_type=F32).astype(BF16)
        yield
    kn_ref[0:rows, :] = lax.dot_general(u, w3_ref[W3_K:W3_G, :], CONTRACT_LAST,
                                        preferred_element_type=F32).astype(BF16)
    yield
    gt_ref[0:rows, :] = lax.dot_general(u, w3_ref[W3_G:W3_G + GATE_LANES, :], CONTRACT_LAST,
                                        preferred_element_type=F32)
    yield


BIAS_PHASES_PER_ROUND = 3


def _inproj_kernel(x_ref, lead_ref, g_ref, wn_ref, w3_ref, rb_ref, bkt_band_ref, bkt_meta_ref,
                   nat_ref, kn_ref, t_ref, gt_ref, nat_lead_ref, kn_lead_ref, t_lead_ref, gt_lead_ref,
                   band_ref, meta_ref, u_ref, u_lead_ref, splat_ref):
    i = pl.program_id(0)
    j = pl.program_id(1)

    @pl.when(j == 0)
    def _():
        u_ref[...] = _rms(x_ref[...], g_ref[...]).astype(BF16)

    @pl.when((j == 0) & (i == 0))
    def _():
        u_lead_ref[...] = _rms(lead_ref[...], g_ref[...]).astype(BF16)
        _bias_fill_splats(rb_ref, splat_ref)

    def with_bias(phases):
        bias = _bias_step_phases(bkt_band_ref, bkt_meta_ref, band_ref, meta_ref, splat_ref)
        _drain(_interleave([phases, bias], [1, BIAS_PHASES_PER_ROUND]))

    @pl.when(j < IN_NAT_STEPS)
    def _():
        with_bias(_inproj_nat_phases(IN_TM, u_ref, wn_ref, nat_ref))

    @pl.when(j == IN_NAT_STEPS)
    def _():
        with_bias(_inproj_last_phases(IN_TM, u_ref, w3_ref, kn_ref, t_ref, gt_ref))

    @pl.when((i == 0) & (j < IN_NAT_STEPS))
    def _():
        _drain(_inproj_nat_phases(BLOCK, u_lead_ref, wn_ref, nat_lead_ref))

    @pl.when((i == 0) & (j == IN_NAT_STEPS))
    def _():
        _drain(_inproj_last_phases(BLOCK, u_lead_ref, w3_ref, kn_lead_ref, t_lead_ref, gt_lead_ref))


def _inproj_weight_specs():
    last_nat = IN_NAT_STEPS - 1
    return [
        pl.BlockSpec((IN_TN, D_MODEL), lambda i, j: (jnp.minimum(j, last_nat), 0)),
        pl.BlockSpec((W3_ROWS, D_MODEL), lambda i, j: (0, 0)),
    ]


def _inproj_out(tm, rows):
    last_nat = IN_NAT_STEPS - 1
    specs = [
        pl.BlockSpec((tm, IN_TN), lambda i, j: (i, jnp.minimum(j, last_nat))),
        pl.BlockSpec((tm, ATT_KV_WIDTH), lambda i, j: (i, 0)),
        pl.BlockSpec((T_ROWS, tm), lambda i, j: (0, i)),
        pl.BlockSpec((tm, GATE_LANES), lambda i, j: (i, 0)),
    ]
    shapes = [
        jax.ShapeDtypeStruct((rows, NAT_COLS), BF16),
        jax.ShapeDtypeStruct((rows, ATT_KV_WIDTH), BF16),
        jax.ShapeDtypeStruct((T_ROWS, rows), BF16),
        jax.ShapeDtypeStruct((rows, GATE_LANES), F32),
    ]
    return specs, shapes


def _inproj(x2, lead, g, w_nat, w3, rel_bias, n_tiles, nblk):
    out_specs, out_shape = _inproj_out(IN_TM, n_tiles * IN_TM)
    last_nat = IN_NAT_STEPS - 1
    n_steps = n_tiles * (IN_NAT_STEPS + 1)
    assert n_steps * BIAS_ROWS == 2 * BLOCK and nblk % n_steps == 0
    meta_per_step = nblk // n_steps
    bkt_band, bkt_meta = _bias_buckets(nblk)
    step = lambda i, j: i * (IN_NAT_STEPS + 1) + j
    bias_in_specs = [
        pl.BlockSpec(memory_space=pltpu.SMEM),
        pl.BlockSpec((BIAS_ROWS, BLOCK), lambda i, j: (step(i, j), 0)),
        pl.BlockSpec((meta_per_step, N_META, BLOCK), lambda i, j: (step(i, j), 0, 0)),
    ]
    bias_out_specs = [
        pl.BlockSpec((ATT_KV_HEADS, BIAS_ROWS, GQ), lambda i, j: (0, step(i, j), 0)),
        pl.BlockSpec((meta_per_step, ATT_KV_HEADS, N_META, GQ), lambda i, j: (step(i, j), 0, 0, 0)),
    ]
    bias_shapes = [
        jax.ShapeDtypeStruct((ATT_KV_HEADS, 2 * BLOCK, GQ), F32),
        jax.ShapeDtypeStruct((nblk, ATT_KV_HEADS, N_META, GQ), F32),
    ]
    lead_col = lambda i, j: jnp.where(i == 0, jnp.minimum(j, last_nat), last_nat)
    lead_specs = [
        pl.BlockSpec((BLOCK, IN_TN), lambda i, j: (0, lead_col(i, j))),
        pl.BlockSpec((BLOCK, ATT_KV_WIDTH), lambda i, j: (0, 0)),
        pl.BlockSpec((T_ROWS, BLOCK), lambda i, j: (0, 0)),
        pl.BlockSpec((BLOCK, GATE_LANES), lambda i, j: (0, 0)),
    ]
    lead_shapes = [
        jax.ShapeDtypeStruct((BLOCK, NAT_COLS), BF16),
        jax.ShapeDtypeStruct((BLOCK, ATT_KV_WIDTH), BF16),
        jax.ShapeDtypeStruct((T_ROWS, BLOCK), BF16),
        jax.ShapeDtypeStruct((BLOCK, GATE_LANES), F32),
    ]
    outs = pl.pallas_call(
        _inproj_kernel,
        grid=(n_tiles, IN_NAT_STEPS + 1),
        in_specs=[
            pl.BlockSpec((IN_TM, D_MODEL), lambda i, j: (i, 0)),
            pl.BlockSpec((BLOCK, D_MODEL), lambda i, j: (0, 0)),
            pl.BlockSpec((1, D_MODEL), lambda i, j: (0, 0)),
        ] + _inproj_weight_specs() + bias_in_specs,
        out_specs=out_specs + lead_specs + bias_out_specs,
        out_shape=out_shape + lead_shapes + bias_shapes,
        scratch_shapes=[pltpu.VMEM((IN_TM, D_MODEL), BF16), pltpu.VMEM((BLOCK, D_MODEL), BF16),
                        pltpu.VMEM((N_BUCKETS * ATT_HEADS, SUBLANES, BLOCK), F32)],
        compiler_params=_params(2),
        name="inproj",
    )(x2, lead, g, w_nat, w3, rel_bias.astype(F32), bkt_band, bkt_meta)
    return outs[:N_PROJ_OUT], outs[N_PROJ_OUT:2 * N_PROJ_OUT], outs[2 * N_PROJ_OUT:]


GQ = ATT_GROUP * BLOCK
ATT_SLABS_PER_PHASE = 2


def _attn_prologue(n, kc_ref, kp_ref, km_ref):
    c = lax.broadcasted_iota(jnp.int32, (2 * BLOCK, BLOCK), 0)
    r = lax.broadcasted_iota(jnp.int32, (2 * BLOCK, BLOCK), 1)
    dist = r + BLOCK - c
    band_ok = (dist >= 0) & (dist < WINDOW) & ((c >= BLOCK) | (n >= 1))
    mm = lax.broadcasted_iota(jnp.int32, (N_META, BLOCK), 0)
    rr = lax.broadcasted_iota(jnp.int32, (N_META, BLOCK), 1)
    meta_ok = (n + 1) * BLOCK + rr >= N_PAD + mm
    kall = jnp.concatenate([kp_ref[...], kc_ref[...], km_ref[...]], axis=0)
    return band_ok, meta_ok, kall


BF16_ROWS = 16


def _ones_row(width):
    row = lax.broadcasted_iota(jnp.int32, (BF16_ROWS, width), 0)
    return jnp.where(row == 0, 1.0, 0.0).astype(BF16)


def _attn_head(h, pro, sink_ref, qt_ref, vc_ref, vp_ref, vl_ref, bb_ref, bm_ref, o_ref):
    band_ok, meta_ok, kall = pro
    lo = (h // 2) * LANE
    kpair = kall[:, lo:lo + LANE]
    qh = jnp.concatenate(
        [qt_ref[(ATT_GROUP * h + g) * ATT_HEAD_DIM:(ATT_GROUP * h + g + 1) * ATT_HEAD_DIM, :]
         for g in range(ATT_GROUP)], axis=1)
    zq = jnp.zeros_like(qh)
    qz = jnp.concatenate([qh, zq] if h % 2 == 0 else [zq, qh], axis=0)
    st = jnp.dot(kpair, qz, preferred_element_type=F32)
    yield
    zpad = jnp.zeros((N_PAD, BLOCK), BF16)
    p_band, p_lead, sink_p = [], [], []
    for g in range(ATT_GROUP):
        gs = slice(g * BLOCK, (g + 1) * BLOCK)
        sl = st[:, gs]
        sink = jnp.full((1, BLOCK), sink_ref[ATT_GROUP * h + g], F32) * LOG2E
        lb = jnp.where(band_ok, sl[:2 * BLOCK] + bb_ref[h, :, gs], NEG * LOG2E)
        lm = jnp.where(meta_ok, sl[2 * BLOCK:] + bm_ref[0, h, :, gs], NEG * LOG2E)
        m = jnp.maximum(jnp.max(lb, axis=0, keepdims=True), jnp.max(lm, axis=0, keepdims=True))
        m = jnp.maximum(m, sink)
        pb = jnp.exp2(lb - m)
        pm = jnp.exp2(lm - m)
        sink_p.append(jnp.exp2(sink - m))
        p_band.append(pb.astype(BF16))
        p_lead.append(jnp.concatenate([zpad, pm.astype(BF16)], axis=0))
        if g % ATT_SLABS_PER_PHASE == ATT_SLABS_PER_PHASE - 1:
            yield
    pbt = jnp.concatenate(p_band, axis=1)
    plt = jnp.concatenate(p_lead, axis=1)
    hs = slice(h * ATT_HEAD_DIM, (h + 1) * ATT_HEAD_DIM)
    vband = jnp.concatenate([vp_ref[hs, :], vc_ref[hs, :]], axis=1)
    ones_b = _ones_row(2 * BLOCK)
    ones_l = _ones_row(BLOCK)
    ot = (jnp.dot(jnp.concatenate([vband, ones_b], axis=0), pbt, preferred_element_type=F32)
          + jnp.dot(jnp.concatenate([vl_ref[hs, :], ones_l], axis=0), plt,
                    preferred_element_type=F32))
    for g in range(ATT_GROUP):
        gs = slice(g * BLOCK, (g + 1) * BLOCK)
        den = ot[ATT_HEAD_DIM:ATT_HEAD_DIM + 1, gs] + sink_p[g]
        row = (ATT_GROUP * h + g) * ATT_HEAD_DIM
        o_ref[row:row + ATT_HEAD_DIM, :] = (ot[:ATT_HEAD_DIM, gs] * (1.0 / den)).astype(BF16)


def _conv_shift_matrix():
    s = np.zeros((CONV_WIDTH * BLOCK, 2 * BLOCK), np.float32)
    for k in range(CONV_WIDTH):
        t = np.arange(BLOCK)
        s[k * BLOCK + t, BLOCK + t - k] = 1.0
    return jnp.asarray(s, BF16)


def _mlstm_gates(n, gate_ref, gb_ref):
    t_idx = lax.broadcasted_iota(jnp.int32, (BLOCK, GATE_LANES), 0)
    valid = (n > 0) | (t_idx >= N_PAD)
    pre = GATE_SOFTCAP * jnp.tanh((gate_ref[...] + gb_ref[...]) / GATE_SOFTCAP)
    log_i = jnp.where(valid, pre, NEG)
    log_sig = jnp.minimum(pre, 0.0) - jnp.log1p(jnp.exp(-jnp.abs(pre)))
    log_f = jnp.where(valid, log_sig, 0.0)
    row = lax.broadcasted_iota(jnp.int32, (BLOCK, BLOCK), 0)
    col = lax.broadcasted_iota(jnp.int32, (BLOCK, BLOCK), 1)
    causal = col <= row
    tril = jnp.where(causal, 1.0, 0.0).astype(F32)
    b_all = jnp.dot(tril, log_f, preferred_element_type=F32,
                    precision=lax.Precision.HIGHEST)
    return log_i, b_all, log_i.T, b_all.T, causal


def _mlstm_conv(p_ref, shift_ref, cw_ref, cb_ref, xprev_ref):
    xcur = p_ref[:, NAT_MQK:NAT_MV]
    xcat = jnp.concatenate([xprev_ref[...], xcur], axis=0)
    sh = jnp.dot(shift_ref[...], xcat, preferred_element_type=F32)
    xprev_ref[...] = xcur
    y = cb_ref[...]
    for k in range(CONV_WIDTH):
        y = y + cw_ref[CONV_WIDTH - 1 - k:CONV_WIDTH - k, :] * sh[k * BLOCK:(k + 1) * BLOCK]
    act = y * jax.nn.sigmoid(y)
    return act[:, :ML_QK_WIDTH] * (ML_QK_DIM ** -0.5), act[:, ML_QK_WIDTH:]


def _mlstm_head(h, gates, q_all, k_all, p_ref, mhn_ref, o_ref, c_ref, n_ref, m_ref):
    log_i, b_all, log_i_t, b_all_t, causal = gates
    q_f = q_all[:, h * ML_QK_DIM:(h + 1) * ML_QK_DIM]
    k_f = k_all[:, h * ML_QK_DIM:(h + 1) * ML_QK_DIM]
    q_h = q_f.astype(BF16)
    k_h = k_f.astype(BF16)
    v_h = p_ref[:, NAT_MV + h * ML_V_DIM:NAT_MV + (h + 1) * ML_V_DIM]
    b_col = b_all[:, ML_HEADS + h:ML_HEADS + h + 1]
    li_col = log_i[:, h:h + 1]
    b_row = b_all_t[ML_HEADS + h:ML_HEADS + h + 1, :]
    li_row = log_i_t[h:h + 1, :]
    b_last = b_all[BLOCK - 1:BLOCK, ML_HEADS + h:ML_HEADS + h + 1]
    c_prev = c_ref[h]
    n_prev = n_ref[h][0:1, :]
    m_prev = m_ref[h][0:1, 0:1]

    dmat = jnp.where(causal, b_col - b_row + li_row, -jnp.inf)
    inter = b_col + m_prev
    m_t = jnp.maximum(inter, jnp.max(dmat, axis=1, keepdims=True))
    s = lax.dot_general(q_h, k_h, CONTRACT_LAST, preferred_element_type=F32) * jnp.exp(dmat - m_t)
    yield
    a_t = jnp.exp(inter - m_t)
    num = (jnp.dot(s.astype(BF16), v_h, preferred_element_type=F32)
           + a_t * jnp.dot(q_h, c_prev.astype(BF16), preferred_element_type=F32))
    den = (jnp.sum(s, axis=1, keepdims=True)
           + a_t * jnp.sum(q_f * n_prev, axis=1, keepdims=True))
    den = jnp.maximum(jnp.abs(den), jnp.exp(-m_t))
    hh = num / den
    hh = hh * lax.rsqrt(jnp.mean(hh * hh, axis=-1, keepdims=True) + EPS)
    vs = slice(h * ML_V_DIM, (h + 1) * ML_V_DIM)
    hh = hh * mhn_ref[:, vs]
    m_o = p_ref[:, NAT_MO + h * ML_V_DIM:NAT_MO + (h + 1) * ML_V_DIM].astype(F32)
    o_ref[:, vs] = (hh * jax.nn.sigmoid(m_o)).astype(BF16)
    yield

    g_col = b_last - b_col + li_col
    m_loc = jnp.max(g_col, axis=0, keepdims=True)
    wk = jnp.exp(g_col - m_loc) * k_f
    c_loc = lax.dot_general(wk.astype(BF16), v_h, CONTRACT_FIRST,
                            preferred_element_type=F32)
    n_loc = jnp.sum(wk, axis=0, keepdims=True)
    m_new = jnp.maximum(b_last + m_prev, m_loc)
    a = jnp.exp(b_last + m_prev - m_new)
    cc = jnp.exp(m_loc - m_new)
    c_ref[h] = a * c_prev + cc * c_loc
    n_ref[h] = jnp.broadcast_to(a * n_prev + cc * n_loc, (8, ML_QK_DIM))
    m_ref[h] = jnp.broadcast_to(m_new, (8, LANE))


N_MIX_IN = 17
N_STATE = 4
MIXER_PHASES_PER_NAT_PHASE = 6
MIXER_PHASES_PER_LAST_PHASE = 3


N_ATT_IN = 10


def _attention_phases(n, att_in, att_ref):
    sink_ref, qt_ref, kc_ref, kp_ref, km_ref, vc_ref, vp_ref, vl_ref, bb_ref, bm_ref = att_in
    pro = _attn_prologue(n - 1, kc_ref, kp_ref, km_ref)
    for h in range(ATT_KV_HEADS):
        yield from _attn_head(h, pro, sink_ref, qt_ref, vc_ref, vp_ref, vl_ref, bb_ref, bm_ref, att_ref)


def _mlstm_phases(n, ml_in, hm_ref, state):
    p_ref, gate_ref, gb_ref, shift_ref, cw_ref, cb_ref, mhn_ref = ml_in
    c_ref, n_ref, m_ref, xprev_ref = state
    gates = _mlstm_gates(n, gate_ref, gb_ref)
    yield
    q_all, k_all = _mlstm_conv(p_ref, shift_ref, cw_ref, cb_ref, xprev_ref)
    yield
    for h in range(ML_HEADS):
        yield from _mlstm_head(h, gates, q_all, k_all, p_ref, mhn_ref, hm_ref, c_ref, n_ref, m_ref)


def _mixer_phases(n, mix_in, att_ref, hm_ref, state):
    return _interleave([_mlstm_phases(n, mix_in[N_ATT_IN:], hm_ref, state),
                        _attention_phases(n, mix_in[:N_ATT_IN], att_ref)], [1, 1])


def _load_state(state, state_in):
    for dst, src in zip(state, state_in):
        dst[...] = src[...]


def _state_shapes():
    return [
        jax.ShapeDtypeStruct((ML_HEADS, ML_QK_DIM, ML_V_DIM), F32),
        jax.ShapeDtypeStruct((ML_HEADS, SUBLANES, ML_QK_DIM), F32),
        jax.ShapeDtypeStruct((ML_HEADS, SUBLANES, LANE), F32),
        jax.ShapeDtypeStruct((BLOCK, 2 * ML_QK_WIDTH), BF16),
    ]


def _state_scratch():
    return [pltpu.VMEM(s.shape, s.dtype) for s in _state_shapes()]


def _whole(shape):
    return pl.BlockSpec(shape, lambda *ids: (0,) * len(shape))


def _mixer_in_specs(step):
    cur = lambda *ids: step(*ids)
    prev = lambda *ids: jnp.maximum(step(*ids) - 1, 0)
    v_row = ATT_WIDTH // ATT_KV_WIDTH
    meta_rows = N_PAD // N_META
    return [
        pl.BlockSpec(memory_space=pltpu.SMEM),
        pl.BlockSpec((ATT_WIDTH, BLOCK), lambda *ids: (0, cur(*ids))),
        pl.BlockSpec((BLOCK, ATT_KV_WIDTH), lambda *ids: (cur(*ids), 0)),
        pl.BlockSpec((BLOCK, ATT_KV_WIDTH), lambda *ids: (prev(*ids), 0)),
        pl.BlockSpec((N_META, ATT_KV_WIDTH), lambda *ids: (meta_rows, 0)),
        pl.BlockSpec((ATT_KV_WIDTH, BLOCK), lambda *ids: (v_row, cur(*ids))),
        pl.BlockSpec((ATT_KV_WIDTH, BLOCK), lambda *ids: (v_row, prev(*ids))),
        pl.BlockSpec((ATT_KV_WIDTH, BLOCK), lambda *ids: (v_row, 0)),
        _whole((ATT_KV_HEADS, 2 * BLOCK, GQ)),
        pl.BlockSpec((1, ATT_KV_HEADS, N_META, GQ), lambda *ids: (cur(*ids), 0, 0, 0)),
        pl.BlockSpec((BLOCK, NAT_COLS), lambda *ids: (cur(*ids), 0)),
        pl.BlockSpec((BLOCK, GATE_LANES), lambda *ids: (cur(*ids), 0)),
        _whole((1, GATE_LANES)),
        _whole((CONV_WIDTH * BLOCK, 2 * BLOCK)),
        _whole((CONV_WIDTH, 2 * ML_QK_WIDTH)),
        _whole((1, 2 * ML_QK_WIDTH)),
        _whole((1, ML_WIDTH)),
    ] + [_whole(s.shape) for s in _state_shapes()]


def _mixer_args(consts, proj_b, proj_lead, state):
    sinks, bias_band_t, bias_meta_t, gate_bias, shift, conv_w, conv_b, mh_norm = consts
    nat_b, kn_b, projt_b, gt_b = proj_b
    _, kn_lead, projt_lead, _ = proj_lead
    return [sinks, projt_b, kn_b, kn_b, kn_lead, projt_b, projt_b, projt_lead, bias_band_t,
            bias_meta_t, nat_b, gt_b, gate_bias, shift, conv_w, conv_b, mh_norm] + list(state)


def _mixer_out(step, n_rows):
    specs = [
        pl.BlockSpec((ATT_WIDTH, BLOCK), lambda *ids: (0, step(*ids))),
        pl.BlockSpec((BLOCK, ML_WIDTH), lambda *ids: (step(*ids), 0)),
    ]
    shapes = [
        jax.ShapeDtypeStruct((ATT_WIDTH, n_rows), BF16),
        jax.ShapeDtypeStruct((n_rows, ML_WIDTH), BF16),
    ]
    return specs, shapes


def _lead_state_kernel(p_ref, gate_ref, gb_ref, shift_ref, cw_ref, cb_ref, mhn_ref,
                       c_ref, n_ref, m_ref, xprev_ref, hm_ref):
    for ref in (c_ref, n_ref, m_ref, xprev_ref):
        ref[...] = jnp.zeros_like(ref)
    gates = _mlstm_gates(0, gate_ref, gb_ref)
    q_all, k_all = _mlstm_conv(p_ref, shift_ref, cw_ref, cb_ref, xprev_ref)
    for h in range(ML_HEADS):
        _drain(_mlstm_head(h, gates, q_all, k_all, p_ref, mhn_ref, hm_ref, c_ref, n_ref, m_ref))


def _lead_state(proj_lead, consts):
    nat_lead, _, _, gates_lead = proj_lead
    _, _, _, gate_bias, shift, conv_w, conv_b, mh_norm = consts
    shapes = _state_shapes()
    return pl.pallas_call(
        _lead_state_kernel,
        grid=(1,),
        in_specs=[
            _whole((BLOCK, NAT_COLS)),
            _whole((BLOCK, GATE_LANES)),
            _whole((1, GATE_LANES)),
            _whole((CONV_WIDTH * BLOCK, 2 * BLOCK)),
            _whole((CONV_WIDTH, 2 * ML_QK_WIDTH)),
            _whole((1, 2 * ML_QK_WIDTH)),
            _whole((1, ML_WIDTH)),
        ],
        out_specs=[_whole(s.shape) for s in shapes],
        out_shape=shapes,
        scratch_shapes=[pltpu.VMEM((BLOCK, ML_WIDTH), BF16)],
        compiler_params=_params(1),
        name="lead_state",
    )(nat_lead, gates_lead, gate_bias, shift, conv_w, conv_b, mh_norm)


FUSED_TM = 512


def _inproj_mixer_kernel(*refs):
    x_ref, g_ref, wn_ref, w3_ref = refs[:4]
    mix_in = refs[4:4 + N_MIX_IN]
    state_in = refs[4 + N_MIX_IN:4 + N_MIX_IN + N_STATE]
    wu_ref = refs[4 + N_MIX_IN + N_STATE]
    (nat_ref, kn_ref, t_ref, gt_ref, att_ref, hm_ref,
     wu_bf_ref) = refs[5 + N_MIX_IN + N_STATE:-(1 + N_STATE)]
    u_ref = refs[-(1 + N_STATE)]
    state = refs[-N_STATE:]
    i = pl.program_id(0)
    j = pl.program_id(1)
    step = i * (IN_NAT_STEPS + 1) + j
    wu_bf_ref[...] = wu_ref[...].astype(BF16)

    @pl.when(j == 0)
    def _():
        u_ref[...] = _rms(x_ref[...], g_ref[...]).astype(BF16)

    @pl.when(step == 0)
    def _():
        _load_state(state, state_in)

    @pl.when(j < IN_NAT_STEPS)
    def _():
        _drain(_interleave([_inproj_nat_phases(FUSED_TM, u_ref, wn_ref, nat_ref),
                            _mixer_phases(step + 1, mix_in, att_ref, hm_ref, state)],
                           [1, MIXER_PHASES_PER_NAT_PHASE]))

    @pl.when(j == IN_NAT_STEPS)
    def _():
        _drain(_interleave([_inproj_last_phases(FUSED_TM, u_ref, w3_ref, kn_ref, t_ref, gt_ref),
                            _mixer_phases(step + 1, mix_in, att_ref, hm_ref, state)],
                           [1, MIXER_PHASES_PER_LAST_PHASE]))


def _inproj_mixer(x2, row_tile0, g, w_nat, w3, mixer_args, w_up, nblk):
    n_tiles = nblk * BLOCK // FUSED_TM
    assert n_tiles * (IN_NAT_STEPS + 1) == nblk
    step = lambda i, j: i * (IN_NAT_STEPS + 1) + j
    proj_specs, proj_shapes = _inproj_out(FUSED_TM, n_tiles * FUSED_TM)
    mix_specs, mix_shapes = _mixer_out(step, nblk * BLOCK)
    wu_spec = pl.BlockSpec((D_MODEL, D_FF // nblk), lambda i, j: (0, step(i, j)))
    return pl.pallas_call(
        _inproj_mixer_kernel,
        grid=(n_tiles, IN_NAT_STEPS + 1),
        in_specs=[
            pl.BlockSpec((FUSED_TM, D_MODEL), lambda i, j: (row_tile0 + i, 0)),
            pl.BlockSpec((1, D_MODEL), lambda i, j: (0, 0)),
        ] + _inproj_weight_specs() + _mixer_in_specs(step) + [wu_spec],
        out_specs=proj_specs + mix_specs + [wu_spec],
        out_shape=proj_shapes + mix_shapes + [jax.ShapeDtypeStruct(w_up.shape, BF16)],
        scratch_shapes=[pltpu.VMEM((FUSED_TM, D_MODEL), BF16)] + _state_scratch(),
        compiler_params=_params(2),
        name="inproj_mixer",
    )(x2, g, w_nat, w3, *mixer_args, w_up)


MIXER_PAIR = 4
PAIR_ROWS = MIXER_PAIR * BLOCK
ATT_PHASES_PER_MLSTM_PHASE = 4
MLSTM_CHUNK_LAG = 3


def _mixer_kernel(*refs):
    (sink_ref, qt_ref, kc_ref, kp_ref, km_ref, vc_ref, vp_ref, vl_ref, bb_ref, bm_ref,
     p_ref, gate_ref, gb_ref, shift_ref, cw_ref, cb_ref, mhn_ref) = refs[:N_MIX_IN]
    state_in = refs[N_MIX_IN:N_MIX_IN + N_STATE]
    wo_ref, wd_ref = refs[N_MIX_IN + N_STATE:N_MIX_IN + N_STATE + 2]
    att_ref, hm_ref, wo_bf_ref, wd_bf_ref = refs[N_MIX_IN + N_STATE + 2:-N_STATE]
    state = refs[-N_STATE:]
    step = pl.program_id(0)

    @pl.when(step == 0)
    def _():
        _load_state(state, state_in)

    wo_bf_ref[...] = wo_ref[...].astype(BF16)
    wd_bf_ref[...] = wd_ref[...].astype(BF16)

    rows = lambda ref, k: ref.at[k * BLOCK:(k + 1) * BLOCK]
    cols = lambda ref, k: ref.at[:, k * BLOCK:(k + 1) * BLOCK]
    attention, chunks = [], []
    for k in range(MIXER_PAIR):
        n = MIXER_PAIR * step + 1 + k
        kp_k = kp_ref if k == 0 else rows(kc_ref, k - 1)
        vp_k = vp_ref if k == 0 else cols(vc_ref, k - 1)
        att_in = (sink_ref, cols(qt_ref, k), rows(kc_ref, k), kp_k, km_ref, cols(vc_ref, k), vp_k,
                  vl_ref, bb_ref, bm_ref.at[k:k + 1])
        attention.append(_attention_phases(n, att_in, cols(att_ref, k)))
        chunks.append((n, (rows(p_ref, k), rows(gate_ref, k), gb_ref, shift_ref, cw_ref, cb_ref,
                           mhn_ref), rows(hm_ref, k)))

    def mlstm_chunk(k):
        for _ in range(k * MLSTM_CHUNK_LAG):
            yield
        n, ml_in, hm_k = chunks[k]
        yield from _mlstm_phases(n, ml_in, hm_k, state)

    _drain(_interleave([mlstm_chunk(k) for k in range(MIXER_PAIR)] + attention,
                       [1] * MIXER_PAIR + [ATT_PHASES_PER_MLSTM_PHASE] * MIXER_PAIR))


def _mixer(mixer_args, w_out, w_down, nblk):
    n_steps = nblk // MIXER_PAIR
    before = lambda s: jnp.maximum(MIXER_PAIR * s - 1, 0)
    v_row = ATT_WIDTH // ATT_KV_WIDTH
    in_specs = [
        pl.BlockSpec(memory_space=pltpu.SMEM),
        pl.BlockSpec((ATT_WIDTH, PAIR_ROWS), lambda s: (0, s)),
        pl.BlockSpec((PAIR_ROWS, ATT_KV_WIDTH), lambda s: (s, 0)),
        pl.BlockSpec((BLOCK, ATT_KV_WIDTH), lambda s: (before(s), 0)),
        pl.BlockSpec((N_META, ATT_KV_WIDTH), lambda s: (N_PAD // N_META, 0)),
        pl.BlockSpec((ATT_KV_WIDTH, PAIR_ROWS), lambda s: (v_row, s)),
        pl.BlockSpec((ATT_KV_WIDTH, BLOCK), lambda s: (v_row, before(s))),
        pl.BlockSpec((ATT_KV_WIDTH, BLOCK), lambda s: (v_row, 0)),
        _whole((ATT_KV_HEADS, 2 * BLOCK, GQ)),
        pl.BlockSpec((MIXER_PAIR, ATT_KV_HEADS, N_META, GQ), lambda s: (s, 0, 0, 0)),
        pl.BlockSpec((PAIR_ROWS, NAT_COLS), lambda s: (s, 0)),
        pl.BlockSpec((PAIR_ROWS, GATE_LANES), lambda s: (s, 0)),
        _whole((1, GATE_LANES)),
        _whole((CONV_WIDTH * BLOCK, 2 * BLOCK)),
        _whole((CONV_WIDTH, 2 * ML_QK_WIDTH)),
        _whole((1, 2 * ML_QK_WIDTH)),
        _whole((1, ML_WIDTH)),
    ] + [_whole(s.shape) for s in _state_shapes()]
    assert len(in_specs) == N_MIX_IN + N_STATE
    w_specs = [
        pl.BlockSpec((w_out.shape[0] // n_steps, D_MODEL), lambda s: (s, 0)),
        pl.BlockSpec((D_FF // n_steps, D_MODEL), lambda s: (s, 0)),
    ]
    return pl.pallas_call(
        _mixer_kernel,
        grid=(n_steps,),
        in_specs=in_specs + w_specs,
        out_specs=[
            pl.BlockSpec((ATT_WIDTH, PAIR_ROWS), lambda s: (0, s)),
            pl.BlockSpec((PAIR_ROWS, ML_WIDTH), lambda s: (s, 0)),
        ] + w_specs,
        out_shape=[
            jax.ShapeDtypeStruct((ATT_WIDTH, nblk * BLOCK), BF16),
            jax.ShapeDtypeStruct((nblk * BLOCK, ML_WIDTH), BF16),
        ] + [jax.ShapeDtypeStruct(w.shape, BF16) for w in (w_out, w_down)],
        scratch_shapes=_state_scratch(),
        compiler_params=_params(1),
        name="mixer",
    )(*mixer_args, w_out, w_down)


OUT_TM = 512


def _outproj_kernel(att0_ref, att1_ref, hm0_ref, hm1_ref, x_ref, wa_ref, wm_ref, g_ref,
                    h_ref, u_ref):
    i = pl.program_id(0)
    half = pl.num_programs(0) // 2

    def body(att_ref, hm_ref):
        h = (x_ref[...]
             + lax.dot_general(att_ref[...], wa_ref[...], CONTRACT_FIRST, preferred_element_type=F32)
             + jnp.dot(hm_ref[...], wm_ref[...], preferred_element_type=F32))
        h_ref[...] = h
        u_ref[...] = _rms(h, g_ref[...]).astype(BF16)

    @pl.when(i < half)
    def _():
        body(att0_ref, hm0_ref)

    @pl.when(i >= half)
    def _():
        body(att1_ref, hm1_ref)


def _outproj(att0, att1, hm0, hm1, x2, w_out, g):
    rows = x2.shape[0]
    n = rows // OUT_TM
    half = n // 2
    first = lambda i: jnp.minimum(i, half - 1)
    second = lambda i: jnp.maximum(i - half, 0)
    return pl.pallas_call(
        _outproj_kernel,
        grid=(n,),
        in_specs=[
            pl.BlockSpec((ATT_WIDTH, OUT_TM), lambda i: (0, first(i))),
            pl.BlockSpec((ATT_WIDTH, OUT_TM), lambda i: (0, second(i))),
            pl.BlockSpec((OUT_TM, ML_WIDTH), lambda i: (first(i), 0)),
            pl.BlockSpec((OUT_TM, ML_WIDTH), lambda i: (second(i), 0)),
            pl.BlockSpec((OUT_TM, D_MODEL), lambda i: (i, 0)),
            pl.BlockSpec((ATT_WIDTH, D_MODEL), lambda i: (0, 0)),
            pl.BlockSpec((ML_WIDTH, D_MODEL), lambda i: (ATT_WIDTH // ML_WIDTH, 0)),
            pl.BlockSpec((1, D_MODEL), lambda i: (0, 0)),
        ],
        out_specs=[
            pl.BlockSpec((OUT_TM, D_MODEL), lambda i: (i, 0)),
            pl.BlockSpec((OUT_TM, D_MODEL), lambda i: (i, 0)),
        ],
        out_shape=[
            jax.ShapeDtypeStruct((rows, D_MODEL), F32),
            jax.ShapeDtypeStruct((rows, D_MODEL), BF16),
        ],
        compiler_params=_params(1),
        name="outproj",
    )(att0, att1, hm0, hm1, x2, w_out, w_out, g)


MLP_TM = 1024
MLP_TF = 1024


def _mlp_step(mode, u_ref, wu_ref, wd_ref, o_ref, hbuf_ref=None, ssq_ref=None):
    acts = []
    for c in range(MLP_TF // MXU_COLS):
        cs = slice(c * MXU_COLS, (c + 1) * MXU_COLS)
        a = jnp.dot(u_ref[...], wu_ref[:, cs], preferred_element_type=F32)
        acts.append(jnp.square(jnp.maximum(a, 0.0)).astype(BF16))
        yield
    act = jnp.concatenate(acts, axis=1)
    for c in range(D_MODEL // MXU_COLS):
        cs = slice(c * MXU_COLS, (c + 1) * MXU_COLS)
        d = jnp.dot(act, wd_ref[:, cs], preferred_element_type=F32)
        if mode == "first":
            o_ref[:, cs] = d
        elif mode == "mid":
            o_ref[:, cs] += d
        else:
            v = o_ref[:, cs] + d + hbuf_ref[:, cs]
            o_ref[:, cs] = v
            sq = v * v
            part = sum(sq[:, k * LANE:(k + 1) * LANE] for k in range(MXU_COLS // LANE))
            if c == 0:
                ssq_ref[...] = part
            else:
                ssq_ref[...] += part
        yield


def _mlp_kernel(u_ref, h_hbm, wu_ref, wd_ref, g_ref, o_ref, hbuf_ref, ssq_ref, hsem):
    i = pl.program_id(0)
    j = pl.program_id(1)
    last = pl.num_programs(1) - 1
    r0 = pl.multiple_of(i * MLP_TM, MLP_TM)
    h_copy = pltpu.make_async_copy(h_hbm.at[pl.ds(r0, MLP_TM), :], hbuf_ref, hsem)

    @pl.when(j == 0)
    def _():
        h_copy.start()
        _drain(_mlp_step("first", u_ref, wu_ref, wd_ref, o_ref))

    @pl.when((j > 0) & (j < last))
    def _():
        _drain(_mlp_step("mid", u_ref, wu_ref, wd_ref, o_ref))

    @pl.when(j == last)
    def _():
        h_copy.wait()
        _drain(_mlp_step("last", u_ref, wu_ref, wd_ref, o_ref, hbuf_ref, ssq_ref))
        ms = jnp.sum(ssq_ref[...], axis=-1, keepdims=True) / D_MODEL
        o_ref[...] = o_ref[...] * lax.rsqrt(ms + EPS) * g_ref[...]


def _mlp(u, h, w_up, w_down, g):
    rows = u.shape[0]
    return pl.pallas_call(
        _mlp_kernel,
        grid=(rows // MLP_TM, D_FF // MLP_TF),
        in_specs=[
            pl.BlockSpec((MLP_TM, D_MODEL), lambda i, j: (i, 0)),
            pl.BlockSpec(memory_space=pl.ANY),
            pl.BlockSpec((D_MODEL, MLP_TF), lambda i, j: (0, j)),
            pl.BlockSpec((MLP_TF, D_MODEL), lambda i, j: (j, 0)),
            pl.BlockSpec((1, D_MODEL), lambda i, j: (0, 0)),
        ],
        out_specs=pl.BlockSpec((MLP_TM, D_MODEL), lambda i, j: (i, 0)),
        out_shape=jax.ShapeDtypeStruct((rows, D_MODEL), F32),
        scratch_shapes=[pltpu.VMEM((MLP_TM, D_MODEL), F32), pltpu.VMEM((MLP_TM, LANE), F32),
                        pltpu.SemaphoreType.DMA(())],
        compiler_params=_params(2),
        name="mlp",
    )(u, h, w_up, w_down, g)


def _t5_bucket(dist):
    max_exact = N_BUCKETS // 2
    d = jnp.maximum(dist, 0)
    ratio = jnp.maximum(d, max_exact).astype(F32) / max_exact
    large = max_exact + (jnp.log(ratio) / math.log(MAX_DISTANCE / max_exact)
                         * (N_BUCKETS - max_exact)).astype(jnp.int32)
    large = jnp.minimum(large, N_BUCKETS - 1)
    return jnp.where(d < max_exact, d, large)


BIAS_ROWS = 16


BIAS_BUCKETS_PER_PHASE = 8


def _bias_fill_splats(rb_ref, splat_ref):
    def fill(b, carry):
        for hd in range(ATT_HEADS):
            splat_ref[b * ATT_HEADS + hd] = jnp.full((SUBLANES, BLOCK), rb_ref[b, hd], F32) * LOG2E
        return carry

    lax.fori_loop(0, N_BUCKETS, fill, 0)


def _bias_lookup_phases(bkt, splat_ref, write):
    accs = [jnp.zeros(bkt.shape, F32)] * ATT_HEADS
    for b in range(N_BUCKETS):
        hit = (bkt >= b) & (bkt < b + 1)
        accs = [jnp.where(hit, jnp.concatenate([splat_ref[b * ATT_HEADS + hd]]
                                               * (BIAS_ROWS // SUBLANES), axis=0), a)
                for hd, a in enumerate(accs)]
        if b % BIAS_BUCKETS_PER_PHASE == BIAS_BUCKETS_PER_PHASE - 1:
            yield
    write(accs)


def _bias_step_phases(bkt_band_ref, bkt_meta_ref, band_ref, meta_ref, splat_ref):
    def write_band(tables):
        for hd, t in enumerate(tables):
            h, g = divmod(hd, ATT_GROUP)
            band_ref[h, :, g * BLOCK:(g + 1) * BLOCK] = t

    yield from _bias_lookup_phases(bkt_band_ref[...], splat_ref, write_band)
    for k in range(bkt_meta_ref.shape[0]):
        def write_meta(tables, k=k):
            for hd, t in enumerate(tables):
                h, g = divmod(hd, ATT_GROUP)
                meta_ref[k, h, :, g * BLOCK:(g + 1) * BLOCK] = t

        yield from _bias_lookup_phases(bkt_meta_ref[k], splat_ref, write_meta)


def _bias_buckets(nblk):
    c = jnp.arange(2 * BLOCK)[:, None]
    r = jnp.arange(BLOCK)[None, :]
    bkt_band = _t5_bucket(r + BLOCK - c).astype(jnp.int32)
    q_pos = BLOCK + jnp.arange(nblk * BLOCK)[None, :]
    m_pos = N_PAD + jnp.arange(N_META)[:, None]
    bkt_meta = _t5_bucket(q_pos - m_pos).astype(jnp.int32)
    return bkt_band, bkt_meta.reshape(N_META, nblk, BLOCK).transpose(1, 0, 2)


W_IN_OFFS = dict(q=0, k=1024, v=1280, mq=1536, mk=2048, mv=2560, mo=3584, gates=4608)
PREP_ROWS = 256
NAT_SRC = [W_IN_OFFS["mq"] // PREP_ROWS + t for t in range(NAT_COLS // PREP_ROWS)]
W3_SRC = ([W_IN_OFFS["q"] // PREP_ROWS + t for t in range(ATT_WIDTH // PREP_ROWS)]
          + [W_IN_OFFS["v"] // PREP_ROWS + t for t in range(ATT_KV_WIDTH // PREP_ROWS)]
          + [W_IN_OFFS["k"] // PREP_ROWS + t for t in range(ATT_KV_WIDTH // PREP_ROWS)]
          + [W_IN_OFFS["gates"] // PREP_ROWS])
N_NAT_BLK = len(NAT_SRC)
N_Q_BLK = ATT_WIDTH // PREP_ROWS
N_PREP = N_NAT_BLK + len(W3_SRC)
assert W_IN_OFFS["mq"] + NAT_COLS == W_IN_OFFS["gates"] and len(W3_SRC) * PREP_ROWS == W3_ROWS


def _win_prep_kernel(src_ref, w_ref, nat_ref, w3_ref):
    j = pl.program_id(0)
    wv = w_ref[0]

    @pl.when(j < N_NAT_BLK)
    def _():
        nat_ref[...] = wv.astype(BF16)

    @pl.when((j >= N_NAT_BLK) & (j < N_NAT_BLK + N_Q_BLK))
    def _():
        w3_ref[...] = (wv * (ATT_HEAD_DIM ** -0.5 * LOG2E)).astype(BF16)

    @pl.when((j >= N_NAT_BLK + N_Q_BLK) & (j < N_PREP - 1))
    def _():
        w3_ref[...] = wv.astype(BF16)

    @pl.when(j == N_PREP - 1)
    def _():
        row = lax.broadcasted_iota(jnp.int32, wv.shape, 0)
        w3_ref[...] = jnp.where(row < 2 * ML_HEADS, wv, 0.0).astype(BF16)


def _win_prep(w_in_t):
    src = jnp.asarray(NAT_SRC + W3_SRC, jnp.int32)
    d = w_in_t.shape[2]
    return pl.pallas_call(
        _win_prep_kernel,
        grid_spec=pltpu.PrefetchScalarGridSpec(
            num_scalar_prefetch=1,
            grid=(N_PREP,),
            in_specs=[pl.BlockSpec((1, PREP_ROWS, d), lambda j, src: (0, src[j], 0))],
            out_specs=[
                pl.BlockSpec((PREP_ROWS, d), lambda j, src: (jnp.minimum(j, N_NAT_BLK - 1), 0)),
                pl.BlockSpec((PREP_ROWS, d), lambda j, src: (jnp.maximum(j - N_NAT_BLK, 0), 0)),
            ],
        ),
        out_shape=[
            jax.ShapeDtypeStruct((NAT_COLS, d), BF16),
            jax.ShapeDtypeStruct((W3_ROWS, d), BF16),
        ],
        compiler_params=_params(1),
        name="w_in_prep",
    )(src, w_in_t)


def kernel(x, meta_tokens, w_in, conv_w, conv_b, b_igate, b_fgate, attn_sinks, rel_bias,
           mh_norm, w_out, norm_mix, norm_mlp, w_up, w_down, norm_final):
    batch, seq, d = x.shape
    assert batch == 2, "the two batches are processed one projection / mixer stage apart"
    nblk = seq // BLOCK
    x2 = x.reshape(batch * seq, d)
    lead = jnp.concatenate([jnp.zeros((N_PAD, d), x.dtype), meta_tokens.astype(x.dtype)], axis=0)
    g_mix = norm_mix[0][None].astype(F32)

    w_nat, w3 = _win_prep(jnp.swapaxes(w_in, 1, 2))
    gate_bias = jnp.pad(jnp.concatenate([b_igate[0], b_fgate[0]]).astype(F32),
                        (0, GATE_LANES - 2 * ML_HEADS))[None]

    proj0, proj_lead, (bias_band_t, bias_meta_t) = _inproj(
        x2, lead, g_mix, w_nat, w3, rel_bias, seq // IN_TM, nblk)
    consts = (attn_sinks[0].astype(F32), bias_band_t, bias_meta_t, gate_bias, _conv_shift_matrix(),
              conv_w[0].astype(F32), conv_b[0][None].astype(F32), mh_norm[0][None].astype(F32))
    state = _lead_state(proj_lead, consts)
    outs = _inproj_mixer(x2, seq // FUSED_TM, g_mix, w_nat, w3,
                         _mixer_args(consts, proj0, proj_lead, state), w_up[0], nblk)
    proj1, (att0, hm0, w_up_bf) = outs[:N_PROJ_OUT], outs[N_PROJ_OUT:]
    att1, hm1, w_out_bf, w_down_bf = _mixer(_mixer_args(consts, proj1, proj_lead, state),
                                            w_out[0], w_down[0], nblk)

    h2, u2 = _outproj(att0, att1, hm0, hm1, x2, w_out_bf, norm_mlp[0][None].astype(F32))
    out = _mlp(u2, h2, w_up_bf, w_down_bf, norm_final[None].astype(F32))
    return out.reshape(batch, seq, d)
```

```python
import math

import numpy as np
import jax
import jax.numpy as jnp
from jax import lax
from jax.experimental import pallas as pl
from jax.experimental.pallas import tpu as pltpu

F32 = jnp.float32
BF16 = jnp.bfloat16

D_MODEL = 2048
N_META = 16
BLOCK = 128
N_PAD = BLOCK - N_META
WINDOW = 128
ATT_HEADS = 16
ATT_KV_HEADS = 4
ATT_GROUP = ATT_HEADS // ATT_KV_HEADS
ATT_HEAD_DIM = 64
ATT_WIDTH = ATT_HEADS * ATT_HEAD_DIM
ATT_KV_WIDTH = ATT_KV_HEADS * ATT_HEAD_DIM
ML_HEADS = 4
ML_V_DIM = 256
ML_QK_DIM = 128
ML_WIDTH = ML_HEADS * ML_V_DIM
ML_QK_WIDTH = ML_HEADS * ML_QK_DIM
CONV_WIDTH = 4
GATE_SOFTCAP = 15.0
D_FF = 4 * D_MODEL
N_BUCKETS = 32
MAX_DISTANCE = 128
EPS = 1e-6
NEG = -1e30
LOG2E = math.log2(math.e)

NAT_MQK = 0
NAT_MV = 2 * ML_QK_WIDTH
NAT_MO = NAT_MV + ML_WIDTH
NAT_COLS = NAT_MO + ML_WIDTH
T_ROWS = ATT_WIDTH + ATT_KV_WIDTH
GATE_LANES = 128
W3_K = T_ROWS
W3_G = W3_K + ATT_KV_WIDTH
W3_ROWS = W3_G + 256

V7X_VMEM_LIMIT = 60 * 1024 * 1024
LANE = 128
SUBLANES = 8
CONTRACT_LAST = (((1,), (1,)), ((), ()))
CONTRACT_FIRST = (((0,), (0,)), ((), ()))


def _params(n_axes, vmem=V7X_VMEM_LIMIT):
    return pltpu.CompilerParams(dimension_semantics=("arbitrary",) * n_axes,
                                vmem_limit_bytes=vmem)


def _rms(x, g):
    return x * lax.rsqrt(jnp.mean(x * x, axis=-1, keepdims=True) + EPS) * g


_DONE = object()


def _drain(phases):
    for _ in phases:
        pass


def _interleave(streams, weights):
    live = list(zip(streams, weights))
    while live:
        for item in list(live):
            stream, weight = item
            for _ in range(weight):
                if next(stream, _DONE) is _DONE:
                    live.remove(item)
                    break
        yield


IN_TM = 1024
IN_NAT_STEPS = 3
IN_TN = NAT_COLS // IN_NAT_STEPS
MXU_COLS = 256
N_PROJ_OUT = 4


def _inproj_nat_phases(rows, u_ref, wn_ref, nat_ref):
    u = u_ref[0:rows, :]
    for c0 in range(0, IN_TN, MXU_COLS):
        cs = slice(c0, c0 + MXU_COLS)
        nat_ref[0:rows, cs] = lax.dot_general(u, wn_ref[cs, :], CONTRACT_LAST,
                                              preferred_element_type=F32).astype(BF16)
        yield


def _inproj_last_phases(rows, u_ref, w3_ref, kn_ref, t_ref, gt_ref):
    u = u_ref[0:rows, :]
    for c0 in range(0, T_ROWS, MXU_COLS):
        cs = slice(c0, c0 + MXU_COLS)
        t_ref[cs, 0:rows] = lax.dot_general(w3_ref[cs, :], u, CONTRACT_LAST,
                                            preferred_element_type=F32).astype(BF16)
        yield
    kn_ref[0:rows, :] = lax.dot_general(u, w3_ref[W3_K:W3_G, :], CONTRACT_LAST,
                                        preferred_element_type=F32).astype(BF16)
    yield
    gt_ref[0:rows, :] = lax.dot_general(u, w3_ref[W3_G:W3_G + GATE_LANES, :], CONTRACT_LAST,
                                        preferred_element_type=F32)
    yield


BIAS_PHASES_PER_ROUND = 3


def _inproj_kernel(x_ref, lead_ref, g_ref, wn_ref, w3_ref, rb_ref, bkt_band_ref, bkt_meta_ref,
                   nat_ref, kn_ref, t_ref, gt_ref, nat_lead_ref, kn_lead_ref, t_lead_ref, gt_lead_ref,
                   band_ref, meta_ref, u_ref, u_lead_ref, splat_ref):
    i = pl.program_id(0)
    j = pl.program_id(1)

    @pl.when(j == 0)
    def _():
        u_ref[...] = _rms(x_ref[...], g_ref[...]).astype(BF16)

    @pl.when((j == 0) & (i == 0))
    def _():
        u_lead_ref[...] = _rms(lead_ref[...], g_ref[...]).astype(BF16)
        _bias_fill_splats(rb_ref, splat_ref)

    def with_bias(phases):
        bias = _bias_step_phases(bkt_band_ref, bkt_meta_ref, band_ref, meta_ref, splat_ref)
        _drain(_interleave([phases, bias], [1, BIAS_PHASES_PER_ROUND]))

    @pl.when(j < IN_NAT_STEPS)
    def _():
        with_bias(_inproj_nat_phases(IN_TM, u_ref, wn_ref, nat_ref))

    @pl.when(j == IN_NAT_STEPS)
    def _():
        with_bias(_inproj_last_phases(IN_TM, u_ref, w3_ref, kn_ref, t_ref, gt_ref))

    @pl.when((i == 0) & (j < IN_NAT_STEPS))
    def _():
        _drain(_inproj_nat_phases(BLOCK, u_lead_ref, wn_ref, nat_lead_ref))

    @pl.when((i == 0) & (j == IN_NAT_STEPS))
    def _():
        _drain(_inproj_last_phases(BLOCK, u_lead_ref, w3_ref, kn_lead_ref, t_lead_ref, gt_lead_ref))


def _inproj_weight_specs():
    last_nat = IN_NAT_STEPS - 1
    return [
        pl.BlockSpec((IN_TN, D_MODEL), lambda i, j: (jnp.minimum(j, last_nat), 0)),
        pl.BlockSpec((W3_ROWS, D_MODEL), lambda i, j: (0, 0)),
    ]


def _inproj_out(tm, rows):
    last_nat = IN_NAT_STEPS - 1
    specs = [
        pl.BlockSpec((tm, IN_TN), lambda i, j: (i, jnp.minimum(j, last_nat))),
        pl.BlockSpec((tm, ATT_KV_WIDTH), lambda i, j: (i, 0)),
        pl.BlockSpec((T_ROWS, tm), lambda i, j: (0, i)),
        pl.BlockSpec((tm, GATE_LANES), lambda i, j: (i, 0)),
    ]
    shapes = [
        jax.ShapeDtypeStruct((rows, NAT_COLS), BF16),
        jax.ShapeDtypeStruct((rows, ATT_KV_WIDTH), BF16),
        jax.ShapeDtypeStruct((T_ROWS, rows), BF16),
        jax.ShapeDtypeStruct((rows, GATE_LANES), F32),
    ]
    return specs, shapes


def _inproj(x2, lead, g, w_nat, w3, rel_bias, n_tiles, nblk):
    out_specs, out_shape = _inproj_out(IN_TM, n_tiles * IN_TM)
    last_nat = IN_NAT_STEPS - 1
    n_steps = n_tiles * (IN_NAT_STEPS + 1)
    assert n_steps * BIAS_ROWS == 2 * BLOCK and nblk % n_steps == 0
    meta_per_step = nblk // n_steps
    bkt_band, bkt_meta = _bias_buckets(nblk)
    step = lambda i, j: i * (IN_NAT_STEPS + 1) + j
    bias_in_specs = [
        pl.BlockSpec(memory_space=pltpu.SMEM),
        pl.BlockSpec((BIAS_ROWS, BLOCK), lambda i, j: (step(i, j), 0)),
        pl.BlockSpec((meta_per_step, N_META, BLOCK), lambda i, j: (step(i, j), 0, 0)),
    ]
    bias_out_specs = [
        pl.BlockSpec((ATT_KV_HEADS, BIAS_ROWS, GQ), lambda i, j: (0, step(i, j), 0)),
        pl.BlockSpec((meta_per_step, ATT_KV_HEADS, N_META, GQ), lambda i, j: (step(i, j), 0, 0, 0)),
    ]
    bias_shapes = [
        jax.ShapeDtypeStruct((ATT_KV_HEADS, 2 * BLOCK, GQ), F32),
        jax.ShapeDtypeStruct((nblk, ATT_KV_HEADS, N_META, GQ), F32),
    ]
    lead_col = lambda i, j: jnp.where(i == 0, jnp.minimum(j, last_nat), last_nat)
    lead_specs = [
        pl.BlockSpec((BLOCK, IN_TN), lambda i, j: (0, lead_col(i, j))),
        pl.BlockSpec((BLOCK, ATT_KV_WIDTH), lambda i, j: (0, 0)),
        pl.BlockSpec((T_ROWS, BLOCK), lambda i, j: (0, 0)),
        pl.BlockSpec((BLOCK, GATE_LANES), lambda i, j: (0, 0)),
    ]
    lead_shapes = [
        jax.ShapeDtypeStruct((BLOCK, NAT_COLS), BF16),
        jax.ShapeDtypeStruct((BLOCK, ATT_KV_WIDTH), BF16),
        jax.ShapeDtypeStruct((T_ROWS, BLOCK), BF16),
        jax.ShapeDtypeStruct((BLOCK, GATE_LANES), F32),
    ]
    outs = pl.pallas_call(
        _inproj_kernel,
        grid=(n_tiles, IN_NAT_STEPS + 1),
        in_specs=[
            pl.BlockSpec((IN_TM, D_MODEL), lambda i, j: (i, 0)),
            pl.BlockSpec((BLOCK, D_MODEL), lambda i, j: (0, 0)),
            pl.BlockSpec((1, D_MODEL), lambda i, j: (0, 0)),
        ] + _inproj_weight_specs() + bias_in_specs,
        out_specs=out_specs + lead_specs + bias_out_specs,
        out_shape=out_shape + lead_shapes + bias_shapes,
        scratch_shapes=[pltpu.VMEM((IN_TM, D_MODEL), BF16), pltpu.VMEM((BLOCK, D_MODEL), BF16),
                        pltpu.VMEM((N_BUCKETS * ATT_HEADS, SUBLANES, BLOCK), F32)],
        compiler_params=_params(2),
        name="inproj",
    )(x2, lead, g, w_nat, w3, rel_bias.astype(F32), bkt_band, bkt_meta)
    return outs[:N_PROJ_OUT], outs[N_PROJ_OUT:2 * N_PROJ_OUT], outs[2 * N_PROJ_OUT:]


GQ = ATT_GROUP * BLOCK
ATT_SLABS_PER_PHASE = 2
FUSED_ATT_SLABS_PER_PHASE = 4


def _attn_prologue(n, kc_ref, kp_ref, km_ref):
    c = lax.broadcasted_iota(jnp.int32, (2 * BLOCK, BLOCK), 0)
    r = lax.broadcasted_iota(jnp.int32, (2 * BLOCK, BLOCK), 1)
    dist = r + BLOCK - c
    band_ok = (dist >= 0) & (dist < WINDOW) & ((c >= BLOCK) | (n >= 1))
    mm = lax.broadcasted_iota(jnp.int32, (N_META, BLOCK), 0)
    rr = lax.broadcasted_iota(jnp.int32, (N_META, BLOCK), 1)
    meta_ok = (n + 1) * BLOCK + rr >= N_PAD + mm
    kall = jnp.concatenate([kp_ref[...], kc_ref[...], km_ref[...]], axis=0)
    return band_ok, meta_ok, kall


BF16_ROWS = 16


def _ones_row(width):
    row = lax.broadcasted_iota(jnp.int32, (BF16_ROWS, width), 0)
    return jnp.where(row == 0, 1.0, 0.0).astype(BF16)


def _attn_head(h, pro, sink_ref, qt_ref, vc_ref, vp_ref, vl_ref, bb_ref, bm_ref, o_ref,
               slabs_per_phase=ATT_SLABS_PER_PHASE):
    band_ok, meta_ok, kall = pro
    lo = (h // 2) * LANE
    kpair = kall[:, lo:lo + LANE]
    qh = jnp.concatenate(
        [qt_ref[(ATT_GROUP * h + g) * ATT_HEAD_DIM:(ATT_GROUP * h + g + 1) * ATT_HEAD_DIM, :]
         for g in range(ATT_GROUP)], axis=1)
    zq = jnp.zeros_like(qh)
    qz = jnp.concatenate([qh, zq] if h % 2 == 0 else [zq, qh], axis=0)
    st = jnp.dot(kpair, qz, preferred_element_type=F32)
    yield
    zpad = jnp.zeros((N_PAD, BLOCK), BF16)
    p_band, p_lead, sink_p = [], [], []
    for g in range(ATT_GROUP):
        gs = slice(g * BLOCK, (g + 1) * BLOCK)
        sl = st[:, gs]
        sink = jnp.full((1, BLOCK), sink_ref[ATT_GROUP * h + g], F32) * LOG2E
        lb = jnp.where(band_ok, sl[:2 * BLOCK] + bb_ref[h, :, gs], NEG * LOG2E)
        lm = jnp.where(meta_ok, sl[2 * BLOCK:] + bm_ref[0, h, :, gs], NEG * LOG2E)
        m = jnp.maximum(jnp.max(lb, axis=0, keepdims=True), jnp.max(lm, axis=0, keepdims=True))
        m = jnp.maximum(m, sink)
        pb = jnp.exp2(lb - m)
        pm = jnp.exp2(lm - m)
        sink_p.append(jnp.exp2(sink - m))
        p_band.append(pb.astype(BF16))
        p_lead.append(jnp.concatenate([zpad, pm.astype(BF16)], axis=0))
        if g % slabs_per_phase == slabs_per_phase - 1:
            yield
    pbt = jnp.concatenate(p_band, axis=1)
    plt = jnp.concatenate(p_lead, axis=1)
    hs = slice(h * ATT_HEAD_DIM, (h + 1) * ATT_HEAD_DIM)
    vband = jnp.concatenate([vp_ref[hs, :], vc_ref[hs, :]], axis=1)
    ones_b = _ones_row(2 * BLOCK)
    ones_l = _ones_row(BLOCK)
    ot = (jnp.dot(jnp.concatenate([vband, ones_b], axis=0), pbt, preferred_element_type=F32)
          + jnp.dot(jnp.concatenate([vl_ref[hs, :], ones_l], axis=0), plt,
                    preferred_element_type=F32))
    for g in range(ATT_GROUP):
        gs = slice(g * BLOCK, (g + 1) * BLOCK)
        den = ot[ATT_HEAD_DIM:ATT_HEAD_DIM + 1, gs] + sink_p[g]
        row = (ATT_GROUP * h + g) * ATT_HEAD_DIM
        o_ref[row:row + ATT_HEAD_DIM, :] = (ot[:ATT_HEAD_DIM, gs] * (1.0 / den)).astype(BF16)


def _conv_shift_matrix():
    s = np.zeros((CONV_WIDTH * BLOCK, 2 * BLOCK), np.float32)
    for k in range(CONV_WIDTH):
        t = np.arange(BLOCK)
        s[k * BLOCK + t, BLOCK + t - k] = 1.0
    return jnp.asarray(s, BF16)


def _mlstm_gates(n, gate_ref, gb_ref):
    t_idx = lax.broadcasted_iota(jnp.int32, (BLOCK, GATE_LANES), 0)
    valid = (n > 0) | (t_idx >= N_PAD)
    pre = GATE_SOFTCAP * jnp.tanh((gate_ref[...] + gb_ref[...]) / GATE_SOFTCAP)
    log_i = jnp.where(valid, pre, NEG)
    log_sig = jnp.minimum(pre, 0.0) - jnp.log1p(jnp.exp(-jnp.abs(pre)))
    log_f = jnp.where(valid, log_sig, 0.0)
    row = lax.broadcasted_iota(jnp.int32, (BLOCK, BLOCK), 0)
    col = lax.broadcasted_iota(jnp.int32, (BLOCK, BLOCK), 1)
    causal = col <= row
    tril = jnp.where(causal, 1.0, 0.0).astype(F32)
    b_all = jnp.dot(tril, log_f, preferred_element_type=F32,
                    precision=lax.Precision.HIGHEST)
    return log_i, b_all, log_i.T, b_all.T, causal


def _mlstm_conv(p_ref, shift_ref, cw_ref, cb_ref, xprev_ref):
    xcur = p_ref[:, NAT_MQK:NAT_MV]
    xcat = jnp.concatenate([xprev_ref[...], xcur], axis=0)
    sh = jnp.dot(shift_ref[...], xcat, preferred_element_type=F32)
    xprev_ref[...] = xcur
    y = cb_ref[...]
    for k in range(CONV_WIDTH):
        y = y + cw_ref[CONV_WIDTH - 1 - k:CONV_WIDTH - k, :] * sh[k * BLOCK:(k + 1) * BLOCK]
    act = y * jax.nn.sigmoid(y)
    return act[:, :ML_QK_WIDTH] * (ML_QK_DIM ** -0.5), act[:, ML_QK_WIDTH:]


def _mlstm_head(h, gates, q_all, k_all, p_ref, mhn_ref, o_ref, c_ref, n_ref, m_ref):
    log_i, b_all, log_i_t, b_all_t, causal = gates
    q_f = q_all[:, h * ML_QK_DIM:(h + 1) * ML_QK_DIM]
    k_f = k_all[:, h * ML_QK_DIM:(h + 1) * ML_QK_DIM]
    q_h = q_f.astype(BF16)
    k_h = k_f.astype(BF16)
    v_h = p_ref[:, NAT_MV + h * ML_V_DIM:NAT_MV + (h + 1) * ML_V_DIM]
    b_col = b_all[:, ML_HEADS + h:ML_HEADS + h + 1]
    li_col = log_i[:, h:h + 1]
    b_row = b_all_t[ML_HEADS + h:ML_HEADS + h + 1, :]
    li_row = log_i_t[h:h + 1, :]
    b_last = b_all[BLOCK - 1:BLOCK, ML_HEADS + h:ML_HEADS + h + 1]
    c_prev = c_ref[h]
    n_prev = n_ref[h][0:1, :]
    m_prev = m_ref[h][0:1, 0:1]

    dmat = jnp.where(causal, b_col - b_row + li_row, -jnp.inf)
    inter = b_col + m_prev
    m_t = jnp.maximum(inter, jnp.max(dmat, axis=1, keepdims=True))
    s = lax.dot_general(q_h, k_h, CONTRACT_LAST, preferred_element_type=F32) * jnp.exp(dmat - m_t)
    yield
    a_t = jnp.exp(inter - m_t)
    num = (jnp.dot(s.astype(BF16), v_h, preferred_element_type=F32)
           + a_t * jnp.dot(q_h, c_prev.astype(BF16), preferred_element_type=F32))
    den = (jnp.sum(s, axis=1, keepdims=True)
           + a_t * jnp.sum(q_f * n_prev, axis=1, keepdims=True))
    den = jnp.maximum(jnp.abs(den), jnp.exp(-m_t))
    hh = num / den
    hh = hh * lax.rsqrt(jnp.mean(hh * hh, axis=-1, keepdims=True) + EPS)
    vs = slice(h * ML_V_DIM, (h + 1) * ML_V_DIM)
    hh = hh * mhn_ref[:, vs]
    m_o = p_ref[:, NAT_MO + h * ML_V_DIM:NAT_MO + (h + 1) * ML_V_DIM].astype(F32)
    o_ref[:, vs] = (hh * jax.nn.sigmoid(m_o)).astype(BF16)
    yield

    g_col = b_last - b_col + li_col
    m_loc = jnp.max(g_col, axis=0, keepdims=True)
    wk = jnp.exp(g_col - m_loc) * k_f
    c_loc = lax.dot_general(wk.astype(BF16), v_h, CONTRACT_FIRST,
                            preferred_element_type=F32)
    n_loc = jnp.sum(wk, axis=0, keepdims=True)
    m_new = jnp.maximum(b_last + m_prev, m_loc)
    a = jnp.exp(b_last + m_prev - m_new)
    cc = jnp.exp(m_loc - m_new)
    c_ref[h] = a * c_prev + cc * c_loc
    n_ref[h] = jnp.broadcast_to(a * n_prev + cc * n_loc, (8, ML_QK_DIM))
    m_ref[h] = jnp.broadcast_to(m_new, (8, LANE))


N_MIX_IN = 17
N_STATE = 4
MIXER_PHASES_PER_NAT_PHASE = 6
MIXER_PHASES_PER_LAST_PHASE = 2


N_ATT_IN = 10


def _attention_phases(n, att_in, att_ref, slabs_per_phase=ATT_SLABS_PER_PHASE):
    sink_ref, qt_ref, kc_ref, kp_ref, km_ref, vc_ref, vp_ref, vl_ref, bb_ref, bm_ref = att_in
    pro = _attn_prologue(n - 1, kc_ref, kp_ref, km_ref)
    for h in range(ATT_KV_HEADS):
        yield from _attn_head(h, pro, sink_ref, qt_ref, vc_ref, vp_ref, vl_ref, bb_ref, bm_ref, att_ref,
                              slabs_per_phase)


def _mlstm_phases(n, ml_in, hm_ref, state):
    p_ref, gate_ref, gb_ref, shift_ref, cw_ref, cb_ref, mhn_ref = ml_in
    c_ref, n_ref, m_ref, xprev_ref = state
    gates = _mlstm_gates(n, gate_ref, gb_ref)
    yield
    q_all, k_all = _mlstm_conv(p_ref, shift_ref, cw_ref, cb_ref, xprev_ref)
    yield
    for h in range(ML_HEADS):
        yield from _mlstm_head(h, gates, q_all, k_all, p_ref, mhn_ref, hm_ref, c_ref, n_ref, m_ref)


def _mixer_phases(n, mix_in, att_ref, hm_ref, state):
    return _interleave([_mlstm_phases(n, mix_in[N_ATT_IN:], hm_ref, state),
                        _attention_phases(n, mix_in[:N_ATT_IN], att_ref, FUSED_ATT_SLABS_PER_PHASE)],
                       [1, 1])


def _load_state(state, state_in):
    for dst, src in zip(state, state_in):
        dst[...] = src[...]


def _state_shapes():
    return [
        jax.ShapeDtypeStruct((ML_HEADS, ML_QK_DIM, ML_V_DIM), F32),
        jax.ShapeDtypeStruct((ML_HEADS, SUBLANES, ML_QK_DIM), F32),
        jax.ShapeDtypeStruct((ML_HEADS, SUBLANES, LANE), F32),
        jax.ShapeDtypeStruct((BLOCK, 2 * ML_QK_WIDTH), BF16),
    ]


def _state_scratch():
    return [pltpu.VMEM(s.shape, s.dtype) for s in _state_shapes()]


def _whole(shape):
    return pl.BlockSpec(shape, lambda *ids: (0,) * len(shape))


def _mixer_in_specs(step):
    cur = lambda *ids: step(*ids)
    prev = lambda *ids: jnp.maximum(step(*ids) - 1, 0)
    v_row = ATT_WIDTH // ATT_KV_WIDTH
    meta_rows = N_PAD // N_META
    return [
        pl.BlockSpec(memory_space=pltpu.SMEM),
        pl.BlockSpec((ATT_WIDTH, BLOCK), lambda *ids: (0, cur(*ids))),
        pl.BlockSpec((BLOCK, ATT_KV_WIDTH), lambda *ids: (cur(*ids), 0)),
        pl.BlockSpec((BLOCK, ATT_KV_WIDTH), lambda *ids: (prev(*ids), 0)),
        pl.BlockSpec((N_META, ATT_KV_WIDTH), lambda *ids: (meta_rows, 0)),
        pl.BlockSpec((ATT_KV_WIDTH, BLOCK), lambda *ids: (v_row, cur(*ids))),
        pl.BlockSpec((ATT_KV_WIDTH, BLOCK), lambda *ids: (v_row, prev(*ids))),
        pl.BlockSpec((ATT_KV_WIDTH, BLOCK), lambda *ids: (v_row, 0)),
        _whole((ATT_KV_HEADS, 2 * BLOCK, GQ)),
        pl.BlockSpec((1, ATT_KV_HEADS, N_META, GQ), lambda *ids: (cur(*ids), 0, 0, 0)),
        pl.BlockSpec((BLOCK, NAT_COLS), lambda *ids: (cur(*ids), 0)),
        pl.BlockSpec((BLOCK, GATE_LANES), lambda *ids: (cur(*ids), 0)),
        _whole((1, GATE_LANES)),
        _whole((CONV_WIDTH * BLOCK, 2 * BLOCK)),
        _whole((CONV_WIDTH, 2 * ML_QK_WIDTH)),
        _whole((1, 2 * ML_QK_WIDTH)),
        _whole((1, ML_WIDTH)),
    ] + [_whole(s.shape) for s in _state_shapes()]


def _mixer_args(consts, proj_b, proj_lead, state):
    sinks, bias_band_t, bias_meta_t, gate_bias, shift, conv_w, conv_b, mh_norm = consts
    nat_b, kn_b, projt_b, gt_b = proj_b
    _, kn_lead, projt_lead, _ = proj_lead
    return [sinks, projt_b, kn_b, kn_b, kn_lead, projt_b, projt_b, projt_lead, bias_band_t,
            bias_meta_t, nat_b, gt_b, gate_bias, shift, conv_w, conv_b, mh_norm] + list(state)


def _mixer_out(step, n_rows):
    specs = [
        pl.BlockSpec((ATT_WIDTH, BLOCK), lambda *ids: (0, step(*ids))),
        pl.BlockSpec((BLOCK, ML_WIDTH), lambda *ids: (step(*ids), 0)),
    ]
    shapes = [
        jax.ShapeDtypeStruct((ATT_WIDTH, n_rows), BF16),
        jax.ShapeDtypeStruct((n_rows, ML_WIDTH), BF16),
    ]
    return specs, shapes


def _lead_state_kernel(p_ref, gate_ref, gb_ref, shift_ref, cw_ref, cb_ref, mhn_ref,
                       c_ref, n_ref, m_ref, xprev_ref, hm_ref):
    for ref in (c_ref, n_ref, m_ref, xprev_ref):
        ref[...] = jnp.zeros_like(ref)
    gates = _mlstm_gates(0, gate_ref, gb_ref)
    q_all, k_all = _mlstm_conv(p_ref, shift_ref, cw_ref, cb_ref, xprev_ref)
    for h in range(ML_HEADS):
        _drain(_mlstm_head(h, gates, q_all, k_all, p_ref, mhn_ref, hm_ref, c_ref, n_ref, m_ref))


def _lead_state(proj_lead, consts):
    nat_lead, _, _, gates_lead = proj_lead
    _, _, _, gate_bias, shift, conv_w, conv_b, mh_norm = consts
    shapes = _state_shapes()
    return pl.pallas_call(
        _lead_state_kernel,
        grid=(1,),
        in_specs=[
            _whole((BLOCK, NAT_COLS)),
            _whole((BLOCK, GATE_LANES)),
            _whole((1, GATE_LANES)),
            _whole((CONV_WIDTH * BLOCK, 2 * BLOCK)),
            _whole((CONV_WIDTH, 2 * ML_QK_WIDTH)),
            _whole((1, 2 * ML_QK_WIDTH)),
            _whole((1, ML_WIDTH)),
        ],
        out_specs=[_whole(s.shape) for s in shapes],
        out_shape=shapes,
        scratch_shapes=[pltpu.VMEM((BLOCK, ML_WIDTH), BF16)],
        compiler_params=_params(1),
        name="lead_state",
    )(nat_lead, gates_lead, gate_bias, shift, conv_w, conv_b, mh_norm)


FUSED_TM = 512


def _inproj_mixer_kernel(*refs):
    x_ref, g_ref, wn_ref, w3_ref = refs[:4]
    mix_in = refs[4:4 + N_MIX_IN]
    state_in = refs[4 + N_MIX_IN:4 + N_MIX_IN + N_STATE]
    wu_ref = refs[4 + N_MIX_IN + N_STATE]
    (nat_ref, kn_ref, t_ref, gt_ref, att_ref, hm_ref,
     wu_bf_ref) = refs[5 + N_MIX_IN + N_STATE:-(1 + N_STATE)]
    u_ref = refs[-(1 + N_STATE)]
    state = refs[-N_STATE:]
    i = pl.program_id(0)
    j = pl.program_id(1)
    step = i * (IN_NAT_STEPS + 1) + j
    wu_bf_ref[...] = wu_ref[...].astype(BF16)

    @pl.when(j == 0)
    def _():
        u_ref[...] = _rms(x_ref[...], g_ref[...]).astype(BF16)

    @pl.when(step == 0)
    def _():
        _load_state(state, state_in)

    @pl.when(j < IN_NAT_STEPS)
    def _():
        _drain(_interleave([_inproj_nat_phases(FUSED_TM, u_ref, wn_ref, nat_ref),
                            _mixer_phases(step + 1, mix_in, att_ref, hm_ref, state)],
                           [1, MIXER_PHASES_PER_NAT_PHASE]))

    @pl.when(j == IN_NAT_STEPS)
    def _():
        _drain(_interleave([_inproj_last_phases(FUSED_TM, u_ref, w3_ref, kn_ref, t_ref, gt_ref),
                            _mixer_phases(step + 1, mix_in, att_ref, hm_ref, state)],
                           [1, MIXER_PHASES_PER_LAST_PHASE]))


def _inproj_mixer(x2, row_tile0, g, w_nat, w3, mixer_args, w_up, nblk):
    n_tiles = nblk * BLOCK // FUSED_TM
    assert n_tiles * (IN_NAT_STEPS + 1) == nblk
    step = lambda i, j: i * (IN_NAT_STEPS + 1) + j
    proj_specs, proj_shapes = _inproj_out(FUSED_TM, n_tiles * FUSED_TM)
    mix_specs, mix_shapes = _mixer_out(step, nblk * BLOCK)
    wu_spec = pl.BlockSpec((D_MODEL, D_FF // nblk), lambda i, j: (0, step(i, j)))
    return pl.pallas_call(
        _inproj_mixer_kernel,
        grid=(n_tiles, IN_NAT_STEPS + 1),
        in_specs=[
            pl.BlockSpec((FUSED_TM, D_MODEL), lambda i, j: (row_tile0 + i, 0)),
            pl.BlockSpec((1, D_MODEL), lambda i, j: (0, 0)),
        ] + _inproj_weight_specs() + _mixer_in_specs(step) + [wu_spec],
        out_specs=proj_specs + mix_specs + [wu_spec],
        out_shape=proj_shapes + mix_shapes + [jax.ShapeDtypeStruct(w_up.shape, BF16)],
        scratch_shapes=[pltpu.VMEM((FUSED_TM, D_MODEL), BF16)] + _state_scratch(),
        compiler_params=_params(2),
        name="inproj_mixer",
    )(x2, g, w_nat, w3, *mixer_args, w_up)


MIXER_PAIR = 2
PAIR_ROWS = MIXER_PAIR * BLOCK
ATT_PHASES_PER_MLSTM_PHASE = 4
MLSTM_CHUNK_LAG = 3


def _mixer_kernel(*refs):
    (sink_ref, qt_ref, kc_ref, kp_ref, km_ref, vc_ref, vp_ref, vl_ref, bb_ref, bm_ref,
     p_ref, gate_ref, gb_ref, shift_ref, cw_ref, cb_ref, mhn_ref) = refs[:N_MIX_IN]
    state_in = refs[N_MIX_IN:N_MIX_IN + N_STATE]
    wo_ref, wd_ref = refs[N_MIX_IN + N_STATE:N_MIX_IN + N_STATE + 2]
    att_ref, hm_ref, wo_bf_ref, wd_bf_ref = refs[N_MIX_IN + N_STATE + 2:-N_STATE]
    state = refs[-N_STATE:]
    step = pl.program_id(0)

    @pl.when(step == 0)
    def _():
        _load_state(state, state_in)

    wo_bf_ref[...] = wo_ref[...].astype(BF16)
    wd_bf_ref[...] = wd_ref[...].astype(BF16)

    rows = lambda ref, k: ref.at[k * BLOCK:(k + 1) * BLOCK]
    cols = lambda ref, k: ref.at[:, k * BLOCK:(k + 1) * BLOCK]
    attention, chunks = [], []
    for k in range(MIXER_PAIR):
        n = MIXER_PAIR * step + 1 + k
        kp_k = kp_ref if k == 0 else rows(kc_ref, k - 1)
        vp_k = vp_ref if k == 0 else cols(vc_ref, k - 1)
        att_in = (sink_ref, cols(qt_ref, k), rows(kc_ref, k), kp_k, km_ref, cols(vc_ref, k), vp_k,
                  vl_ref, bb_ref, bm_ref.at[k:k + 1])
        attention.append(_attention_phases(n, att_in, cols(att_ref, k)))
        chunks.append((n, (rows(p_ref, k), rows(gate_ref, k), gb_ref, shift_ref, cw_ref, cb_ref,
                           mhn_ref), rows(hm_ref, k)))

    def mlstm_chunk(k):
        for _ in range(k * MLSTM_CHUNK_LAG):
            yield
        n, ml_in, hm_k = chunks[k]
        yield from _mlstm_phases(n, ml_in, hm_k, state)

    _drain(_interleave([mlstm_chunk(k) for k in range(MIXER_PAIR)] + attention,
                       [1] * MIXER_PAIR + [ATT_PHASES_PER_MLSTM_PHASE] * MIXER_PAIR))


def _mixer(mixer_args, w_out, w_down, nblk):
    n_steps = nblk // MIXER_PAIR
    before = lambda s: jnp.maximum(MIXER_PAIR * s - 1, 0)
    v_row = ATT_WIDTH // ATT_KV_WIDTH
    in_specs = [
        pl.BlockSpec(memory_space=pltpu.SMEM),
        pl.BlockSpec((ATT_WIDTH, PAIR_ROWS), lambda s: (0, s)),
        pl.BlockSpec((PAIR_ROWS, ATT_KV_WIDTH), lambda s: (s, 0)),
        pl.BlockSpec((BLOCK, ATT_KV_WIDTH), lambda s: (before(s), 0)),
        pl.BlockSpec((N_META, ATT_KV_WIDTH), lambda s: (N_PAD // N_META, 0)),
        pl.BlockSpec((ATT_KV_WIDTH, PAIR_ROWS), lambda s: (v_row, s)),
        pl.BlockSpec((ATT_KV_WIDTH, BLOCK), lambda s: (v_row, before(s))),
        pl.BlockSpec((ATT_KV_WIDTH, BLOCK), lambda s: (v_row, 0)),
        _whole((ATT_KV_HEADS, 2 * BLOCK, GQ)),
        pl.BlockSpec((MIXER_PAIR, ATT_KV_HEADS, N_META, GQ), lambda s: (s, 0, 0, 0)),
        pl.BlockSpec((PAIR_ROWS, NAT_COLS), lambda s: (s, 0)),
        pl.BlockSpec((PAIR_ROWS, GATE_LANES), lambda s: (s, 0)),
        _whole((1, GATE_LANES)),
        _whole((CONV_WIDTH * BLOCK, 2 * BLOCK)),
        _whole((CONV_WIDTH, 2 * ML_QK_WIDTH)),
        _whole((1, 2 * ML_QK_WIDTH)),
        _whole((1, ML_WIDTH)),
    ] + [_whole(s.shape) for s in _state_shapes()]
    assert len(in_specs) == N_MIX_IN + N_STATE
    w_specs = [
        pl.BlockSpec((w_out.shape[0] // n_steps, D_MODEL), lambda s: (s, 0)),
        pl.BlockSpec((D_FF // n_steps, D_MODEL), lambda s: (s, 0)),
    ]
    return pl.pallas_call(
        _mixer_kernel,
        grid=(n_steps,),
        in_specs=in_specs + w_specs,
        out_specs=[
            pl.BlockSpec((ATT_WIDTH, PAIR_ROWS), lambda s: (0, s)),
            pl.BlockSpec((PAIR_ROWS, ML_WIDTH), lambda s: (s, 0)),
        ] + w_specs,
        out_shape=[
            jax.ShapeDtypeStruct((ATT_WIDTH, nblk * BLOCK), BF16),
            jax.ShapeDtypeStruct((nblk * BLOCK, ML_WIDTH), BF16),
        ] + [jax.ShapeDtypeStruct(w.shape, BF16) for w in (w_out, w_down)],
        scratch_shapes=_state_scratch(),
        compiler_params=_params(1),
        name="mixer",
    )(*mixer_args, w_out, w_down)


OUT_TM = 512


def _outproj_kernel(att0_ref, att1_ref, hm0_ref, hm1_ref, x_ref, wa_ref, wm_ref, g_ref,
                    h_ref, u_ref):
    i = pl.program_id(0)
    half = pl.num_programs(0) // 2

    def body(att_ref, hm_ref):
        h = (x_ref[...]
             + lax.dot_general(att_ref[...], wa_ref[...], CONTRACT_FIRST, preferred_element_type=F32)
             + jnp.dot(hm_ref[...], wm_ref[...], preferred_element_type=F32))
        h_ref[...] = h
        u_ref[...] = _rms(h, g_ref[...]).astype(BF16)

    @pl.when(i < half)
    def _():
        body(att0_ref, hm0_ref)

    @pl.when(i >= half)
    def _():
        body(att1_ref, hm1_ref)


def _outproj(att0, att1, hm0, hm1, x2, w_out, g):
    rows = x2.shape[0]
    n = rows // OUT_TM
    half = n // 2
    first = lambda i: jnp.minimum(i, half - 1)
    second = lambda i: jnp.maximum(i - half, 0)
    return pl.pallas_call(
        _outproj_kernel,
        grid=(n,),
        in_specs=[
            pl.BlockSpec((ATT_WIDTH, OUT_TM), lambda i: (0, first(i))),
            pl.BlockSpec((ATT_WIDTH, OUT_TM), lambda i: (0, second(i))),
            pl.BlockSpec((OUT_TM, ML_WIDTH), lambda i: (first(i), 0)),
            pl.BlockSpec((OUT_TM, ML_WIDTH), lambda i: (second(i), 0)),
            pl.BlockSpec((OUT_TM, D_MODEL), lambda i: (i, 0)),
            pl.BlockSpec((ATT_WIDTH, D_MODEL), lambda i: (0, 0)),
            pl.BlockSpec((ML_WIDTH, D_MODEL), lambda i: (ATT_WIDTH // ML_WIDTH, 0)),
            pl.BlockSpec((1, D_MODEL), lambda i: (0, 0)),
        ],
        out_specs=[
            pl.BlockSpec((OUT_TM, D_MODEL), lambda i: (i, 0)),
            pl.BlockSpec((OUT_TM, D_MODEL), lambda i: (i, 0)),
        ],
        out_shape=[
            jax.ShapeDtypeStruct((rows, D_MODEL), F32),
            jax.ShapeDtypeStruct((rows, D_MODEL), BF16),
        ],
        compiler_params=_params(1),
        name="outproj",
    )(att0, att1, hm0, hm1, x2, w_out, w_out, g)


MLP_TM = 1024
MLP_TF = 1024


def _mlp_step(mode, u_ref, wu_ref, wd_ref, o_ref, hbuf_ref=None, ssq_ref=None):
    acts = []
    for c in range(MLP_TF // MXU_COLS):
        cs = slice(c * MXU_COLS, (c + 1) * MXU_COLS)
        a = jnp.dot(u_ref[...], wu_ref[:, cs], preferred_element_type=F32)
        acts.append(jnp.square(jnp.maximum(a, 0.0)).astype(BF16))
        yield
    act = jnp.concatenate(acts, axis=1)
    for c in range(D_MODEL // MXU_COLS):
        cs = slice(c * MXU_COLS, (c + 1) * MXU_COLS)
        d = jnp.dot(act, wd_ref[:, cs], preferred_element_type=F32)
        if mode == "first":
            o_ref[:, cs] = d
        elif mode == "mid":
            o_ref[:, cs] += d
        else:
            v = o_ref[:, cs] + d + hbuf_ref[:, cs]
            o_ref[:, cs] = v
            sq = v * v
            part = sum(sq[:, k * LANE:(k + 1) * LANE] for k in range(MXU_COLS // LANE))
            if c == 0:
                ssq_ref[...] = part
            else:
                ssq_ref[...] += part
        yield


def _mlp_kernel(u_ref, h_hbm, wu_ref, wd_ref, g_ref, o_ref, hbuf_ref, ssq_ref, hsem):
    i = pl.program_id(0)
    j = pl.program_id(1)
    last = pl.num_programs(1) - 1
    r0 = pl.multiple_of(i * MLP_TM, MLP_TM)
    h_copy = pltpu.make_async_copy(h_hbm.at[pl.ds(r0, MLP_TM), :], hbuf_ref, hsem)

    @pl.when(j == 0)
    def _():
        h_copy.start()
        _drain(_mlp_step("first", u_ref, wu_ref, wd_ref, o_ref))

    @pl.when((j > 0) & (j < last))
    def _():
        _drain(_mlp_step("mid", u_ref, wu_ref, wd_ref, o_ref))

    @pl.when(j == last)
    def _():
        h_copy.wait()
        _drain(_mlp_step("last", u_ref, wu_ref, wd_ref, o_ref, hbuf_ref, ssq_ref))
        ms = jnp.sum(ssq_ref[...], axis=-1, keepdims=True) / D_MODEL
        o_ref[...] = o_ref[...] * lax.rsqrt(ms + EPS) * g_ref[...]


def _mlp(u, h, w_up, w_down, g):
    rows = u.shape[0]
    return pl.pallas_call(
        _mlp_kernel,
        grid=(rows // MLP_TM, D_FF // MLP_TF),
        in_specs=[
            pl.BlockSpec((MLP_TM, D_MODEL), lambda i, j: (i, 0)),
            pl.BlockSpec(memory_space=pl.ANY),
            pl.BlockSpec((D_MODEL, MLP_TF), lambda i, j: (0, j)),
            pl.BlockSpec((MLP_TF, D_MODEL), lambda i, j: (j, 0)),
            pl.BlockSpec((1, D_MODEL), lambda i, j: (0, 0)),
        ],
        out_specs=pl.BlockSpec((MLP_TM, D_MODEL), lambda i, j: (i, 0)),
        out_shape=jax.ShapeDtypeStruct((rows, D_MODEL), F32),
        scratch_shapes=[pltpu.VMEM((MLP_TM, D_MODEL), F32), pltpu.VMEM((MLP_TM, LANE), F32),
                        pltpu.SemaphoreType.DMA(())],
        compiler_params=_params(2),
        name="mlp",
    )(u, h, w_up, w_down, g)


def _t5_bucket(dist):
    max_exact = N_BUCKETS // 2
    d = jnp.maximum(dist, 0)
    ratio = jnp.maximum(d, max_exact).astype(F32) / max_exact
    large = max_exact + (jnp.log(ratio) / math.log(MAX_DISTANCE / max_exact)
                         * (N_BUCKETS - max_exact)).astype(jnp.int32)
    large = jnp.minimum(large, N_BUCKETS - 1)
    return jnp.where(d < max_exact, d, large)


BIAS_ROWS = 16


BIAS_BUCKETS_PER_PHASE = 8


def _bias_fill_splats(rb_ref, splat_ref):
    def fill(b, carry):
        for hd in range(ATT_HEADS):
            splat_ref[b * ATT_HEADS + hd] = jnp.full((SUBLANES, BLOCK), rb_ref[b, hd], F32) * LOG2E
        return carry

    lax.fori_loop(0, N_BUCKETS, fill, 0)


def _bias_lookup_phases(bkt, splat_ref, write):
    accs = [jnp.zeros(bkt.shape, F32)] * ATT_HEADS
    for b in range(N_BUCKETS):
        hit = (bkt >= b) & (bkt < b + 1)
        accs = [jnp.where(hit, jnp.concatenate([splat_ref[b * ATT_HEADS + hd]]
                                               * (BIAS_ROWS // SUBLANES), axis=0), a)
                for hd, a in enumerate(accs)]
        if b % BIAS_BUCKETS_PER_PHASE == BIAS_BUCKETS_PER_PHASE - 1:
            yield
    write(accs)


def _bias_step_phases(bkt_band_ref, bkt_meta_ref, band_ref, meta_ref, splat_ref):
    def write_band(tables):
        for hd, t in enumerate(tables):
            h, g = divmod(hd, ATT_GROUP)
            band_ref[h, :, g * BLOCK:(g + 1) * BLOCK] = t

    yield from _bias_lookup_phases(bkt_band_ref[...], splat_ref, write_band)
    for k in range(bkt_meta_ref.shape[0]):
        def write_meta(tables, k=k):
            for hd, t in enumerate(tables):
                h, g = divmod(hd, ATT_GROUP)
                meta_ref[k, h, :, g * BLOCK:(g + 1) * BLOCK] = t

        yield from _bias_lookup_phases(bkt_meta_ref[k], splat_ref, write_meta)


def _bias_buckets(nblk):
    c = jnp.arange(2 * BLOCK)[:, None]
    r = jnp.arange(BLOCK)[None, :]
    bkt_band = _t5_bucket(r + BLOCK - c).astype(jnp.int32)
    q_pos = BLOCK + jnp.arange(nblk * BLOCK)[None, :]
    m_pos = N_PAD + jnp.arange(N_META)[:, None]
    bkt_meta = _t5_bucket(q_pos - m_pos).astype(jnp.int32)
    return bkt_band, bkt_meta.reshape(N_META, nblk, BLOCK).transpose(1, 0, 2)


W_IN_OFFS = dict(q=0, k=1024, v=1280, mq=1536, mk=2048, mv=2560, mo=3584, gates=4608)
PREP_ROWS = 256
NAT_SRC = [W_IN_OFFS["mq"] // PREP_ROWS + t for t in range(NAT_COLS // PREP_ROWS)]
W3_SRC = ([W_IN_OFFS["q"] // PREP_ROWS + t for t in range(ATT_WIDTH // PREP_ROWS)]
          + [W_IN_OFFS["v"] // PREP_ROWS + t for t in range(ATT_KV_WIDTH // PREP_ROWS)]
          + [W_IN_OFFS["k"] // PREP_ROWS + t for t in range(ATT_KV_WIDTH // PREP_ROWS)]
          + [W_IN_OFFS["gates"] // PREP_ROWS])
N_NAT_BLK = len(NAT_SRC)
N_Q_BLK = ATT_WIDTH // PREP_ROWS
N_PREP = N_NAT_BLK + len(W3_SRC)
assert W_IN_OFFS["mq"] + NAT_COLS == W_IN_OFFS["gates"] and len(W3_SRC) * PREP_ROWS == W3_ROWS


def _win_prep_kernel(src_ref, w_ref, nat_ref, w3_ref):
    j = pl.program_id(0)
    wv = w_ref[0]

    @pl.when(j < N_NAT_BLK)
    def _():
        nat_ref[...] = wv.astype(BF16)

    @pl.when((j >= N_NAT_BLK) & (j < N_NAT_BLK + N_Q_BLK))
    def _():
        w3_ref[...] = (wv * (ATT_HEAD_DIM ** -0.5 * LOG2E)).astype(BF16)

    @pl.when((j >= N_NAT_BLK + N_Q_BLK) & (j < N_PREP - 1))
    def _():
        w3_ref[...] = wv.astype(BF16)

    @pl.when(j == N_PREP - 1)
    def _():
        row = lax.broadcasted_iota(jnp.int32, wv.shape, 0)
        w3_ref[...] = jnp.where(row < 2 * ML_HEADS, wv, 0.0).astype(BF16)


def _win_prep(w_in_t):
    src = jnp.asarray(NAT_SRC + W3_SRC, jnp.int32)
    d = w_in_t.shape[2]
    return pl.pallas_call(
        _win_prep_kernel,
        grid_spec=pltpu.PrefetchScalarGridSpec(
            num_scalar_prefetch=1,
            grid=(N_PREP,),
            in_specs=[pl.BlockSpec((1, PREP_ROWS, d), lambda j, src: (0, src[j], 0))],
            out_specs=[
                pl.BlockSpec((PREP_ROWS, d), lambda j, src: (jnp.minimum(j, N_NAT_BLK - 1), 0)),
                pl.BlockSpec((PREP_ROWS, d), lambda j, src: (jnp.maximum(j - N_NAT_BLK, 0), 0)),
            ],
        ),
        out_shape=[
            jax.ShapeDtypeStruct((NAT_COLS, d), BF16),
            jax.ShapeDtypeStruct((W3_ROWS, d), BF16),
        ],
        compiler_params=_params(1),
        name="w_in_prep",
    )(src, w_in_t)


def kernel(x, meta_tokens, w_in, conv_w, conv_b, b_igate, b_fgate, attn_sinks, rel_bias,
           mh_norm, w_out, norm_mix, norm_mlp, w_up, w_down, norm_final):
    batch, seq, d = x.shape
    assert batch == 2, "the two batches are processed one projection / mixer stage apart"
    nblk = seq // BLOCK
    x2 = x.reshape(batch * seq, d)
    lead = jnp.concatenate([jnp.zeros((N_PAD, d), x.dtype), meta_tokens.astype(x.dtype)], axis=0)
    g_mix = norm_mix[0][None].astype(F32)

    w_nat, w3 = _win_prep(jnp.swapaxes(w_in, 1, 2))
    gate_bias = jnp.pad(jnp.concatenate([b_igate[0], b_fgate[0]]).astype(F32),
                        (0, GATE_LANES - 2 * ML_HEADS))[None]

    proj0, proj_lead, (bias_band_t, bias_meta_t) = _inproj(
        x2, lead, g_mix, w_nat, w3, rel_bias, seq // IN_TM, nblk)
    consts = (attn_sinks[0].astype(F32), bias_band_t, bias_meta_t, gate_bias, _conv_shift_matrix(),
              conv_w[0].astype(F32), conv_b[0][None].astype(F32), mh_norm[0][None].astype(F32))
    state = _lead_state(proj_lead, consts)
    outs = _inproj_mixer(x2, seq // FUSED_TM, g_mix, w_nat, w3,
                         _mixer_args(consts, proj0, proj_lead, state), w_up[0], nblk)
    proj1, (att0, hm0, w_up_bf) = outs[:N_PROJ_OUT], outs[N_PROJ_OUT:]
    att1, hm1, w_out_bf, w_down_bf = _mixer(_mixer_args(consts, proj1, proj_lead, state),
                                            w_out[0], w_down[0], nblk)

    h2, u2 = _outproj(att0, att1, hm0, hm1, x2, w_out_bf, norm_mlp[0][None].astype(F32))
    out = _mlp(u2, h2, w_up_bf, w_down_bf, norm_final[None].astype(F32))
    return out.reshape(batch, seq, d)
```

```python
import math

import numpy as np
import jax
import jax.numpy as jnp
from jax import lax
from jax.experimental import pallas as pl
from jax.experimental.pallas import tpu as pltpu

F32 = jnp.float32
BF16 = jnp.bfloat16

D_MODEL = 2048
N_META = 16
BLOCK = 128
N_PAD = BLOCK - N_META
WINDOW = 128
ATT_HEADS = 16
ATT_KV_HEADS = 4
ATT_GROUP = ATT_HEADS // ATT_KV_HEADS
ATT_HEAD_DIM = 64
ATT_WIDTH = ATT_HEADS * ATT_HEAD_DIM
ATT_KV_WIDTH = ATT_KV_HEADS * ATT_HEAD_DIM
ML_HEADS = 4
ML_V_DIM = 256
ML_QK_DIM = 128
ML_WIDTH = ML_HEADS * ML_V_DIM
ML_QK_WIDTH = ML_HEADS * ML_QK_DIM
CONV_WIDTH = 4
GATE_SOFTCAP = 15.0
D_FF = 4 * D_MODEL
N_BUCKETS = 32
MAX_DISTANCE = 128
EPS = 1e-6
NEG = -1e30
LOG2E = math.log2(math.e)

NAT_MQK = 0
NAT_MV = 2 * ML_QK_WIDTH
NAT_MO = NAT_MV + ML_WIDTH
NAT_COLS = NAT_MO + ML_WIDTH
T_ROWS = ATT_WIDTH + ATT_KV_WIDTH
GATE_LANES = 128
W3_K = T_ROWS
W3_G = W3_K + ATT_KV_WIDTH
W3_ROWS = W3_G + 256

V7X_VMEM_LIMIT = 60 * 1024 * 1024
LANE = 128
SUBLANES = 8
CONTRACT_LAST = (((1,), (1,)), ((), ()))
CONTRACT_FIRST = (((0,), (0,)), ((), ()))


def _params(n_axes, vmem=V7X_VMEM_LIMIT):
    return pltpu.CompilerParams(dimension_semantics=("arbitrary",) * n_axes,
                                vmem_limit_bytes=vmem)


def _rms(x, g):
    return x * lax.rsqrt(jnp.mean(x * x, axis=-1, keepdims=True) + EPS) * g


_DONE = object()


def _drain(phases):
    for _ in phases:
        pass


def _interleave(streams, weights):
    live = list(zip(streams, weights))
    while live:
        for item in list(live):
            stream, weight = item
            for _ in range(weight):
                if next(stream, _DONE) is _DONE:
                    live.remove(item)
                    break
        yield


IN_TM = 1024
IN_NAT_STEPS = 3
IN_TN = NAT_COLS // IN_NAT_STEPS
MXU_COLS = 256
N_PROJ_OUT = 4


def _inproj_nat_phases(rows, u_ref, wn_ref, nat_ref):
    u = u_ref[0:rows, :]
    for c0 in range(0, IN_TN, MXU_COLS):
        cs = slice(c0, c0 + MXU_COLS)
        nat_ref[0:rows, cs] = lax.dot_general(u, wn_ref[cs, :], CONTRACT_LAST,
                                              preferred_element_type=F32).astype(BF16)
        yield


def _inproj_last_phases(rows, u_ref, w3_ref, kn_ref, t_ref, gt_ref):
    u = u_ref[0:rows, :]
    for c0 in range(0, T_ROWS, MXU_COLS):
        cs = slice(c0, c0 + MXU_COLS)
        t_ref[cs, 0:rows] = lax.dot_general(w3_ref[cs, :], u, CONTRACT_LAST,
                                            preferred_element_type=F32).astype(BF16)
        yield
    kn_ref[0:rows, :] = lax.dot_general(u, w3_ref[W3_K:W3_G, :], CONTRACT_LAST,
                                        preferred_element_type=F32).astype(BF16)
    yield
    gt_ref[0:rows, :] = lax.dot_general(u, w3_ref[W3_G:W3_G + GATE_LANES, :], CONTRACT_LAST,
                                        preferred_element_type=F32)
    yield


BIAS_PHASES_PER_ROUND = 3


def _inproj_kernel(x_ref, lead_ref, g_ref, wn_ref, w3_ref, rb_ref, bkt_band_ref, bkt_meta_ref,
                   nat_ref, kn_ref, t_ref, gt_ref, nat_lead_ref, kn_lead_ref, t_lead_ref, gt_lead_ref,
                   band_ref, meta_ref, u_ref, u_lead_ref, splat_ref):
    i = pl.program_id(0)
    j = pl.program_id(1)

    @pl.when(j == 0)
    def _():
        u_ref[...] = _rms(x_ref[...], g_ref[...]).astype(BF16)

    @pl.when((j == 0) & (i == 0))
    def _():
        u_lead_ref[...] = _rms(lead_ref[...], g_ref[...]).astype(BF16)
        _bias_fill_splats(rb_ref, splat_ref)

    def with_bias(phases):
        bias = _bias_step_phases(bkt_band_ref, bkt_meta_ref, band_ref, meta_ref, splat_ref)
        _drain(_interleave([phases, bias], [1, BIAS_PHASES_PER_ROUND]))

    @pl.when(j < IN_NAT_STEPS)
    def _():
        with_bias(_inproj_nat_phases(IN_TM, u_ref, wn_ref, nat_ref))

    @pl.when(j == IN_NAT_STEPS)
    def _():
        with_bias(_inproj_last_phases(IN_TM, u_ref, w3_ref, kn_ref, t_ref, gt_ref))

    @pl.when((i == 0) & (j < IN_NAT_STEPS))
    def _():
        _drain(_inproj_nat_phases(BLOCK, u_lead_ref, wn_ref, nat_lead_ref))

    @pl.when((i == 0) & (j == IN_NAT_STEPS))
    def _():
        _drain(_inproj_last_phases(BLOCK, u_lead_ref, w3_ref, kn_lead_ref, t_lead_ref, gt_lead_ref))


def _inproj_weight_specs():
    last_nat = IN_NAT_STEPS - 1
    return [
        pl.BlockSpec((IN_TN, D_MODEL), lambda i, j: (jnp.minimum(j, last_nat), 0)),
        pl.BlockSpec((W3_ROWS, D_MODEL), lambda i, j: (0, 0)),
    ]


def _inproj_out(tm, rows):
    last_nat = IN_NAT_STEPS - 1
    specs = [
        pl.BlockSpec((tm, IN_TN), lambda i, j: (i, jnp.minimum(j, last_nat))),
        pl.BlockSpec((tm, ATT_KV_WIDTH), lambda i, j: (i, 0)),
        pl.BlockSpec((T_ROWS, tm), lambda i, j: (0, i)),
        pl.BlockSpec((tm, GATE_LANES), lambda i, j: (i, 0)),
    ]
    shapes = [
        jax.ShapeDtypeStruct((rows, NAT_COLS), BF16),
        jax.ShapeDtypeStruct((rows, ATT_KV_WIDTH), BF16),
        jax.ShapeDtypeStruct((T_ROWS, rows), BF16),
        jax.ShapeDtypeStruct((rows, GATE_LANES), F32),
    ]
    return specs, shapes


def _inproj(x2, lead, g, w_nat, w3, rel_bias, n_tiles, nblk):
    out_specs, out_shape = _inproj_out(IN_TM, n_tiles * IN_TM)
    last_nat = IN_NAT_STEPS - 1
    n_steps = n_tiles * (IN_NAT_STEPS + 1)
    assert n_steps * BIAS_ROWS == 2 * BLOCK and nblk % n_steps == 0
    meta_per_step = nblk // n_steps
    bkt_band, bkt_meta = _bias_buckets(nblk)
    step = lambda i, j: i * (IN_NAT_STEPS + 1) + j
    bias_in_specs = [
        pl.BlockSpec(memory_space=pltpu.SMEM),
        pl.BlockSpec((BIAS_ROWS, BLOCK), lambda i, j: (step(i, j), 0)),
        pl.BlockSpec((meta_per_step, N_META, BLOCK), lambda i, j: (step(i, j), 0, 0)),
    ]
    bias_out_specs = [
        pl.BlockSpec((ATT_KV_HEADS, BIAS_ROWS, GQ), lambda i, j: (0, step(i, j), 0)),
        pl.BlockSpec((meta_per_step, ATT_KV_HEADS, N_META, GQ), lambda i, j: (step(i, j), 0, 0, 0)),
    ]
    bias_shapes = [
        jax.ShapeDtypeStruct((ATT_KV_HEADS, 2 * BLOCK, GQ), F32),
        jax.ShapeDtypeStruct((nblk, ATT_KV_HEADS, N_META, GQ), F32),
    ]
    lead_col = lambda i, j: jnp.where(i == 0, jnp.minimum(j, last_nat), last_nat)
    lead_specs = [
        pl.BlockSpec((BLOCK, IN_TN), lambda i, j: (0, lead_col(i, j))),
        pl.BlockSpec((BLOCK, ATT_KV_WIDTH), lambda i, j: (0, 0)),
        pl.BlockSpec((T_ROWS, BLOCK), lambda i, j: (0, 0)),
        pl.BlockSpec((BLOCK, GATE_LANES), lambda i, j: (0, 0)),
    ]
    lead_shapes = [
        jax.ShapeDtypeStruct((BLOCK, NAT_COLS), BF16),
        jax.ShapeDtypeStruct((BLOCK, ATT_KV_WIDTH), BF16),
        jax.ShapeDtypeStruct((T_ROWS, BLOCK), BF16),
        jax.ShapeDtypeStruct((BLOCK, GATE_LANES), F32),
    ]
    outs = pl.pallas_call(
        _inproj_kernel,
        grid=(n_tiles, IN_NAT_STEPS + 1),
        in_specs=[
            pl.BlockSpec((IN_TM, D_MODEL), lambda i, j: (i, 0)),
            pl.BlockSpec((BLOCK, D_MODEL), lambda i, j: (0, 0)),
            pl.BlockSpec((1, D_MODEL), lambda i, j: (0, 0)),
        ] + _inproj_weight_specs() + bias_in_specs,
        out_specs=out_specs + lead_specs + bias_out_specs,
        out_shape=out_shape + lead_shapes + bias_shapes,
        scratch_shapes=[pltpu.VMEM((IN_TM, D_MODEL), BF16), pltpu.VMEM((BLOCK, D_MODEL), BF16),
                        pltpu.VMEM((N_BUCKETS * ATT_HEADS, SUBLANES, BLOCK), F32)],
        compiler_params=_params(2),
        name="inproj",
    )(x2, lead, g, w_nat, w3, rel_bias.astype(F32), bkt_band, bkt_meta)
    return outs[:N_PROJ_OUT], outs[N_PROJ_OUT:2 * N_PROJ_OUT], outs[2 * N_PROJ_OUT:]


GQ = ATT_GROUP * BLOCK
ATT_SLABS_PER_PHASE = 2
FUSED_ATT_SLABS_PER_PHASE = 4


def _attn_prologue(n, kc_ref, kp_ref, km_ref):
    c = lax.broadcasted_iota(jnp.int32, (2 * BLOCK, BLOCK), 0)
    r = lax.broadcasted_iota(jnp.int32, (2 * BLOCK, BLOCK), 1)
    dist = r + BLOCK - c
    band_ok = (dist >= 0) & (dist < WINDOW) & ((c >= BLOCK) | (n >= 1))
    mm = lax.broadcasted_iota(jnp.int32, (N_META, BLOCK), 0)
    rr = lax.broadcasted_iota(jnp.int32, (N_META, BLOCK), 1)
    meta_ok = (n + 1) * BLOCK + rr >= N_PAD + mm
    kall = jnp.concatenate([kp_ref[...], kc_ref[...], km_ref[...]], axis=0)
    return band_ok, meta_ok, kall


BF16_ROWS = 16


def _ones_row(width):
    row = lax.broadcasted_iota(jnp.int32, (BF16_ROWS, width), 0)
    return jnp.where(row == 0, 1.0, 0.0).astype(BF16)


def _attn_head(h, pro, sink_ref, qt_ref, vc_ref, vp_ref, vl_ref, bb_ref, bm_ref, o_ref,
               slabs_per_phase=ATT_SLABS_PER_PHASE):
    band_ok, meta_ok, kall = pro
    lo = (h // 2) * LANE
    kpair = kall[:, lo:lo + LANE]
    qh = jnp.concatenate(
        [qt_ref[(ATT_GROUP * h + g) * ATT_HEAD_DIM:(ATT_GROUP * h + g + 1) * ATT_HEAD_DIM, :]
         for g in range(ATT_GROUP)], axis=1)
    zq = jnp.zeros_like(qh)
    qz = jnp.concatenate([qh, zq] if h % 2 == 0 else [zq, qh], axis=0)
    st = jnp.dot(kpair, qz, preferred_element_type=F32)
    yield
    zpad = jnp.zeros((N_PAD, BLOCK), BF16)
    p_band, p_lead, sink_p = [], [], []
    for g in range(ATT_GROUP):
        gs = slice(g * BLOCK, (g + 1) * BLOCK)
        sl = st[:, gs]
        sink = jnp.full((1, BLOCK), sink_ref[ATT_GROUP * h + g], F32) * LOG2E
        lb = jnp.where(band_ok, sl[:2 * BLOCK] + bb_ref[h, :, gs], NEG * LOG2E)
        lm = jnp.where(meta_ok, sl[2 * BLOCK:] + bm_ref[0, h, :, gs], NEG * LOG2E)
        m = jnp.maximum(jnp.max(lb, axis=0, keepdims=True), jnp.max(lm, axis=0, keepdims=True))
        m = jnp.maximum(m, sink)
        pb = jnp.exp2(lb - m)
        pm = jnp.exp2(lm - m)
        sink_p.append(jnp.exp2(sink - m))
        p_band.append(pb.astype(BF16))
        p_lead.append(jnp.concatenate([zpad, pm.astype(BF16)], axis=0))
        if g % slabs_per_phase == slabs_per_phase - 1:
            yield
    pbt = jnp.concatenate(p_band, axis=1)
    plt = jnp.concatenate(p_lead, axis=1)
    hs = slice(h * ATT_HEAD_DIM, (h + 1) * ATT_HEAD_DIM)
    vband = jnp.concatenate([vp_ref[hs, :], vc_ref[hs, :]], axis=1)
    ones_b = _ones_row(2 * BLOCK)
    ones_l = _ones_row(BLOCK)
    ot = (jnp.dot(jnp.concatenate([vband, ones_b], axis=0), pbt, preferred_element_type=F32)
          + jnp.dot(jnp.concatenate([vl_ref[hs, :], ones_l], axis=0), plt,
                    preferred_element_type=F32))
    for g in range(ATT_GROUP):
        gs = slice(g * BLOCK, (g + 1) * BLOCK)
        den = ot[ATT_HEAD_DIM:ATT_HEAD_DIM + 1, gs] + sink_p[g]
        row = (ATT_GROUP * h + g) * ATT_HEAD_DIM
        o_ref[row:row + ATT_HEAD_DIM, :] = (ot[:ATT_HEAD_DIM, gs] * (1.0 / den)).astype(BF16)


def _conv_shift_matrix():
    s = np.zeros((CONV_WIDTH * BLOCK, 2 * BLOCK), np.float32)
    for k in range(CONV_WIDTH):
        t = np.arange(BLOCK)
        s[k * BLOCK + t, BLOCK + t - k] = 1.0
    return jnp.asarray(s, BF16)


def _mlstm_gates(n, gate_ref, gb_ref):
    t_idx = lax.broadcasted_iota(jnp.int32, (BLOCK, GATE_LANES), 0)
    valid = (n > 0) | (t_idx >= N_PAD)
    pre = GATE_SOFTCAP * jnp.tanh((gate_ref[...] + gb_ref[...]) / GATE_SOFTCAP)
    log_i = jnp.where(valid, pre, NEG)
    log_sig = jnp.minimum(pre, 0.0) - jnp.log1p(jnp.exp(-jnp.abs(pre)))
    log_f = jnp.where(valid, log_sig, 0.0)
    row = lax.broadcasted_iota(jnp.int32, (BLOCK, BLOCK), 0)
    col = lax.broadcasted_iota(jnp.int32, (BLOCK, BLOCK), 1)
    causal = col <= row
    tril = jnp.where(causal, 1.0, 0.0).astype(F32)
    b_all = jnp.dot(tril, log_f, preferred_element_type=F32,
                    precision=lax.Precision.HIGHEST)
    return log_i, b_all, log_i.T, b_all.T, causal


def _mlstm_conv(p_ref, shift_ref, cw_ref, cb_ref, xprev_ref):
    xcur = p_ref[:, NAT_MQK:NAT_MV]
    xcat = jnp.concatenate([xprev_ref[...], xcur], axis=0)
    sh = jnp.dot(shift_ref[...], xcat, preferred_element_type=F32)
    xprev_ref[...] = xcur
    y = cb_ref[...]
    for k in range(CONV_WIDTH):
        y = y + cw_ref[CONV_WIDTH - 1 - k:CONV_WIDTH - k, :] * sh[k * BLOCK:(k + 1) * BLOCK]
    act = y * jax.nn.sigmoid(y)
    return act[:, :ML_QK_WIDTH] * (ML_QK_DIM ** -0.5), act[:, ML_QK_WIDTH:]


def _mlstm_head(h, gates, q_all, k_all, p_ref, mhn_ref, o_ref, c_ref, n_ref, m_ref):
    log_i, b_all, log_i_t, b_all_t, causal = gates
    q_f = q_all[:, h * ML_QK_DIM:(h + 1) * ML_QK_DIM]
    k_f = k_all[:, h * ML_QK_DIM:(h + 1) * ML_QK_DIM]
    q_h = q_f.astype(BF16)
    k_h = k_f.astype(BF16)
    v_h = p_ref[:, NAT_MV + h * ML_V_DIM:NAT_MV + (h + 1) * ML_V_DIM]
    b_col = b_all[:, ML_HEADS + h:ML_HEADS + h + 1]
    li_col = log_i[:, h:h + 1]
    b_row = b_all_t[ML_HEADS + h:ML_HEADS + h + 1, :]
    li_row = log_i_t[h:h + 1, :]
    b_last = b_all[BLOCK - 1:BLOCK, ML_HEADS + h:ML_HEADS + h + 1]
    c_prev = c_ref[h]
    n_prev = n_ref[h][0:1, :]
    m_prev = m_ref[h][0:1, 0:1]

    dmat = jnp.where(causal, b_col - b_row + li_row, -jnp.inf)
    inter = b_col + m_prev
    m_t = jnp.maximum(inter, jnp.max(dmat, axis=1, keepdims=True))
    s = lax.dot_general(q_h, k_h, CONTRACT_LAST, preferred_element_type=F32) * jnp.exp(dmat - m_t)
    yield
    a_t = jnp.exp(inter - m_t)
    num = (jnp.dot(s.astype(BF16), v_h, preferred_element_type=F32)
           + a_t * jnp.dot(q_h, c_prev.astype(BF16), preferred_element_type=F32))
    den = (jnp.sum(s, axis=1, keepdims=True)
           + a_t * jnp.sum(q_f * n_prev, axis=1, keepdims=True))
    den = jnp.maximum(jnp.abs(den), jnp.exp(-m_t))
    hh = num / den
    hh = hh * lax.rsqrt(jnp.mean(hh * hh, axis=-1, keepdims=True) + EPS)
    vs = slice(h * ML_V_DIM, (h + 1) * ML_V_DIM)
    hh = hh * mhn_ref[:, vs]
    m_o = p_ref[:, NAT_MO + h * ML_V_DIM:NAT_MO + (h + 1) * ML_V_DIM].astype(F32)
    o_ref[:, vs] = (hh * jax.nn.sigmoid(m_o)).astype(BF16)
    yield

    g_col = b_last - b_col + li_col
    m_loc = jnp.max(g_col, axis=0, keepdims=True)
    wk = jnp.exp(g_col - m_loc) * k_f
    c_loc = lax.dot_general(wk.astype(BF16), v_h, CONTRACT_FIRST,
                            preferred_element_type=F32)
    n_loc = jnp.sum(wk, axis=0, keepdims=True)
    m_new = jnp.maximum(b_last + m_prev, m_loc)
    a = jnp.exp(b_last + m_prev - m_new)
    cc = jnp.exp(m_loc - m_new)
    c_ref[h] = a * c_prev + cc * c_loc
    n_ref[h] = jnp.broadcast_to(a * n_prev + cc * n_loc, (8, ML_QK_DIM))
    m_ref[h] = jnp.broadcast_to(m_new, (8, LANE))


N_MIX_IN = 17
N_STATE = 4
MIXER_PHASES_PER_NAT_PHASE = 6
MIXER_PHASES_PER_LAST_PHASE = 2


N_ATT_IN = 10


def _attention_phases(n, att_in, att_ref, slabs_per_phase=ATT_SLABS_PER_PHASE):
    sink_ref, qt_ref, kc_ref, kp_ref, km_ref, vc_ref, vp_ref, vl_ref, bb_ref, bm_ref = att_in
    pro = _attn_prologue(n - 1, kc_ref, kp_ref, km_ref)
    for h in range(ATT_KV_HEADS):
        yield from _attn_head(h, pro, sink_ref, qt_ref, vc_ref, vp_ref, vl_ref, bb_ref, bm_ref, att_ref,
                              slabs_per_phase)


def _mlstm_phases(n, ml_in, hm_ref, state):
    p_ref, gate_ref, gb_ref, shift_ref, cw_ref, cb_ref, mhn_ref = ml_in
    c_ref, n_ref, m_ref, xprev_ref = state
    gates = _mlstm_gates(n, gate_ref, gb_ref)
    yield
    q_all, k_all = _mlstm_conv(p_ref, shift_ref, cw_ref, cb_ref, xprev_ref)
    yield
    for h in range(ML_HEADS):
        yield from _mlstm_head(h, gates, q_all, k_all, p_ref, mhn_ref, hm_ref, c_ref, n_ref, m_ref)


def _mixer_phases(n, mix_in, att_ref, hm_ref, state):
    return _interleave([_mlstm_phases(n, mix_in[N_ATT_IN:], hm_ref, state),
                        _attention_phases(n, mix_in[:N_ATT_IN], att_ref, FUSED_ATT_SLABS_PER_PHASE)],
                       [1, 1])


def _load_state(state, state_in):
    for dst, src in zip(state, state_in):
        dst[...] = src[...]


def _state_shapes():
    return [
        jax.ShapeDtypeStruct((ML_HEADS, ML_QK_DIM, ML_V_DIM), F32),
        jax.ShapeDtypeStruct((ML_HEADS, SUBLANES, ML_QK_DIM), F32),
        jax.ShapeDtypeStruct((ML_HEADS, SUBLANES, LANE), F32),
        jax.ShapeDtypeStruct((BLOCK, 2 * ML_QK_WIDTH), BF16),
    ]


def _state_scratch():
    return [pltpu.VMEM(s.shape, s.dtype) for s in _state_shapes()]


def _whole(shape):
    return pl.BlockSpec(shape, lambda *ids: (0,) * len(shape))


def _mixer_in_specs(step):
    cur = lambda *ids: step(*ids)
    prev = lambda *ids: jnp.maximum(step(*ids) - 1, 0)
    v_row = ATT_WIDTH // ATT_KV_WIDTH
    meta_rows = N_PAD // N_META
    return [
        pl.BlockSpec(memory_space=pltpu.SMEM),
        pl.BlockSpec((ATT_WIDTH, BLOCK), lambda *ids: (0, cur(*ids))),
        pl.BlockSpec((BLOCK, ATT_KV_WIDTH), lambda *ids: (cur(*ids), 0)),
        pl.BlockSpec((BLOCK, ATT_KV_WIDTH), lambda *ids: (prev(*ids), 0)),
        pl.BlockSpec((N_META, ATT_KV_WIDTH), lambda *ids: (meta_rows, 0)),
        pl.BlockSpec((ATT_KV_WIDTH, BLOCK), lambda *ids: (v_row, cur(*ids))),
        pl.BlockSpec((ATT_KV_WIDTH, BLOCK), lambda *ids: (v_row, prev(*ids))),
        pl.BlockSpec((ATT_KV_WIDTH, BLOCK), lambda *ids: (v_row, 0)),
        _whole((ATT_KV_HEADS, 2 * BLOCK, GQ)),
        pl.BlockSpec((1, ATT_KV_HEADS, N_META, GQ), lambda *ids: (cur(*ids), 0, 0, 0)),
        pl.BlockSpec((BLOCK, NAT_COLS), lambda *ids: (cur(*ids), 0)),
        pl.BlockSpec((BLOCK, GATE_LANES), lambda *ids: (cur(*ids), 0)),
        _whole((1, GATE_LANES)),
        _whole((CONV_WIDTH * BLOCK, 2 * BLOCK)),
        _whole((CONV_WIDTH, 2 * ML_QK_WIDTH)),
        _whole((1, 2 * ML_QK_WIDTH)),
        _whole((1, ML_WIDTH)),
    ] + [_whole(s.shape) for s in _state_shapes()]


def _mixer_args(consts, proj_b, proj_lead, state):
    sinks, bias_band_t, bias_meta_t, gate_bias, shift, conv_w, conv_b, mh_norm = consts
    nat_b, kn_b, projt_b, gt_b = proj_b
    _, kn_lead, projt_lead, _ = proj_lead
    return [sinks, projt_b, kn_b, kn_b, kn_lead, projt_b, projt_b, projt_lead, bias_band_t,
            bias_meta_t, nat_b, gt_b, gate_bias, shift, conv_w, conv_b, mh_norm] + list(state)


def _mixer_out(step, n_rows):
    specs = [
        pl.BlockSpec((ATT_WIDTH, BLOCK), lambda *ids: (0, step(*ids))),
        pl.BlockSpec((BLOCK, ML_WIDTH), lambda *ids: (step(*ids), 0)),
    ]
    shapes = [
        jax.ShapeDtypeStruct((ATT_WIDTH, n_rows), BF16),
        jax.ShapeDtypeStruct((n_rows, ML_WIDTH), BF16),
    ]
    return specs, shapes


def _lead_state_kernel(p_ref, gate_ref, gb_ref, shift_ref, cw_ref, cb_ref, mhn_ref,
                       c_ref, n_ref, m_ref, xprev_ref, hm_ref):
    for ref in (c_ref, n_ref, m_ref, xprev_ref):
        ref[...] = jnp.zeros_like(ref)
    gates = _mlstm_gates(0, gate_ref, gb_ref)
    q_all, k_all = _mlstm_conv(p_ref, shift_ref, cw_ref, cb_ref, xprev_ref)
    for h in range(ML_HEADS):
        _drain(_mlstm_head(h, gates, q_all, k_all, p_ref, mhn_ref, hm_ref, c_ref, n_ref, m_ref))


def _lead_state(proj_lead, consts):
    nat_lead, _, _, gates_lead = proj_lead
    _, _, _, gate_bias, shift, conv_w, conv_b, mh_norm = consts
    shapes = _state_shapes()
    return pl.pallas_call(
        _lead_state_kernel,
        grid=(1,),
        in_specs=[
            _whole((BLOCK, NAT_COLS)),
            _whole((BLOCK, GATE_LANES)),
            _whole((1, GATE_LANES)),
            _whole((CONV_WIDTH * BLOCK, 2 * BLOCK)),
            _whole((CONV_WIDTH, 2 * ML_QK_WIDTH)),
            _whole((1, 2 * ML_QK_WIDTH)),
            _whole((1, ML_WIDTH)),
        ],
        out_specs=[_whole(s.shape) for s in shapes],
        out_shape=shapes,
        scratch_shapes=[pltpu.VMEM((BLOCK, ML_WIDTH), BF16)],
        compiler_params=_params(1),
        name="lead_state",
    )(nat_lead, gates_lead, gate_bias, shift, conv_w, conv_b, mh_norm)


FUSED_TM = 512


def _inproj_mixer_kernel(*refs):
    x_ref, g_ref, wn_ref, w3_ref = refs[:4]
    mix_in = refs[4:4 + N_MIX_IN]
    state_in = refs[4 + N_MIX_IN:4 + N_MIX_IN + N_STATE]
    wu_ref = refs[4 + N_MIX_IN + N_STATE]
    (nat_ref, kn_ref, t_ref, gt_ref, att_ref, hm_ref,
     wu_bf_ref) = refs[5 + N_MIX_IN + N_STATE:-(1 + N_STATE)]
    u_ref = refs[-(1 + N_STATE)]
    state = refs[-N_STATE:]
    i = pl.program_id(0)
    j = pl.program_id(1)
    step = i * (IN_NAT_STEPS + 1) + j
    wu_bf_ref[...] = wu_ref[...].astype(BF16)

    @pl.when(j == 0)
    def _():
        u_ref[...] = _rms(x_ref[...], g_ref[...]).astype(BF16)

    @pl.when(step == 0)
    def _():
        _load_state(state, state_in)

    @pl.when(j < IN_NAT_STEPS)
    def _():
        _drain(_interleave([_inproj_nat_phases(FUSED_TM, u_ref, wn_ref, nat_ref),
                            _mixer_phases(step + 1, mix_in, att_ref, hm_ref, state)],
                           [1, MIXER_PHASES_PER_NAT_PHASE]))

    @pl.when(j == IN_NAT_STEPS)
    def _():
        _drain(_interleave([_inproj_last_phases(FUSED_TM, u_ref, w3_ref, kn_ref, t_ref, gt_ref),
                            _mixer_phases(step + 1, mix_in, att_ref, hm_ref, state)],
                           [1, MIXER_PHASES_PER_LAST_PHASE]))


def _inproj_mixer(x2, row_tile0, g, w_nat, w3, mixer_args, w_up, nblk):
    n_tiles = nblk * BLOCK // FUSED_TM
    assert n_tiles * (IN_NAT_STEPS + 1) == nblk
    step = lambda i, j: i * (IN_NAT_STEPS + 1) + j
    proj_specs, proj_shapes = _inproj_out(FUSED_TM, n_tiles * FUSED_TM)
    mix_specs, mix_shapes = _mixer_out(step, nblk * BLOCK)
    wu_spec = pl.BlockSpec((D_MODEL, D_FF // nblk), lambda i, j: (0, step(i, j)))
    return pl.pallas_call(
        _inproj_mixer_kernel,
        grid=(n_tiles, IN_NAT_STEPS + 1),
        in_specs=[
            pl.BlockSpec((FUSED_TM, D_MODEL), lambda i, j: (row_tile0 + i, 0)),
            pl.BlockSpec((1, D_MODEL), lambda i, j: (0, 0)),
        ] + _inproj_weight_specs() + _mixer_in_specs(step) + [wu_spec],
        out_specs=proj_specs + mix_specs + [wu_spec],
        out_shape=proj_shapes + mix_shapes + [jax.ShapeDtypeStruct(w_up.shape, BF16)],
        scratch_shapes=[pltpu.VMEM((FUSED_TM, D_MODEL), BF16)] + _state_scratch(),
        compiler_params=_params(2),
        name="inproj_mixer",
    )(x2, g, w_nat, w3, *mixer_args, w_up)


MIXER_PAIR = 2
PAIR_ROWS = MIXER_PAIR * BLOCK
ATT_PHASES_PER_MLSTM_PHASE = 4
MLSTM_CHUNK_LAG = 2


def _mixer_kernel(*refs):
    (sink_ref, qt_ref, kc_ref, kp_ref, km_ref, vc_ref, vp_ref, vl_ref, bb_ref, bm_ref,
     p_ref, gate_ref, gb_ref, shift_ref, cw_ref, cb_ref, mhn_ref) = refs[:N_MIX_IN]
    state_in = refs[N_MIX_IN:N_MIX_IN + N_STATE]
    wo_ref, wd_ref = refs[N_MIX_IN + N_STATE:N_MIX_IN + N_STATE + 2]
    att_ref, hm_ref, wo_bf_ref, wd_bf_ref = refs[N_MIX_IN + N_STATE + 2:-N_STATE]
    state = refs[-N_STATE:]
    step = pl.program_id(0)

    @pl.when(step == 0)
    def _():
        _load_state(state, state_in)

    wo_bf_ref[...] = wo_ref[...].astype(BF16)
    wd_bf_ref[...] = wd_ref[...].astype(BF16)

    rows = lambda ref, k: ref.at[k * BLOCK:(k + 1) * BLOCK]
    cols = lambda ref, k: ref.at[:, k * BLOCK:(k + 1) * BLOCK]
    attention, chunks = [], []
    for k in range(MIXER_PAIR):
        n = MIXER_PAIR * step + 1 + k
        kp_k = kp_ref if k == 0 else rows(kc_ref, k - 1)
        vp_k = vp_ref if k == 0 else cols(vc_ref, k - 1)
        att_in = (sink_ref, cols(qt_ref, k), rows(kc_ref, k), kp_k, km_ref, cols(vc_ref, k), vp_k,
                  vl_ref, bb_ref, bm_ref.at[k:k + 1])
        attention.append(_attention_phases(n, att_in, cols(att_ref, k)))
        chunks.append((n, (rows(p_ref, k), rows(gate_ref, k), gb_ref, shift_ref, cw_ref, cb_ref,
                           mhn_ref), rows(hm_ref, k)))

    def mlstm_chunk(k):
        for _ in range(k * MLSTM_CHUNK_LAG):
            yield
        n, ml_in, hm_k = chunks[k]
        yield from _mlstm_phases(n, ml_in, hm_k, state)

    _drain(_interleave([mlstm_chunk(k) for k in range(MIXER_PAIR)] + attention,
                       [1] * MIXER_PAIR + [ATT_PHASES_PER_MLSTM_PHASE] * MIXER_PAIR))


def _mixer(mixer_args, w_out, w_down, nblk):
    n_steps = nblk // MIXER_PAIR
    before = lambda s: jnp.maximum(MIXER_PAIR * s - 1, 0)
    v_row = ATT_WIDTH // ATT_KV_WIDTH
    in_specs = [
        pl.BlockSpec(memory_space=pltpu.SMEM),
        pl.BlockSpec((ATT_WIDTH, PAIR_ROWS), lambda s: (0, s)),
        pl.BlockSpec((PAIR_ROWS, ATT_KV_WIDTH), lambda s: (s, 0)),
        pl.BlockSpec((BLOCK, ATT_KV_WIDTH), lambda s: (before(s), 0)),
        pl.BlockSpec((N_META, ATT_KV_WIDTH), lambda s: (N_PAD // N_META, 0)),
        pl.BlockSpec((ATT_KV_WIDTH, PAIR_ROWS), lambda s: (v_row, s)),
        pl.BlockSpec((ATT_KV_WIDTH, BLOCK), lambda s: (v_row, before(s))),
        pl.BlockSpec((ATT_KV_WIDTH, BLOCK), lambda s: (v_row, 0)),
        _whole((ATT_KV_HEADS, 2 * BLOCK, GQ)),
        pl.BlockSpec((MIXER_PAIR, ATT_KV_HEADS, N_META, GQ), lambda s: (s, 0, 0, 0)),
        pl.BlockSpec((PAIR_ROWS, NAT_COLS), lambda s: (s, 0)),
        pl.BlockSpec((PAIR_ROWS, GATE_LANES), lambda s: (s, 0)),
        _whole((1, GATE_LANES)),
        _whole((CONV_WIDTH * BLOCK, 2 * BLOCK)),
        _whole((CONV_WIDTH, 2 * ML_QK_WIDTH)),
        _whole((1, 2 * ML_QK_WIDTH)),
        _whole((1, ML_WIDTH)),
    ] + [_whole(s.shape) for s in _state_shapes()]
    assert len(in_specs) == N_MIX_IN + N_STATE
    w_specs = [
        pl.BlockSpec((w_out.shape[0] // n_steps, D_MODEL), lambda s: (s, 0)),
        pl.BlockSpec((D_FF // n_steps, D_MODEL), lambda s: (s, 0)),
    ]
    return pl.pallas_call(
        _mixer_kernel,
        grid=(n_steps,),
        in_specs=in_specs + w_specs,
        out_specs=[
            pl.BlockSpec((ATT_WIDTH, PAIR_ROWS), lambda s: (0, s)),
            pl.BlockSpec((PAIR_ROWS, ML_WIDTH), lambda s: (s, 0)),
        ] + w_specs,
        out_shape=[
            jax.ShapeDtypeStruct((ATT_WIDTH, nblk * BLOCK), BF16),
            jax.ShapeDtypeStruct((nblk * BLOCK, ML_WIDTH), BF16),
        ] + [jax.ShapeDtypeStruct(w.shape, BF16) for w in (w_out, w_down)],
        scratch_shapes=_state_scratch(),
        compiler_params=_params(1),
        name="mixer",
    )(*mixer_args, w_out, w_down)


OUT_TM = 512


def _outproj_kernel(att0_ref, att1_ref, hm0_ref, hm1_ref, x_ref, wa_ref, wm_ref, g_ref,
                    h_ref, u_ref):
    i = pl.program_id(0)
    half = pl.num_programs(0) // 2

    def body(att_ref, hm_ref):
        h = (x_ref[...]
             + lax.dot_general(att_ref[...], wa_ref[...], CONTRACT_FIRST, preferred_element_type=F32)
             + jnp.dot(hm_ref[...], wm_ref[...], preferred_element_type=F32))
        h_ref[...] = h
        u_ref[...] = _rms(h, g_ref[...]).astype(BF16)

    @pl.when(i < half)
    def _():
        body(att0_ref, hm0_ref)

    @pl.when(i >= half)
    def _():
        body(att1_ref, hm1_ref)


def _outproj(att0, att1, hm0, hm1, x2, w_out, g):
    rows = x2.shape[0]
    n = rows // OUT_TM
    half = n // 2
    first = lambda i: jnp.minimum(i, half - 1)
    second = lambda i: jnp.maximum(i - half, 0)
    return pl.pallas_call(
        _outproj_kernel,
        grid=(n,),
        in_specs=[
            pl.BlockSpec((ATT_WIDTH, OUT_TM), lambda i: (0, first(i))),
            pl.BlockSpec((ATT_WIDTH, OUT_TM), lambda i: (0, second(i))),
            pl.BlockSpec((OUT_TM, ML_WIDTH), lambda i: (first(i), 0)),
            pl.BlockSpec((OUT_TM, ML_WIDTH), lambda i: (second(i), 0)),
            pl.BlockSpec((OUT_TM, D_MODEL), lambda i: (i, 0)),
            pl.BlockSpec((ATT_WIDTH, D_MODEL), lambda i: (0, 0)),
            pl.BlockSpec((ML_WIDTH, D_MODEL), lambda i: (ATT_WIDTH // ML_WIDTH, 0)),
            pl.BlockSpec((1, D_MODEL), lambda i: (0, 0)),
        ],
        out_specs=[
            pl.BlockSpec((OUT_TM, D_MODEL), lambda i: (i, 0)),
            pl.BlockSpec((OUT_TM, D_MODEL), lambda i: (i, 0)),
        ],
        out_shape=[
            jax.ShapeDtypeStruct((rows, D_MODEL), F32),
            jax.ShapeDtypeStruct((rows, D_MODEL), BF16),
        ],
        compiler_params=_params(1),
        name="outproj",
    )(att0, att1, hm0, hm1, x2, w_out, w_out, g)


MLP_TM = 1024
MLP_TF = 1024


def _mlp_step(mode, u_ref, wu_ref, wd_ref, o_ref, hbuf_ref=None, ssq_ref=None):
    acts = []
    for c in range(MLP_TF // MXU_COLS):
        cs = slice(c * MXU_COLS, (c + 1) * MXU_COLS)
        a = jnp.dot(u_ref[...], wu_ref[:, cs], preferred_element_type=F32)
        acts.append(jnp.square(jnp.maximum(a, 0.0)).astype(BF16))
        yield
    act = jnp.concatenate(acts, axis=1)
    for c in range(D_MODEL // MXU_COLS):
        cs = slice(c * MXU_COLS, (c + 1) * MXU_COLS)
        d = jnp.dot(act, wd_ref[:, cs], preferred_element_type=F32)
        if mode == "first":
            o_ref[:, cs] = d
        elif mode == "mid":
            o_ref[:, cs] += d
        else:
            v = o_ref[:, cs] + d + hbuf_ref[:, cs]
            o_ref[:, cs] = v
            sq = v * v
            part = sum(sq[:, k * LANE:(k + 1) * LANE] for k in range(MXU_COLS // LANE))
            if c == 0:
                ssq_ref[...] = part
            else:
                ssq_ref[...] += part
        yield


def _mlp_kernel(u_ref, h_hbm, wu_ref, wd_ref, g_ref, o_ref, hbuf_ref, ssq_ref, hsem):
    i = pl.program_id(0)
    j = pl.program_id(1)
    last = pl.num_programs(1) - 1
    r0 = pl.multiple_of(i * MLP_TM, MLP_TM)
    h_copy = pltpu.make_async_copy(h_hbm.at[pl.ds(r0, MLP_TM), :], hbuf_ref, hsem)

    @pl.when(j == 0)
    def _():
        h_copy.start()
        _drain(_mlp_step("first", u_ref, wu_ref, wd_ref, o_ref))

    @pl.when((j > 0) & (j < last))
    def _():
        _drain(_mlp_step("mid", u_ref, wu_ref, wd_ref, o_ref))

    @pl.when(j == last)
    def _():
        h_copy.wait()
        _drain(_mlp_step("last", u_ref, wu_ref, wd_ref, o_ref, hbuf_ref, ssq_ref))
        ms = jnp.sum(ssq_ref[...], axis=-1, keepdims=True) / D_MODEL
        o_ref[...] = o_ref[...] * lax.rsqrt(ms + EPS) * g_ref[...]


def _mlp(u, h, w_up, w_down, g):
    rows = u.shape[0]
    return pl.pallas_call(
        _mlp_kernel,
        grid=(rows // MLP_TM, D_FF // MLP_TF),
        in_specs=[
            pl.BlockSpec((MLP_TM, D_MODEL), lambda i, j: (i, 0)),
            pl.BlockSpec(memory_space=pl.ANY),
            pl.BlockSpec((D_MODEL, MLP_TF), lambda i, j: (0, j)),
            pl.BlockSpec((MLP_TF, D_MODEL), lambda i, j: (j, 0)),
            pl.BlockSpec((1, D_MODEL), lambda i, j: (0, 0)),
        ],
        out_specs=pl.BlockSpec((MLP_TM, D_MODEL), lambda i, j: (i, 0)),
        out_shape=jax.ShapeDtypeStruct((rows, D_MODEL), F32),
        scratch_shapes=[pltpu.VMEM((MLP_TM, D_MODEL), F32), pltpu.VMEM((MLP_TM, LANE), F32),
                        pltpu.SemaphoreType.DMA(())],
        compiler_params=_params(2),
        name="mlp",
    )(u, h, w_up, w_down, g)


def _t5_bucket(dist):
    max_exact = N_BUCKETS // 2
    d = jnp.maximum(dist, 0)
    ratio = jnp.maximum(d, max_exact).astype(F32) / max_exact
    large = max_exact + (jnp.log(ratio) / math.log(MAX_DISTANCE / max_exact)
                         * (N_BUCKETS - max_exact)).astype(jnp.int32)
    large = jnp.minimum(large, N_BUCKETS - 1)
    return jnp.where(d < max_exact, d, large)


BIAS_ROWS = 16


BIAS_BUCKETS_PER_PHASE = 8


def _bias_fill_splats(rb_ref, splat_ref):
    def fill(b, carry):
        for hd in range(ATT_HEADS):
            splat_ref[b * ATT_HEADS + hd] = jnp.full((SUBLANES, BLOCK), rb_ref[b, hd], F32) * LOG2E
        return carry

    lax.fori_loop(0, N_BUCKETS, fill, 0)


def _bias_lookup_phases(bkt, splat_ref, write):
    accs = [jnp.zeros(bkt.shape, F32)] * ATT_HEADS
    for b in range(N_BUCKETS):
        hit = (bkt >= b) & (bkt < b + 1)
        accs = [jnp.where(hit, jnp.concatenate([splat_ref[b * ATT_HEADS + hd]]
                                               * (BIAS_ROWS // SUBLANES), axis=0), a)
                for hd, a in enumerate(accs)]
        if b % BIAS_BUCKETS_PER_PHASE == BIAS_BUCKETS_PER_PHASE - 1:
            yield
    write(accs)


def _bias_step_phases(bkt_band_ref, bkt_meta_ref, band_ref, meta_ref, splat_ref):
    def write_band(tables):
        for hd, t in enumerate(tables):
            h, g = divmod(hd, ATT_GROUP)
            band_ref[h, :, g * BLOCK:(g + 1) * BLOCK] = t

    yield from _bias_lookup_phases(bkt_band_ref[...], splat_ref, write_band)
    for k in range(bkt_meta_ref.shape[0]):
        def write_meta(tables, k=k):
            for hd, t in enumerate(tables):
                h, g = divmod(hd, ATT_GROUP)
                meta_ref[k, h, :, g * BLOCK:(g + 1) * BLOCK] = t

        yield from _bias_lookup_phases(bkt_meta_ref[k], splat_ref, write_meta)


def _bias_buckets(nblk):
    c = jnp.arange(2 * BLOCK)[:, None]
    r = jnp.arange(BLOCK)[None, :]
    bkt_band = _t5_bucket(r + BLOCK - c).astype(jnp.int32)
    q_pos = BLOCK + jnp.arange(nblk * BLOCK)[None, :]
    m_pos = N_PAD + jnp.arange(N_META)[:, None]
    bkt_meta = _t5_bucket(q_pos - m_pos).astype(jnp.int32)
    return bkt_band, bkt_meta.reshape(N_META, nblk, BLOCK).transpose(1, 0, 2)


W_IN_OFFS = dict(q=0, k=1024, v=1280, mq=1536, mk=2048, mv=2560, mo=3584, gates=4608)
PREP_ROWS = 256
NAT_SRC = [W_IN_OFFS["mq"] // PREP_ROWS + t for t in range(NAT_COLS // PREP_ROWS)]
W3_SRC = ([W_IN_OFFS["q"] // PREP_ROWS + t for t in range(ATT_WIDTH // PREP_ROWS)]
          + [W_IN_OFFS["v"] // PREP_ROWS + t for t in range(ATT_KV_WIDTH // PREP_ROWS)]
          + [W_IN_OFFS["k"] // PREP_ROWS + t for t in range(ATT_KV_WIDTH // PREP_ROWS)]
          + [W_IN_OFFS["gates"] // PREP_ROWS])
N_NAT_BLK = len(NAT_SRC)
N_Q_BLK = ATT_WIDTH // PREP_ROWS
N_PREP = N_NAT_BLK + len(W3_SRC)
assert W_IN_OFFS["mq"] + NAT_COLS == W_IN_OFFS["gates"] and len(W3_SRC) * PREP_ROWS == W3_ROWS


def _win_prep_kernel(src_ref, w_ref, nat_ref, w3_ref):
    j = pl.program_id(0)
    wv = w_ref[0]

    @pl.when(j < N_NAT_BLK)
    def _():
        nat_ref[...] = wv.astype(BF16)

    @pl.when((j >= N_NAT_BLK) & (j < N_NAT_BLK + N_Q_BLK))
    def _():
        w3_ref[...] = (wv * (ATT_HEAD_DIM ** -0.5 * LOG2E)).astype(BF16)

    @pl.when((j >= N_NAT_BLK + N_Q_BLK) & (j < N_PREP - 1))
    def _():
        w3_ref[...] = wv.astype(BF16)

    @pl.when(j == N_PREP - 1)
    def _():
        row = lax.broadcasted_iota(jnp.int32, wv.shape, 0)
        w3_ref[...] = jnp.where(row < 2 * ML_HEADS, wv, 0.0).astype(BF16)


def _win_prep(w_in_t):
    src = jnp.asarray(NAT_SRC + W3_SRC, jnp.int32)
    d = w_in_t.shape[2]
    return pl.pallas_call(
        _win_prep_kernel,
        grid_spec=pltpu.PrefetchScalarGridSpec(
            num_scalar_prefetch=1,
            grid=(N_PREP,),
            in_specs=[pl.BlockSpec((1, PREP_ROWS, d), lambda j, src: (0, src[j], 0))],
            out_specs=[
                pl.BlockSpec((PREP_ROWS, d), lambda j, src: (jnp.minimum(j, N_NAT_BLK - 1), 0)),
                pl.BlockSpec((PREP_ROWS, d), lambda j, src: (jnp.maximum(j - N_NAT_BLK, 0), 0)),
            ],
        ),
        out_shape=[
            jax.ShapeDtypeStruct((NAT_COLS, d), BF16),
            jax.ShapeDtypeStruct((W3_ROWS, d), BF16),
        ],
        compiler_params=_params(1),
        name="w_in_prep",
    )(src, w_in_t)


def kernel(x, meta_tokens, w_in, conv_w, conv_b, b_igate, b_fgate, attn_sinks, rel_bias,
           mh_norm, w_out, norm_mix, norm_mlp, w_up, w_down, norm_final):
    batch, seq, d = x.shape
    assert batch == 2, "the two batches are processed one projection / mixer stage apart"
    nblk = seq // BLOCK
    x2 = x.reshape(batch * seq, d)
    lead = jnp.concatenate([jnp.zeros((N_PAD, d), x.dtype), meta_tokens.astype(x.dtype)], axis=0)
    g_mix = norm_mix[0][None].astype(F32)

    w_nat, w3 = _win_prep(jnp.swapaxes(w_in, 1, 2))
    gate_bias = jnp.pad(jnp.concatenate([b_igate[0], b_fgate[0]]).astype(F32),
                        (0, GATE_LANES - 2 * ML_HEADS))[None]

    proj0, proj_lead, (bias_band_t, bias_meta_t) = _inproj(
        x2, lead, g_mix, w_nat, w3, rel_bias, seq // IN_TM, nblk)
    consts = (attn_sinks[0].astype(F32), bias_band_t, bias_meta_t, gate_bias, _conv_shift_matrix(),
              conv_w[0].astype(F32), conv_b[0][None].astype(F32), mh_norm[0][None].astype(F32))
    state = _lead_state(proj_lead, consts)
    outs = _inproj_mixer(x2, seq // FUSED_TM, g_mix, w_nat, w3,
                         _mixer_args(consts, proj0, proj_lead, state), w_up[0], nblk)
    proj1, (att0, hm0, w_up_bf) = outs[:N_PROJ_OUT], outs[N_PROJ_OUT:]
    att1, hm1, w_out_bf, w_down_bf = _mixer(_mixer_args(consts, proj1, proj_lead, state),
                                            w_out[0], w_down[0], nblk)

    h2, u2 = _outproj(att0, att1, hm0, hm1, x2, w_out_bf, norm_mlp[0][None].astype(F32))
    out = _mlp(u2, h2, w_up_bf, w_down_bf, norm_final[None].astype(F32))
    return out.reshape(batch, seq, d)
```

```python
import math

import numpy as np
import jax
import jax.numpy as jnp
from jax import lax
from jax.experimental import pallas as pl
from jax.experimental.pallas import tpu as pltpu

F32 = jnp.float32
BF16 = jnp.bfloat16

D_MODEL = 2048
N_META = 16
BLOCK = 128
N_PAD = BLOCK - N_META
WINDOW = 128
ATT_HEADS = 16
ATT_KV_HEADS = 4
ATT_GROUP = ATT_HEADS // ATT_KV_HEADS
ATT_HEAD_DIM = 64
ATT_WIDTH = ATT_HEADS * ATT_HEAD_DIM
ATT_KV_WIDTH = ATT_KV_HEADS * ATT_HEAD_DIM
ML_HEADS = 4
ML_V_DIM = 256
ML_QK_DIM = 128
ML_WIDTH = ML_HEADS * ML_V_DIM
ML_QK_WIDTH = ML_HEADS * ML_QK_DIM
CONV_WIDTH = 4
GATE_SOFTCAP = 15.0
D_FF = 4 * D_MODEL
N_BUCKETS = 32
MAX_DISTANCE = 128
EPS = 1e-6
NEG = -1e30
LOG2E = math.log2(math.e)

NAT_MQK = 0
NAT_MV = 2 * ML_QK_WIDTH
NAT_MO = NAT_MV + ML_WIDTH
NAT_COLS = NAT_MO + ML_WIDTH
T_ROWS = ATT_WIDTH + ATT_KV_WIDTH
GATE_LANES = 128
W3_K = T_ROWS
W3_G = W3_K + ATT_KV_WIDTH
W3_ROWS = W3_G + 256

V7X_VMEM_LIMIT = 60 * 1024 * 1024
LANE = 128
SUBLANES = 8
CONTRACT_LAST = (((1,), (1,)), ((), ()))
CONTRACT_FIRST = (((0,), (0,)), ((), ()))


def _params(n_axes, vmem=V7X_VMEM_LIMIT):
    return pltpu.CompilerParams(dimension_semantics=("arbitrary",) * n_axes,
                                vmem_limit_bytes=vmem)


def _rms(x, g):
    return x * lax.rsqrt(jnp.mean(x * x, axis=-1, keepdims=True) + EPS) * g


_DONE = object()


def _drain(phases):
    for _ in phases:
        pass


def _interleave(streams, weights):
    live = list(zip(streams, weights))
    while live:
        for item in list(live):
            stream, weight = item
            for _ in range(weight):
                if next(stream, _DONE) is _DONE:
                    live.remove(item)
                    break
        yield


IN_TM = 1024
IN_NAT_STEPS = 3
IN_TN = NAT_COLS // IN_NAT_STEPS
MXU_COLS = 256
N_PROJ_OUT = 4


def _inproj_nat_phases(rows, u_ref, wn_ref, nat_ref):
    u = u_ref[0:rows, :]
    for c0 in range(0, IN_TN, MXU_COLS):
        cs = slice(c0, c0 + MXU_COLS)
        nat_ref[0:rows, cs] = lax.dot_general(u, wn_ref[cs, :], CONTRACT_LAST,
                                              preferred_element_type=F32).astype(BF16)
        yield


def _inproj_last_phases(rows, u_ref, w3_ref, kn_ref, t_ref, gt_ref):
    u = u_ref[0:rows, :]
    for c0 in range(0, T_ROWS, MXU_COLS):
        cs = slice(c0, c0 + MXU_COLS)
        t_ref[cs, 0:rows] = lax.dot_general(w3_ref[cs, :], u, CONTRACT_LAST,
                                            preferred_element_type=F32).astype(BF16)
        yield
    kn_ref[0:rows, :] = lax.dot_general(u, w3_ref[W3_K:W3_G, :], CONTRACT_LAST,
                                        preferred_element_type=F32).astype(BF16)
    yield
    gt_ref[0:rows, :] = lax.dot_general(u, w3_ref[W3_G:W3_G + GATE_LANES, :], CONTRACT_LAST,
                                        preferred_element_type=F32)
    yield


BIAS_PHASES_PER_ROUND = 3


def _inproj_kernel(x_ref, lead_ref, g_ref, wn_ref, w3_ref, rb_ref, bkt_band_ref, bkt_meta_ref,
                   nat_ref, kn_ref, t_ref, gt_ref, nat_lead_ref, kn_lead_ref, t_lead_ref, gt_lead_ref,
                   band_ref, meta_ref, u_ref, u_lead_ref, splat_ref):
    i = pl.program_id(0)
    j = pl.program_id(1)

    @pl.when(j == 0)
    def _():
        u_ref[...] = _rms(x_ref[...], g_ref[...]).astype(BF16)

    @pl.when((j == 0) & (i == 0))
    def _():
        lead = jnp.concatenate([jnp.zeros((N_PAD, D_MODEL), F32), lead_ref[...]], axis=0)
        u_lead_ref[...] = _rms(lead, g_ref[...]).astype(BF16)
        _bias_fill_splats(rb_ref, splat_ref)

    def with_bias(phases):
        bias = _bias_step_phases(bkt_band_ref, bkt_meta_ref, band_ref, meta_ref, splat_ref)
        _drain(_interleave([phases, bias], [1, BIAS_PHASES_PER_ROUND]))

    @pl.when(j < IN_NAT_STEPS)
    def _():
        with_bias(_inproj_nat_phases(IN_TM, u_ref, wn_ref, nat_ref))

    @pl.when(j == IN_NAT_STEPS)
    def _():
        with_bias(_inproj_last_phases(IN_TM, u_ref, w3_ref, kn_ref, t_ref, gt_ref))

    @pl.when((i == 0) & (j < IN_NAT_STEPS))
    def _():
        _drain(_inproj_nat_phases(BLOCK, u_lead_ref, wn_ref, nat_lead_ref))

    @pl.when((i == 0) & (j == IN_NAT_STEPS))
    def _():
        _drain(_inproj_last_phases(BLOCK, u_lead_ref, w3_ref, kn_lead_ref, t_lead_ref, gt_lead_ref))


def _inproj_weight_specs():
    last_nat = IN_NAT_STEPS - 1
    return [
        pl.BlockSpec((IN_TN, D_MODEL), lambda i, j: (jnp.minimum(j, last_nat), 0)),
        pl.BlockSpec((W3_ROWS, D_MODEL), lambda i, j: (0, 0)),
    ]


def _inproj_out(tm, rows):
    last_nat = IN_NAT_STEPS - 1
    specs = [
        pl.BlockSpec((tm, IN_TN), lambda i, j: (i, jnp.minimum(j, last_nat))),
        pl.BlockSpec((tm, ATT_KV_WIDTH), lambda i, j: (i, 0)),
        pl.BlockSpec((T_ROWS, tm), lambda i, j: (0, i)),
        pl.BlockSpec((tm, GATE_LANES), lambda i, j: (i, 0)),
    ]
    shapes = [
        jax.ShapeDtypeStruct((rows, NAT_COLS), BF16),
        jax.ShapeDtypeStruct((rows, ATT_KV_WIDTH), BF16),
        jax.ShapeDtypeStruct((T_ROWS, rows), BF16),
        jax.ShapeDtypeStruct((rows, GATE_LANES), F32),
    ]
    return specs, shapes


def _inproj(x2, lead, g, w_nat, w3, rel_bias, n_tiles, nblk):
    out_specs, out_shape = _inproj_out(IN_TM, n_tiles * IN_TM)
    last_nat = IN_NAT_STEPS - 1
    n_steps = n_tiles * (IN_NAT_STEPS + 1)
    assert n_steps * BIAS_ROWS == 2 * BLOCK and nblk % n_steps == 0
    meta_per_step = nblk // n_steps
    bkt_band, bkt_meta = _bias_buckets(nblk)
    step = lambda i, j: i * (IN_NAT_STEPS + 1) + j
    bias_in_specs = [
        pl.BlockSpec(memory_space=pltpu.SMEM),
        pl.BlockSpec((BIAS_ROWS, BLOCK), lambda i, j: (step(i, j), 0)),
        pl.BlockSpec((meta_per_step, N_META, BLOCK), lambda i, j: (step(i, j), 0, 0)),
    ]
    bias_out_specs = [
        pl.BlockSpec((ATT_KV_HEADS, BIAS_ROWS, GQ), lambda i, j: (0, step(i, j), 0)),
        pl.BlockSpec((meta_per_step, ATT_KV_HEADS, N_META, GQ), lambda i, j: (step(i, j), 0, 0, 0)),
    ]
    bias_shapes = [
        jax.ShapeDtypeStruct((ATT_KV_HEADS, 2 * BLOCK, GQ), F32),
        jax.ShapeDtypeStruct((nblk, ATT_KV_HEADS, N_META, GQ), F32),
    ]
    lead_col = lambda i, j: jnp.where(i == 0, jnp.minimum(j, last_nat), last_nat)
    lead_specs = [
        pl.BlockSpec((BLOCK, IN_TN), lambda i, j: (0, lead_col(i, j))),
        pl.BlockSpec((BLOCK, ATT_KV_WIDTH), lambda i, j: (0, 0)),
        pl.BlockSpec((T_ROWS, BLOCK), lambda i, j: (0, 0)),
        pl.BlockSpec((BLOCK, GATE_LANES), lambda i, j: (0, 0)),
    ]
    lead_shapes = [
        jax.ShapeDtypeStruct((BLOCK, NAT_COLS), BF16),
        jax.ShapeDtypeStruct((BLOCK, ATT_KV_WIDTH), BF16),
        jax.ShapeDtypeStruct((T_ROWS, BLOCK), BF16),
        jax.ShapeDtypeStruct((BLOCK, GATE_LANES), F32),
    ]
    outs = pl.pallas_call(
        _inproj_kernel,
        grid=(n_tiles, IN_NAT_STEPS + 1),
        in_specs=[
            pl.BlockSpec((IN_TM, D_MODEL), lambda i, j: (i, 0)),
            pl.BlockSpec((N_META, D_MODEL), lambda i, j: (0, 0)),
            pl.BlockSpec((1, D_MODEL), lambda i, j: (0, 0)),
        ] + _inproj_weight_specs() + bias_in_specs,
        out_specs=out_specs + lead_specs + bias_out_specs,
        out_shape=out_shape + lead_shapes + bias_shapes,
        scratch_shapes=[pltpu.VMEM((IN_TM, D_MODEL), BF16), pltpu.VMEM((BLOCK, D_MODEL), BF16),
                        pltpu.VMEM((N_BUCKETS * ATT_HEADS, SUBLANES, BLOCK), F32)],
        compiler_params=_params(2),
        name="inproj",
    )(x2, lead, g, w_nat, w3, rel_bias.astype(F32), bkt_band, bkt_meta)
    return outs[:N_PROJ_OUT], outs[N_PROJ_OUT:2 * N_PROJ_OUT], outs[2 * N_PROJ_OUT:]


GQ = ATT_GROUP * BLOCK
ATT_SLABS_PER_PHASE = 2
FUSED_ATT_SLABS_PER_PHASE = 4


def _attn_prologue(n, kc_ref, kp_ref, km_ref):
    c = lax.broadcasted_iota(jnp.int32, (2 * BLOCK, BLOCK), 0)
    r = lax.broadcasted_iota(jnp.int32, (2 * BLOCK, BLOCK), 1)
    dist = r + BLOCK - c
    band_ok = (dist >= 0) & (dist < WINDOW) & ((c >= BLOCK) | (n >= 1))
    mm = lax.broadcasted_iota(jnp.int32, (N_META, BLOCK), 0)
    rr = lax.broadcasted_iota(jnp.int32, (N_META, BLOCK), 1)
    meta_ok = (n + 1) * BLOCK + rr >= N_PAD + mm
    kall = jnp.concatenate([kp_ref[...], kc_ref[...], km_ref[...]], axis=0)
    return band_ok, meta_ok, kall


BF16_ROWS = 16


def _ones_row(width):
    row = lax.broadcasted_iota(jnp.int32, (BF16_ROWS, width), 0)
    return jnp.where(row == 0, 1.0, 0.0).astype(BF16)


def _attn_head(h, pro, sink_ref, qt_ref, vc_ref, vp_ref, vl_ref, bb_ref, bm_ref, o_ref,
               slabs_per_phase=ATT_SLABS_PER_PHASE):
    band_ok, meta_ok, kall = pro
    lo = (h // 2) * LANE
    kpair = kall[:, lo:lo + LANE]
    qh = jnp.concatenate(
        [qt_ref[(ATT_GROUP * h + g) * ATT_HEAD_DIM:(ATT_GROUP * h + g + 1) * ATT_HEAD_DIM, :]
         for g in range(ATT_GROUP)], axis=1)
    zq = jnp.zeros_like(qh)
    qz = jnp.concatenate([qh, zq] if h % 2 == 0 else [zq, qh], axis=0)
    st = jnp.dot(kpair, qz, preferred_element_type=F32)
    yield
    zpad = jnp.zeros((N_PAD, BLOCK), BF16)
    p_band, p_lead, sink_p = [], [], []
    for g in range(ATT_GROUP):
        gs = slice(g * BLOCK, (g + 1) * BLOCK)
        sl = st[:, gs]
        sink = jnp.full((1, BLOCK), sink_ref[ATT_GROUP * h + g], F32) * LOG2E
        lb = jnp.where(band_ok, sl[:2 * BLOCK] + bb_ref[h, :, gs], NEG * LOG2E)
        lm = jnp.where(meta_ok, sl[2 * BLOCK:] + bm_ref[0, h, :, gs], NEG * LOG2E)
        m = jnp.maximum(jnp.max(lb, axis=0, keepdims=True), jnp.max(lm, axis=0, keepdims=True))
        m = jnp.maximum(m, sink)
        pb = jnp.exp2(lb - m)
        pm = jnp.exp2(lm - m)
        sink_p.append(jnp.exp2(sink - m))
        p_band.append(pb.astype(BF16))
        p_lead.append(jnp.concatenate([zpad, pm.astype(BF16)], axis=0))
        if g % slabs_per_phase == slabs_per_phase - 1:
            yield
    pbt = jnp.concatenate(p_band, axis=1)
    plt = jnp.concatenate(p_lead, axis=1)
    hs = slice(h * ATT_HEAD_DIM, (h + 1) * ATT_HEAD_DIM)
    vband = jnp.concatenate([vp_ref[hs, :], vc_ref[hs, :]], axis=1)
    ones_b = _ones_row(2 * BLOCK)
    ones_l = _ones_row(BLOCK)
    ot = (jnp.dot(jnp.concatenate([vband, ones_b], axis=0), pbt, preferred_element_type=F32)
          + jnp.dot(jnp.concatenate([vl_ref[hs, :], ones_l], axis=0), plt,
                    preferred_element_type=F32))
    for g in range(ATT_GROUP):
        gs = slice(g * BLOCK, (g + 1) * BLOCK)
        den = ot[ATT_HEAD_DIM:ATT_HEAD_DIM + 1, gs] + sink_p[g]
        row = (ATT_GROUP * h + g) * ATT_HEAD_DIM
        o_ref[row:row + ATT_HEAD_DIM, :] = (ot[:ATT_HEAD_DIM, gs] * (1.0 / den)).astype(BF16)


def _conv_shift_matrix():
    s = np.zeros((CONV_WIDTH * BLOCK, 2 * BLOCK), np.float32)
    for k in range(CONV_WIDTH):
        t = np.arange(BLOCK)
        s[k * BLOCK + t, BLOCK + t - k] = 1.0
    return jnp.asarray(s, BF16)


def _mlstm_gates(n, gate_ref, gb_ref):
    t_idx = lax.broadcasted_iota(jnp.int32, (BLOCK, GATE_LANES), 0)
    valid = (n > 0) | (t_idx >= N_PAD)
    pre = GATE_SOFTCAP * jnp.tanh((gate_ref[...] + gb_ref[...]) / GATE_SOFTCAP)
    log_i = jnp.where(valid, pre, NEG)
    log_sig = jnp.minimum(pre, 0.0) - jnp.log1p(jnp.exp(-jnp.abs(pre)))
    log_f = jnp.where(valid, log_sig, 0.0)
    row = lax.broadcasted_iota(jnp.int32, (BLOCK, BLOCK), 0)
    col = lax.broadcasted_iota(jnp.int32, (BLOCK, BLOCK), 1)
    causal = col <= row
    tril = jnp.where(causal, 1.0, 0.0).astype(F32)
    b_all = jnp.dot(tril, log_f, preferred_element_type=F32,
                    precision=lax.Precision.HIGHEST)
    return log_i, b_all, log_i.T, b_all.T, causal


def _mlstm_conv(p_ref, shift_ref, cw_ref, cb_ref, xprev_ref):
    xcur = p_ref[:, NAT_MQK:NAT_MV]
    xcat = jnp.concatenate([xprev_ref[...], xcur], axis=0)
    sh = jnp.dot(shift_ref[...], xcat, preferred_element_type=F32)
    xprev_ref[...] = xcur
    y = cb_ref[...]
    for k in range(CONV_WIDTH):
        y = y + cw_ref[CONV_WIDTH - 1 - k:CONV_WIDTH - k, :] * sh[k * BLOCK:(k + 1) * BLOCK]
    act = y * jax.nn.sigmoid(y)
    return act[:, :ML_QK_WIDTH] * (ML_QK_DIM ** -0.5), act[:, ML_QK_WIDTH:]


def _mlstm_head(h, gates, q_all, k_all, p_ref, mhn_ref, o_ref, c_ref, n_ref, m_ref):
    log_i, b_all, log_i_t, b_all_t, causal = gates
    q_f = q_all[:, h * ML_QK_DIM:(h + 1) * ML_QK_DIM]
    k_f = k_all[:, h * ML_QK_DIM:(h + 1) * ML_QK_DIM]
    q_h = q_f.astype(BF16)
    k_h = k_f.astype(BF16)
    v_h = p_ref[:, NAT_MV + h * ML_V_DIM:NAT_MV + (h + 1) * ML_V_DIM]
    b_col = b_all[:, ML_HEADS + h:ML_HEADS + h + 1]
    li_col = log_i[:, h:h + 1]
    b_row = b_all_t[ML_HEADS + h:ML_HEADS + h + 1, :]
    li_row = log_i_t[h:h + 1, :]
    b_last = b_all[BLOCK - 1:BLOCK, ML_HEADS + h:ML_HEADS + h + 1]
    c_prev = c_ref[h]
    n_prev = n_ref[h][0:1, :]
    m_prev = m_ref[h][0:1, 0:1]

    dmat = jnp.where(causal, b_col - b_row + li_row, -jnp.inf)
    inter = b_col + m_prev
    m_t = jnp.maximum(inter, jnp.max(dmat, axis=1, keepdims=True))
    s = lax.dot_general(q_h, k_h, CONTRACT_LAST, preferred_element_type=F32) * jnp.exp(dmat - m_t)
    yield
    a_t = jnp.exp(inter - m_t)
    num = (jnp.dot(s.astype(BF16), v_h, preferred_element_type=F32)
           + a_t * jnp.dot(q_h, c_prev.astype(BF16), preferred_element_type=F32))
    den = (jnp.sum(s, axis=1, keepdims=True)
           + a_t * jnp.sum(q_f * n_prev, axis=1, keepdims=True))
    den = jnp.maximum(jnp.abs(den), jnp.exp(-m_t))
    hh = num / den
    hh = hh * lax.rsqrt(jnp.mean(hh * hh, axis=-1, keepdims=True) + EPS)
    vs = slice(h * ML_V_DIM, (h + 1) * ML_V_DIM)
    hh = hh * mhn_ref[:, vs]
    m_o = p_ref[:, NAT_MO + h * ML_V_DIM:NAT_MO + (h + 1) * ML_V_DIM].astype(F32)
    o_ref[:, vs] = (hh * jax.nn.sigmoid(m_o)).astype(BF16)
    yield

    g_col = b_last - b_col + li_col
    m_loc = jnp.max(g_col, axis=0, keepdims=True)
    wk = jnp.exp(g_col - m_loc) * k_f
    c_loc = lax.dot_general(wk.astype(BF16), v_h, CONTRACT_FIRST,
                            preferred_element_type=F32)
    n_loc = jnp.sum(wk, axis=0, keepdims=True)
    m_new = jnp.maximum(b_last + m_prev, m_loc)
    a = jnp.exp(b_last + m_prev - m_new)
    cc = jnp.exp(m_loc - m_new)
    c_ref[h] = a * c_prev + cc * c_loc
    n_ref[h] = jnp.broadcast_to(a * n_prev + cc * n_loc, (8, ML_QK_DIM))
    m_ref[h] = jnp.broadcast_to(m_new, (8, LANE))


N_MIX_IN = 17
N_STATE = 4
MIXER_PHASES_PER_NAT_PHASE = 6
MIXER_PHASES_PER_LAST_PHASE = 2


N_ATT_IN = 10


def _attention_phases(n, att_in, att_ref, slabs_per_phase=ATT_SLABS_PER_PHASE):
    sink_ref, qt_ref, kc_ref, kp_ref, km_ref, vc_ref, vp_ref, vl_ref, bb_ref, bm_ref = att_in
    pro = _attn_prologue(n - 1, kc_ref, kp_ref, km_ref)
    for h in range(ATT_KV_HEADS):
        yield from _attn_head(h, pro, sink_ref, qt_ref, vc_ref, vp_ref, vl_ref, bb_ref, bm_ref, att_ref,
                              slabs_per_phase)


def _mlstm_phases(n, ml_in, hm_ref, state):
    p_ref, gate_ref, gb_ref, shift_ref, cw_ref, cb_ref, mhn_ref = ml_in
    c_ref, n_ref, m_ref, xprev_ref = state
    gates = _mlstm_gates(n, gate_ref, gb_ref)
    yield
    q_all, k_all = _mlstm_conv(p_ref, shift_ref, cw_ref, cb_ref, xprev_ref)
    yield
    for h in range(ML_HEADS):
        yield from _mlstm_head(h, gates, q_all, k_all, p_ref, mhn_ref, hm_ref, c_ref, n_ref, m_ref)


def _mixer_phases(n, mix_in, att_ref, hm_ref, state):
    return _interleave([_mlstm_phases(n, mix_in[N_ATT_IN:], hm_ref, state),
                        _attention_phases(n, mix_in[:N_ATT_IN], att_ref, FUSED_ATT_SLABS_PER_PHASE)],
                       [1, 1])


def _load_state(state, state_in):
    for dst, src in zip(state, state_in):
        dst[...] = src[...]


def _state_shapes():
    return [
        jax.ShapeDtypeStruct((ML_HEADS, ML_QK_DIM, ML_V_DIM), F32),
        jax.ShapeDtypeStruct((ML_HEADS, SUBLANES, ML_QK_DIM), F32),
        jax.ShapeDtypeStruct((ML_HEADS, SUBLANES, LANE), F32),
        jax.ShapeDtypeStruct((BLOCK, 2 * ML_QK_WIDTH), BF16),
    ]


def _state_scratch():
    return [pltpu.VMEM(s.shape, s.dtype) for s in _state_shapes()]


def _whole(shape):
    return pl.BlockSpec(shape, lambda *ids: (0,) * len(shape))


def _mixer_in_specs(step):
    cur = lambda *ids: step(*ids)
    prev = lambda *ids: jnp.maximum(step(*ids) - 1, 0)
    v_row = ATT_WIDTH // ATT_KV_WIDTH
    meta_rows = N_PAD // N_META
    return [
        pl.BlockSpec(memory_space=pltpu.SMEM),
        pl.BlockSpec((ATT_WIDTH, BLOCK), lambda *ids: (0, cur(*ids))),
        pl.BlockSpec((BLOCK, ATT_KV_WIDTH), lambda *ids: (cur(*ids), 0)),
        pl.BlockSpec((BLOCK, ATT_KV_WIDTH), lambda *ids: (prev(*ids), 0)),
        pl.BlockSpec((N_META, ATT_KV_WIDTH), lambda *ids: (meta_rows, 0)),
        pl.BlockSpec((ATT_KV_WIDTH, BLOCK), lambda *ids: (v_row, cur(*ids))),
        pl.BlockSpec((ATT_KV_WIDTH, BLOCK), lambda *ids: (v_row, prev(*ids))),
        pl.BlockSpec((ATT_KV_WIDTH, BLOCK), lambda *ids: (v_row, 0)),
        _whole((ATT_KV_HEADS, 2 * BLOCK, GQ)),
        pl.BlockSpec((1, ATT_KV_HEADS, N_META, GQ), lambda *ids: (cur(*ids), 0, 0, 0)),
        pl.BlockSpec((BLOCK, NAT_COLS), lambda *ids: (cur(*ids), 0)),
        pl.BlockSpec((BLOCK, GATE_LANES), lambda *ids: (cur(*ids), 0)),
        _whole((1, GATE_LANES)),
        _whole((CONV_WIDTH * BLOCK, 2 * BLOCK)),
        _whole((CONV_WIDTH, 2 * ML_QK_WIDTH)),
        _whole((1, 2 * ML_QK_WIDTH)),
        _whole((1, ML_WIDTH)),
    ] + [_whole(s.shape) for s in _state_shapes()]


def _mixer_args(consts, proj_b, proj_lead, state):
    sinks, bias_band_t, bias_meta_t, gate_bias, shift, conv_w, conv_b, mh_norm = consts
    nat_b, kn_b, projt_b, gt_b = proj_b
    _, kn_lead, projt_lead, _ = proj_lead
    return [sinks, projt_b, kn_b, kn_b, kn_lead, projt_b, projt_b, projt_lead, bias_band_t,
            bias_meta_t, nat_b, gt_b, gate_bias, shift, conv_w, conv_b, mh_norm] + list(state)


def _mixer_out(step, n_rows):
    specs = [
        pl.BlockSpec((ATT_WIDTH, BLOCK), lambda *ids: (0, step(*ids))),
        pl.BlockSpec((BLOCK, ML_WIDTH), lambda *ids: (step(*ids), 0)),
    ]
    shapes = [
        jax.ShapeDtypeStruct((ATT_WIDTH, n_rows), BF16),
        jax.ShapeDtypeStruct((n_rows, ML_WIDTH), BF16),
    ]
    return specs, shapes


def _lead_state_kernel(p_ref, gate_ref, gb_ref, shift_ref, cw_ref, cb_ref, mhn_ref,
                       c_ref, n_ref, m_ref, xprev_ref, hm_ref):
    for ref in (c_ref, n_ref, m_ref, xprev_ref):
        ref[...] = jnp.zeros_like(ref)
    gates = _mlstm_gates(0, gate_ref, gb_ref)
    q_all, k_all = _mlstm_conv(p_ref, shift_ref, cw_ref, cb_ref, xprev_ref)
    for h in range(ML_HEADS):
        _drain(_mlstm_head(h, gates, q_all, k_all, p_ref, mhn_ref, hm_ref, c_ref, n_ref, m_ref))


def _lead_state(proj_lead, consts):
    nat_lead, _, _, gates_lead = proj_lead
    _, _, _, gate_bias, shift, conv_w, conv_b, mh_norm = consts
    shapes = _state_shapes()
    return pl.pallas_call(
        _lead_state_kernel,
        grid=(1,),
        in_specs=[
            _whole((BLOCK, NAT_COLS)),
            _whole((BLOCK, GATE_LANES)),
            _whole((1, GATE_LANES)),
            _whole((CONV_WIDTH * BLOCK, 2 * BLOCK)),
            _whole((CONV_WIDTH, 2 * ML_QK_WIDTH)),
            _whole((1, 2 * ML_QK_WIDTH)),
            _whole((1, ML_WIDTH)),
        ],
        out_specs=[_whole(s.shape) for s in shapes],
        out_shape=shapes,
        scratch_shapes=[pltpu.VMEM((BLOCK, ML_WIDTH), BF16)],
        compiler_params=_params(1),
        name="lead_state",
    )(nat_lead, gates_lead, gate_bias, shift, conv_w, conv_b, mh_norm)


FUSED_TM = 512


def _inproj_mixer_kernel(*refs):
    x_ref, g_ref, wn_ref, w3_ref = refs[:4]
    mix_in = refs[4:4 + N_MIX_IN]
    state_in = refs[4 + N_MIX_IN:4 + N_MIX_IN + N_STATE]
    wu_ref = refs[4 + N_MIX_IN + N_STATE]
    (nat_ref, kn_ref, t_ref, gt_ref, att_ref, hm_ref,
     wu_bf_ref) = refs[5 + N_MIX_IN + N_STATE:-(1 + N_STATE)]
    u_ref = refs[-(1 + N_STATE)]
    state = refs[-N_STATE:]
    i = pl.program_id(0)
    j = pl.program_id(1)
    step = i * (IN_NAT_STEPS + 1) + j
    wu_bf_ref[...] = wu_ref[...].astype(BF16)

    @pl.when(j == 0)
    def _():
        u_ref[...] = _rms(x_ref[...], g_ref[...]).astype(BF16)

    @pl.when(step == 0)
    def _():
        _load_state(state, state_in)

    @pl.when(j < IN_NAT_STEPS)
    def _():
        _drain(_interleave([_inproj_nat_phases(FUSED_TM, u_ref, wn_ref, nat_ref),
                            _mixer_phases(step + 1, mix_in, att_ref, hm_ref, state)],
                           [1, MIXER_PHASES_PER_NAT_PHASE]))

    @pl.when(j == IN_NAT_STEPS)
    def _():
        _drain(_interleave([_inproj_last_phases(FUSED_TM, u_ref, w3_ref, kn_ref, t_ref, gt_ref),
                            _mixer_phases(step + 1, mix_in, att_ref, hm_ref, state)],
                           [1, MIXER_PHASES_PER_LAST_PHASE]))


def _inproj_mixer(x2, row_tile0, g, w_nat, w3, mixer_args, w_up, nblk):
    n_tiles = nblk * BLOCK // FUSED_TM
    assert n_tiles * (IN_NAT_STEPS + 1) == nblk
    step = lambda i, j: i * (IN_NAT_STEPS + 1) + j
    proj_specs, proj_shapes = _inproj_out(FUSED_TM, n_tiles * FUSED_TM)
    mix_specs, mix_shapes = _mixer_out(step, nblk * BLOCK)
    wu_spec = pl.BlockSpec((D_MODEL, D_FF // nblk), lambda i, j: (0, step(i, j)))
    return pl.pallas_call(
        _inproj_mixer_kernel,
        grid=(n_tiles, IN_NAT_STEPS + 1),
        in_specs=[
            pl.BlockSpec((FUSED_TM, D_MODEL), lambda i, j: (row_tile0 + i, 0)),
            pl.BlockSpec((1, D_MODEL), lambda i, j: (0, 0)),
        ] + _inproj_weight_specs() + _mixer_in_specs(step) + [wu_spec],
        out_specs=proj_specs + mix_specs + [wu_spec],
        out_shape=proj_shapes + mix_shapes + [jax.ShapeDtypeStruct(w_up.shape, BF16)],
        scratch_shapes=[pltpu.VMEM((FUSED_TM, D_MODEL), BF16)] + _state_scratch(),
        compiler_params=_params(2),
        name="inproj_mixer",
    )(x2, g, w_nat, w3, *mixer_args, w_up)


MIXER_PAIR = 2
PAIR_ROWS = MIXER_PAIR * BLOCK
ATT_PHASES_PER_MLSTM_PHASE = 4
MLSTM_CHUNK_LAG = 2


def _mixer_kernel(*refs):
    (sink_ref, qt_ref, kc_ref, kp_ref, km_ref, vc_ref, vp_ref, vl_ref, bb_ref, bm_ref,
     p_ref, gate_ref, gb_ref, shift_ref, cw_ref, cb_ref, mhn_ref) = refs[:N_MIX_IN]
    state_in = refs[N_MIX_IN:N_MIX_IN + N_STATE]
    wo_ref, wd_ref = refs[N_MIX_IN + N_STATE:N_MIX_IN + N_STATE + 2]
    att_ref, hm_ref, wo_bf_ref, wd_bf_ref = refs[N_MIX_IN + N_STATE + 2:-N_STATE]
    state = refs[-N_STATE:]
    step = pl.program_id(0)

    @pl.when(step == 0)
    def _():
        _load_state(state, state_in)

    wo_bf_ref[...] = wo_ref[...].astype(BF16)
    wd_bf_ref[...] = wd_ref[...].astype(BF16)

    rows = lambda ref, k: ref.at[k * BLOCK:(k + 1) * BLOCK]
    cols = lambda ref, k: ref.at[:, k * BLOCK:(k + 1) * BLOCK]
    attention, chunks = [], []
    for k in range(MIXER_PAIR):
        n = MIXER_PAIR * step + 1 + k
        kp_k = kp_ref if k == 0 else rows(kc_ref, k - 1)
        vp_k = vp_ref if k == 0 else cols(vc_ref, k - 1)
        att_in = (sink_ref, cols(qt_ref, k), rows(kc_ref, k), kp_k, km_ref, cols(vc_ref, k), vp_k,
                  vl_ref, bb_ref, bm_ref.at[k:k + 1])
        attention.append(_attention_phases(n, att_in, cols(att_ref, k)))
        chunks.append((n, (rows(p_ref, k), rows(gate_ref, k), gb_ref, shift_ref, cw_ref, cb_ref,
                           mhn_ref), rows(hm_ref, k)))

    def mlstm_chunk(k):
        for _ in range(k * MLSTM_CHUNK_LAG):
            yield
        n, ml_in, hm_k = chunks[k]
        yield from _mlstm_phases(n, ml_in, hm_k, state)

    _drain(_interleave([mlstm_chunk(k) for k in range(MIXER_PAIR)] + attention,
                       [1] * MIXER_PAIR + [ATT_PHASES_PER_MLSTM_PHASE] * MIXER_PAIR))


def _mixer(mixer_args, w_out, w_down, nblk):
    n_steps = nblk // MIXER_PAIR
    before = lambda s: jnp.maximum(MIXER_PAIR * s - 1, 0)
    v_row = ATT_WIDTH // ATT_KV_WIDTH
    in_specs = [
        pl.BlockSpec(memory_space=pltpu.SMEM),
        pl.BlockSpec((ATT_WIDTH, PAIR_ROWS), lambda s: (0, s)),
        pl.BlockSpec((PAIR_ROWS, ATT_KV_WIDTH), lambda s: (s, 0)),
        pl.BlockSpec((BLOCK, ATT_KV_WIDTH), lambda s: (before(s), 0)),
        pl.BlockSpec((N_META, ATT_KV_WIDTH), lambda s: (N_PAD // N_META, 0)),
        pl.BlockSpec((ATT_KV_WIDTH, PAIR_ROWS), lambda s: (v_row, s)),
        pl.BlockSpec((ATT_KV_WIDTH, BLOCK), lambda s: (v_row, before(s))),
        pl.BlockSpec((ATT_KV_WIDTH, BLOCK), lambda s: (v_row, 0)),
        _whole((ATT_KV_HEADS, 2 * BLOCK, GQ)),
        pl.BlockSpec((MIXER_PAIR, ATT_KV_HEADS, N_META, GQ), lambda s: (s, 0, 0, 0)),
        pl.BlockSpec((PAIR_ROWS, NAT_COLS), lambda s: (s, 0)),
        pl.BlockSpec((PAIR_ROWS, GATE_LANES), lambda s: (s, 0)),
        _whole((1, GATE_LANES)),
        _whole((CONV_WIDTH * BLOCK, 2 * BLOCK)),
        _whole((CONV_WIDTH, 2 * ML_QK_WIDTH)),
        _whole((1, 2 * ML_QK_WIDTH)),
        _whole((1, ML_WIDTH)),
    ] + [_whole(s.shape) for s in _state_shapes()]
    assert len(in_specs) == N_MIX_IN + N_STATE
    w_specs = [
        pl.BlockSpec((w_out.shape[0] // n_steps, D_MODEL), lambda s: (s, 0)),
        pl.BlockSpec((D_FF // n_steps, D_MODEL), lambda s: (s, 0)),
    ]
    return pl.pallas_call(
        _mixer_kernel,
        grid=(n_steps,),
        in_specs=in_specs + w_specs,
        out_specs=[
            pl.BlockSpec((ATT_WIDTH, PAIR_ROWS), lambda s: (0, s)),
            pl.BlockSpec((PAIR_ROWS, ML_WIDTH), lambda s: (s, 0)),
        ] + w_specs,
        out_shape=[
            jax.ShapeDtypeStruct((ATT_WIDTH, nblk * BLOCK), BF16),
            jax.ShapeDtypeStruct((nblk * BLOCK, ML_WIDTH), BF16),
        ] + [jax.ShapeDtypeStruct(w.shape, BF16) for w in (w_out, w_down)],
        scratch_shapes=_state_scratch(),
        compiler_params=_params(1),
        name="mixer",
    )(*mixer_args, w_out, w_down)


OUT_TM = 512


def _outproj_kernel(att0_ref, att1_ref, hm0_ref, hm1_ref, x_ref, wa_ref, wm_ref, g_ref,
                    h_ref, u_ref):
    i = pl.program_id(0)
    half = pl.num_programs(0) // 2

    def body(att_ref, hm_ref):
        h = (x_ref[...]
             + lax.dot_general(att_ref[...], wa_ref[...], CONTRACT_FIRST, preferred_element_type=F32)
             + jnp.dot(hm_ref[...], wm_ref[...], preferred_element_type=F32))
        h_ref[...] = h
        u_ref[...] = _rms(h, g_ref[...]).astype(BF16)

    @pl.when(i < half)
    def _():
        body(att0_ref, hm0_ref)

    @pl.when(i >= half)
    def _():
        body(att1_ref, hm1_ref)


def _outproj(att0, att1, hm0, hm1, x2, w_out, g):
    rows = x2.shape[0]
    n = rows // OUT_TM
    half = n // 2
    first = lambda i: jnp.minimum(i, half - 1)
    second = lambda i: jnp.maximum(i - half, 0)
    return pl.pallas_call(
        _outproj_kernel,
        grid=(n,),
        in_specs=[
            pl.BlockSpec((ATT_WIDTH, OUT_TM), lambda i: (0, first(i))),
            pl.BlockSpec((ATT_WIDTH, OUT_TM), lambda i: (0, second(i))),
            pl.BlockSpec((OUT_TM, ML_WIDTH), lambda i: (first(i), 0)),
            pl.BlockSpec((OUT_TM, ML_WIDTH), lambda i: (second(i), 0)),
            pl.BlockSpec((OUT_TM, D_MODEL), lambda i: (i, 0)),
            pl.BlockSpec((ATT_WIDTH, D_MODEL), lambda i: (0, 0)),
            pl.BlockSpec((ML_WIDTH, D_MODEL), lambda i: (ATT_WIDTH // ML_WIDTH, 0)),
            pl.BlockSpec((1, D_MODEL), lambda i: (0, 0)),
        ],
        out_specs=[
            pl.BlockSpec((OUT_TM, D_MODEL), lambda i: (i, 0)),
            pl.BlockSpec((OUT_TM, D_MODEL), lambda i: (i, 0)),
        ],
        out_shape=[
            jax.ShapeDtypeStruct((rows, D_MODEL), F32),
            jax.ShapeDtypeStruct((rows, D_MODEL), BF16),
        ],
        compiler_params=_params(1),
        name="outproj",
    )(att0, att1, hm0, hm1, x2, w_out, w_out, g)


MLP_TM = 1024
MLP_TF = 1024


def _mlp_step(mode, u_ref, wu_ref, wd_ref, o_ref, hbuf_ref=None, ssq_ref=None):
    acts = []
    for c in range(MLP_TF // MXU_COLS):
        cs = slice(c * MXU_COLS, (c + 1) * MXU_COLS)
        a = jnp.dot(u_ref[...], wu_ref[:, cs], preferred_element_type=F32)
        acts.append(jnp.square(jnp.maximum(a, 0.0)).astype(BF16))
        yield
    act = jnp.concatenate(acts, axis=1)
    for c in range(D_MODEL // MXU_COLS):
        cs = slice(c * MXU_COLS, (c + 1) * MXU_COLS)
        d = jnp.dot(act, wd_ref[:, cs], preferred_element_type=F32)
        if mode == "first":
            o_ref[:, cs] = d
        elif mode == "mid":
            o_ref[:, cs] += d
        else:
            v = o_ref[:, cs] + d + hbuf_ref[:, cs]
            o_ref[:, cs] = v
            sq = v * v
            part = sum(sq[:, k * LANE:(k + 1) * LANE] for k in range(MXU_COLS // LANE))
            if c == 0:
                ssq_ref[...] = part
            else:
                ssq_ref[...] += part
        yield


def _mlp_kernel(u_ref, h_hbm, wu_ref, wd_ref, g_ref, o_ref, hbuf_ref, ssq_ref, hsem):
    i = pl.program_id(0)
    j = pl.program_id(1)
    last = pl.num_programs(1) - 1
    r0 = pl.multiple_of(i * MLP_TM, MLP_TM)
    h_copy = pltpu.make_async_copy(h_hbm.at[pl.ds(r0, MLP_TM), :], hbuf_ref, hsem)

    @pl.when(j == 0)
    def _():
        h_copy.start()
        _drain(_mlp_step("first", u_ref, wu_ref, wd_ref, o_ref))

    @pl.when((j > 0) & (j < last))
    def _():
        _drain(_mlp_step("mid", u_ref, wu_ref, wd_ref, o_ref))

    @pl.when(j == last)
    def _():
        h_copy.wait()
        _drain(_mlp_step("last", u_ref, wu_ref, wd_ref, o_ref, hbuf_ref, ssq_ref))
        ms = jnp.sum(ssq_ref[...], axis=-1, keepdims=True) / D_MODEL
        o_ref[...] = o_ref[...] * lax.rsqrt(ms + EPS) * g_ref[...]


def _mlp(u, h, w_up, w_down, g):
    rows = u.shape[0]
    return pl.pallas_call(
        _mlp_kernel,
        grid=(rows // MLP_TM, D_FF // MLP_TF),
        in_specs=[
            pl.BlockSpec((MLP_TM, D_MODEL), lambda i, j: (i, 0)),
            pl.BlockSpec(memory_space=pl.ANY),
            pl.BlockSpec((D_MODEL, MLP_TF), lambda i, j: (0, j)),
            pl.BlockSpec((MLP_TF, D_MODEL), lambda i, j: (j, 0)),
            pl.BlockSpec((1, D_MODEL), lambda i, j: (0, 0)),
        ],
        out_specs=pl.BlockSpec((MLP_TM, D_MODEL), lambda i, j: (i, 0)),
        out_shape=jax.ShapeDtypeStruct((rows, D_MODEL), F32),
        scratch_shapes=[pltpu.VMEM((MLP_TM, D_MODEL), F32), pltpu.VMEM((MLP_TM, LANE), F32),
                        pltpu.SemaphoreType.DMA(())],
        compiler_params=_params(2),
        name="mlp",
    )(u, h, w_up, w_down, g)


def _t5_bucket(dist):
    max_exact = N_BUCKETS // 2
    d = jnp.maximum(dist, 0)
    ratio = jnp.maximum(d, max_exact).astype(F32) / max_exact
    large = max_exact + (jnp.log(ratio) / math.log(MAX_DISTANCE / max_exact)
                         * (N_BUCKETS - max_exact)).astype(jnp.int32)
    large = jnp.minimum(large, N_BUCKETS - 1)
    return jnp.where(d < max_exact, d, large)


BIAS_ROWS = 16


BIAS_BUCKETS_PER_PHASE = 8


def _bias_fill_splats(rb_ref, splat_ref):
    def fill(b, carry):
        for hd in range(ATT_HEADS):
            splat_ref[b * ATT_HEADS + hd] = jnp.full((SUBLANES, BLOCK), rb_ref[b, hd], F32) * LOG2E
        return carry

    lax.fori_loop(0, N_BUCKETS, fill, 0)


def _bias_lookup_phases(bkt, splat_ref, write):
    accs = [jnp.zeros(bkt.shape, F32)] * ATT_HEADS
    for b in range(N_BUCKETS):
        hit = (bkt >= b) & (bkt < b + 1)
        accs = [jnp.where(hit, jnp.concatenate([splat_ref[b * ATT_HEADS + hd]]
                                               * (BIAS_ROWS // SUBLANES), axis=0), a)
                for hd, a in enumerate(accs)]
        if b % BIAS_BUCKETS_PER_PHASE == BIAS_BUCKETS_PER_PHASE - 1:
            yield
    write(accs)


def _bias_step_phases(bkt_band_ref, bkt_meta_ref, band_ref, meta_ref, splat_ref):
    def write_band(tables):
        for hd, t in enumerate(tables):
            h, g = divmod(hd, ATT_GROUP)
            band_ref[h, :, g * BLOCK:(g + 1) * BLOCK] = t

    yield from _bias_lookup_phases(bkt_band_ref[...], splat_ref, write_band)
    for k in range(bkt_meta_ref.shape[0]):
        def write_meta(tables, k=k):
            for hd, t in enumerate(tables):
                h, g = divmod(hd, ATT_GROUP)
                meta_ref[k, h, :, g * BLOCK:(g + 1) * BLOCK] = t

        yield from _bias_lookup_phases(bkt_meta_ref[k], splat_ref, write_meta)


def _bias_buckets(nblk):
    c = jnp.arange(2 * BLOCK)[:, None]
    r = jnp.arange(BLOCK)[None, :]
    bkt_band = _t5_bucket(r + BLOCK - c).astype(jnp.int32)
    q_pos = BLOCK + jnp.arange(nblk * BLOCK)[None, :]
    m_pos = N_PAD + jnp.arange(N_META)[:, None]
    bkt_meta = _t5_bucket(q_pos - m_pos).astype(jnp.int32)
    return bkt_band, bkt_meta.reshape(N_META, nblk, BLOCK).transpose(1, 0, 2)


W_IN_OFFS = dict(q=0, k=1024, v=1280, mq=1536, mk=2048, mv=2560, mo=3584, gates=4608)
PREP_ROWS = 256
NAT_SRC = [W_IN_OFFS["mq"] // PREP_ROWS + t for t in range(NAT_COLS // PREP_ROWS)]
W3_SRC = ([W_IN_OFFS["q"] // PREP_ROWS + t for t in range(ATT_WIDTH // PREP_ROWS)]
          + [W_IN_OFFS["v"] // PREP_ROWS + t for t in range(ATT_KV_WIDTH // PREP_ROWS)]
          + [W_IN_OFFS["k"] // PREP_ROWS + t for t in range(ATT_KV_WIDTH // PREP_ROWS)]
          + [W_IN_OFFS["gates"] // PREP_ROWS])
N_NAT_BLK = len(NAT_SRC)
N_Q_BLK = ATT_WIDTH // PREP_ROWS
N_PREP = N_NAT_BLK + len(W3_SRC)
assert W_IN_OFFS["mq"] + NAT_COLS == W_IN_OFFS["gates"] and len(W3_SRC) * PREP_ROWS == W3_ROWS


def _win_prep_kernel(src_ref, w_ref, nat_ref, w3_ref):
    j = pl.program_id(0)
    wv = w_ref[0]

    @pl.when(j < N_NAT_BLK)
    def _():
        nat_ref[...] = wv.astype(BF16)

    @pl.when((j >= N_NAT_BLK) & (j < N_NAT_BLK + N_Q_BLK))
    def _():
        w3_ref[...] = (wv * (ATT_HEAD_DIM ** -0.5 * LOG2E)).astype(BF16)

    @pl.when((j >= N_NAT_BLK + N_Q_BLK) & (j < N_PREP - 1))
    def _():
        w3_ref[...] = wv.astype(BF16)

    @pl.when(j == N_PREP - 1)
    def _():
        row = lax.broadcasted_iota(jnp.int32, wv.shape, 0)
        w3_ref[...] = jnp.where(row < 2 * ML_HEADS, wv, 0.0).astype(BF16)


def _win_prep(w_in_t):
    src = jnp.asarray(NAT_SRC + W3_SRC, jnp.int32)
    d = w_in_t.shape[2]
    return pl.pallas_call(
        _win_prep_kernel,
        grid_spec=pltpu.PrefetchScalarGridSpec(
            num_scalar_prefetch=1,
            grid=(N_PREP,),
            in_specs=[pl.BlockSpec((1, PREP_ROWS, d), lambda j, src: (0, src[j], 0))],
            out_specs=[
                pl.BlockSpec((PREP_ROWS, d), lambda j, src: (jnp.minimum(j, N_NAT_BLK - 1), 0)),
                pl.BlockSpec((PREP_ROWS, d), lambda j, src: (jnp.maximum(j - N_NAT_BLK, 0), 0)),
            ],
        ),
        out_shape=[
            jax.ShapeDtypeStruct((NAT_COLS, d), BF16),
            jax.ShapeDtypeStruct((W3_ROWS, d), BF16),
        ],
        compiler_params=_params(1),
        name="w_in_prep",
    )(src, w_in_t)


def kernel(x, meta_tokens, w_in, conv_w, conv_b, b_igate, b_fgate, attn_sinks, rel_bias,
           mh_norm, w_out, norm_mix, norm_mlp, w_up, w_down, norm_final):
    batch, seq, d = x.shape
    assert batch == 2, "the two batches are processed one projection / mixer stage apart"
    nblk = seq // BLOCK
    x2 = x.reshape(batch * seq, d)
    lead = meta_tokens.astype(F32)
    g_mix = norm_mix[0][None].astype(F32)

    w_nat, w3 = _win_prep(jnp.swapaxes(w_in, 1, 2))
    gate_bias = jnp.pad(jnp.concatenate([b_igate[0], b_fgate[0]]).astype(F32),
                        (0, GATE_LANES - 2 * ML_HEADS))[None]

    proj0, proj_lead, (bias_band_t, bias_meta_t) = _inproj(
        x2, lead, g_mix, w_nat, w3, rel_bias, seq // IN_TM, nblk)
    consts = (attn_sinks[0].astype(F32), bias_band_t, bias_meta_t, gate_bias, _conv_shift_matrix(),
              conv_w[0].astype(F32), conv_b[0][None].astype(F32), mh_norm[0][None].astype(F32))
    state = _lead_state(proj_lead, consts)
    outs = _inproj_mixer(x2, seq // FUSED_TM, g_mix, w_nat, w3,
                         _mixer_args(consts, proj0, proj_lead, state), w_up[0], nblk)
    proj1, (att0, hm0, w_up_bf) = outs[:N_PROJ_OUT], outs[N_PROJ_OUT:]
    att1, hm1, w_out_bf, w_down_bf = _mixer(_mixer_args(consts, proj1, proj_lead, state),
                                            w_out[0], w_down[0], nblk)

    h2, u2 = _outproj(att0, att1, hm0, hm1, x2, w_out_bf, norm_mlp[0][None].astype(F32))
    out = _mlp(u2, h2, w_up_bf, w_down_bf, norm_final[None].astype(F32))
    return out.reshape(batch, seq, d)
```
